```python
import math
import jax
import jax.numpy as jnp
from jax import lax
import numpy as np

D_MODEL = 1024
BATCH = 1
SEQ = 16384
DEPTH = 2
DEC_BATCH = 32
DEC_SEQ = 4
PAST_LEN = 16384
PAGE_SIZE = 128

N_A_LAYERS = (DEPTH + 1) // 2
N_C_LAYERS = DEPTH // 2

SSD_HEADS = 8
SSD_HEAD_DIM = 64
SSD_INNER = SSD_HEADS * SSD_HEAD_DIM
SSD_STATE = 64
SSD_GROUPS = 2
SSD_CONV = 4
SSD_CHUNK = 128
SSD_GN = SSD_GROUPS * SSD_STATE
SSD_CONV_DIM = SSD_INNER + 2 * SSD_GN

MOBA_HEADS = 8
MOBA_HEAD_DIM = 64
MOBA_INNER = MOBA_HEADS * MOBA_HEAD_DIM
MOBA_BLOCK = 256
MOBA_TOPK = 3
MOBA_QBLOCK = 128

OFF_XBC = SSD_INNER
OFF_DT = OFF_XBC + SSD_CONV_DIM
OFF_Q = OFF_DT + SSD_HEADS
OFF_K = OFF_Q + MOBA_INNER
OFF_V = OFF_K + MOBA_INNER
IN_A = OFF_V + MOBA_INNER
MIX_A = SSD_INNER + MOBA_INNER

HGRN_HEADS = 8
HGRN_KEY_DIM = 128
HGRN_VAL_DIM = D_MODEL // HGRN_HEADS
HGRN_K = HGRN_HEADS * HGRN_KEY_DIM
HGRN_V = HGRN_HEADS * HGRN_VAL_DIM
HGRN_CHUNK = 64
IN_C = 2 * HGRN_K + 2 * HGRN_V

FFN_DIM = 2816
FFN_CONV = 3

RMS_EPS = 1e-6
F32 = jnp.float32

kernel_name = 'hybrid_ssd_moba_hgrn2_convffn_step'


def rmsnorm(x, w):
    xf = x.astype(F32)
    y = xf * lax.rsqrt(jnp.mean(xf * xf, axis=-1, keepdims=True) + RMS_EPS)
    return (y * w.astype(F32)).astype(x.dtype)


def norm_f32(x):
    return x * lax.rsqrt(jnp.mean(x * x, axis=-1, keepdims=True) + RMS_EPS)


def alibi_slopes(n_heads):
    return jnp.asarray(np.exp2(-8.0 * np.arange(1, n_heads + 1) / n_heads), dtype=F32)


def causal_dwconv(full, w, b):
    width = w.shape[0]
    length = full.shape[1] - (width - 1)
    wf = w.astype(F32)
    acc = full[:, 0:length].astype(F32) * wf[0]
    for j in range(1, width):
        acc = acc + full[:, j:j + length].astype(F32) * wf[j]
    return acc + b.astype(F32)


def to_chunks(a, q, n_chunks):
    pad = n_chunks * q - a.shape[1]
    a = jnp.pad(a, [(0, 0), (0, pad)] + [(0, 0)] * (a.ndim - 2))
    return a.reshape((a.shape[0], n_chunks, q) + a.shape[2:]).swapaxes(0, 1)


def from_chunks(a, length):
    a = a.swapaxes(0, 1)
    return a.reshape((a.shape[0], a.shape[1] * a.shape[2]) + a.shape[3:])[:, :length]


def ssd_chunked(x, dt, a_head, bm, cm, s0):
    length = x.shape[1]
    q = min(SSD_CHUNK, length)
    nc = -(-length // q)
    xs, dts, bs, cs = (to_chunks(t, q, nc) for t in (x, dt, bm, cm))
    tril = jnp.tril(jnp.ones((q, q), dtype=bool))

    def step(s, inp):
        xc, dtc, bc, cc = inp
        acum = jnp.cumsum(dtc * a_head, axis=1)
        at = acum.swapaxes(1, 2)
        decay = jnp.exp(jnp.where(tril, at[..., :, None] - at[..., None, :], -jnp.inf))
        cb = jnp.einsum('bthn,bshn->bhts', cc, bc) * decay
        y = jnp.einsum('bhts,bshp->bthp', cb, xc * dtc[..., None])
        y = y + jnp.einsum('bthn,bhpn->bthp', cc * jnp.exp(acum)[..., None], s)
        a_last = acum[:, -1]
        w_in = jnp.exp(a_last[:, None, :] - acum) * dtc
        s = s * jnp.exp(a_last)[..., None, None] + jnp.einsum('bsh,bshp,bshn->bhpn', w_in, xc, bc)
        return s, y

    s_fin, ys = lax.scan(step, s0, (xs, dts, bs, cs))
    return from_chunks(ys, length), s_fin


def gla_chunked(q, k, v, logf, s0):
    length = q.shape[1]
    cq = min(HGRN_CHUNK, length)
    nc = -(-length // cq)
    qs, kss, vs, gs = (to_chunks(t, cq, nc) for t in (q, k, v, logf))
    tril = jnp.tril(jnp.ones((cq, cq), dtype=bool))

    def step(s, inp):
        qc, kc, vc, gc = inp
        b = jnp.cumsum(gc, axis=1)
        qd = qc * jnp.exp(b)
        kd = kc * jnp.exp(-b)
        att = jnp.where(tril, jnp.einsum('bthk,bshk->bhts', qd, kd), 0.0)
        o = jnp.einsum('bhts,bshv->bthv', att, vc) + jnp.einsum('bthk,bhkv->bthv', qd, s)
        b_last = b[:, -1]
        s = s * jnp.exp(b_last)[..., None] + jnp.einsum('bshk,bshv->bhkv', kc * jnp.exp(b_last[:, None] - b), vc)
        return s, o

    s_fin, os_ = lax.scan(step, s0, (qs, kss, vs, gs))
    return from_chunks(os_, length), s_fin


def ssd_branch(z, xbc, dt_raw, conv_hist, conv_w, conv_b, dt_bias, a_log, d_skip, norm_w, s0):
    bsz, length, _ = xbc.shape
    full = jnp.concatenate([conv_hist.astype(xbc.dtype), xbc], axis=1)
    new_hist = full[:, -(SSD_CONV - 1):]
    xbc_c = jax.nn.silu(causal_dwconv(full, conv_w, conv_b))
    rep = SSD_HEADS // SSD_GROUPS
    xs = xbc_c[..., :SSD_INNER].reshape(bsz, length, SSD_HEADS, SSD_HEAD_DIM)
    bm = jnp.repeat(xbc_c[..., SSD_INNER:SSD_INNER + SSD_GN].reshape(bsz, length, SSD_GROUPS, SSD_STATE), rep, axis=2)
    cm = jnp.repeat(xbc_c[..., SSD_INNER + SSD_GN:].reshape(bsz, length, SSD_GROUPS, SSD_STATE), rep, axis=2)
    dt = jax.nn.softplus(dt_raw.astype(F32) + dt_bias.astype(F32))
    a_head = -jnp.exp(a_log.astype(F32))
    y, s_new = ssd_chunked(xs, dt, a_head, bm, cm, s0.astype(F32))
    y = y + xs * d_skip.astype(F32)[:, None]
    y = y.reshape(bsz, length, SSD_INNER) * jax.nn.silu(z.astype(F32))
    y = norm_f32(y.reshape(bsz, length, SSD_GROUPS, SSD_INNER // SSD_GROUPS)).reshape(bsz, length, SSD_INNER)
    return y * norm_w.astype(F32), new_hist, s_new


def hgrn_lower_bounds(gamma):
    sm = jax.nn.softmax(gamma.astype(F32), axis=0)
    return jnp.cumsum(sm, axis=0) - sm[0]


def hgrn_branch(u, lb, norm_w, s0):
    bsz, length, _ = u.shape
    q = u[..., :HGRN_K].reshape(bsz, length, HGRN_HEADS, HGRN_KEY_DIM).astype(F32)
    fx = u[..., HGRN_K:2 * HGRN_K].reshape(bsz, length, HGRN_HEADS, HGRN_KEY_DIM).astype(F32)
    iv = u[..., 2 * HGRN_K:2 * HGRN_K + HGRN_V].reshape(bsz, length, HGRN_HEADS, HGRN_VAL_DIM).astype(F32)
    g = u[..., 2 * HGRN_K + HGRN_V:].astype(F32)
    lbh = lb.reshape(HGRN_HEADS, HGRN_KEY_DIM)
    f = lbh + (1.0 - lbh) * jax.nn.sigmoid(fx)
    o, s_new = gla_chunked(q, 1.0 - f, iv, jnp.log(f), s0.astype(F32))
    o = norm_f32(o) * norm_w.astype(F32)
    return o.reshape(bsz, length, HGRN_V) * jax.nn.silu(g), s_new


def moba_attend(q, q_pos, k_sel, v_sel, sel_pos, sel_mask, k_own, v_own, own_pos, own_mask, slopes):
    qf = q.astype(F32) * (MOBA_HEAD_DIM ** -0.5)
    m = slopes[None, :, None, None]
    s_sel = jnp.einsum('bhqd,bhqkd->bhqk', qf, k_sel.astype(F32))
    s_sel = s_sel - m * (q_pos[None, None, :, None] - sel_pos).astype(F32)
    s_sel = jnp.where(sel_mask, s_sel, -jnp.inf)
    s_own = jnp.einsum('bhqd,bhkd->bhqk', qf, k_own.astype(F32))
    s_own = s_own - m * (q_pos[:, None] - own_pos[None, :]).astype(F32)
    s_own = jnp.where(own_mask, s_own, -jnp.inf)
    p = jax.nn.softmax(jnp.concatenate([s_sel, s_own], axis=-1), axis=-1)
    ks = k_sel.shape[3]
    return (jnp.einsum('bhqk,bhqkd->bhqd', p[..., :ks], v_sel.astype(F32))
            + jnp.einsum('bhqk,bhkd->bhqd', p[..., ks:], v_own.astype(F32)))


def moba_prompt(q, k, v, slopes):
    bsz, length, h, d = q.shape
    nb = -(-length // MOBA_BLOCK)
    pad = nb * MOBA_BLOCK - length

    def blocks(t):
        t = jnp.pad(t, ((0, 0), (0, pad), (0, 0), (0, 0)))
        return t.reshape(bsz, nb, MOBA_BLOCK, h, d).transpose(0, 3, 1, 2, 4)

    kb, vb = blocks(k), blocks(v)
    kmean = jnp.mean(kb.astype(F32), axis=3)
    qh = q.transpose(0, 2, 1, 3)
    topk = min(MOBA_TOPK, nb)
    bi = jnp.arange(bsz)[:, None, None, None]
    hi = jnp.arange(h)[None, :, None, None]
    blk_ar = jnp.arange(MOBA_BLOCK)

    def one_block(c):
        start = c * MOBA_QBLOCK
        qc = lax.dynamic_slice_in_dim(qh, start, MOBA_QBLOCK, axis=2)
        t_pos = start + jnp.arange(MOBA_QBLOCK)
        cur = start // MOBA_BLOCK
        gate = jnp.einsum('bhqd,bhnd->bhqn', qc.astype(F32), kmean)
        gate = jnp.where(jnp.arange(nb) < cur, gate, -jnp.inf)
        _, idx = lax.top_k(gate, topk)
        k_sel = kb[bi, hi, idx].reshape(bsz, h, MOBA_QBLOCK, topk * MOBA_BLOCK, d)
        v_sel = vb[bi, hi, idx].reshape(bsz, h, MOBA_QBLOCK, topk * MOBA_BLOCK, d)
        sel_pos = (idx[..., None] * MOBA_BLOCK + blk_ar).reshape(bsz, h, MOBA_QBLOCK, topk * MOBA_BLOCK)
        sel_mask = jnp.repeat(jnp.arange(topk) < cur, MOBA_BLOCK)
        k_own = lax.dynamic_index_in_dim(kb, cur, axis=2, keepdims=False)
        v_own = lax.dynamic_index_in_dim(vb, cur, axis=2, keepdims=False)
        own_pos = cur * MOBA_BLOCK + blk_ar
        own_mask = own_pos[None, :] <= t_pos[:, None]
        return moba_attend(qc, t_pos, k_sel, v_sel, sel_pos, sel_mask, k_own, v_own, own_pos, own_mask, slopes)

    out = lax.map(one_block, jnp.arange(length // MOBA_QBLOCK))
    out = out.transpose(1, 0, 3, 2, 4).reshape(bsz, length, h * d)
    k_rows = k.reshape(bsz, length // PAGE_SIZE, PAGE_SIZE, h, d).transpose(0, 1, 3, 2, 4)
    v_rows = v.reshape(bsz, length // PAGE_SIZE, PAGE_SIZE, h, d).transpose(0, 1, 3, 2, 4)
    return out, k_rows, v_rows


def moba_sample(q, k, v, k_pool, v_pool, page_table, slopes):
    db, ds, h, d = q.shape
    n_pages = page_table.shape[1]
    past = n_pages * PAGE_SIZE
    ppb = MOBA_BLOCK // PAGE_SIZE
    n_full = past // MOBA_BLOCK
    qh, kh, vh = (t.transpose(0, 2, 1, 3) for t in (q, k, v))
    q_pos = past + jnp.arange(ds)
    if n_full > 0:
        topk = min(MOBA_TOPK, n_full)
        full_pages = page_table[:, :n_full * ppb]
        kmean = jnp.mean(k_pool[full_pages].astype(F32).reshape(db, n_full, ppb, h, PAGE_SIZE, d), axis=(2, 4))
        gate = jnp.einsum('bhqd,bnhd->bhqn', qh.astype(F32), kmean)
        _, idx = lax.top_k(gate, topk)
        phys = full_pages.reshape(db, n_full, ppb)[jnp.arange(db)[:, None, None, None], idx]
        hsel = jnp.arange(h)[None, :, None, None, None]
        k_sel = k_pool[phys, hsel].reshape(db, h, ds, topk * MOBA_BLOCK, d)
        v_sel = v_pool[phys, hsel].reshape(db, h, ds, topk * MOBA_BLOCK, d)
        sel_pos = (idx[..., None] * MOBA_BLOCK + jnp.arange(MOBA_BLOCK)).reshape(db, h, ds, topk * MOBA_BLOCK)
    else:
        k_sel = jnp.zeros((db, h, ds, 0, d), k.dtype)
        v_sel = jnp.zeros((db, h, ds, 0, d), v.dtype)
        sel_pos = jnp.zeros((db, h, ds, 0), jnp.int32)
    n_own = n_pages - n_full * ppb
    own_pages = page_table[:, n_full * ppb:]

    def own_rows(pool, new):
        past_rows = pool[own_pages].transpose(0, 2, 1, 3, 4).reshape(db, h, n_own * PAGE_SIZE, d)
        return jnp.concatenate([past_rows.astype(new.dtype), new], axis=2)

    k_own = own_rows(k_pool, kh)
    v_own = own_rows(v_pool, vh)
    own_pos = n_full * MOBA_BLOCK + jnp.arange(n_own * PAGE_SIZE + ds)
    own_mask = own_pos[None, :] <= q_pos[:, None]
    o = moba_attend(qh, q_pos, k_sel, v_sel, sel_pos, jnp.ones((), bool), k_own, v_own, own_pos, own_mask, slopes)
    return o.transpose(0, 2, 1, 3).reshape(db, ds, h * d), kh, vh


def conv_ffn(h, w_up, conv_w, conv_b, w_down, hist):
    up = jnp.einsum('bld,df->blf', h, w_up)
    gate, val = up[..., :FFN_DIM], up[..., FFN_DIM:]
    full = jnp.concatenate([hist.astype(gate.dtype), gate], axis=1)
    act = jax.nn.silu(causal_dwconv(full, conv_w, conv_b)) * val.astype(F32)
    return jnp.einsum('blf,fd->bld', act.astype(h.dtype), w_down), full[:, -(FFN_CONV - 1):]


def trunk(x, attend, ssm_conv0, ssm0, hgrn0, ffn_conv0, p):
    lbs = hgrn_lower_bounds(p['hgrn_lb_gamma'])
    k_rows, v_rows, ssm_c, ssm_s, hgrn_s, ffn_c = [], [], [], [], [], []
    ia = 0
    ic = 0
    for layer in range(DEPTH):
        h = rmsnorm(x, p['norm_mix'][layer])
        bsz, length, _ = h.shape
        if layer % 2 == 0:
            u = jnp.einsum('bld,de->ble', h, p['w_in_a'][ia])
            q = u[..., OFF_Q:OFF_K].reshape(bsz, length, MOBA_HEADS, MOBA_HEAD_DIM)
            k = u[..., OFF_K:OFF_V].reshape(bsz, length, MOBA_HEADS, MOBA_HEAD_DIM)
            v = u[..., OFF_V:].reshape(bsz, length, MOBA_HEADS, MOBA_HEAD_DIM)
            y_ssd, hist, s = ssd_branch(u[..., :OFF_XBC], u[..., OFF_XBC:OFF_DT], u[..., OFF_DT:OFF_Q],
                                        ssm_conv0[ia], p['ssd_conv_w'][ia], p['ssd_conv_b'][ia],
                                        p['ssd_dt_bias'][ia], p['ssd_a_log'][ia], p['ssd_d'][ia],
                                        p['ssd_norm_w'][ia], ssm0[ia])
            o_att, kr, vr = attend(ia, q, k, v)
            merged = jnp.concatenate([y_ssd, o_att.astype(F32)], axis=-1).astype(x.dtype)
            mix = jnp.einsum('ble,ed->bld', merged, p['w_out_a'][ia])
            k_rows.append(kr)
            v_rows.append(vr)
            ssm_c.append(hist)
            ssm_s.append(s)
            ia += 1
        else:
            u = jnp.einsum('bld,de->ble', h, p['w_in_c'][ic])
            o, s = hgrn_branch(u, lbs[layer], p['hgrn_norm_w'][ic], hgrn0[ic])
            mix = jnp.einsum('ble,ed->bld', o.astype(x.dtype), p['w_out_c'][ic])
            hgrn_s.append(s)
            ic += 1
        x = x + mix.astype(x.dtype)
        h = rmsnorm(x, p['norm_ffn'][layer])
        f, fh = conv_ffn(h, p['ffn_w_up'][layer], p['ffn_conv_w'][layer], p['ffn_conv_b'][layer],
                         p['ffn_w_down'][layer], ffn_conv0[layer])
        ffn_c.append(fh)
        x = x + f.astype(x.dtype)
    y = rmsnorm(x, p['norm_final'])
    dt = x.dtype
    return (y, jnp.stack(k_rows).astype(dt), jnp.stack(v_rows).astype(dt), jnp.stack(ssm_s).astype(dt),
            jnp.stack(ssm_c).astype(dt), jnp.stack(hgrn_s).astype(dt), jnp.stack(ffn_c).astype(dt))


def setup_inputs(seed: int = 0) -> dict:
    key = jax.random.key(seed)
    ks = jax.random.split(key, 32)
    n_pages = PAST_LEN // PAGE_SIZE
    n_pool = (DEC_BATCH * n_pages * 5) // 4

    def nrm(k, shape, scale):
        return jax.random.normal(k, shape, F32) * scale

    def gain(k, shape):
        return 1.0 + 0.01 * jax.random.normal(k, shape, F32)

    perm = jax.random.permutation(ks[4], n_pool)[:DEC_BATCH * n_pages]
    dt0 = jnp.exp(jax.random.uniform(ks[17], (N_A_LAYERS, SSD_HEADS), F32, math.log(1e-3), math.log(1e-1)))
    return {
        'x_prompt': nrm(ks[0], (BATCH, SEQ, D_MODEL), 1.0),
        'x_sample': nrm(ks[1], (DEC_BATCH, DEC_SEQ, D_MODEL), 1.0),
        'cache_k_pool': nrm(ks[2], (N_A_LAYERS, n_pool, MOBA_HEADS, PAGE_SIZE, MOBA_HEAD_DIM), 1.0),
        'cache_v_pool': nrm(ks[3], (N_A_LAYERS, n_pool, MOBA_HEADS, PAGE_SIZE, MOBA_HEAD_DIM), 1.0),
        'page_table': perm.reshape(DEC_BATCH, n_pages).astype(jnp.int32),
        'state_ssm': nrm(ks[5], (N_A_LAYERS, DEC_BATCH, SSD_HEADS, SSD_HEAD_DIM, SSD_STATE), 0.3),
        'state_ssm_conv': nrm(ks[6], (N_A_LAYERS, DEC_BATCH, SSD_CONV - 1, SSD_CONV_DIM), 1.0),
        'state_hgrn': nrm(ks[7], (N_C_LAYERS, DEC_BATCH, HGRN_HEADS, HGRN_KEY_DIM, HGRN_VAL_DIM), 0.5),
        'state_ffn_conv': nrm(ks[8], (DEPTH, DEC_BATCH, FFN_CONV - 1, FFN_DIM), 1.0),
        'norm_mix': gain(ks[9], (DEPTH, D_MODEL)),
        'norm_ffn': gain(ks[10], (DEPTH, D_MODEL)),
        'norm_final': gain(ks[11], (D_MODEL,)),
        'w_in_a': nrm(ks[12], (N_A_LAYERS, D_MODEL, IN_A), D_MODEL ** -0.5),
        'w_out_a': nrm(ks[13], (N_A_LAYERS, MIX_A, D_MODEL), MIX_A ** -0.5),
        'ssd_conv_w': nrm(ks[14], (N_A_LAYERS, SSD_CONV, SSD_CONV_DIM), SSD_CONV ** -0.5),
        'ssd_conv_b': nrm(ks[15], (N_A_LAYERS, SSD_CONV_DIM), 0.01),
        'ssd_dt_bias': dt0 + jnp.log(-jnp.expm1(-dt0)),
        'ssd_a_log': jnp.log(jax.random.uniform(ks[18], (N_A_LAYERS, SSD_HEADS), F32, 1.0, 16.0)),
        'ssd_d': gain(ks[19], (N_A_LAYERS, SSD_HEADS)),
        'ssd_norm_w': gain(ks[20], (N_A_LAYERS, SSD_INNER)),
        'w_in_c': nrm(ks[21], (N_C_LAYERS, D_MODEL, IN_C), D_MODEL ** -0.5),
        'w_out_c': nrm(ks[22], (N_C_LAYERS, HGRN_V, D_MODEL), HGRN_V ** -0.5),
        'hgrn_lb_gamma': nrm(ks[23], (DEPTH, HGRN_K), 0.1),
        'hgrn_norm_w': gain(ks[24], (N_C_LAYERS, HGRN_VAL_DIM)),
        'ffn_w_up': nrm(ks[25], (DEPTH, D_MODEL, 2 * FFN_DIM), D_MODEL ** -0.5),
        'ffn_conv_w': nrm(ks[26], (DEPTH, FFN_CONV, FFN_DIM), FFN_CONV ** -0.5),
        'ffn_conv_b': nrm(ks[27], (DEPTH, FFN_DIM), 0.01),
        'ffn_w_down': nrm(ks[28], (DEPTH, FFN_DIM, D_MODEL), FFN_DIM ** -0.5),
    }


def reference(x_prompt, x_sample, cache_k_pool, cache_v_pool, page_table, state_ssm, state_ssm_conv,
              state_hgrn, state_ffn_conv, norm_mix, norm_ffn, norm_final, w_in_a, w_out_a, ssd_conv_w,
              ssd_conv_b, ssd_dt_bias, ssd_a_log, ssd_d, ssd_norm_w, w_in_c, w_out_c, hgrn_lb_gamma,
              hgrn_norm_w, ffn_w_up, ffn_conv_w, ffn_conv_b, ffn_w_down):
    p = dict(norm_mix=norm_mix, norm_ffn=norm_ffn, norm_final=norm_final, w_in_a=w_in_a, w_out_a=w_out_a,
             ssd_conv_w=ssd_conv_w, ssd_conv_b=ssd_conv_b, ssd_dt_bias=ssd_dt_bias, ssd_a_log=ssd_a_log,
             ssd_d=ssd_d, ssd_norm_w=ssd_norm_w, w_in_c=w_in_c, w_out_c=w_out_c, hgrn_lb_gamma=hgrn_lb_gamma,
             hgrn_norm_w=hgrn_norm_w, ffn_w_up=ffn_w_up, ffn_conv_w=ffn_conv_w, ffn_conv_b=ffn_conv_b,
             ffn_w_down=ffn_w_down)
    slopes = alibi_slopes(MOBA_HEADS)
    bp = x_prompt.shape[0]
    dtp = x_prompt.dtype
    (y_prompt, k_rows_prompt, v_rows_prompt, ssm_prompt, ssm_conv_prompt, hgrn_prompt,
     ffn_conv_prompt) = trunk(
        x_prompt, lambda ia, q, k, v: moba_prompt(q, k, v, slopes),
        jnp.zeros((N_A_LAYERS, bp, SSD_CONV - 1, SSD_CONV_DIM), dtp),
        jnp.zeros((N_A_LAYERS, bp, SSD_HEADS, SSD_HEAD_DIM, SSD_STATE), dtp),
        jnp.zeros((N_C_LAYERS, bp, HGRN_HEADS, HGRN_KEY_DIM, HGRN_VAL_DIM), dtp),
        jnp.zeros((DEPTH, bp, FFN_CONV - 1, FFN_DIM), dtp), p)
    (y_sample, k_rows_sample, v_rows_sample, ssm_sample, ssm_conv_sample, hgrn_sample,
     ffn_conv_sample) = trunk(
        x_sample,
        lambda ia, q, k, v: moba_sample(q, k, v, cache_k_pool[ia], cache_v_pool[ia], page_table, slopes),
        state_ssm_conv, state_ssm, state_hgrn, state_ffn_conv, p)
    return (y_prompt, y_sample, k_rows_prompt, v_rows_prompt, ssm_prompt, ssm_conv_prompt, hgrn_prompt,
            ffn_conv_prompt, k_rows_sample, v_rows_sample, ssm_sample, ssm_conv_sample, hgrn_sample,
            ffn_conv_sample)
```

```python
import functools
import math

import numpy as np
import jax
import jax.numpy as jnp
from jax import lax
from jax.experimental import pallas as pl
from jax.experimental.pallas import tpu as pltpu

F32 = jnp.float32
BF16 = jnp.bfloat16
HIGHEST = lax.Precision.HIGHEST

D_MODEL = 1024
DEPTH = 2
PAGE_SIZE = 128
SSD_HEADS = 8
SSD_HEAD_DIM = 64
SSD_INNER = SSD_HEADS * SSD_HEAD_DIM
SSD_STATE = 64
SSD_GROUPS = 2
SSD_CONV = 4
SSD_CHUNK = 128
SSD_GN = SSD_GROUPS * SSD_STATE
SSD_CONV_DIM = SSD_INNER + 2 * SSD_GN
MOBA_HEADS = 8
MOBA_HEAD_DIM = 64
MOBA_INNER = MOBA_HEADS * MOBA_HEAD_DIM
MOBA_BLOCK = 256
MOBA_TOPK = 3
OFF_XBC = SSD_INNER
OFF_DT = OFF_XBC + SSD_CONV_DIM
OFF_Q = OFF_DT + SSD_HEADS
OFF_K = OFF_Q + MOBA_INNER
OFF_V = OFF_K + MOBA_INNER
HGRN_HEADS = 8
HGRN_KEY_DIM = 128
HGRN_VAL_DIM = D_MODEL // HGRN_HEADS
HGRN_K = HGRN_HEADS * HGRN_KEY_DIM
HGRN_V = HGRN_HEADS * HGRN_VAL_DIM
HGRN_CHUNK = 64
FFN_DIM = 2816
FFN_CONV = 3
RMS_EPS = 1e-6

LANES = 128
SUBLANES = 8
VMEM_LIMIT_BYTES = 56 * 1024 * 1024

ROW_TILE = 256
FFN_COL_CHUNK = 256
NEG_BIG = -1e30


def _cparams(sem):
    return pltpu.CompilerParams(dimension_semantics=sem, vmem_limit_bytes=VMEM_LIMIT_BYTES)


def _const_spec(shape):
    nd = len(shape)
    return pl.BlockSpec(shape, lambda *_: (0,) * nd)


def _rms_bf16(x, w):
    y = x * lax.rsqrt(jnp.mean(x * x, axis=-1, keepdims=True) + RMS_EPS)
    return (y * w).astype(BF16)


def _dot(a, b):
    return jnp.dot(a, b, preferred_element_type=F32)


def _dot_nt(a, b):
    return lax.dot_general(a, b, (((1,), (1,)), ((), ())), preferred_element_type=F32)


def _dot_tn(a, b):
    return lax.dot_general(a, b, (((0,), (0,)), ((), ())), preferred_element_type=F32)


def _tril_f32(n):
    r = lax.broadcasted_iota(jnp.int32, (n, n), 0)
    c = lax.broadcasted_iota(jnp.int32, (n, n), 1)
    return r >= c


def _cumsum_rows(x, q):
    tri = _tril_f32(q).astype(F32)
    return jnp.dot(tri, x, precision=HIGHEST, preferred_element_type=F32)


def _silu(x):
    return x * jax.nn.sigmoid(x)


def _inproj_a_kernel(x_ref, nw_ref, wz, wxbc, wdt, wq, wk, wv, *outs, prompt):
    hb = _rms_bf16(x_ref[...], nw_ref[...])
    z_o, xbc_o, dt_o, q_o = outs[:4]
    z_o[...] = _dot(hb, wz[...])
    xbc_o[...] = _dot(hb, wxbc[...])
    dt_o[...] = _dot(hb, wdt[...])
    q_o[...] = _dot(hb, wq[...])
    k = _dot(hb, wk[...])
    v = _dot(hb, wv[...])
    if not prompt:
        k_o, v_o = outs[4:]
        k_o[...] = k
        v_o[...] = v
        return
    kr_o, vr_o, kb_o, vt_o, km_o = outs[4:]
    tm = k.shape[0]
    for p in range(tm // PAGE_SIZE):
        rows = slice(p * PAGE_SIZE, (p + 1) * PAGE_SIZE)
        for h in range(MOBA_HEADS):
            cols = slice(h * MOBA_HEAD_DIM, (h + 1) * MOBA_HEAD_DIM)
            kr_o[p, h] = k[rows, cols]
            vr_o[p, h] = v[rows, cols]
    kb_o[0] = k.astype(BF16)
    vt_o[0] = v.T.astype(BF16)
    km_o[0] = jnp.mean(k, axis=0, keepdims=True)


def _inproj_a(x, nw, ws, prompt):
    m = x.shape[0]
    tm = ROW_TILE if prompt else m
    assert m % tm == 0
    if prompt:
        assert tm == MOBA_BLOCK
    nt = m // tm
    widths = [w.shape[1] for w in ws]
    row = lambda i: (i, 0)
    in_specs = [pl.BlockSpec((tm, D_MODEL), row), _const_spec((1, D_MODEL))]
    in_specs += [_const_spec(w.shape) for w in ws]
    out_shape = [jax.ShapeDtypeStruct((m, n), F32) for n in widths[:4]]
    out_specs = [pl.BlockSpec((tm, n), row) for n in widths[:4]]
    if prompt:
        npg = m // PAGE_SIZE
        ppt = tm // PAGE_SIZE
        rows_shape = (npg, MOBA_HEADS, PAGE_SIZE, MOBA_HEAD_DIM)
        rows_spec = pl.BlockSpec((ppt, MOBA_HEADS, PAGE_SIZE, MOBA_HEAD_DIM), lambda i: (i, 0, 0, 0))
        out_shape += [jax.ShapeDtypeStruct(rows_shape, F32)] * 2
        out_specs += [rows_spec, rows_spec]
        out_shape += [jax.ShapeDtypeStruct((nt, tm, MOBA_INNER), BF16),
                      jax.ShapeDtypeStruct((nt, MOBA_INNER, tm), BF16),
                      jax.ShapeDtypeStruct((nt, 1, MOBA_INNER), F32)]
        out_specs += [pl.BlockSpec((1, tm, MOBA_INNER), lambda i: (i, 0, 0)),
                      pl.BlockSpec((1, MOBA_INNER, tm), lambda i: (i, 0, 0)),
                      pl.BlockSpec((1, 1, MOBA_INNER), lambda i: (i, 0, 0))]
    else:
        out_shape += [jax.ShapeDtypeStruct((m, MOBA_INNER), F32)] * 2
        out_specs += [pl.BlockSpec((tm, MOBA_INNER), row)] * 2
    return pl.pallas_call(
        functools.partial(_inproj_a_kernel, prompt=prompt),
        grid=(nt,), in_specs=in_specs, out_specs=out_specs, out_shape=out_shape,
        compiler_params=_cparams(("arbitrary",)), name="inproj_a",
    )(x, nw, *ws)


def _norm_matmul_kernel(x_ref, nw_ref, *refs):
    n = len(refs) // 2
    hb = _rms_bf16(x_ref[...], nw_ref[...])
    for w, o in zip(refs[:n], refs[n:]):
        o[...] = _dot(hb, w[...])


def _norm_matmul(x, nw, ws):
    m = x.shape[0]
    tm = min(ROW_TILE, m)
    assert m % tm == 0
    row = lambda i: (i, 0)
    in_specs = [pl.BlockSpec((tm, D_MODEL), row), _const_spec((1, D_MODEL))]
    in_specs += [_const_spec(w.shape) for w in ws]
    return pl.pallas_call(
        _norm_matmul_kernel, grid=(m // tm,), in_specs=in_specs,
        out_specs=[pl.BlockSpec((tm, w.shape[1]), row) for w in ws],
        out_shape=[jax.ShapeDtypeStruct((m, w.shape[1]), F32) for w in ws],
        compiler_params=_cparams(("arbitrary",)), name="norm_matmul",
    )(x, nw, *ws)


def _outproj_kernel(*refs):
    n = (len(refs) - 2) // 2
    x_ref, o_ref = refs[2 * n], refs[2 * n + 1]
    acc = x_ref[...]
    for a, w in zip(refs[:n], refs[n:2 * n]):
        acc = acc + _dot(a[...], w[...])
    o_ref[...] = acc


def _outproj(acts, ws, x):
    m = x.shape[0]
    tm = min(ROW_TILE, m)
    assert m % tm == 0
    row = lambda i: (i, 0)
    in_specs = [pl.BlockSpec((tm, a.shape[1]), row) for a in acts]
    in_specs += [_const_spec(w.shape) for w in ws]
    in_specs += [pl.BlockSpec((tm, D_MODEL), row)]
    return pl.pallas_call(
        _outproj_kernel, grid=(m // tm,), in_specs=in_specs,
        out_specs=pl.BlockSpec((tm, D_MODEL), row),
        out_shape=jax.ShapeDtypeStruct((m, D_MODEL), F32),
        compiler_params=_cparams(("arbitrary",)), name="outproj",
    )(*acts, *ws, x)


def _ffn_kernel(*refs, tm, seq_len, final_norm):
    it = iter(refs)
    x_ref, nw_ref, wup_ref, cw_ref, cb_ref, wdn_ref = (next(it) for _ in range(6))
    if seq_len is None:
        hist_ref = next(it)
    else:
        h1_ref, h2_ref = next(it), next(it)
    nf_ref = next(it) if final_norm else None
    o_ref = next(it)
    y_ref = next(it) if final_norm else None
    tail_ref = next(it)
    up_s, act_s = next(it), next(it)

    i = pl.program_id(0)
    f = FFN_DIM
    if seq_len is None:
        @pl.when(i == 0)
        def _():
            up_s[0:SUBLANES, 0:f] = hist_ref[...]
    else:
        up_s[0:SUBLANES, 0:f] = jnp.zeros((SUBLANES, f), F32)

    x = x_ref[...]
    hb = _rms_bf16(x, nw_ref[...])
    up_s[SUBLANES:SUBLANES + tm, :] = _dot(hb, wup_ref[...])

    if seq_len is not None:
        t = lax.broadcasted_iota(jnp.int32, (tm, 1), 0) % seq_len
    for c in range(0, f, FFN_COL_CHUNK):
        cols = slice(c, c + FFN_COL_CHUNK)
        g0 = up_s[SUBLANES:SUBLANES + tm, cols]
        g1 = up_s[SUBLANES - 1:SUBLANES - 1 + tm, cols]
        g2 = up_s[SUBLANES - 2:SUBLANES - 2 + tm, cols]
        if seq_len is not None:
            g1 = jnp.where(t >= 1, g1, h1_ref[:, cols])
            g2 = jnp.where(t >= 2, g2, h2_ref[:, cols])
        val = up_s[SUBLANES:SUBLANES + tm, f + c:f + c + FFN_COL_CHUNK]
        conv = g2 * cw_ref[0:1, cols]
        conv = conv + g1 * cw_ref[1:2, cols]
        conv = conv + g0 * cw_ref[2:3, cols]
        conv = conv + cb_ref[:, cols]
        act_s[:, cols] = (_silu(conv) * val).astype(BF16)

    out = x + _dot(act_s[...], wdn_ref[...])
    o_ref[...] = out
    if final_norm:
        y = out * lax.rsqrt(jnp.mean(out * out, axis=-1, keepdims=True) + RMS_EPS)
        y_ref[...] = y * nf_ref[...]

    if seq_len is None:
        last = up_s[tm:tm + SUBLANES, 0:f]
        up_s[0:SUBLANES, 0:f] = last

        @pl.when(i == pl.num_programs(0) - 1)
        def _():
            tail_ref[...] = last
    else:
        tail_ref[...] = up_s[SUBLANES:SUBLANES + tm, 0:f]


def _ffn(x, nw, wup, cw, cb, wdn, hist, seq_len, nf):
    assert FFN_CONV == 3
    m = x.shape[0]
    f = FFN_DIM
    final_norm = nf is not None
    row = lambda i: (i, 0)
    if seq_len is None:
        tm = ROW_TILE
        hist8 = jnp.zeros((SUBLANES, f), F32).at[SUBLANES - 2:].set(hist)
        extra = [hist8]
        extra_specs = [_const_spec((SUBLANES, f))]
        tail_shape = (SUBLANES, f)
    else:
        tm = m
        nb = m // seq_len
        assert seq_len >= 2 and nb * seq_len == m
        h1 = jnp.zeros((nb, seq_len, f), F32).at[:, 0].set(hist[:, 1]).reshape(m, f)
        h2 = jnp.zeros((nb, seq_len, f), F32).at[:, 0:2].set(hist).reshape(m, f)
        extra = [h1, h2]
        extra_specs = [_const_spec((m, f))] * 2
        tail_shape = (m, f)
    assert m % tm == 0
    ins = [x, nw, wup, cw, cb, wdn] + extra
    in_specs = [pl.BlockSpec((tm, D_MODEL), row), _const_spec((1, D_MODEL)), _const_spec(wup.shape),
                _const_spec(cw.shape), _const_spec(cb.shape), _const_spec(wdn.shape)] + extra_specs
    out_shape = [jax.ShapeDtypeStruct((m, D_MODEL), F32)]
    out_specs = [pl.BlockSpec((tm, D_MODEL), row)]
    if final_norm:
        ins.append(nf)
        in_specs.append(_const_spec((1, D_MODEL)))
        out_shape.append(jax.ShapeDtypeStruct((m, D_MODEL), F32))
        out_specs.append(pl.BlockSpec((tm, D_MODEL), row))
    out_shape.append(jax.ShapeDtypeStruct(tail_shape, F32))
    out_specs.append(_const_spec(tail_shape))
    res = pl.pallas_call(
        functools.partial(_ffn_kernel, tm=tm, seq_len=seq_len, final_norm=final_norm),
        grid=(m // tm,), in_specs=in_specs, out_specs=out_specs, out_shape=out_shape,
        scratch_shapes=[pltpu.VMEM((SUBLANES + tm, 2 * f), F32), pltpu.VMEM((tm, f), BF16)],
        compiler_params=_cparams(("arbitrary",)), name="conv_ffn",
    )(*ins)
    if final_norm:
        xo, y, tail = res
    else:
        (xo, tail), y = res, None
    if seq_len is None:
        new_hist = tail[SUBLANES - 2:]
    else:
        new_hist = tail.reshape(m // seq_len, seq_len, f)[:, seq_len - 2:]
    return xo, y, new_hist


def _ssd_kernel(z_ref, xbc_ref, dt_ref, hist_ref, s0_ref, cw_ref, cb_ref, dtb_ref, alog_ref, dsk_ref,
                nw_ref, y_ref, hist_o, s_o, cbuf, st, ybuf, *, q, seq_len, n_chunks):
    c = pl.program_id(1)
    hrows = SSD_CONV - 1
    h0 = SUBLANES - hrows

    @pl.when(c == 0)
    def _():
        cbuf[0:SUBLANES, :] = hist_ref[0]
        st[...] = s0_ref[0]

    cbuf[SUBLANES:SUBLANES + q, :] = xbc_ref[0]
    conv = cbuf[h0:h0 + q, :] * cw_ref[0:1, :]
    for j in range(1, SSD_CONV):
        conv = conv + cbuf[h0 + j:h0 + j + q, :] * cw_ref[j:j + 1, :]
    xc = _silu(conv + cb_ref[...])

    xdt_raw = dt_ref[0] + dtb_ref[...]
    dt = jnp.maximum(xdt_raw, 0.0) + jnp.log1p(jnp.exp(-jnp.abs(xdt_raw)))
    if seq_len % q != 0:
        row = c * q + lax.broadcasted_iota(jnp.int32, (q, 1), 0)
        dt = jnp.where(row < seq_len, dt, 0.0)
    a = -jnp.exp(alog_ref[...])
    acum = _cumsum_rows(dt * a, q)
    acum_t = acum.T
    a_last = acum[q - 1:q, :]
    e_acum = jnp.exp(acum)
    w_in = jnp.exp(a_last - acum) * dt
    e_last = jnp.exp(a_last)
    tril = _tril_f32(q)

    rep = SSD_HEADS // SSD_GROUPS
    cbs, bms, cms = [], [], []
    for g in range(SSD_GROUPS):
        bm = xc[:, SSD_INNER + g * SSD_STATE:SSD_INNER + (g + 1) * SSD_STATE]
        cm = xc[:, SSD_INNER + SSD_GN + g * SSD_STATE:SSD_INNER + SSD_GN + (g + 1) * SSD_STATE]
        bms.append(bm)
        cms.append(cm)
        cbs.append(_dot_nt(cm.astype(BF16), bm.astype(BF16)))

    for h in range(SSD_HEADS):
        g = h // rep
        xs = xc[:, h * SSD_HEAD_DIM:(h + 1) * SSD_HEAD_DIM]
        seg = acum[:, h:h + 1] - acum_t[h:h + 1, :]
        decay = jnp.exp(jnp.where(tril, seg, -jnp.inf))
        y = _dot((cbs[g] * decay).astype(BF16), (xs * dt[:, h:h + 1]).astype(BF16))
        s_h = st[h]
        y = y + _dot_nt((cms[g] * e_acum[:, h:h + 1]).astype(BF16), s_h.astype(BF16))
        st[h] = s_h * e_last[:, h:h + 1] + _dot_tn((xs * w_in[:, h:h + 1]).astype(BF16),
                                                    bms[g].astype(BF16))
        ybuf[:, h * SSD_HEAD_DIM:(h + 1) * SSD_HEAD_DIM] = y + xs * dsk_ref[:, h:h + 1]

    yz = ybuf[...] * _silu(z_ref[0])
    gw = SSD_INNER // SSD_GROUPS
    for g in range(SSD_GROUPS):
        yg = yz[:, g * gw:(g + 1) * gw]
        yn = yg * lax.rsqrt(jnp.mean(yg * yg, axis=-1, keepdims=True) + RMS_EPS)
        y_ref[0, :, g * gw:(g + 1) * gw] = (yn * nw_ref[:, g * gw:(g + 1) * gw]).astype(BF16)

    @pl.when(c == n_chunks - 1)
    def _():
        l_last = seq_len - (n_chunks - 1) * q
        hist_o[0] = cbuf[l_last:l_last + SUBLANES, :]
        s_o[0] = st[...]

    cbuf[0:SUBLANES, :] = cbuf[q:q + SUBLANES, :]


def _pad_lanes(v, fill=0.0):
    return jnp.full((1, LANES), fill, F32).at[0, :v.shape[0]].set(v.astype(F32))


def _ssd(z, xbc, dtr, hist, s0, conv_w, conv_b, dt_bias, a_log, d_skip, norm_w, seq_len):
    b, lp, _ = z.shape
    q = SSD_CHUNK
    assert lp % q == 0 and lp - seq_len < q
    hrows = SSD_CONV - 1
    hist8 = jnp.zeros((b, SUBLANES, SSD_CONV_DIM), F32).at[:, SUBLANES - hrows:].set(hist)
    cw8 = jnp.zeros((SUBLANES, SSD_CONV_DIM), F32).at[:SSD_CONV].set(conv_w)
    seq = lambda i, c: (i, c, 0)
    per_b3 = lambda i, c: (i, 0, 0)
    per_b4 = lambda i, c: (i, 0, 0, 0)
    in_specs = [pl.BlockSpec((1, q, SSD_INNER), seq), pl.BlockSpec((1, q, SSD_CONV_DIM), seq),
                pl.BlockSpec((1, q, LANES), seq), pl.BlockSpec((1, SUBLANES, SSD_CONV_DIM), per_b3),
                pl.BlockSpec((1, SSD_HEADS, SSD_HEAD_DIM, SSD_STATE), per_b4),
                _const_spec((SUBLANES, SSD_CONV_DIM)), _const_spec((1, SSD_CONV_DIM)),
                _const_spec((1, LANES)), _const_spec((1, LANES)), _const_spec((1, LANES)),
                _const_spec((1, SSD_INNER))]
    out_shape = [jax.ShapeDtypeStruct((b, lp, SSD_INNER), BF16),
                 jax.ShapeDtypeStruct((b, SUBLANES, SSD_CONV_DIM), F32),
                 jax.ShapeDtypeStruct((b, SSD_HEADS, SSD_HEAD_DIM, SSD_STATE), F32)]
    out_specs = [pl.BlockSpec((1, q, SSD_INNER), seq), pl.BlockSpec((1, SUBLANES, SSD_CONV_DIM), per_b3),
                 pl.BlockSpec((1, SSD_HEADS, SSD_HEAD_DIM, SSD_STATE), per_b4)]
    y, hist_o, s_o = pl.pallas_call(
        functools.partial(_ssd_kernel, q=q, seq_len=seq_len, n_chunks=lp // q),
        grid=(b, lp // q), in_specs=in_specs, out_specs=out_specs, out_shape=out_shape,
        scratch_shapes=[pltpu.VMEM((SUBLANES + q, SSD_CONV_DIM), F32),
                        pltpu.VMEM((SSD_HEADS, SSD_HEAD_DIM, SSD_STATE), F32),
                        pltpu.VMEM((q, SSD_INNER), F32)],
        compiler_params=_cparams(("arbitrary", "arbitrary")), name="ssd",
    )(z, xbc, dtr, hist8, s0, cw8, conv_b.reshape(1, -1), _pad_lanes(dt_bias), _pad_lanes(a_log),
      _pad_lanes(d_skip), norm_w.reshape(1, -1))
    return y, hist_o[:, SUBLANES - hrows:], s_o


def _gla_kernel(q_ref, fx_ref, iv_ref, g_ref, gam_ref, nw_ref, s0_ref, o_ref, s_o, st_t, *, q, seq_len,
                layer):
    c = pl.program_id(1)
    n_chunks = pl.num_programs(1)

    @pl.when(c == 0)
    def _():
        for h in range(HGRN_HEADS):
            st_t[h] = s0_ref[0, h].T

    rows = [gam_ref[l:l + 1, :] for l in range(DEPTH)]
    mx = functools.reduce(jnp.maximum, rows)
    es = [jnp.exp(r - mx) for r in rows]
    lb = sum(es[1:layer + 1]) / sum(es) if layer >= 1 else jnp.zeros_like(mx)

    f = lb + (1.0 - lb) * jax.nn.sigmoid(fx_ref[0])
    if seq_len % q != 0:
        row = c * q + lax.broadcasted_iota(jnp.int32, (q, 1), 0)
        f = jnp.where(row < seq_len, f, 1.0)
    kk = 1.0 - f
    b = _cumsum_rows(jnp.log(f), q)
    b_last = b[q - 1:q, :]
    qd = q_ref[0] * jnp.exp(b)
    kd = kk * jnp.exp(-b)
    kdec = kk * jnp.exp(b_last - b)
    e_last = jnp.exp(b_last)
    tril = _tril_f32(q)
    iv = iv_ref[0]

    for h in range(HGRN_HEADS):
        ks = slice(h * HGRN_KEY_DIM, (h + 1) * HGRN_KEY_DIM)
        vs = slice(h * HGRN_VAL_DIM, (h + 1) * HGRN_VAL_DIM)
        qd_h = qd[:, ks].astype(BF16)
        v_h = iv[:, vs].astype(BF16)
        att = jnp.where(tril, _dot_nt(qd_h, kd[:, ks].astype(BF16)), 0.0)
        s_t = st_t[h]
        o = _dot(att.astype(BF16), v_h) + _dot_nt(qd_h, s_t.astype(BF16))
        st_t[h] = s_t * e_last[:, ks] + _dot_tn(v_h, kdec[:, ks].astype(BF16))
        on = o * lax.rsqrt(jnp.mean(o * o, axis=-1, keepdims=True) + RMS_EPS) * nw_ref[...]
        o_ref[0, :, vs] = (on * _silu(g_ref[0, :, vs])).astype(BF16)

    @pl.when(c == n_chunks - 1)
    def _():
        for h in range(HGRN_HEADS):
            s_o[0, h] = st_t[h].T


def _gla(qa, fx, iv, g, gamma, norm_w, s0, seq_len, layer):
    b, lp, _ = qa.shape
    q = HGRN_CHUNK
    assert lp % q == 0 and lp - seq_len < q
    gam8 = jnp.zeros((SUBLANES, HGRN_K), F32).at[:DEPTH].set(gamma)
    seq = lambda i, c: (i, c, 0)
    per_b4 = lambda i, c: (i, 0, 0, 0)
    st_shape = (HGRN_HEADS, HGRN_KEY_DIM, HGRN_VAL_DIM)
    in_specs = [pl.BlockSpec((1, q, HGRN_K), seq), pl.BlockSpec((1, q, HGRN_K), seq),
                pl.BlockSpec((1, q, HGRN_V), seq), pl.BlockSpec((1, q, HGRN_V), seq),
                _const_spec((SUBLANES, HGRN_K)), _const_spec((1, HGRN_VAL_DIM)),
                pl.BlockSpec((1,) + st_shape, per_b4)]
    o, s_o = pl.pallas_call(
        functools.partial(_gla_kernel, q=q, seq_len=seq_len, layer=layer),
        grid=(b, lp // q), in_specs=in_specs,
        out_specs=[pl.BlockSpec((1, q, HGRN_V), seq), pl.BlockSpec((1,) + st_shape, per_b4)],
        out_shape=[jax.ShapeDtypeStruct((b, lp, HGRN_V), BF16), jax.ShapeDtypeStruct((b,) + st_shape, F32)],
        scratch_shapes=[pltpu.VMEM((HGRN_HEADS, HGRN_VAL_DIM, HGRN_KEY_DIM), F32)],
        compiler_params=_cparams(("arbitrary", "arbitrary")), name="gla",
    )(qa, fx, iv, g, gam8, norm_w.reshape(1, -1), s0)
    return o, s_o


def _select_topk_rows(gate_t, n_valid, nblk, width):
    blk = lax.broadcasted_iota(jnp.int32, (nblk, width), 0)
    blk_f = blk.astype(F32)
    g = jnp.where(blk < n_valid, gate_t, -jnp.inf)
    sel = jnp.zeros((nblk, width), F32)
    for _ in range(MOBA_TOPK):
        mx = jnp.max(g, axis=0, keepdims=True)
        first = jnp.min(jnp.where(g == mx, blk_f, float(nblk)), axis=0, keepdims=True)
        pick = (blk_f == first) & (mx > -jnp.inf)
        sel = jnp.where(pick, 1.0, sel)
        g = jnp.where(pick, -jnp.inf, g)
    return jnp.where(sel > 0.0, 0.0, -jnp.inf)


def _moba_prompt_kernel(slopes_ref, q_ref, kb_ref, vt_ref, km_ref, o_ref, sel_s, *, nblk):
    hp = pl.program_id(0)
    i = pl.program_id(1)
    tq = MOBA_BLOCK
    d = MOBA_HEAD_DIM
    q = q_ref[...]
    lane = lax.broadcasted_iota(jnp.int32, (tq, 2 * d), 1)
    r_io = lax.broadcasted_iota(jnp.int32, (MOBA_BLOCK, tq), 0)
    c_io = lax.broadcasted_iota(jnp.int32, (MOBA_BLOCK, tq), 1)
    dmat = (r_io - c_io).astype(F32)
    causal = r_io <= c_io

    def update(carry, s_t, v_t):
        m_run, l_run, o_t = carry
        m_new = jnp.maximum(m_run, jnp.max(s_t, axis=0, keepdims=True))
        alpha = jnp.exp(m_run - m_new)
        p = jnp.exp(s_t - m_new)
        l_new = alpha * l_run + jnp.sum(p, axis=0, keepdims=True)
        o_new = alpha * o_t + _dot(v_t, p.astype(BF16))
        return m_new, l_new, o_new

    outs = []
    for j in range(2):
        slope = slopes_ref[2 * hp + j]
        qm_t = jnp.where((lane >= j * d) & (lane < (j + 1) * d), q, 0.0).T
        gate_t = jnp.dot(km_ref[...], qm_t, precision=HIGHEST, preferred_element_type=F32)
        sel_s[j] = _select_topk_rows(gate_t, i, nblk, tq)
        qs_t = (qm_t * (d ** -0.5)).astype(BF16)
        dm = dmat * slope

        def body(n, carry):
            s_t = _dot(kb_ref[n], qs_t)
            brow = sel_s[j, pl.ds(n, 1), :] + (slope * MOBA_BLOCK) * (n - i).astype(F32)
            return update(carry, s_t + dm + brow, vt_ref[n, j * d:(j + 1) * d, :])

        init = (jnp.full((1, tq), NEG_BIG, F32), jnp.zeros((1, tq), F32), jnp.zeros((d, tq), F32))
        carry = lax.fori_loop(0, i, body, init)
        s_own = jnp.where(causal, _dot(kb_ref[i], qs_t) + dm, -jnp.inf)
        _, l_fin, o_t = update(carry, s_own, vt_ref[i, j * d:(j + 1) * d, :])
        outs.append(o_t / l_fin)
    o_ref[...] = jnp.concatenate(outs, axis=0).T.astype(BF16)


def _alibi_slopes():
    return jnp.asarray(np.exp2(-8.0 * np.arange(1, MOBA_HEADS + 1) / MOBA_HEADS), dtype=F32)


def _moba_prompt(q, kb, vt, km):
    s = q.shape[0]
    nblk = s // MOBA_BLOCK
    assert nblk * MOBA_BLOCK == s and 2 * MOBA_HEAD_DIM == LANES
    pairs = MOBA_HEADS // 2
    return pl.pallas_call(
        functools.partial(_moba_prompt_kernel, nblk=nblk),
        grid=(pairs, nblk),
        in_specs=[pl.BlockSpec(memory_space=pltpu.SMEM),
                  pl.BlockSpec((MOBA_BLOCK, LANES), lambda hp, i: (i, hp)),
                  pl.BlockSpec((nblk, MOBA_BLOCK, LANES), lambda hp, i: (0, 0, hp)),
                  pl.BlockSpec((nblk, LANES, MOBA_BLOCK), lambda hp, i: (0, hp, 0)),
                  pl.BlockSpec((nblk, LANES), lambda hp, i: (0, hp))],
        out_specs=pl.BlockSpec((MOBA_BLOCK, LANES), lambda hp, i: (i, hp)),
        out_shape=jax.ShapeDtypeStruct((s, MOBA_INNER), BF16),
        scratch_shapes=[pltpu.VMEM((2, nblk, MOBA_BLOCK), F32)],
        compiler_params=_cparams(("arbitrary", "arbitrary")), name="moba_prompt",
    )(_alibi_slopes(), q, kb, vt, km)


PAGES_PER_STEP = 16


def _kmean_kernel(pt_ref, *refs):
    del pt_ref
    o_ref = refs[-1]
    ppb = MOBA_BLOCK // PAGE_SIZE
    for m in range(PAGES_PER_STEP // ppb):
        acc = jnp.sum(refs[m * ppb][0], axis=1)
        for p in range(1, ppb):
            acc = acc + jnp.sum(refs[m * ppb + p][0], axis=1)
        o_ref[0, m] = acc * (1.0 / MOBA_BLOCK)


def _sample_kmean(k_pool, page_table):
    db, n_pages = page_table.shape
    ppb = MOBA_BLOCK // PAGE_SIZE
    n_full = n_pages // ppb
    assert n_full * ppb == n_pages and n_pages % PAGES_PER_STEP == 0
    steps = n_pages // PAGES_PER_STEP
    bps = PAGES_PER_STEP // ppb
    blk = (1, MOBA_HEADS, PAGE_SIZE, MOBA_HEAD_DIM)

    def page_spec(p):
        return pl.BlockSpec(blk, lambda b, s, pt: (pt[b * n_pages + s * PAGES_PER_STEP + p], 0, 0, 0))

    grid_spec = pltpu.PrefetchScalarGridSpec(
        num_scalar_prefetch=1, grid=(db, steps),
        in_specs=[page_spec(p) for p in range(PAGES_PER_STEP)],
        out_specs=pl.BlockSpec((1, bps, MOBA_HEADS, MOBA_HEAD_DIM), lambda b, s, pt: (b, s, 0, 0)))
    return pl.pallas_call(
        _kmean_kernel, grid_spec=grid_spec,
        out_shape=jax.ShapeDtypeStruct((db, n_full, MOBA_HEADS, MOBA_HEAD_DIM), F32),
        compiler_params=_cparams(("arbitrary", "arbitrary")), name="sample_kmean",
    )(page_table.reshape(-1), *([k_pool] * PAGES_PER_STEP))


def _sample_select_kernel(q_ref, km_ref, idx_ref, *, n_full):
    t8 = q_ref.shape[2]
    blk = lax.broadcasted_iota(jnp.int32, (t8, n_full), 1).astype(F32)
    lane = lax.broadcasted_iota(jnp.int32, (t8, LANES), 1)
    for h in range(MOBA_HEADS):
        g = lax.dot_general(q_ref[0, h], km_ref[0, h], (((1,), (1,)), ((), ())), precision=HIGHEST,
                            preferred_element_type=F32)
        out = jnp.zeros((t8, LANES), jnp.int32)
        for k in range(MOBA_TOPK):
            mx = jnp.max(g, axis=-1, keepdims=True)
            first = jnp.min(jnp.where(g == mx, blk, float(n_full)), axis=-1, keepdims=True)
            out = jnp.where(lane == k, first.astype(jnp.int32), out)
            g = jnp.where(blk == first, -jnp.inf, g)
        idx_ref[0, h] = out


def _sample_select(qh, km):
    db, h, t8, d = qh.shape
    n_full = km.shape[2]
    assert n_full >= MOBA_TOPK
    b4 = lambda b: (b, 0, 0, 0)
    return pl.pallas_call(
        functools.partial(_sample_select_kernel, n_full=n_full), grid=(db,),
        in_specs=[pl.BlockSpec((1, h, t8, d), b4), pl.BlockSpec((1, h, n_full, d), b4)],
        out_specs=pl.BlockSpec((1, h, t8, LANES), b4),
        out_shape=jax.ShapeDtypeStruct((db, h, t8, LANES), jnp.int32),
        compiler_params=_cparams(("arbitrary",)), name="sample_select",
    )(qh, km)


def _sample_attn_kernel(idx_ref, phys_ref, slopes_ref, q_ref, kn_ref, vn_ref, *refs, ds, past):
    del phys_ref
    ppb = MOBA_BLOCK // PAGE_SIZE
    n_sel = MOBA_TOPK * ppb
    k_refs = refs[:ds * n_sel]
    v_refs = refs[ds * n_sel:2 * ds * n_sel]
    o_ref = refs[2 * ds * n_sel]
    b = pl.program_id(0)
    h = pl.program_id(1)
    slope = slopes_ref[h]
    t8 = q_ref.shape[2]
    r_col = lax.broadcasted_iota(jnp.int32, (PAGE_SIZE, 1), 0).astype(F32)
    r_new = lax.broadcasted_iota(jnp.int32, (t8, 1), 0)
    qf = q_ref[0, 0] * (MOBA_HEAD_DIM ** -0.5)
    k_new = kn_ref[0, 0]
    v_new = vn_ref[0, 0]
    o_ref[0, 0] = jnp.zeros((t8, MOBA_HEAD_DIM), F32)
    for t in range(ds):
        q_t = qf[t:t + 1, :]
        t_pos = float(past + t)
        scores = []
        for sp in range(n_sel):
            slot, p = divmod(sp, ppb)
            blk_idx = idx_ref[((b * MOBA_HEADS + h) * ds + t) * MOBA_TOPK + slot]
            pos0 = (blk_idx * MOBA_BLOCK + p * PAGE_SIZE).astype(F32)
            s = jnp.sum(k_refs[t * n_sel + sp][0, 0] * q_t, axis=-1, keepdims=True)
            scores.append(s - slope * (t_pos - (pos0 + r_col)))
        s_new = jnp.sum(k_new * q_t, axis=-1, keepdims=True)
        s_new = s_new - slope * (t_pos - (float(past) + r_new.astype(F32)))
        s_new = jnp.where(r_new <= t, s_new, -jnp.inf)
        mx = jnp.max(s_new, axis=0, keepdims=True)
        for s in scores:
            mx = jnp.maximum(mx, jnp.max(s, axis=0, keepdims=True))
        p_new = jnp.exp(s_new - mx)
        l = jnp.sum(p_new, axis=0, keepdims=True)
        o = jnp.sum(p_new * v_new, axis=0, keepdims=True)
        for sp in range(n_sel):
            pr = jnp.exp(scores[sp] - mx)
            l = l + jnp.sum(pr, axis=0, keepdims=True)
            o = o + jnp.sum(pr * v_refs[t * n_sel + sp][0, 0], axis=0, keepdims=True)
        o_ref[0, 0, t:t + 1, :] = o / l


def _sample_attn(qh, kh, vh, idx, k_pool, v_pool, page_table, ds):
    db, h, t8, d = qh.shape
    n_pages = page_table.shape[1]
    ppb = MOBA_BLOCK // PAGE_SIZE
    assert n_pages % ppb == 0
    past = n_pages * PAGE_SIZE
    n_sel = MOBA_TOPK * ppb
    logical = idx[..., None] * ppb + jnp.arange(ppb, dtype=jnp.int32)
    phys = page_table[jnp.arange(db)[:, None, None, None], logical.reshape(db, h, ds, n_sel)]
    blk = (1, 1, PAGE_SIZE, d)

    def page_spec(t, sp):
        return pl.BlockSpec(
            blk, lambda b, hh, idx_r, phys_r: (phys_r[((b * h + hh) * ds + t) * n_sel + sp], hh, 0, 0))

    tok = pl.BlockSpec((1, 1, t8, d), lambda b, hh, *_: (b, hh, 0, 0))
    page_specs = [page_spec(t, sp) for t in range(ds) for sp in range(n_sel)]
    grid_spec = pltpu.PrefetchScalarGridSpec(
        num_scalar_prefetch=2, grid=(db, h),
        in_specs=[pl.BlockSpec(memory_space=pltpu.SMEM), tok, tok, tok] + page_specs + page_specs,
        out_specs=tok)
    return pl.pallas_call(
        functools.partial(_sample_attn_kernel, ds=ds, past=past), grid_spec=grid_spec,
        out_shape=jax.ShapeDtypeStruct((db, h, t8, d), F32),
        compiler_params=_cparams(("arbitrary", "arbitrary")), name="sample_attn",
    )(idx.reshape(-1), phys.reshape(-1), _alibi_slopes(), qh, kh, vh,
      *([k_pool] * (ds * n_sel)), *([v_pool] * (ds * n_sel)))


def _split_w_in_a(w):
    wb = w.astype(BF16)
    wdt = jnp.zeros((D_MODEL, LANES), BF16).at[:, :SSD_HEADS].set(wb[:, OFF_DT:OFF_Q])
    return [wb[:, :OFF_XBC], wb[:, OFF_XBC:OFF_DT], wdt, wb[:, OFF_Q:OFF_K], wb[:, OFF_K:OFF_V], wb[:, OFF_V:]]


def _pad_seq(a, b, l, lp):
    a = a.reshape(b, l, a.shape[-1])
    return a if lp == l else jnp.pad(a, ((0, 0), (0, lp - l), (0, 0)))


def _round_up(n, m):
    return -(-n // m) * m


def _heads_t8(a, b, l):
    a = a.reshape(b, l, MOBA_HEADS, MOBA_HEAD_DIM).transpose(0, 2, 1, 3)
    return jnp.pad(a, ((0, 0), (0, 0), (0, _round_up(l, SUBLANES) - l), (0, 0)))


def _trunk(x, bsz, length, prompt, caches, p):
    ssm_conv0, ssm0, hgrn0, ffn_conv0 = caches[:4]
    m = bsz * length
    k_rows = v_rows = ssm_c = ssm_s = hgrn_s = None
    ffn_c = []
    y = None
    for layer in range(DEPTH):
        nw = p['norm_mix'][layer].reshape(1, -1)
        if layer % 2 == 0:
            ia = layer // 2
            assert ia == 0
            ws = _split_w_in_a(p['w_in_a'][ia])
            outs = _inproj_a(x, nw, ws, prompt)
            z, xbc, dtr, q = outs[:4]
            lp = _round_up(length, SSD_CHUNK)
            y_ssd, hist, s_new = _ssd(
                _pad_seq(z, bsz, length, lp), _pad_seq(xbc, bsz, length, lp), _pad_seq(dtr, bsz, length, lp),
                ssm_conv0[ia], ssm0[ia], p['ssd_conv_w'][ia], p['ssd_conv_b'][ia], p['ssd_dt_bias'][ia],
                p['ssd_a_log'][ia], p['ssd_d'][ia], p['ssd_norm_w'][ia], length)
            y_ssd = y_ssd[:, :length].reshape(m, SSD_INNER)
            if prompt:
                assert bsz == 1
                kr, vr, kb, vt, km = outs[4:]
                o_att = _moba_prompt(q, kb, vt, km.reshape(-1, MOBA_INNER))
                k_rows, v_rows = kr[None], vr[None]
            else:
                k, v = outs[4:]
                k_pool, v_pool, page_table = caches[4:]
                qh, kh, vh = (_heads_t8(t, bsz, length) for t in (q, k, v))
                km = _sample_kmean(k_pool[ia], page_table).transpose(0, 2, 1, 3)
                idx = _sample_select(qh, km)[:, :, :length, :MOBA_TOPK]
                o = _sample_attn(qh, kh, vh, idx, k_pool[ia], v_pool[ia], page_table, length)
                o_att = o[:, :, :length].transpose(0, 2, 1, 3).reshape(m, MOBA_INNER).astype(BF16)
                k_rows, v_rows = kh[:, :, :length], vh[:, :, :length]
            wo = p['w_out_a'][ia].astype(BF16)
            x = _outproj([y_ssd, o_att], [wo[:SSD_INNER], wo[SSD_INNER:]], x)
            ssm_c, ssm_s = hist, s_new
        else:
            ic = layer // 2
            assert ic == 0
            wc = p['w_in_c'][ic].astype(BF16)
            ws = [wc[:, :HGRN_K], wc[:, HGRN_K:2 * HGRN_K], wc[:, 2 * HGRN_K:2 * HGRN_K + HGRN_V],
                  wc[:, 2 * HGRN_K + HGRN_V:]]
            qa, fx, iv, g = _norm_matmul(x, nw, ws)
            lp = _round_up(length, HGRN_CHUNK)
            o, s_new = _gla(*(_pad_seq(t, bsz, length, lp) for t in (qa, fx, iv, g)),
                            p['hgrn_lb_gamma'], p['hgrn_norm_w'][ic], hgrn0[ic], length, layer)
            o = o[:, :length].reshape(m, HGRN_V)
            x = _outproj([o], [p['w_out_c'][ic].astype(BF16)], x)
            hgrn_s = s_new
        nf = p['norm_final'].reshape(1, -1) if layer == DEPTH - 1 else None
        hist = ffn_conv0[layer, 0] if prompt else ffn_conv0[layer]
        x, y, fh = _ffn(x, p['norm_ffn'][layer].reshape(1, -1), p['ffn_w_up'][layer].astype(BF16),
                        p['ffn_conv_w'][layer], p['ffn_conv_b'][layer].reshape(1, -1),
                        p['ffn_w_down'][layer].astype(BF16), hist, None if prompt else length, nf)
        ffn_c.append(fh[None] if prompt else fh)
    return (y.reshape(bsz, length, D_MODEL), k_rows[None], v_rows[None], ssm_s[None], ssm_c[None],
            hgrn_s[None], jnp.stack(ffn_c))


def kernel(x_prompt, x_sample, cache_k_pool, cache_v_pool, page_table, state_ssm, state_ssm_conv, state_hgrn,
           state_ffn_conv, norm_mix, norm_ffn, norm_final, w_in_a, w_out_a, ssd_conv_w, ssd_conv_b,
           ssd_dt_bias, ssd_a_log, ssd_d, ssd_norm_w, w_in_c, w_out_c, hgrn_lb_gamma, hgrn_norm_w, ffn_w_up,
           ffn_conv_w, ffn_conv_b, ffn_w_down):
    p = dict(norm_mix=norm_mix, norm_ffn=norm_ffn, norm_final=norm_final, w_in_a=w_in_a, w_out_a=w_out_a,
             ssd_conv_w=ssd_conv_w, ssd_conv_b=ssd_conv_b, ssd_dt_bias=ssd_dt_bias, ssd_a_log=ssd_a_log,
             ssd_d=ssd_d, ssd_norm_w=ssd_norm_w, w_in_c=w_in_c, w_out_c=w_out_c, hgrn_lb_gamma=hgrn_lb_gamma,
             hgrn_norm_w=hgrn_norm_w, ffn_w_up=ffn_w_up, ffn_conv_w=ffn_conv_w, ffn_conv_b=ffn_conv_b,
             ffn_w_down=ffn_w_down)
    bp, sp, _ = x_prompt.shape
    db, ds, _ = x_sample.shape
    na, nc = (DEPTH + 1) // 2, DEPTH // 2
    zeros = lambda *s: jnp.zeros(s, F32)
    prompt_caches = (zeros(na, bp, SSD_CONV - 1, SSD_CONV_DIM), zeros(na, bp, SSD_HEADS, SSD_HEAD_DIM, SSD_STATE),
                     zeros(nc, bp, HGRN_HEADS, HGRN_KEY_DIM, HGRN_VAL_DIM), zeros(DEPTH, bp, FFN_CONV - 1, FFN_DIM))
    outs_p = _trunk(x_prompt.reshape(bp * sp, D_MODEL), bp, sp, True, prompt_caches, p)
    sample_caches = (state_ssm_conv, state_ssm, state_hgrn, state_ffn_conv, cache_k_pool, cache_v_pool, page_table)
    outs_s = _trunk(x_sample.reshape(db * ds, D_MODEL), db, ds, False, sample_caches, p)
    yp, krp, vrp, ssp, scp, hgp, fcp = outs_p
    ys, krs, vrs, sss, scs, hgs, fcs = outs_s
    return (yp, ys, krp, vrp, ssp, scp, hgp, fcp, krs, vrs, sss, scs, hgs, fcs)
```

```python
import functools
import math

import numpy as np
import jax
import jax.numpy as jnp
from jax import lax
from jax.experimental import pallas as pl
from jax.experimental.pallas import tpu as pltpu

F32 = jnp.float32
BF16 = jnp.bfloat16
HIGHEST = lax.Precision.HIGHEST

D_MODEL = 1024
DEPTH = 2
PAGE_SIZE = 128
SSD_HEADS = 8
SSD_HEAD_DIM = 64
SSD_INNER = SSD_HEADS * SSD_HEAD_DIM
SSD_STATE = 64
SSD_GROUPS = 2
SSD_CONV = 4
SSD_CHUNK = 128
SSD_GN = SSD_GROUPS * SSD_STATE
SSD_CONV_DIM = SSD_INNER + 2 * SSD_GN
MOBA_HEADS = 8
MOBA_HEAD_DIM = 64
MOBA_INNER = MOBA_HEADS * MOBA_HEAD_DIM
MOBA_BLOCK = 256
MOBA_TOPK = 3
OFF_XBC = SSD_INNER
OFF_DT = OFF_XBC + SSD_CONV_DIM
OFF_Q = OFF_DT + SSD_HEADS
OFF_K = OFF_Q + MOBA_INNER
OFF_V = OFF_K + MOBA_INNER
HGRN_HEADS = 8
HGRN_KEY_DIM = 128
HGRN_VAL_DIM = D_MODEL // HGRN_HEADS
HGRN_K = HGRN_HEADS * HGRN_KEY_DIM
HGRN_V = HGRN_HEADS * HGRN_VAL_DIM
HGRN_CHUNK = 64
FFN_DIM = 2816
FFN_CONV = 3
RMS_EPS = 1e-6

LANES = 128
SUBLANES = 8
VMEM_LIMIT_BYTES = 56 * 1024 * 1024

ROW_TILE = 256
FFN_COL_CHUNK = 256
NEG_BIG = -1e30
LOG2E = math.log2(math.e)
ALIBI_PARTS = 3
MOBA_UNROLL = 4


def _cparams(sem):
    return pltpu.CompilerParams(dimension_semantics=sem, vmem_limit_bytes=VMEM_LIMIT_BYTES)


def _const_spec(shape):
    nd = len(shape)
    return pl.BlockSpec(shape, lambda *_: (0,) * nd)


def _rms(x, w):
    y = x * lax.rsqrt(jnp.mean(x * x, axis=-1, keepdims=True) + RMS_EPS)
    return y * w


def _rms_bf16(x, w):
    return _rms(x, w).astype(BF16)


def _dot(a, b):
    return jnp.dot(a, b, preferred_element_type=F32)


def _dot_nt(a, b):
    return lax.dot_general(a, b, (((1,), (1,)), ((), ())), preferred_element_type=F32)


def _dot_tn(a, b):
    return lax.dot_general(a, b, (((0,), (0,)), ((), ())), preferred_element_type=F32)


def _tril_f32(n):
    r = lax.broadcasted_iota(jnp.int32, (n, n), 0)
    c = lax.broadcasted_iota(jnp.int32, (n, n), 1)
    return r >= c


def _cumsum_rows(x, q):
    tri = _tril_f32(q).astype(F32)
    return jnp.dot(tri, x, precision=HIGHEST, preferred_element_type=F32)


def _silu(x):
    return x * jax.nn.sigmoid(x)


def _inproj_a_kernel(x_ref, nw_ref, wz, wxbc, wdt, wq, wk, wv, *outs, prompt):
    h = _rms(x_ref[...], nw_ref[...])
    hb = h.astype(BF16)
    z_o, xbc_o, dt_o, q_o = outs[:4]
    z_o[...] = _dot(hb, wz[...])
    xbc_o[...] = _dot(hb, wxbc[...])
    dt_o[...] = _dot(hb, wdt[...])
    if wq.dtype == F32:
        q_o[...] = jnp.dot(h, wq[...], precision=HIGHEST, preferred_element_type=F32)
    else:
        q_o[...] = _dot(hb, wq[...])
    k = _dot(hb, wk[...])
    v = _dot(hb, wv[...])
    if not prompt:
        k_o, v_o = outs[4:]
        k_o[...] = k
        v_o[...] = v
        return
    kr_o, vr_o, ka_o, vt_o, km_o = outs[4:]
    tm = k.shape[0]
    lane = lax.broadcasted_iota(jnp.int32, (tm, LANES), 1)
    key_aug = _key_aug_columns(tm)
    for hp in range(MOBA_HEADS // 2):
        pair = k[:, hp * LANES:(hp + 1) * LANES]
        for j, kh in enumerate((pair, pltpu.roll(pair, MOBA_HEAD_DIM, 1))):
            ka_o[2 * hp + j, 0] = jnp.where(lane < MOBA_HEAD_DIM, kh, key_aug).astype(BF16)
    k_t = k.T
    v_t = v.T
    for p in range(tm // PAGE_SIZE):
        toks = slice(p * PAGE_SIZE, (p + 1) * PAGE_SIZE)
        for h in range(MOBA_HEADS):
            dims = slice(h * MOBA_HEAD_DIM, (h + 1) * MOBA_HEAD_DIM)
            kr_o[p, h] = k_t[dims, toks]
            vr_o[p, h] = v_t[dims, toks]
    vt_o[0] = v_t.astype(BF16)
    km_o[0] = jnp.mean(k, axis=0, keepdims=True)


def _inproj_a(x, nw, ws, prompt):
    m = x.shape[0]
    tm = ROW_TILE if prompt else m
    assert m % tm == 0
    if prompt:
        assert tm == MOBA_BLOCK
    nt = m // tm
    widths = [w.shape[1] for w in ws]
    row = lambda i: (i, 0)
    in_specs = [pl.BlockSpec((tm, D_MODEL), row), _const_spec((1, D_MODEL))]
    in_specs += [_const_spec(w.shape) for w in ws]
    out_shape = [jax.ShapeDtypeStruct((m, n), F32) for n in widths[:4]]
    out_specs = [pl.BlockSpec((tm, n), row) for n in widths[:4]]
    if prompt:
        npg = m // PAGE_SIZE
        ppt = tm // PAGE_SIZE
        rows_shape = (npg, MOBA_HEADS, MOBA_HEAD_DIM, PAGE_SIZE)
        rows_spec = pl.BlockSpec((ppt, MOBA_HEADS, MOBA_HEAD_DIM, PAGE_SIZE), lambda i: (i, 0, 0, 0))
        out_shape += [jax.ShapeDtypeStruct(rows_shape, F32)] * 2
        out_specs += [rows_spec, rows_spec]
        out_shape += [jax.ShapeDtypeStruct((MOBA_HEADS, nt, tm, LANES), BF16),
                      jax.ShapeDtypeStruct((nt, MOBA_INNER, tm), BF16),
                      jax.ShapeDtypeStruct((nt, 1, MOBA_INNER), F32)]
        out_specs += [pl.BlockSpec((MOBA_HEADS, 1, tm, LANES), lambda i: (0, i, 0, 0)),
                      pl.BlockSpec((1, MOBA_INNER, tm), lambda i: (i, 0, 0)),
                      pl.BlockSpec((1, 1, MOBA_INNER), lambda i: (i, 0, 0))]
    else:
        out_shape += [jax.ShapeDtypeStruct((m, MOBA_INNER), F32)] * 2
        out_specs += [pl.BlockSpec((tm, MOBA_INNER), row)] * 2
    return pl.pallas_call(
        functools.partial(_inproj_a_kernel, prompt=prompt),
        grid=(nt,), in_specs=in_specs, out_specs=out_specs, out_shape=out_shape,
        compiler_params=_cparams(("arbitrary",)), name="inproj_a",
    )(x, nw, *ws)


def _norm_matmul_kernel(x_ref, nw_ref, *refs):
    n = len(refs) // 2
    hb = _rms_bf16(x_ref[...], nw_ref[...])
    for w, o in zip(refs[:n], refs[n:]):
        o[...] = _dot(hb, w[...])


def _norm_matmul(x, nw, ws):
    m = x.shape[0]
    tm = min(ROW_TILE, m)
    assert m % tm == 0
    row = lambda i: (i, 0)
    in_specs = [pl.BlockSpec((tm, D_MODEL), row), _const_spec((1, D_MODEL))]
    in_specs += [_const_spec(w.shape) for w in ws]
    return pl.pallas_call(
        _norm_matmul_kernel, grid=(m // tm,), in_specs=in_specs,
        out_specs=[pl.BlockSpec((tm, w.shape[1]), row) for w in ws],
        out_shape=[jax.ShapeDtypeStruct((m, w.shape[1]), F32) for w in ws],
        compiler_params=_cparams(("arbitrary",)), name="norm_matmul",
    )(x, nw, *ws)


def _outproj_kernel(*refs):
    n = (len(refs) - 2) // 2
    x_ref, o_ref = refs[2 * n], refs[2 * n + 1]
    acc = x_ref[...]
    for a, w in zip(refs[:n], refs[n:2 * n]):
        acc = acc + _dot(a[...], w[...])
    o_ref[...] = acc


def _outproj(acts, ws, x):
    m = x.shape[0]
    tm = min(ROW_TILE, m)
    assert m % tm == 0
    row = lambda i: (i, 0)
    in_specs = [pl.BlockSpec((tm, a.shape[1]), row) for a in acts]
    in_specs += [_const_spec(w.shape) for w in ws]
    in_specs += [pl.BlockSpec((tm, D_MODEL), row)]
    return pl.pallas_call(
        _outproj_kernel, grid=(m // tm,), in_specs=in_specs,
        out_specs=pl.BlockSpec((tm, D_MODEL), row),
        out_shape=jax.ShapeDtypeStruct((m, D_MODEL), F32),
        compiler_params=_cparams(("arbitrary",)), name="outproj",
    )(*acts, *ws, x)


def _ffn_kernel(*refs, tm, seq_len, final_norm):
    it = iter(refs)
    x_ref, nw_ref, wup_ref, cw_ref, cb_ref, wdn_ref = (next(it) for _ in range(6))
    if seq_len is None:
        hist_ref = next(it)
    else:
        h1_ref, h2_ref = next(it), next(it)
    nf_ref = next(it) if final_norm else None
    o_ref = next(it)
    y_ref = next(it) if final_norm else None
    tail_ref = next(it)
    up_s, act_s = next(it), next(it)

    i = pl.program_id(0)
    f = FFN_DIM
    if seq_len is None:
        @pl.when(i == 0)
        def _():
            up_s[0:SUBLANES, 0:f] = hist_ref[...]
    else:
        up_s[0:SUBLANES, 0:f] = jnp.zeros((SUBLANES, f), F32)

    x = x_ref[...]
    hb = _rms_bf16(x, nw_ref[...])
    up_s[SUBLANES:SUBLANES + tm, :] = _dot(hb, wup_ref[...])

    if seq_len is not None:
        t = lax.broadcasted_iota(jnp.int32, (tm, 1), 0) % seq_len
    for c in range(0, f, FFN_COL_CHUNK):
        cols = slice(c, c + FFN_COL_CHUNK)
        g0 = up_s[SUBLANES:SUBLANES + tm, cols]
        g1 = up_s[SUBLANES - 1:SUBLANES - 1 + tm, cols]
        g2 = up_s[SUBLANES - 2:SUBLANES - 2 + tm, cols]
        if seq_len is not None:
            g1 = jnp.where(t >= 1, g1, h1_ref[:, cols])
            g2 = jnp.where(t >= 2, g2, h2_ref[:, cols])
        val = up_s[SUBLANES:SUBLANES + tm, f + c:f + c + FFN_COL_CHUNK]
        conv = g2 * cw_ref[0:1, cols]
        conv = conv + g1 * cw_ref[1:2, cols]
        conv = conv + g0 * cw_ref[2:3, cols]
        conv = conv + cb_ref[:, cols]
        act_s[:, cols] = (_silu(conv) * val).astype(BF16)

    out = x + _dot(act_s[...], wdn_ref[...])
    o_ref[...] = out
    if final_norm:
        y = out * lax.rsqrt(jnp.mean(out * out, axis=-1, keepdims=True) + RMS_EPS)
        y_ref[...] = y * nf_ref[...]

    if seq_len is None:
        last = up_s[tm:tm + SUBLANES, 0:f]
        up_s[0:SUBLANES, 0:f] = last

        @pl.when(i == pl.num_programs(0) - 1)
        def _():
            tail_ref[...] = last
    else:
        tail_ref[...] = up_s[SUBLANES:SUBLANES + tm, 0:f]


def _ffn(x, nw, wup, cw, cb, wdn, hist, seq_len, nf):
    assert FFN_CONV == 3
    m = x.shape[0]
    f = FFN_DIM
    final_norm = nf is not None
    row = lambda i: (i, 0)
    if seq_len is None:
        tm = ROW_TILE
        hist8 = jnp.zeros((SUBLANES, f), F32).at[SUBLANES - 2:].set(hist)
        extra = [hist8]
        extra_specs = [_const_spec((SUBLANES, f))]
        tail_shape = (SUBLANES, f)
    else:
        tm = m
        nb = m // seq_len
        assert seq_len >= 2 and nb * seq_len == m
        h1 = jnp.zeros((nb, seq_len, f), F32).at[:, 0].set(hist[:, 1]).reshape(m, f)
        h2 = jnp.zeros((nb, seq_len, f), F32).at[:, 0:2].set(hist).reshape(m, f)
        extra = [h1, h2]
        extra_specs = [_const_spec((m, f))] * 2
        tail_shape = (m, f)
    assert m % tm == 0
    ins = [x, nw, wup, cw, cb, wdn] + extra
    in_specs = [pl.BlockSpec((tm, D_MODEL), row), _const_spec((1, D_MODEL)), _const_spec(wup.shape),
                _const_spec(cw.shape), _const_spec(cb.shape), _const_spec(wdn.shape)] + extra_specs
    out_shape = [jax.ShapeDtypeStruct((m, D_MODEL), F32)]
    out_specs = [pl.BlockSpec((tm, D_MODEL), row)]
    if final_norm:
        ins.append(nf)
        in_specs.append(_const_spec((1, D_MODEL)))
        out_shape.append(jax.ShapeDtypeStruct((m, D_MODEL), F32))
        out_specs.append(pl.BlockSpec((tm, D_MODEL), row))
    out_shape.append(jax.ShapeDtypeStruct(tail_shape, F32))
    out_specs.append(_const_spec(tail_shape))
    res = pl.pallas_call(
        functools.partial(_ffn_kernel, tm=tm, seq_len=seq_len, final_norm=final_norm),
        grid=(m // tm,), in_specs=in_specs, out_specs=out_specs, out_shape=out_shape,
        scratch_shapes=[pltpu.VMEM((SUBLANES + tm, 2 * f), F32), pltpu.VMEM((tm, f), BF16)],
        compiler_params=_cparams(("arbitrary",)), name="conv_ffn",
    )(*ins)
    if final_norm:
        xo, y, tail = res
    else:
        (xo, tail), y = res, None
    if seq_len is None:
        new_hist = tail[SUBLANES - 2:]
    else:
        new_hist = tail.reshape(m // seq_len, seq_len, f)[:, seq_len - 2:]
    return xo, y, new_hist


def _ssd_kernel(z_ref, xbc_ref, dt_ref, hist_ref, s0_ref, cw_ref, cb_ref, dtb_ref, alog_ref, dsk_ref,
                nw_ref, y_ref, hist_o, s_o, cbuf, st, ybuf, *, q, seq_len, n_chunks):
    c = pl.program_id(1)
    hrows = SSD_CONV - 1
    h0 = SUBLANES - hrows

    @pl.when(c == 0)
    def _():
        cbuf[0:SUBLANES, :] = hist_ref[0]
        st[...] = s0_ref[0]

    cbuf[SUBLANES:SUBLANES + q, :] = xbc_ref[0]
    conv = cbuf[h0:h0 + q, :] * cw_ref[0:1, :]
    for j in range(1, SSD_CONV):
        conv = conv + cbuf[h0 + j:h0 + j + q, :] * cw_ref[j:j + 1, :]
    xc = _silu(conv + cb_ref[...])

    xdt_raw = dt_ref[0] + dtb_ref[...]
    dt = jnp.maximum(xdt_raw, 0.0) + jnp.log1p(jnp.exp(-jnp.abs(xdt_raw)))
    if seq_len % q != 0:
        row = c * q + lax.broadcasted_iota(jnp.int32, (q, 1), 0)
        dt = jnp.where(row < seq_len, dt, 0.0)
    a = -jnp.exp(alog_ref[...])
    acum = _cumsum_rows(dt * a, q)
    acum_t = acum.T
    a_last = acum[q - 1:q, :]
    e_acum = jnp.exp(acum)
    w_in = jnp.exp(a_last - acum) * dt
    e_last = jnp.exp(a_last)
    tril = _tril_f32(q)

    rep = SSD_HEADS // SSD_GROUPS
    cbs, bms, cms = [], [], []
    for g in range(SSD_GROUPS):
        bm = xc[:, SSD_INNER + g * SSD_STATE:SSD_INNER + (g + 1) * SSD_STATE]
        cm = xc[:, SSD_INNER + SSD_GN + g * SSD_STATE:SSD_INNER + SSD_GN + (g + 1) * SSD_STATE]
        bms.append(bm)
        cms.append(cm)
        cbs.append(_dot_nt(cm.astype(BF16), bm.astype(BF16)))

    for h in range(SSD_HEADS):
        g = h // rep
        xs = xc[:, h * SSD_HEAD_DIM:(h + 1) * SSD_HEAD_DIM]
        seg = acum[:, h:h + 1] - acum_t[h:h + 1, :]
        decay = jnp.exp(jnp.where(tril, seg, -jnp.inf))
        y = _dot((cbs[g] * decay).astype(BF16), (xs * dt[:, h:h + 1]).astype(BF16))
        s_h = st[h]
        y = y + _dot_nt((cms[g] * e_acum[:, h:h + 1]).astype(BF16), s_h.astype(BF16))
        st[h] = s_h * e_last[:, h:h + 1] + _dot_tn((xs * w_in[:, h:h + 1]).astype(BF16),
                                                    bms[g].astype(BF16))
        ybuf[:, h * SSD_HEAD_DIM:(h + 1) * SSD_HEAD_DIM] = y + xs * dsk_ref[:, h:h + 1]

    yz = ybuf[...] * _silu(z_ref[0])
    gw = SSD_INNER // SSD_GROUPS
    for g in range(SSD_GROUPS):
        yg = yz[:, g * gw:(g + 1) * gw]
        yn = yg * lax.rsqrt(jnp.mean(yg * yg, axis=-1, keepdims=True) + RMS_EPS)
        y_ref[0, :, g * gw:(g + 1) * gw] = (yn * nw_ref[:, g * gw:(g + 1) * gw]).astype(BF16)

    @pl.when(c == n_chunks - 1)
    def _():
        l_last = seq_len - (n_chunks - 1) * q
        hist_o[0] = cbuf[l_last:l_last + SUBLANES, :]
        s_o[0] = st[...]

    cbuf[0:SUBLANES, :] = cbuf[q:q + SUBLANES, :]


def _pad_lanes(v, fill=0.0):
    return jnp.full((1, LANES), fill, F32).at[0, :v.shape[0]].set(v.astype(F32))


def _ssd(z, xbc, dtr, hist, s0, conv_w, conv_b, dt_bias, a_log, d_skip, norm_w, seq_len):
    b, lp, _ = z.shape
    q = SSD_CHUNK
    assert lp % q == 0 and lp - seq_len < q
    hrows = SSD_CONV - 1
    hist8 = jnp.zeros((b, SUBLANES, SSD_CONV_DIM), F32).at[:, SUBLANES - hrows:].set(hist)
    cw8 = jnp.zeros((SUBLANES, SSD_CONV_DIM), F32).at[:SSD_CONV].set(conv_w)
    seq = lambda i, c: (i, c, 0)
    per_b3 = lambda i, c: (i, 0, 0)
    per_b4 = lambda i, c: (i, 0, 0, 0)
    in_specs = [pl.BlockSpec((1, q, SSD_INNER), seq), pl.BlockSpec((1, q, SSD_CONV_DIM), seq),
                pl.BlockSpec((1, q, LANES), seq), pl.BlockSpec((1, SUBLANES, SSD_CONV_DIM), per_b3),
                pl.BlockSpec((1, SSD_HEADS, SSD_HEAD_DIM, SSD_STATE), per_b4),
                _const_spec((SUBLANES, SSD_CONV_DIM)), _const_spec((1, SSD_CONV_DIM)),
                _const_spec((1, LANES)), _const_spec((1, LANES)), _const_spec((1, LANES)),
                _const_spec((1, SSD_INNER))]
    out_shape = [jax.ShapeDtypeStruct((b, lp, SSD_INNER), BF16),
                 jax.ShapeDtypeStruct((b, SUBLANES, SSD_CONV_DIM), F32),
                 jax.ShapeDtypeStruct((b, SSD_HEADS, SSD_HEAD_DIM, SSD_STATE), F32)]
    out_specs = [pl.BlockSpec((1, q, SSD_INNER), seq), pl.BlockSpec((1, SUBLANES, SSD_CONV_DIM), per_b3),
                 pl.BlockSpec((1, SSD_HEADS, SSD_HEAD_DIM, SSD_STATE), per_b4)]
    y, hist_o, s_o = pl.pallas_call(
        functools.partial(_ssd_kernel, q=q, seq_len=seq_len, n_chunks=lp // q),
        grid=(b, lp // q), in_specs=in_specs, out_specs=out_specs, out_shape=out_shape,
        scratch_shapes=[pltpu.VMEM((SUBLANES + q, SSD_CONV_DIM), F32),
                        pltpu.VMEM((SSD_HEADS, SSD_HEAD_DIM, SSD_STATE), F32),
                        pltpu.VMEM((q, SSD_INNER), F32)],
        compiler_params=_cparams(("arbitrary", "arbitrary")), name="ssd",
    )(z, xbc, dtr, hist8, s0, cw8, conv_b.reshape(1, -1), _pad_lanes(dt_bias), _pad_lanes(a_log),
      _pad_lanes(d_skip), norm_w.reshape(1, -1))
    return y, hist_o[:, SUBLANES - hrows:], s_o


def _gla_kernel(q_ref, fx_ref, iv_ref, g_ref, gam_ref, nw_ref, s0_ref, o_ref, s_o, st_t, *, q, seq_len,
                layer):
    c = pl.program_id(1)
    n_chunks = pl.num_programs(1)

    @pl.when(c == 0)
    def _():
        for h in range(HGRN_HEADS):
            st_t[h] = s0_ref[0, h].T

    rows = [gam_ref[l:l + 1, :] for l in range(DEPTH)]
    mx = functools.reduce(jnp.maximum, rows)
    es = [jnp.exp(r - mx) for r in rows]
    lb = sum(es[1:layer + 1]) / sum(es) if layer >= 1 else jnp.zeros_like(mx)

    f = lb + (1.0 - lb) * jax.nn.sigmoid(fx_ref[0])
    if seq_len % q != 0:
        row = c * q + lax.broadcasted_iota(jnp.int32, (q, 1), 0)
        f = jnp.where(row < seq_len, f, 1.0)
    kk = 1.0 - f
    b = _cumsum_rows(jnp.log(f), q)
    b_last = b[q - 1:q, :]
    qd = q_ref[0] * jnp.exp(b)
    kd = kk * jnp.exp(-b)
    kdec = kk * jnp.exp(b_last - b)
    e_last = jnp.exp(b_last)
    tril = _tril_f32(q)
    iv = iv_ref[0]

    for h in range(HGRN_HEADS):
        ks = slice(h * HGRN_KEY_DIM, (h + 1) * HGRN_KEY_DIM)
        vs = slice(h * HGRN_VAL_DIM, (h + 1) * HGRN_VAL_DIM)
        qd_h = qd[:, ks].astype(BF16)
        v_h = iv[:, vs].astype(BF16)
        att = jnp.where(tril, _dot_nt(qd_h, kd[:, ks].astype(BF16)), 0.0)
        s_t = st_t[h]
        o = _dot(att.astype(BF16), v_h) + _dot_nt(qd_h, s_t.astype(BF16))
        st_t[h] = s_t * e_last[:, ks] + _dot_tn(v_h, kdec[:, ks].astype(BF16))
        on = o * lax.rsqrt(jnp.mean(o * o, axis=-1, keepdims=True) + RMS_EPS) * nw_ref[...]
        o_ref[0, :, vs] = (on * _silu(g_ref[0, :, vs])).astype(BF16)

    @pl.when(c == n_chunks - 1)
    def _():
        for h in range(HGRN_HEADS):
            s_o[0, h] = st_t[h].T


def _gla(qa, fx, iv, g, gamma, norm_w, s0, seq_len, layer):
    b, lp, _ = qa.shape
    q = HGRN_CHUNK
    assert lp % q == 0 and lp - seq_len < q
    gam8 = jnp.zeros((SUBLANES, HGRN_K), F32).at[:DEPTH].set(gamma)
    seq = lambda i, c: (i, c, 0)
    per_b4 = lambda i, c: (i, 0, 0, 0)
    st_shape = (HGRN_HEADS, HGRN_KEY_DIM, HGRN_VAL_DIM)
    in_specs = [pl.BlockSpec((1, q, HGRN_K), seq), pl.BlockSpec((1, q, HGRN_K), seq),
                pl.BlockSpec((1, q, HGRN_V), seq), pl.BlockSpec((1, q, HGRN_V), seq),
                _const_spec((SUBLANES, HGRN_K)), _const_spec((1, HGRN_VAL_DIM)),
                pl.BlockSpec((1,) + st_shape, per_b4)]
    o, s_o = pl.pallas_call(
        functools.partial(_gla_kernel, q=q, seq_len=seq_len, layer=layer),
        grid=(b, lp // q), in_specs=in_specs,
        out_specs=[pl.BlockSpec((1, q, HGRN_V), seq), pl.BlockSpec((1,) + st_shape, per_b4)],
        out_shape=[jax.ShapeDtypeStruct((b, lp, HGRN_V), BF16), jax.ShapeDtypeStruct((b,) + st_shape, F32)],
        scratch_shapes=[pltpu.VMEM((HGRN_HEADS, HGRN_VAL_DIM, HGRN_KEY_DIM), F32)],
        compiler_params=_cparams(("arbitrary", "arbitrary")), name="gla",
    )(qa, fx, iv, g, gam8, norm_w.reshape(1, -1), s0)
    return o, s_o


def _select_topk_rows(gate_t, n_valid, nblk, width):
    blk = lax.broadcasted_iota(jnp.int32, (nblk, width), 0)
    blk_f = blk.astype(F32)
    g = jnp.where(blk < n_valid, gate_t, -jnp.inf)
    sel = jnp.zeros((nblk, width), F32)
    for _ in range(MOBA_TOPK):
        mx = jnp.max(g, axis=0, keepdims=True)
        first = jnp.min(jnp.where(g == mx, blk_f, float(nblk)), axis=0, keepdims=True)
        pick = (blk_f == first) & (mx > -jnp.inf)
        sel = jnp.where(pick, 1.0, sel)
        g = jnp.where(pick, -jnp.inf, g)
    return jnp.where(sel > 0.0, 0.0, -jnp.inf)


def _key_aug_columns(tm):
    lane = lax.broadcasted_iota(jnp.int32, (tm, LANES), 1)
    r = lax.broadcasted_iota(jnp.int32, (tm, LANES), 0).astype(F32)
    return jnp.where((lane >= MOBA_HEAD_DIM) & (lane < MOBA_HEAD_DIM + ALIBI_PARTS), r, 0.0)


def _moba_prompt_kernel(slopes_ref, q_ref, ka_ref, vt_ref, km_ref, o_ref, brow_s, ubuf, *, nblk):
    hp = pl.program_id(0)
    i = pl.program_id(1)
    tq = MOBA_BLOCK
    d = MOBA_HEAD_DIM
    q = q_ref[...]
    lane = lax.broadcasted_iota(jnp.int32, (tq, 2 * d), 1)
    causal = (lax.broadcasted_iota(jnp.int32, (MOBA_BLOCK, tq), 0)
              <= lax.broadcasted_iota(jnp.int32, (MOBA_BLOCK, tq), 1))
    blk_f = lax.broadcasted_iota(jnp.int32, (nblk, tq), 0).astype(F32)
    col_f = lax.broadcasted_iota(jnp.int32, (nblk, tq), 1).astype(F32)
    aug_row = lax.broadcasted_iota(jnp.int32, (d, tq), 0)
    i_f = i.astype(F32)

    rhs = []
    for j in range(2):
        a2 = slopes_ref[2 * hp + j] * LOG2E
        qm_t = jnp.where((lane >= j * d) & (lane < (j + 1) * d), q, 0.0).T
        gate_t = jnp.dot(km_ref[...], qm_t, precision=HIGHEST, preferred_element_type=F32)
        sel = _select_topk_rows(gate_t, i, nblk, tq)
        brow_s[j] = sel + a2 * (MOBA_BLOCK * (blk_f - i_f) - col_f)
        rest = jnp.full((d, tq), a2, F32)
        aug = jnp.zeros((d, tq), F32)
        for part in range(ALIBI_PARTS):
            term = rest.astype(BF16).astype(F32)
            aug = jnp.where(aug_row == part, term, aug)
            rest = rest - term
        q_t = qm_t[j * d:(j + 1) * d, :] * (d ** -0.5 * LOG2E)
        rhs.append(jnp.concatenate([q_t, aug], axis=0).astype(BF16))

    def block_update(st, u, b_row, v_t):
        m_run, l_run, o_t = st
        m_new = jnp.maximum(m_run, jnp.max(u, axis=0, keepdims=True) + b_row)
        alpha = jnp.exp2(m_run - m_new)
        p = jnp.exp2(u - (m_new - b_row))
        l_new = alpha * l_run + jnp.sum(p, axis=0, keepdims=True)
        o_new = alpha * o_t + _dot(v_t, p.astype(BF16))
        return m_new, l_new, o_new

    def score(n, slot):
        for j in range(2):
            ubuf[slot, j] = _dot(ka_ref[j, n], rhs[j])

    def consume(n, slot, sts, own):
        new = []
        for j in range(2):
            u = ubuf[slot, j]
            if own:
                u = jnp.where(causal, u, -jnp.inf)
                b_row = -(slopes_ref[2 * hp + j] * LOG2E) * col_f[0:1, :]
            else:
                b_row = brow_s[j, pl.ds(n, 1), :]
            new.append(block_update(sts[j], u, b_row, vt_ref[n, j * d:(j + 1) * d, :]))
        return tuple(new)

    def group(base, unroll, sts):
        for k in range(unroll):
            score(base + k + 1, (k + 1) % 2)
            sts = consume(base + k, k % 2, sts, False)
        return sts

    def finish(sts, slot):
        outs = []
        for j, (_, l_fin, o_t) in enumerate(consume(i, slot, sts, True)):
            outs.append(o_t / l_fin)
        o_ref[...] = jnp.concatenate(outs, axis=0).T.astype(BF16)

    init = (jnp.full((1, tq), NEG_BIG, F32), jnp.zeros((1, tq), F32), jnp.zeros((d, tq), F32))
    score(0, 0)
    n_big = i // MOBA_UNROLL
    sts = lax.fori_loop(0, n_big, lambda g, s: group(g * MOBA_UNROLL, MOBA_UNROLL, s), (init, init))
    base2 = n_big * MOBA_UNROLL
    n_pair = (i - base2) // 2
    sts = lax.fori_loop(0, n_pair, lambda g, s: group(base2 + 2 * g, 2, s), sts)

    @pl.when(i % 2 == 1)
    def _():
        finish(group(i - 1, 1, sts), 1)

    @pl.when(i % 2 == 0)
    def _():
        finish(sts, 0)


def _alibi_slopes():
    return jnp.asarray(np.exp2(-8.0 * np.arange(1, MOBA_HEADS + 1) / MOBA_HEADS), dtype=F32)


def _moba_prompt(q, ka, vt, km):
    s = q.shape[0]
    nblk = s // MOBA_BLOCK
    assert nblk * MOBA_BLOCK == s and 2 * MOBA_HEAD_DIM == LANES
    pairs = MOBA_HEADS // 2
    return pl.pallas_call(
        functools.partial(_moba_prompt_kernel, nblk=nblk),
        grid=(pairs, nblk),
        in_specs=[pl.BlockSpec(memory_space=pltpu.SMEM),
                  pl.BlockSpec((MOBA_BLOCK, LANES), lambda hp, i: (i, hp)),
                  pl.BlockSpec((2, nblk, MOBA_BLOCK, LANES), lambda hp, i: (hp, 0, 0, 0)),
                  pl.BlockSpec((nblk, LANES, MOBA_BLOCK), lambda hp, i: (0, hp, 0)),
                  pl.BlockSpec((nblk, LANES), lambda hp, i: (0, hp))],
        out_specs=pl.BlockSpec((MOBA_BLOCK, LANES), lambda hp, i: (i, hp)),
        out_shape=jax.ShapeDtypeStruct((s, MOBA_INNER), BF16),
        scratch_shapes=[pltpu.VMEM((2, nblk, MOBA_BLOCK), F32),
                        pltpu.VMEM((2, 2, MOBA_BLOCK, MOBA_BLOCK), F32)],
        compiler_params=_cparams(("arbitrary", "arbitrary")), name="moba_prompt",
    )(_alibi_slopes(), q, ka, vt, km)


PAGES_PER_STEP = 16


def _kmean_kernel(pt_ref, *refs):
    del pt_ref
    o_ref = refs[-1]
    s = pl.program_id(1)
    ppb = MOBA_BLOCK // PAGE_SIZE
    bps = PAGES_PER_STEP // ppb

    @pl.when(s == 0)
    def _():
        o_ref[...] = jnp.zeros(o_ref.shape, F32)

    acc = o_ref[0]
    lane = lax.broadcasted_iota(jnp.int32, acc.shape, 2)
    for m in range(bps):
        x = refs[m * ppb][0]
        for p in range(1, ppb):
            x = x + refs[m * ppb + p][0]
        mean = jnp.sum(x, axis=-1, keepdims=True) * (1.0 / MOBA_BLOCK)
        acc = jnp.where(lane == s * bps + m, mean, acc)
    o_ref[0] = acc


def _sample_kmean(k_pool_t, page_table):
    db, n_pages = page_table.shape
    ppb = MOBA_BLOCK // PAGE_SIZE
    n_full = n_pages // ppb
    assert n_full * ppb == n_pages and n_pages % PAGES_PER_STEP == 0
    steps = n_pages // PAGES_PER_STEP
    blk = (1, MOBA_HEADS, MOBA_HEAD_DIM, PAGE_SIZE)

    def page_spec(p):
        return pl.BlockSpec(blk, lambda b, s, pt: (pt[b * n_pages + s * PAGES_PER_STEP + p], 0, 0, 0))

    out_blk = (1, MOBA_HEADS, MOBA_HEAD_DIM, n_full)
    grid_spec = pltpu.PrefetchScalarGridSpec(
        num_scalar_prefetch=1, grid=(db, steps),
        in_specs=[page_spec(p) for p in range(PAGES_PER_STEP)],
        out_specs=pl.BlockSpec(out_blk, lambda b, s, pt: (b, 0, 0, 0)))
    return pl.pallas_call(
        _kmean_kernel, grid_spec=grid_spec,
        out_shape=jax.ShapeDtypeStruct((db,) + out_blk[1:], F32),
        compiler_params=_cparams(("arbitrary", "arbitrary")), name="sample_kmean",
    )(page_table.reshape(-1), *([k_pool_t] * PAGES_PER_STEP))


def _sample_select_kernel(q_ref, km_ref, idx_ref, *, n_full):
    t8 = q_ref.shape[2]
    blk = lax.broadcasted_iota(jnp.int32, (t8, n_full), 1).astype(F32)
    lane = lax.broadcasted_iota(jnp.int32, (t8, LANES), 1)
    for h in range(MOBA_HEADS):
        g = jnp.dot(q_ref[0, h], km_ref[0, h], precision=HIGHEST, preferred_element_type=F32)
        out = jnp.zeros((t8, LANES), jnp.int32)
        for k in range(MOBA_TOPK):
            mx = jnp.max(g, axis=-1, keepdims=True)
            first = jnp.min(jnp.where(g == mx, blk, float(n_full)), axis=-1, keepdims=True)
            out = jnp.where(lane == k, first.astype(jnp.int32), out)
            g = jnp.where(blk == first, -jnp.inf, g)
        idx_ref[0, h] = out


def _sample_select(qh, km):
    db, h, t8, d = qh.shape
    n_full = km.shape[3]
    assert n_full >= MOBA_TOPK
    b4 = lambda b: (b, 0, 0, 0)
    return pl.pallas_call(
        functools.partial(_sample_select_kernel, n_full=n_full), grid=(db,),
        in_specs=[pl.BlockSpec((1, h, t8, d), b4), pl.BlockSpec((1, h, d, n_full), b4)],
        out_specs=pl.BlockSpec((1, h, t8, LANES), b4),
        out_shape=jax.ShapeDtypeStruct((db, h, t8, LANES), jnp.int32),
        compiler_params=_cparams(("arbitrary",)), name="sample_select",
    )(qh, km)


def _sample_attn_kernel(idx_ref, phys_ref, slopes_ref, q_ref, kn_ref, vn_ref, *refs, ds, past):
    del phys_ref
    ppb = MOBA_BLOCK // PAGE_SIZE
    n_sel = MOBA_TOPK * ppb
    k_refs = refs[:ds * n_sel]
    v_refs = refs[ds * n_sel:2 * ds * n_sel]
    o_ref = refs[2 * ds * n_sel]
    b = pl.program_id(0)
    h = pl.program_id(1)
    slope = slopes_ref[h]
    t8 = q_ref.shape[3]
    c_page = lax.broadcasted_iota(jnp.int32, (1, PAGE_SIZE), 1).astype(F32)
    c_new = lax.broadcasted_iota(jnp.int32, (1, t8), 1)
    qf = q_ref[0, 0] * (MOBA_HEAD_DIM ** -0.5)
    k_new = kn_ref[0, 0]
    v_new = vn_ref[0, 0]
    o_ref[0, 0] = jnp.zeros((MOBA_HEAD_DIM, t8), F32)
    for t in range(ds):
        q_t = qf[:, t:t + 1]
        t_pos = float(past + t)
        scores = []
        for sp in range(n_sel):
            slot, p = divmod(sp, ppb)
            blk_idx = idx_ref[((b * MOBA_HEADS + h) * ds + t) * MOBA_TOPK + slot]
            pos0 = (blk_idx * MOBA_BLOCK + p * PAGE_SIZE).astype(F32)
            s = jnp.sum(k_refs[t * n_sel + sp][0, 0] * q_t, axis=0, keepdims=True)
            scores.append(s - slope * (t_pos - (pos0 + c_page)))
        s_new = jnp.sum(k_new * q_t, axis=0, keepdims=True)
        s_new = s_new - slope * (t_pos - (float(past) + c_new.astype(F32)))
        s_new = jnp.where(c_new <= t, s_new, -jnp.inf)
        mx = jnp.max(s_new, axis=-1, keepdims=True)
        for s in scores:
            mx = jnp.maximum(mx, jnp.max(s, axis=-1, keepdims=True))
        p_new = jnp.exp(s_new - mx)
        l = jnp.sum(p_new, axis=-1, keepdims=True)
        o = jnp.sum(p_new * v_new, axis=-1, keepdims=True)
        acc = jnp.zeros((MOBA_HEAD_DIM, PAGE_SIZE), F32)
        for sp in range(n_sel):
            pr = jnp.exp(scores[sp] - mx)
            l = l + jnp.sum(pr, axis=-1, keepdims=True)
            acc = acc + pr * v_refs[t * n_sel + sp][0, 0]
        o = o + jnp.sum(acc, axis=-1, keepdims=True)
        o_ref[0, 0, :, t:t + 1] = o / l


def _sample_attn(qh, kh, vh, idx, k_pool, v_pool, page_table, ds):
    db, h, d, t8 = qh.shape
    n_pages = page_table.shape[1]
    ppb = MOBA_BLOCK // PAGE_SIZE
    assert n_pages % ppb == 0
    past = n_pages * PAGE_SIZE
    n_sel = MOBA_TOPK * ppb
    logical = idx[..., None] * ppb + jnp.arange(ppb, dtype=jnp.int32)
    phys = page_table[jnp.arange(db)[:, None, None, None], logical.reshape(db, h, ds, n_sel)]
    blk = (1, 1, d, PAGE_SIZE)

    def page_spec(t, sp):
        return pl.BlockSpec(
            blk, lambda b, hh, idx_r, phys_r: (phys_r[((b * h + hh) * ds + t) * n_sel + sp], hh, 0, 0))

    tok = pl.BlockSpec((1, 1, d, t8), lambda b, hh, *_: (b, hh, 0, 0))
    page_specs = [page_spec(t, sp) for t in range(ds) for sp in range(n_sel)]
    grid_spec = pltpu.PrefetchScalarGridSpec(
        num_scalar_prefetch=2, grid=(db, h),
        in_specs=[pl.BlockSpec(memory_space=pltpu.SMEM), tok, tok, tok] + page_specs + page_specs,
        out_specs=tok)
    return pl.pallas_call(
        functools.partial(_sample_attn_kernel, ds=ds, past=past), grid_spec=grid_spec,
        out_shape=jax.ShapeDtypeStruct((db, h, d, t8), F32),
        compiler_params=_cparams(("arbitrary", "arbitrary")), name="sample_attn",
    )(idx.reshape(-1), phys.reshape(-1), _alibi_slopes(), qh, kh, vh,
      *([k_pool] * (ds * n_sel)), *([v_pool] * (ds * n_sel)))


def _split_w_in_a(w, f32_query):
    wb = w.astype(BF16)
    wdt = jnp.zeros((D_MODEL, LANES), BF16).at[:, :SSD_HEADS].set(wb[:, OFF_DT:OFF_Q])
    wq = w[:, OFF_Q:OFF_K] if f32_query else wb[:, OFF_Q:OFF_K]
    return [wb[:, :OFF_XBC], wb[:, OFF_XBC:OFF_DT], wdt, wq, wb[:, OFF_K:OFF_V], wb[:, OFF_V:]]


def _pad_seq(a, b, l, lp):
    a = a.reshape(b, l, a.shape[-1])
    return a if lp == l else jnp.pad(a, ((0, 0), (0, lp - l), (0, 0)))


def _round_up(n, m):
    return -(-n // m) * m


def _heads(a, b, l):
    return a.reshape(b, l, MOBA_HEADS, MOBA_HEAD_DIM).transpose(0, 2, 1, 3)


def _pad_tokens(a):
    l = a.shape[-2]
    return jnp.pad(a, ((0, 0),) * (a.ndim - 2) + ((0, _round_up(l, SUBLANES) - l), (0, 0)))


def _trunk(x, bsz, length, prompt, caches, p):
    ssm_conv0, ssm0, hgrn0, ffn_conv0 = caches[:4]
    m = bsz * length
    k_rows = v_rows = ssm_c = ssm_s = hgrn_s = None
    ffn_c = []
    y = None
    for layer in range(DEPTH):
        nw = p['norm_mix'][layer].reshape(1, -1)
        if layer % 2 == 0:
            ia = layer // 2
            assert ia == 0
            ws = _split_w_in_a(p['w_in_a'][ia], f32_query=not prompt)
            outs = _inproj_a(x, nw, ws, prompt)
            z, xbc, dtr, q = outs[:4]
            lp = _round_up(length, SSD_CHUNK)
            y_ssd, hist, s_new = _ssd(
                _pad_seq(z, bsz, length, lp), _pad_seq(xbc, bsz, length, lp), _pad_seq(dtr, bsz, length, lp),
                ssm_conv0[ia], ssm0[ia], p['ssd_conv_w'][ia], p['ssd_conv_b'][ia], p['ssd_dt_bias'][ia],
                p['ssd_a_log'][ia], p['ssd_d'][ia], p['ssd_norm_w'][ia], length)
            y_ssd = y_ssd[:, :length].reshape(m, SSD_INNER)
            if prompt:
                assert bsz == 1
                kr, vr, ka, vt, km = outs[4:]
                o_att = _moba_prompt(q, ka, vt, km.reshape(-1, MOBA_INNER))
                k_rows, v_rows = jnp.swapaxes(kr, -1, -2)[None], jnp.swapaxes(vr, -1, -2)[None]
            else:
                k, v = outs[4:]
                k_pool, v_pool, page_table = caches[4:]
                k_pool_t = jnp.swapaxes(k_pool[ia], -1, -2)
                v_pool_t = jnp.swapaxes(v_pool[ia], -1, -2)
                qh, kh, vh = (_pad_tokens(_heads(t, bsz, length)) for t in (q, k, v))
                qt, kt, vt = (jnp.swapaxes(t, -1, -2) for t in (qh, kh, vh))
                km = _sample_kmean(k_pool_t, page_table)
                idx = _sample_select(qh, km)[:, :, :length, :MOBA_TOPK]
                o = _sample_attn(qt, kt, vt, idx, k_pool_t, v_pool_t, page_table, length)
                o_att = o[..., :length].transpose(0, 3, 1, 2).reshape(m, MOBA_INNER).astype(BF16)
                k_rows, v_rows = kh[:, :, :length], vh[:, :, :length]
            wo = p['w_out_a'][ia].astype(BF16)
            x = _outproj([y_ssd, o_att], [wo[:SSD_INNER], wo[SSD_INNER:]], x)
            ssm_c, ssm_s = hist, s_new
        else:
            ic = layer // 2
            assert ic == 0
            wc = p['w_in_c'][ic].astype(BF16)
            ws = [wc[:, :HGRN_K], wc[:, HGRN_K:2 * HGRN_K], wc[:, 2 * HGRN_K:2 * HGRN_K + HGRN_V],
                  wc[:, 2 * HGRN_K + HGRN_V:]]
            qa, fx, iv, g = _norm_matmul(x, nw, ws)
            lp = _round_up(length, HGRN_CHUNK)
            o, s_new = _gla(*(_pad_seq(t, bsz, length, lp) for t in (qa, fx, iv, g)),
                            p['hgrn_lb_gamma'], p['hgrn_norm_w'][ic], hgrn0[ic], length, layer)
            o = o[:, :length].reshape(m, HGRN_V)
            x = _outproj([o], [p['w_out_c'][ic].astype(BF16)], x)
            hgrn_s = s_new
        nf = p['norm_final'].reshape(1, -1) if layer == DEPTH - 1 else None
        hist = ffn_conv0[layer, 0] if prompt else ffn_conv0[layer]
        x, y, fh = _ffn(x, p['norm_ffn'][layer].reshape(1, -1), p['ffn_w_up'][layer].astype(BF16),
                        p['ffn_conv_w'][layer], p['ffn_conv_b'][layer].reshape(1, -1),
                        p['ffn_w_down'][layer].astype(BF16), hist, None if prompt else length, nf)
        ffn_c.append(fh[None] if prompt else fh)
    return (y.reshape(bsz, length, D_MODEL), k_rows[None], v_rows[None], ssm_s[None], ssm_c[None],
            hgrn_s[None], jnp.stack(ffn_c))


def kernel(x_prompt, x_sample, cache_k_pool, cache_v_pool, page_table, state_ssm, state_ssm_conv, state_hgrn,
           state_ffn_conv, norm_mix, norm_ffn, norm_final, w_in_a, w_out_a, ssd_conv_w, ssd_conv_b,
           ssd_dt_bias, ssd_a_log, ssd_d, ssd_norm_w, w_in_c, w_out_c, hgrn_lb_gamma, hgrn_norm_w, ffn_w_up,
           ffn_conv_w, ffn_conv_b, ffn_w_down):
    p = dict(norm_mix=norm_mix, norm_ffn=norm_ffn, norm_final=norm_final, w_in_a=w_in_a, w_out_a=w_out_a,
             ssd_conv_w=ssd_conv_w, ssd_conv_b=ssd_conv_b, ssd_dt_bias=ssd_dt_bias, ssd_a_log=ssd_a_log,
             ssd_d=ssd_d, ssd_norm_w=ssd_norm_w, w_in_c=w_in_c, w_out_c=w_out_c, hgrn_lb_gamma=hgrn_lb_gamma,
             hgrn_norm_w=hgrn_norm_w, ffn_w_up=ffn_w_up, ffn_conv_w=ffn_conv_w, ffn_conv_b=ffn_conv_b,
             ffn_w_down=ffn_w_down)
    bp, sp, _ = x_prompt.shape
    db, ds, _ = x_sample.shape
    na, nc = (DEPTH + 1) // 2, DEPTH // 2
    zeros = lambda *s: jnp.zeros(s, F32)
    prompt_caches = (zeros(na, bp, SSD_CONV - 1, SSD_CONV_DIM), zeros(na, bp, SSD_HEADS, SSD_HEAD_DIM, SSD_STATE),
                     zeros(nc, bp, HGRN_HEADS, HGRN_KEY_DIM, HGRN_VAL_DIM), zeros(DEPTH, bp, FFN_CONV - 1, FFN_DIM))
    outs_p = _trunk(x_prompt.reshape(bp * sp, D_MODEL), bp, sp, True, prompt_caches, p)
    sample_caches = (state_ssm_conv, state_ssm, state_hgrn, state_ffn_conv, cache_k_pool, cache_v_pool, page_table)
    outs_s = _trunk(x_sample.reshape(db * ds, D_MODEL), db, ds, False, sample_caches, p)
    yp, krp, vrp, ssp, scp, hgp, fcp = outs_p
    ys, krs, vrs, sss, scs, hgs, fcs = outs_s
    return (yp, ys, krp, vrp, ssp, scp, hgp, fcp, krs, vrs, sss, scs, hgs, fcs)
```

```python
import functools
import math

import numpy as np
import jax
import jax.numpy as jnp
from jax import lax
from jax.experimental import pallas as pl
from jax.experimental.pallas import tpu as pltpu

F32 = jnp.float32
BF16 = jnp.bfloat16
HIGHEST = lax.Precision.HIGHEST

D_MODEL = 1024
DEPTH = 2
PAGE_SIZE = 128
SSD_HEADS = 8
SSD_HEAD_DIM = 64
SSD_INNER = SSD_HEADS * SSD_HEAD_DIM
SSD_STATE = 64
SSD_GROUPS = 2
SSD_CONV = 4
SSD_CHUNK = 128
SSD_GN = SSD_GROUPS * SSD_STATE
SSD_CONV_DIM = SSD_INNER + 2 * SSD_GN
MOBA_HEADS = 8
MOBA_HEAD_DIM = 64
MOBA_INNER = MOBA_HEADS * MOBA_HEAD_DIM
MOBA_BLOCK = 256
MOBA_TOPK = 3
OFF_XBC = SSD_INNER
OFF_DT = OFF_XBC + SSD_CONV_DIM
OFF_Q = OFF_DT + SSD_HEADS
OFF_K = OFF_Q + MOBA_INNER
OFF_V = OFF_K + MOBA_INNER
HGRN_HEADS = 8
HGRN_KEY_DIM = 128
HGRN_VAL_DIM = D_MODEL // HGRN_HEADS
HGRN_K = HGRN_HEADS * HGRN_KEY_DIM
HGRN_V = HGRN_HEADS * HGRN_VAL_DIM
HGRN_CHUNK = 64
FFN_DIM = 2816
FFN_CONV = 3
RMS_EPS = 1e-6

LANES = 128
SUBLANES = 8
VMEM_LIMIT_BYTES = 56 * 1024 * 1024

ROW_TILE = 256
FFN_COL_CHUNK = 256
NEG_BIG = -1e30
LOG2E = math.log2(math.e)
ALIBI_PARTS = 3
MOBA_VT_ROWS = MOBA_HEAD_DIM + 16
MOBA_AHEAD = 2
MOBA_SLOTS = MOBA_AHEAD + 1
MOBA_UNROLL = 2 * MOBA_SLOTS


def _cparams(sem, flags=None):
    return pltpu.CompilerParams(dimension_semantics=sem, vmem_limit_bytes=VMEM_LIMIT_BYTES, flags=flags)


def _const_spec(shape):
    nd = len(shape)
    return pl.BlockSpec(shape, lambda *_: (0,) * nd)


def _rms(x, w):
    y = x * lax.rsqrt(jnp.mean(x * x, axis=-1, keepdims=True) + RMS_EPS)
    return y * w


def _rms_bf16(x, w):
    return _rms(x, w).astype(BF16)


def _dot(a, b):
    return jnp.dot(a, b, preferred_element_type=F32)


def _dot_nt(a, b):
    return lax.dot_general(a, b, (((1,), (1,)), ((), ())), preferred_element_type=F32)


def _dot_tn(a, b):
    return lax.dot_general(a, b, (((0,), (0,)), ((), ())), preferred_element_type=F32)


def _tril_f32(n):
    r = lax.broadcasted_iota(jnp.int32, (n, n), 0)
    c = lax.broadcasted_iota(jnp.int32, (n, n), 1)
    return r >= c


def _cumsum_rows(x, q, split):
    if not split:
        return jnp.dot(_tril_f32(q).astype(F32), x, precision=HIGHEST, preferred_element_type=F32)
    tri = _tril_f32(q).astype(BF16)
    acc = None
    rest = x
    for _ in range(3):
        term = rest.astype(BF16)
        part = _dot(tri, term)
        acc = part if acc is None else acc + part
        rest = rest - term.astype(F32)
    return acc


def _silu(x):
    return x * jax.nn.sigmoid(x)


def _inproj_a_kernel(x_ref, nw_ref, wz, wxbc, wdt, wq, wk, wv, *outs, prompt):
    h = _rms(x_ref[...], nw_ref[...])
    hb = h.astype(BF16)
    z_o, xbc_o, dt_o, q_o = outs[:4]
    z_o[...] = _dot(hb, wz[...])
    xbc_o[...] = _dot(hb, wxbc[...])
    dt_o[...] = _dot(hb, wdt[...])
    if wq.dtype == F32:
        q_o[...] = jnp.dot(h, wq[...], precision=HIGHEST, preferred_element_type=F32)
    else:
        q_o[...] = _dot(hb, wq[...])
    k = _dot(hb, wk[...])
    v = _dot(hb, wv[...])
    if not prompt:
        k_o, v_o = outs[4:]
        k_o[...] = k
        v_o[...] = v
        return
    kr_o, vr_o, ka_o, vt_o, km_o = outs[4:]
    tm = k.shape[0]
    lane = lax.broadcasted_iota(jnp.int32, (tm, LANES), 1)
    key_aug = _key_aug_columns(tm)
    for hp in range(MOBA_HEADS // 2):
        pair = k[:, hp * LANES:(hp + 1) * LANES]
        for j, kh in enumerate((pair, pltpu.roll(pair, MOBA_HEAD_DIM, 1))):
            ka_o[2 * hp + j, 0] = jnp.where(lane < MOBA_HEAD_DIM, kh, key_aug).astype(BF16)
    k_t = k.T
    v_t = v.T
    for p in range(tm // PAGE_SIZE):
        toks = slice(p * PAGE_SIZE, (p + 1) * PAGE_SIZE)
        for h in range(MOBA_HEADS):
            dims = slice(h * MOBA_HEAD_DIM, (h + 1) * MOBA_HEAD_DIM)
            kr_o[p, h] = k_t[dims, toks]
            vr_o[p, h] = v_t[dims, toks]
    ones_rows = (lax.broadcasted_iota(jnp.int32, (MOBA_VT_ROWS - MOBA_HEAD_DIM, tm), 0) == 0).astype(F32)
    pieces = []
    for h in range(MOBA_HEADS):
        pieces += [v_t[h * MOBA_HEAD_DIM:(h + 1) * MOBA_HEAD_DIM, :], ones_rows]
    vt_o[0] = jnp.concatenate(pieces, axis=0).astype(BF16)
    km_o[0] = jnp.mean(k, axis=0, keepdims=True)


def _inproj_a(x, nw, ws, prompt):
    m = x.shape[0]
    tm = ROW_TILE if prompt else m
    assert m % tm == 0
    if prompt:
        assert tm == MOBA_BLOCK
    nt = m // tm
    widths = [w.shape[1] for w in ws]
    row = lambda i: (i, 0)
    in_specs = [pl.BlockSpec((tm, D_MODEL), row), _const_spec((1, D_MODEL))]
    in_specs += [_const_spec(w.shape) for w in ws]
    out_shape = [jax.ShapeDtypeStruct((m, n), F32) for n in widths[:4]]
    out_specs = [pl.BlockSpec((tm, n), row) for n in widths[:4]]
    if prompt:
        npg = m // PAGE_SIZE
        ppt = tm // PAGE_SIZE
        rows_shape = (npg, MOBA_HEADS, MOBA_HEAD_DIM, PAGE_SIZE)
        rows_spec = pl.BlockSpec((ppt, MOBA_HEADS, MOBA_HEAD_DIM, PAGE_SIZE), lambda i: (i, 0, 0, 0))
        out_shape += [jax.ShapeDtypeStruct(rows_shape, F32)] * 2
        out_specs += [rows_spec, rows_spec]
        out_shape += [jax.ShapeDtypeStruct((MOBA_HEADS, nt, tm, LANES), BF16),
                      jax.ShapeDtypeStruct((nt, MOBA_HEADS * MOBA_VT_ROWS, tm), BF16),
                      jax.ShapeDtypeStruct((nt, 1, MOBA_INNER), F32)]
        out_specs += [pl.BlockSpec((MOBA_HEADS, 1, tm, LANES), lambda i: (0, i, 0, 0)),
                      pl.BlockSpec((1, MOBA_HEADS * MOBA_VT_ROWS, tm), lambda i: (i, 0, 0)),
                      pl.BlockSpec((1, 1, MOBA_INNER), lambda i: (i, 0, 0))]
    else:
        out_shape += [jax.ShapeDtypeStruct((m, MOBA_INNER), F32)] * 2
        out_specs += [pl.BlockSpec((tm, MOBA_INNER), row)] * 2
    return pl.pallas_call(
        functools.partial(_inproj_a_kernel, prompt=prompt),
        grid=(nt,), in_specs=in_specs, out_specs=out_specs, out_shape=out_shape,
        compiler_params=_cparams(("arbitrary",)), name="inproj_a",
    )(x, nw, *ws)


def _norm_matmul_kernel(x_ref, nw_ref, *refs):
    n = len(refs) // 2
    hb = _rms_bf16(x_ref[...], nw_ref[...])
    for w, o in zip(refs[:n], refs[n:]):
        o[...] = _dot(hb, w[...])


def _norm_matmul(x, nw, ws):
    m = x.shape[0]
    tm = min(ROW_TILE, m)
    assert m % tm == 0
    row = lambda i: (i, 0)
    in_specs = [pl.BlockSpec((tm, D_MODEL), row), _const_spec((1, D_MODEL))]
    in_specs += [_const_spec(w.shape) for w in ws]
    return pl.pallas_call(
        _norm_matmul_kernel, grid=(m // tm,), in_specs=in_specs,
        out_specs=[pl.BlockSpec((tm, w.shape[1]), row) for w in ws],
        out_shape=[jax.ShapeDtypeStruct((m, w.shape[1]), F32) for w in ws],
        compiler_params=_cparams(("arbitrary",)), name="norm_matmul",
    )(x, nw, *ws)


def _outproj_kernel(*refs):
    n = (len(refs) - 2) // 2
    x_ref, o_ref = refs[2 * n], refs[2 * n + 1]
    acc = x_ref[...]
    for a, w in zip(refs[:n], refs[n:2 * n]):
        acc = acc + _dot(a[...], w[...])
    o_ref[...] = acc


def _outproj(acts, ws, x):
    m = x.shape[0]
    tm = min(ROW_TILE, m)
    assert m % tm == 0
    row = lambda i: (i, 0)
    in_specs = [pl.BlockSpec((tm, a.shape[1]), row) for a in acts]
    in_specs += [_const_spec(w.shape) for w in ws]
    in_specs += [pl.BlockSpec((tm, D_MODEL), row)]
    return pl.pallas_call(
        _outproj_kernel, grid=(m // tm,), in_specs=in_specs,
        out_specs=pl.BlockSpec((tm, D_MODEL), row),
        out_shape=jax.ShapeDtypeStruct((m, D_MODEL), F32),
        compiler_params=_cparams(("arbitrary",)), name="outproj",
    )(*acts, *ws, x)


def _ffn_kernel(*refs, tm, seq_len, final_norm):
    it = iter(refs)
    x_ref, nw_ref, wup_ref, cw_ref, cb_ref, wdn_ref = (next(it) for _ in range(6))
    if seq_len is None:
        hist_ref = next(it)
    else:
        h1_ref, h2_ref = next(it), next(it)
    nf_ref = next(it) if final_norm else None
    o_ref = next(it)
    y_ref = next(it) if final_norm else None
    tail_ref = next(it)
    up_s, act_s = next(it), next(it)

    i = pl.program_id(0)
    f = FFN_DIM
    if seq_len is None:
        @pl.when(i == 0)
        def _():
            up_s[0:SUBLANES, 0:f] = hist_ref[...]
    else:
        up_s[0:SUBLANES, 0:f] = jnp.zeros((SUBLANES, f), F32)

    x = x_ref[...]
    hb = _rms_bf16(x, nw_ref[...])
    up_s[SUBLANES:SUBLANES + tm, :] = _dot(hb, wup_ref[...])

    if seq_len is not None:
        t = lax.broadcasted_iota(jnp.int32, (tm, 1), 0) % seq_len
    for c in range(0, f, FFN_COL_CHUNK):
        cols = slice(c, c + FFN_COL_CHUNK)
        g0 = up_s[SUBLANES:SUBLANES + tm, cols]
        g1 = up_s[SUBLANES - 1:SUBLANES - 1 + tm, cols]
        g2 = up_s[SUBLANES - 2:SUBLANES - 2 + tm, cols]
        if seq_len is not None:
            g1 = jnp.where(t >= 1, g1, h1_ref[:, cols])
            g2 = jnp.where(t >= 2, g2, h2_ref[:, cols])
        val = up_s[SUBLANES:SUBLANES + tm, f + c:f + c + FFN_COL_CHUNK]
        conv = g2 * cw_ref[0:1, cols]
        conv = conv + g1 * cw_ref[1:2, cols]
        conv = conv + g0 * cw_ref[2:3, cols]
        conv = conv + cb_ref[:, cols]
        act_s[:, cols] = (_silu(conv) * val).astype(BF16)

    out = x + _dot(act_s[...], wdn_ref[...])
    o_ref[...] = out
    if final_norm:
        y = out * lax.rsqrt(jnp.mean(out * out, axis=-1, keepdims=True) + RMS_EPS)
        y_ref[...] = y * nf_ref[...]

    if seq_len is None:
        last = up_s[tm:tm + SUBLANES, 0:f]
        up_s[0:SUBLANES, 0:f] = last

        @pl.when(i == pl.num_programs(0) - 1)
        def _():
            tail_ref[...] = last
    else:
        tail_ref[...] = up_s[SUBLANES:SUBLANES + tm, 0:f]


def _ffn(x, nw, wup, cw, cb, wdn, hist, seq_len, nf):
    assert FFN_CONV == 3
    m = x.shape[0]
    f = FFN_DIM
    final_norm = nf is not None
    row = lambda i: (i, 0)
    if seq_len is None:
        tm = ROW_TILE
        hist8 = jnp.zeros((SUBLANES, f), F32).at[SUBLANES - 2:].set(hist)
        extra = [hist8]
        extra_specs = [_const_spec((SUBLANES, f))]
        tail_shape = (SUBLANES, f)
    else:
        tm = m
        nb = m // seq_len
        assert seq_len >= 2 and nb * seq_len == m
        h1 = jnp.zeros((nb, seq_len, f), F32).at[:, 0].set(hist[:, 1]).reshape(m, f)
        h2 = jnp.zeros((nb, seq_len, f), F32).at[:, 0:2].set(hist).reshape(m, f)
        extra = [h1, h2]
        extra_specs = [_const_spec((m, f))] * 2
        tail_shape = (m, f)
    assert m % tm == 0
    ins = [x, nw, wup, cw, cb, wdn] + extra
    in_specs = [pl.BlockSpec((tm, D_MODEL), row), _const_spec((1, D_MODEL)), _const_spec(wup.shape),
                _const_spec(cw.shape), _const_spec(cb.shape), _const_spec(wdn.shape)] + extra_specs
    out_shape = [jax.ShapeDtypeStruct((m, D_MODEL), F32)]
    out_specs = [pl.BlockSpec((tm, D_MODEL), row)]
    if final_norm:
        ins.append(nf)
        in_specs.append(_const_spec((1, D_MODEL)))
        out_shape.append(jax.ShapeDtypeStruct((m, D_MODEL), F32))
        out_specs.append(pl.BlockSpec((tm, D_MODEL), row))
    out_shape.append(jax.ShapeDtypeStruct(tail_shape, F32))
    out_specs.append(_const_spec(tail_shape))
    res = pl.pallas_call(
        functools.partial(_ffn_kernel, tm=tm, seq_len=seq_len, final_norm=final_norm),
        grid=(m // tm,), in_specs=in_specs, out_specs=out_specs, out_shape=out_shape,
        scratch_shapes=[pltpu.VMEM((SUBLANES + tm, 2 * f), F32), pltpu.VMEM((tm, f), BF16)],
        compiler_params=_cparams(("arbitrary",)), name="conv_ffn",
    )(*ins)
    if final_norm:
        xo, y, tail = res
    else:
        (xo, tail), y = res, None
    if seq_len is None:
        new_hist = tail[SUBLANES - 2:]
    else:
        new_hist = tail.reshape(m // seq_len, seq_len, f)[:, seq_len - 2:]
    return xo, y, new_hist


def _ssd_kernel(z_ref, xbc_ref, dt_ref, hist_ref, s0_ref, cw_ref, cb_ref, dtb_ref, alog_ref, dsk_ref,
                nw_ref, y_ref, hist_o, s_o, cbuf, st, ybuf, *, q, seq_len, n_chunks):
    c = pl.program_id(1)
    hrows = SSD_CONV - 1
    h0 = SUBLANES - hrows

    @pl.when(c == 0)
    def _():
        cbuf[0:SUBLANES, :] = hist_ref[0]
        st[...] = s0_ref[0]

    cbuf[SUBLANES:SUBLANES + q, :] = xbc_ref[0]
    conv = cbuf[h0:h0 + q, :] * cw_ref[0:1, :]
    for j in range(1, SSD_CONV):
        conv = conv + cbuf[h0 + j:h0 + j + q, :] * cw_ref[j:j + 1, :]
    xc = _silu(conv + cb_ref[...])

    xdt_raw = dt_ref[0] + dtb_ref[...]
    dt = jnp.maximum(xdt_raw, 0.0) + jnp.log1p(jnp.exp(-jnp.abs(xdt_raw)))
    if seq_len % q != 0:
        row = c * q + lax.broadcasted_iota(jnp.int32, (q, 1), 0)
        dt = jnp.where(row < seq_len, dt, 0.0)
    a = -jnp.exp(alog_ref[...])
    acum = _cumsum_rows(dt * a, q, split=False)
    acum_t = acum.T
    a_last = acum[q - 1:q, :]
    e_acum = jnp.exp(acum)
    w_in = jnp.exp(a_last - acum) * dt
    e_last = jnp.exp(a_last)
    tril = _tril_f32(q)

    rep = SSD_HEADS // SSD_GROUPS
    cbs, bms, cms = [], [], []
    for g in range(SSD_GROUPS):
        bm = xc[:, SSD_INNER + g * SSD_STATE:SSD_INNER + (g + 1) * SSD_STATE]
        cm = xc[:, SSD_INNER + SSD_GN + g * SSD_STATE:SSD_INNER + SSD_GN + (g + 1) * SSD_STATE]
        bms.append(bm)
        cms.append(cm)
        cbs.append(_dot_nt(cm.astype(BF16), bm.astype(BF16)))

    for h in range(SSD_HEADS):
        g = h // rep
        xs = xc[:, h * SSD_HEAD_DIM:(h + 1) * SSD_HEAD_DIM]
        seg = acum[:, h:h + 1] - acum_t[h:h + 1, :]
        decay = jnp.exp(jnp.where(tril, seg, -jnp.inf))
        y = _dot((cbs[g] * decay).astype(BF16), (xs * dt[:, h:h + 1]).astype(BF16))
        s_h = st[h]
        y = y + _dot_nt((cms[g] * e_acum[:, h:h + 1]).astype(BF16), s_h.astype(BF16))
        st[h] = s_h * e_last[:, h:h + 1] + _dot_tn((xs * w_in[:, h:h + 1]).astype(BF16),
                                                    bms[g].astype(BF16))
        ybuf[:, h * SSD_HEAD_DIM:(h + 1) * SSD_HEAD_DIM] = y + xs * dsk_ref[:, h:h + 1]

    yz = ybuf[...] * _silu(z_ref[0])
    gw = SSD_INNER // SSD_GROUPS
    for g in range(SSD_GROUPS):
        yg = yz[:, g * gw:(g + 1) * gw]
        yn = yg * lax.rsqrt(jnp.mean(yg * yg, axis=-1, keepdims=True) + RMS_EPS)
        y_ref[0, :, g * gw:(g + 1) * gw] = (yn * nw_ref[:, g * gw:(g + 1) * gw]).astype(BF16)

    @pl.when(c == n_chunks - 1)
    def _():
        l_last = seq_len - (n_chunks - 1) * q
        hist_o[0] = cbuf[l_last:l_last + SUBLANES, :]
        s_o[0] = st[...]

    cbuf[0:SUBLANES, :] = cbuf[q:q + SUBLANES, :]


def _pad_lanes(v, fill=0.0):
    return jnp.full((1, LANES), fill, F32).at[0, :v.shape[0]].set(v.astype(F32))


def _ssd(z, xbc, dtr, hist, s0, conv_w, conv_b, dt_bias, a_log, d_skip, norm_w, seq_len):
    b, lp, _ = z.shape
    q = SSD_CHUNK
    assert lp % q == 0 and lp - seq_len < q
    hrows = SSD_CONV - 1
    hist8 = jnp.zeros((b, SUBLANES, SSD_CONV_DIM), F32).at[:, SUBLANES - hrows:].set(hist)
    cw8 = jnp.zeros((SUBLANES, SSD_CONV_DIM), F32).at[:SSD_CONV].set(conv_w)
    seq = lambda i, c: (i, c, 0)
    per_b3 = lambda i, c: (i, 0, 0)
    per_b4 = lambda i, c: (i, 0, 0, 0)
    in_specs = [pl.BlockSpec((1, q, SSD_INNER), seq), pl.BlockSpec((1, q, SSD_CONV_DIM), seq),
                pl.BlockSpec((1, q, LANES), seq), pl.BlockSpec((1, SUBLANES, SSD_CONV_DIM), per_b3),
                pl.BlockSpec((1, SSD_HEADS, SSD_HEAD_DIM, SSD_STATE), per_b4),
                _const_spec((SUBLANES, SSD_CONV_DIM)), _const_spec((1, SSD_CONV_DIM)),
                _const_spec((1, LANES)), _const_spec((1, LANES)), _const_spec((1, LANES)),
                _const_spec((1, SSD_INNER))]
    out_shape = [jax.ShapeDtypeStruct((b, lp, SSD_INNER), BF16),
                 jax.ShapeDtypeStruct((b, SUBLANES, SSD_CONV_DIM), F32),
                 jax.ShapeDtypeStruct((b, SSD_HEADS, SSD_HEAD_DIM, SSD_STATE), F32)]
    out_specs = [pl.BlockSpec((1, q, SSD_INNER), seq), pl.BlockSpec((1, SUBLANES, SSD_CONV_DIM), per_b3),
                 pl.BlockSpec((1, SSD_HEADS, SSD_HEAD_DIM, SSD_STATE), per_b4)]
    y, hist_o, s_o = pl.pallas_call(
        functools.partial(_ssd_kernel, q=q, seq_len=seq_len, n_chunks=lp // q),
        grid=(b, lp // q), in_specs=in_specs, out_specs=out_specs, out_shape=out_shape,
        scratch_shapes=[pltpu.VMEM((SUBLANES + q, SSD_CONV_DIM), F32),
                        pltpu.VMEM((SSD_HEADS, SSD_HEAD_DIM, SSD_STATE), F32),
                        pltpu.VMEM((q, SSD_INNER), F32)],
        compiler_params=_cparams(("arbitrary", "arbitrary")), name="ssd",
    )(z, xbc, dtr, hist8, s0, cw8, conv_b.reshape(1, -1), _pad_lanes(dt_bias), _pad_lanes(a_log),
      _pad_lanes(d_skip), norm_w.reshape(1, -1))
    return y, hist_o[:, SUBLANES - hrows:], s_o


def _gla_kernel(q_ref, fx_ref, iv_ref, g_ref, gam_ref, nw_ref, s0_ref, o_ref, s_o, st_t, *, q, seq_len,
                layer):
    c = pl.program_id(1)
    n_chunks = pl.num_programs(1)

    @pl.when(c == 0)
    def _():
        for h in range(HGRN_HEADS):
            st_t[h] = s0_ref[0, h].T

    rows = [gam_ref[l:l + 1, :] for l in range(DEPTH)]
    mx = functools.reduce(jnp.maximum, rows)
    es = [jnp.exp(r - mx) for r in rows]
    lb = sum(es[1:layer + 1]) / sum(es) if layer >= 1 else jnp.zeros_like(mx)

    f = lb + (1.0 - lb) * jax.nn.sigmoid(fx_ref[0])
    if seq_len % q != 0:
        row = c * q + lax.broadcasted_iota(jnp.int32, (q, 1), 0)
        f = jnp.where(row < seq_len, f, 1.0)
    kk = 1.0 - f
    b = _cumsum_rows(jnp.log(f), q, split=True)
    b_last = b[q - 1:q, :]
    qd = q_ref[0] * jnp.exp(b)
    kd = kk * jnp.exp(-b)
    kdec = kk * jnp.exp(b_last - b)
    e_last = jnp.exp(b_last)
    tril = _tril_f32(q)
    iv = iv_ref[0]

    ks = [slice(h * HGRN_KEY_DIM, (h + 1) * HGRN_KEY_DIM) for h in range(HGRN_HEADS)]
    vs = [slice(h * HGRN_VAL_DIM, (h + 1) * HGRN_VAL_DIM) for h in range(HGRN_HEADS)]
    qd_b = [qd[:, s].astype(BF16) for s in ks]
    v_b = [iv[:, s].astype(BF16) for s in vs]
    att = [_dot_nt(qd_b[h], kd[:, ks[h]].astype(BF16)) for h in range(HGRN_HEADS)]
    o_state = [_dot_nt(qd_b[h], st_t[h].astype(BF16)) for h in range(HGRN_HEADS)]
    kv = [_dot_tn(v_b[h], kdec[:, ks[h]].astype(BF16)) for h in range(HGRN_HEADS)]
    for h in range(HGRN_HEADS):
        o = _dot(jnp.where(tril, att[h], 0.0).astype(BF16), v_b[h]) + o_state[h]
        st_t[h] = st_t[h] * e_last[:, ks[h]] + kv[h]
        on = o * lax.rsqrt(jnp.mean(o * o, axis=-1, keepdims=True) + RMS_EPS) * nw_ref[...]
        o_ref[0, :, vs[h]] = (on * _silu(g_ref[0, :, vs[h]])).astype(BF16)

    @pl.when(c == n_chunks - 1)
    def _():
        for h in range(HGRN_HEADS):
            s_o[0, h] = st_t[h].T


def _gla(qa, fx, iv, g, gamma, norm_w, s0, seq_len, layer):
    b, lp, _ = qa.shape
    q = HGRN_CHUNK
    assert lp % q == 0 and lp - seq_len < q
    gam8 = jnp.zeros((SUBLANES, HGRN_K), F32).at[:DEPTH].set(gamma)
    seq = lambda i, c: (i, c, 0)
    per_b4 = lambda i, c: (i, 0, 0, 0)
    st_shape = (HGRN_HEADS, HGRN_KEY_DIM, HGRN_VAL_DIM)
    in_specs = [pl.BlockSpec((1, q, HGRN_K), seq), pl.BlockSpec((1, q, HGRN_K), seq),
                pl.BlockSpec((1, q, HGRN_V), seq), pl.BlockSpec((1, q, HGRN_V), seq),
                _const_spec((SUBLANES, HGRN_K)), _const_spec((1, HGRN_VAL_DIM)),
                pl.BlockSpec((1,) + st_shape, per_b4)]
    o, s_o = pl.pallas_call(
        functools.partial(_gla_kernel, q=q, seq_len=seq_len, layer=layer),
        grid=(b, lp // q), in_specs=in_specs,
        out_specs=[pl.BlockSpec((1, q, HGRN_V), seq), pl.BlockSpec((1,) + st_shape, per_b4)],
        out_shape=[jax.ShapeDtypeStruct((b, lp, HGRN_V), BF16), jax.ShapeDtypeStruct((b,) + st_shape, F32)],
        scratch_shapes=[pltpu.VMEM((HGRN_HEADS, HGRN_VAL_DIM, HGRN_KEY_DIM), F32)],
        compiler_params=_cparams(("arbitrary", "arbitrary")), name="gla",
    )(qa, fx, iv, g, gam8, norm_w.reshape(1, -1), s0)
    return o, s_o


def _select_topk_rows(gate_t, n_valid, nblk, width):
    blk = lax.broadcasted_iota(jnp.int32, (nblk, width), 0)
    blk_f = blk.astype(F32)
    g = jnp.where(blk < n_valid, gate_t, -jnp.inf)
    sel = jnp.zeros((nblk, width), F32)
    for _ in range(MOBA_TOPK):
        mx = jnp.max(g, axis=0, keepdims=True)
        first = jnp.min(jnp.where(g == mx, blk_f, float(nblk)), axis=0, keepdims=True)
        pick = (blk_f == first) & (mx > -jnp.inf)
        sel = jnp.where(pick, 1.0, sel)
        g = jnp.where(pick, -jnp.inf, g)
    return jnp.where(sel > 0.0, 0.0, -jnp.inf)


def _key_aug_columns(tm):
    lane = lax.broadcasted_iota(jnp.int32, (tm, LANES), 1)
    r = lax.broadcasted_iota(jnp.int32, (tm, LANES), 0).astype(F32)
    return jnp.where((lane >= MOBA_HEAD_DIM) & (lane < MOBA_HEAD_DIM + ALIBI_PARTS), r, 0.0)


def _moba_prompt_kernel(slopes_ref, q_ref, ka_ref, vt_ref, km_ref, o_ref, brow_s, ubuf, *, nblk):
    hp = pl.program_id(0)
    i = pl.program_id(1)
    tq = MOBA_BLOCK
    d = MOBA_HEAD_DIM
    q = q_ref[...]
    lane = lax.broadcasted_iota(jnp.int32, (tq, 2 * d), 1)
    causal = (lax.broadcasted_iota(jnp.int32, (MOBA_BLOCK, tq), 0)
              <= lax.broadcasted_iota(jnp.int32, (MOBA_BLOCK, tq), 1))
    blk_f = lax.broadcasted_iota(jnp.int32, (nblk, tq), 0).astype(F32)
    col_f = lax.broadcasted_iota(jnp.int32, (nblk, tq), 1).astype(F32)
    aug_row = lax.broadcasted_iota(jnp.int32, (d, tq), 0)
    i_f = i.astype(F32)

    rhs = []
    for j in range(2):
        a2 = slopes_ref[2 * hp + j] * LOG2E
        qm_t = jnp.where((lane >= j * d) & (lane < (j + 1) * d), q, 0.0).T
        gate_t = jnp.dot(km_ref[...], qm_t, precision=HIGHEST, preferred_element_type=F32)
        sel = _select_topk_rows(gate_t, i, nblk, tq)
        brow_s[j] = sel + a2 * (MOBA_BLOCK * (blk_f - i_f) - col_f)
        rest = jnp.full((d, tq), a2, F32)
        aug = jnp.zeros((d, tq), F32)
        for part in range(ALIBI_PARTS):
            term = rest.astype(BF16).astype(F32)
            aug = jnp.where(aug_row == part, term, aug)
            rest = rest - term
        q_t = qm_t[j * d:(j + 1) * d, :] * (d ** -0.5 * LOG2E)
        rhs.append(jnp.concatenate([q_t, aug], axis=0).astype(BF16))

    def block_update(st, u, b_row, v_t):
        m_run, o_t = st
        m_new = jnp.maximum(m_run, jnp.max(u, axis=0, keepdims=True) + b_row)
        alpha = jnp.exp2(m_run - m_new)
        p = jnp.exp2(u - (m_new - b_row))
        return m_new, alpha * o_t + _dot(v_t, p.astype(BF16))

    def score(n, slot):
        for j in range(2):
            ubuf[slot, j] = _dot(ka_ref[j, n], rhs[j])

    def consume(n, slot, sts, own):
        new = []
        for j in range(2):
            u = ubuf[slot, j]
            if own:
                u = jnp.where(causal, u, -jnp.inf)
                b_row = -(slopes_ref[2 * hp + j] * LOG2E) * col_f[0:1, :]
            else:
                b_row = brow_s[j, pl.ds(n, 1), :]
            new.append(block_update(sts[j], u, b_row, vt_ref[n, j * MOBA_VT_ROWS:(j + 1) * MOBA_VT_ROWS, :]))
        return tuple(new)

    def step(n, slot, slot_ahead, sts):
        score(jnp.minimum(n + MOBA_AHEAD, i), slot_ahead)
        return consume(n, slot, sts, False)

    def group(base, unroll, sts):
        for k in range(unroll):
            sts = step(base + k, k % MOBA_SLOTS, (k + MOBA_AHEAD) % MOBA_SLOTS, sts)
        return sts

    init = (jnp.full((1, tq), NEG_BIG, F32), jnp.zeros((MOBA_VT_ROWS, tq), F32))
    for n0 in range(MOBA_AHEAD):
        score(jnp.minimum(n0, i), n0)
    n_big = i // MOBA_UNROLL
    sts = lax.fori_loop(0, n_big, lambda g, s: group(g * MOBA_UNROLL, MOBA_UNROLL, s), (init, init))
    base = n_big * MOBA_UNROLL
    n_small = (i - base) // MOBA_SLOTS
    sts = lax.fori_loop(0, n_small, lambda g, s: group(base + g * MOBA_SLOTS, MOBA_SLOTS, s), sts)
    base = base + n_small * MOBA_SLOTS
    rem = i - base
    for k in range(MOBA_SLOTS - 1):
        sts = lax.cond(rem > k, lambda s, k=k: step(base + k, k, (k + MOBA_AHEAD) % MOBA_SLOTS, s),
                       lambda s: s, sts)
    outs = []
    for _, o_t in consume(i, rem, sts, True):
        outs.append(o_t[0:d, :] / o_t[d:d + 1, :])
    o_ref[...] = jnp.concatenate(outs, axis=0).T.astype(BF16)


def _alibi_slopes():
    return jnp.asarray(np.exp2(-8.0 * np.arange(1, MOBA_HEADS + 1) / MOBA_HEADS), dtype=F32)


def _moba_prompt(q, ka, vt, km):
    s = q.shape[0]
    nblk = s // MOBA_BLOCK
    assert nblk * MOBA_BLOCK == s and 2 * MOBA_HEAD_DIM == LANES
    pairs = MOBA_HEADS // 2
    return pl.pallas_call(
        functools.partial(_moba_prompt_kernel, nblk=nblk),
        grid=(pairs, nblk),
        in_specs=[pl.BlockSpec(memory_space=pltpu.SMEM),
                  pl.BlockSpec((MOBA_BLOCK, LANES), lambda hp, i: (i, hp)),
                  pl.BlockSpec((2, nblk, MOBA_BLOCK, LANES), lambda hp, i: (hp, 0, 0, 0)),
                  pl.BlockSpec((nblk, 2 * MOBA_VT_ROWS, MOBA_BLOCK), lambda hp, i: (0, hp, 0)),
                  pl.BlockSpec((nblk, LANES), lambda hp, i: (0, hp))],
        out_specs=pl.BlockSpec((MOBA_BLOCK, LANES), lambda hp, i: (i, hp)),
        out_shape=jax.ShapeDtypeStruct((s, MOBA_INNER), BF16),
        scratch_shapes=[pltpu.VMEM((2, nblk, MOBA_BLOCK), F32),
                        pltpu.VMEM((MOBA_SLOTS, 2, MOBA_BLOCK, MOBA_BLOCK), F32)],
        compiler_params=_cparams(("arbitrary", "arbitrary")), name="moba_prompt",
    )(_alibi_slopes(), q, ka, vt, km)


PAGES_PER_STEP = 32


def _kmean_kernel(pt_ref, *refs):
    del pt_ref
    o_ref = refs[-1]
    s = pl.program_id(1)
    ppb = MOBA_BLOCK // PAGE_SIZE
    bps = PAGES_PER_STEP // ppb

    @pl.when(s == 0)
    def _():
        o_ref[...] = jnp.zeros(o_ref.shape, F32)

    acc = o_ref[0]
    lane = lax.broadcasted_iota(jnp.int32, acc.shape, 2)
    for m in range(bps):
        x = refs[m * ppb][0]
        for p in range(1, ppb):
            x = x + refs[m * ppb + p][0]
        mean = jnp.sum(x, axis=-1, keepdims=True) * (1.0 / MOBA_BLOCK)
        acc = jnp.where(lane == s * bps + m, mean, acc)
    o_ref[0] = acc


def _sample_kmean(k_pool_t, page_table):
    db, n_pages = page_table.shape
    ppb = MOBA_BLOCK // PAGE_SIZE
    n_full = n_pages // ppb
    assert n_full * ppb == n_pages and n_pages % PAGES_PER_STEP == 0
    steps = n_pages // PAGES_PER_STEP
    blk = (1, MOBA_HEADS, MOBA_HEAD_DIM, PAGE_SIZE)

    def page_spec(p):
        return pl.BlockSpec(blk, lambda b, s, pt: (pt[b * n_pages + s * PAGES_PER_STEP + p], 0, 0, 0))

    out_blk = (1, MOBA_HEADS, MOBA_HEAD_DIM, n_full)
    grid_spec = pltpu.PrefetchScalarGridSpec(
        num_scalar_prefetch=1, grid=(db, steps),
        in_specs=[page_spec(p) for p in range(PAGES_PER_STEP)],
        out_specs=pl.BlockSpec(out_blk, lambda b, s, pt: (b, 0, 0, 0)))
    return pl.pallas_call(
        _kmean_kernel, grid_spec=grid_spec,
        out_shape=jax.ShapeDtypeStruct((db,) + out_blk[1:], F32),
        compiler_params=_cparams(("arbitrary", "arbitrary")), name="sample_kmean",
    )(page_table.reshape(-1), *([k_pool_t] * PAGES_PER_STEP))


def _sample_select_kernel(q_ref, km_ref, idx_ref, *, n_full):
    t8 = q_ref.shape[2]
    blk = lax.broadcasted_iota(jnp.int32, (t8, n_full), 1).astype(F32)
    lane = lax.broadcasted_iota(jnp.int32, (t8, LANES), 1)
    for h in range(MOBA_HEADS):
        g = jnp.dot(q_ref[0, h], km_ref[0, h], precision=HIGHEST, preferred_element_type=F32)
        out = jnp.zeros((t8, LANES), jnp.int32)
        for k in range(MOBA_TOPK):
            mx = jnp.max(g, axis=-1, keepdims=True)
            first = jnp.min(jnp.where(g == mx, blk, float(n_full)), axis=-1, keepdims=True)
            out = jnp.where(lane == k, first.astype(jnp.int32), out)
            g = jnp.where(blk == first, -jnp.inf, g)
        idx_ref[0, h] = out


def _sample_select(qh, km):
    db, h, t8, d = qh.shape
    n_full = km.shape[3]
    assert n_full >= MOBA_TOPK
    b4 = lambda b: (b, 0, 0, 0)
    return pl.pallas_call(
        functools.partial(_sample_select_kernel, n_full=n_full), grid=(db,),
        in_specs=[pl.BlockSpec((1, h, t8, d), b4), pl.BlockSpec((1, h, d, n_full), b4)],
        out_specs=pl.BlockSpec((1, h, t8, LANES), b4),
        out_shape=jax.ShapeDtypeStruct((db, h, t8, LANES), jnp.int32),
        compiler_params=_cparams(("arbitrary",)), name="sample_select",
    )(qh, km)


def _sample_attn_kernel(idx_ref, phys_ref, slopes_ref, q_ref, kn_ref, vn_ref, *refs, ds, past):
    del phys_ref
    ppb = MOBA_BLOCK // PAGE_SIZE
    n_sel = MOBA_TOPK * ppb
    k_refs = refs[:ds * n_sel]
    v_refs = refs[ds * n_sel:2 * ds * n_sel]
    o_ref = refs[2 * ds * n_sel]
    b = pl.program_id(0)
    h = pl.program_id(1)
    slope = slopes_ref[h]
    t8 = q_ref.shape[3]
    c_page = lax.broadcasted_iota(jnp.int32, (1, PAGE_SIZE), 1).astype(F32)
    c_new = lax.broadcasted_iota(jnp.int32, (1, t8), 1)
    qf = q_ref[0, 0] * (MOBA_HEAD_DIM ** -0.5)
    k_new = kn_ref[0, 0]
    v_new = vn_ref[0, 0]
    o_ref[0, 0] = jnp.zeros((MOBA_HEAD_DIM, t8), F32)
    for t in range(ds):
        q_t = qf[:, t:t + 1]
        t_pos = float(past + t)
        scores = []
        for sp in range(n_sel):
            slot, p = divmod(sp, ppb)
            blk_idx = idx_ref[((b * MOBA_HEADS + h) * ds + t) * MOBA_TOPK + slot]
            pos0 = (blk_idx * MOBA_BLOCK + p * PAGE_SIZE).astype(F32)
            s = jnp.sum(k_refs[t * n_sel + sp][0, 0] * q_t, axis=0, keepdims=True)
            scores.append(s - slope * (t_pos - (pos0 + c_page)))
        s_new = jnp.sum(k_new * q_t, axis=0, keepdims=True)
        s_new = s_new - slope * (t_pos - (float(past) + c_new.astype(F32)))
        s_new = jnp.where(c_new <= t, s_new, -jnp.inf)
        mx = jnp.max(s_new, axis=-1, keepdims=True)
        for s in scores:
            mx = jnp.maximum(mx, jnp.max(s, axis=-1, keepdims=True))
        p_new = jnp.exp(s_new - mx)
        l = jnp.sum(p_new, axis=-1, keepdims=True)
        o = jnp.sum(p_new * v_new, axis=-1, keepdims=True)
        acc = jnp.zeros((MOBA_HEAD_DIM, PAGE_SIZE), F32)
        for sp in range(n_sel):
            pr = jnp.exp(scores[sp] - mx)
            l = l + jnp.sum(pr, axis=-1, keepdims=True)
            acc = acc + pr * v_refs[t * n_sel + sp][0, 0]
        o = o + jnp.sum(acc, axis=-1, keepdims=True)
        o_ref[0, 0, :, t:t + 1] = o / l


def _sample_attn(qh, kh, vh, idx, k_pool, v_pool, page_table, ds):
    db, h, d, t8 = qh.shape
    n_pages = page_table.shape[1]
    ppb = MOBA_BLOCK // PAGE_SIZE
    assert n_pages % ppb == 0
    past = n_pages * PAGE_SIZE
    n_sel = MOBA_TOPK * ppb
    logical = idx[..., None] * ppb + jnp.arange(ppb, dtype=jnp.int32)
    phys = page_table[jnp.arange(db)[:, None, None, None], logical.reshape(db, h, ds, n_sel)]
    blk = (1, 1, d, PAGE_SIZE)

    def page_spec(t, sp):
        return pl.BlockSpec(
            blk, lambda b, hh, idx_r, phys_r: (phys_r[((b * h + hh) * ds + t) * n_sel + sp], hh, 0, 0))

    tok = pl.BlockSpec((1, 1, d, t8), lambda b, hh, *_: (b, hh, 0, 0))
    page_specs = [page_spec(t, sp) for t in range(ds) for sp in range(n_sel)]
    grid_spec = pltpu.PrefetchScalarGridSpec(
        num_scalar_prefetch=2, grid=(db, h),
        in_specs=[pl.BlockSpec(memory_space=pltpu.SMEM), tok, tok, tok] + page_specs + page_specs,
        out_specs=tok)
    return pl.pallas_call(
        functools.partial(_sample_attn_kernel, ds=ds, past=past), grid_spec=grid_spec,
        out_shape=jax.ShapeDtypeStruct((db, h, d, t8), F32),
        compiler_params=_cparams(("arbitrary", "arbitrary")), name="sample_attn",
    )(idx.reshape(-1), phys.reshape(-1), _alibi_slopes(), qh, kh, vh,
      *([k_pool] * (ds * n_sel)), *([v_pool] * (ds * n_sel)))


def _split_w_in_a(w, f32_query):
    wb = w.astype(BF16)
    wdt = jnp.zeros((D_MODEL, LANES), BF16).at[:, :SSD_HEADS].set(wb[:, OFF_DT:OFF_Q])
    wq = w[:, OFF_Q:OFF_K] if f32_query else wb[:, OFF_Q:OFF_K]
    return [wb[:, :OFF_XBC], wb[:, OFF_XBC:OFF_DT], wdt, wq, wb[:, OFF_K:OFF_V], wb[:, OFF_V:]]


def _pad_seq(a, b, l, lp):
    a = a.reshape(b, l, a.shape[-1])
    return a if lp == l else jnp.pad(a, ((0, 0), (0, lp - l), (0, 0)))


def _round_up(n, m):
    return -(-n // m) * m


def _heads(a, b, l):
    return a.reshape(b, l, MOBA_HEADS, MOBA_HEAD_DIM).transpose(0, 2, 1, 3)


def _pad_tokens(a):
    l = a.shape[-2]
    return jnp.pad(a, ((0, 0),) * (a.ndim - 2) + ((0, _round_up(l, SUBLANES) - l), (0, 0)))


def _trunk(x, bsz, length, prompt, caches, p):
    ssm_conv0, ssm0, hgrn0, ffn_conv0 = caches[:4]
    m = bsz * length
    k_rows = v_rows = ssm_c = ssm_s = hgrn_s = None
    ffn_c = []
    y = None
    for layer in range(DEPTH):
        nw = p['norm_mix'][layer].reshape(1, -1)
        if layer % 2 == 0:
            ia = layer // 2
            assert ia == 0
            ws = _split_w_in_a(p['w_in_a'][ia], f32_query=not prompt)
            outs = _inproj_a(x, nw, ws, prompt)
            z, xbc, dtr, q = outs[:4]
            lp = _round_up(length, SSD_CHUNK)
            y_ssd, hist, s_new = _ssd(
                _pad_seq(z, bsz, length, lp), _pad_seq(xbc, bsz, length, lp), _pad_seq(dtr, bsz, length, lp),
                ssm_conv0[ia], ssm0[ia], p['ssd_conv_w'][ia], p['ssd_conv_b'][ia], p['ssd_dt_bias'][ia],
                p['ssd_a_log'][ia], p['ssd_d'][ia], p['ssd_norm_w'][ia], length)
            y_ssd = y_ssd[:, :length].reshape(m, SSD_INNER)
            if prompt:
                assert bsz == 1
                kr, vr, ka, vt, km = outs[4:]
                o_att = _moba_prompt(q, ka, vt, km.reshape(-1, MOBA_INNER))
                k_rows, v_rows = jnp.swapaxes(kr, -1, -2)[None], jnp.swapaxes(vr, -1, -2)[None]
            else:
                k, v = outs[4:]
                k_pool, v_pool, page_table = caches[4:]
                k_pool_t = jnp.swapaxes(k_pool[ia], -1, -2)
                v_pool_t = jnp.swapaxes(v_pool[ia], -1, -2)
                qh, kh, vh = (_pad_tokens(_heads(t, bsz, length)) for t in (q, k, v))
                qt, kt, vt = (jnp.swapaxes(t, -1, -2) for t in (qh, kh, vh))
                km = _sample_kmean(k_pool_t, page_table)
                idx = _sample_select(qh, km)[:, :, :length, :MOBA_TOPK]
                o = _sample_attn(qt, kt, vt, idx, k_pool_t, v_pool_t, page_table, length)
                o_att = o[..., :length].transpose(0, 3, 1, 2).reshape(m, MOBA_INNER).astype(BF16)
                k_rows, v_rows = kh[:, :, :length], vh[:, :, :length]
            wo = p['w_out_a'][ia].astype(BF16)
            x = _outproj([y_ssd, o_att], [wo[:SSD_INNER], wo[SSD_INNER:]], x)
            ssm_c, ssm_s = hist, s_new
        else:
            ic = layer // 2
            assert ic == 0
            wc = p['w_in_c'][ic].astype(BF16)
            ws = [wc[:, :HGRN_K], wc[:, HGRN_K:2 * HGRN_K], wc[:, 2 * HGRN_K:2 * HGRN_K + HGRN_V],
                  wc[:, 2 * HGRN_K + HGRN_V:]]
            qa, fx, iv, g = _norm_matmul(x, nw, ws)
            lp = _round_up(length, HGRN_CHUNK)
            o, s_new = _gla(*(_pad_seq(t, bsz, length, lp) for t in (qa, fx, iv, g)),
                            p['hgrn_lb_gamma'], p['hgrn_norm_w'][ic], hgrn0[ic], length, layer)
            o = o[:, :length].reshape(m, HGRN_V)
            x = _outproj([o], [p['w_out_c'][ic].astype(BF16)], x)
            hgrn_s = s_new
        nf = p['norm_final'].reshape(1, -1) if layer == DEPTH - 1 else None
        hist = ffn_conv0[layer, 0] if prompt else ffn_conv0[layer]
        x, y, fh = _ffn(x, p['norm_ffn'][layer].reshape(1, -1), p['ffn_w_up'][layer].astype(BF16),
                        p['ffn_conv_w'][layer], p['ffn_conv_b'][layer].reshape(1, -1),
                        p['ffn_w_down'][layer].astype(BF16), hist, None if prompt else length, nf)
        ffn_c.append(fh[None] if prompt else fh)
    return (y.reshape(bsz, length, D_MODEL), k_rows[None], v_rows[None], ssm_s[None], ssm_c[None],
            hgrn_s[None], jnp.stack(ffn_c))


def kernel(x_prompt, x_sample, cache_k_pool, cache_v_pool, page_table, state_ssm, state_ssm_conv, state_hgrn,
           state_ffn_conv, norm_mix, norm_ffn, norm_final, w_in_a, w_out_a, ssd_conv_w, ssd_conv_b,
           ssd_dt_bias, ssd_a_log, ssd_d, ssd_norm_w, w_in_c, w_out_c, hgrn_lb_gamma, hgrn_norm_w, ffn_w_up,
           ffn_conv_w, ffn_conv_b, ffn_w_down):
    p = dict(norm_mix=norm_mix, norm_ffn=norm_ffn, norm_final=norm_final, w_in_a=w_in_a, w_out_a=w_out_a,
             ssd_conv_w=ssd_conv_w, ssd_conv_b=ssd_conv_b, ssd_dt_bias=ssd_dt_bias, ssd_a_log=ssd_a_log,
             ssd_d=ssd_d, ssd_norm_w=ssd_norm_w, w_in_c=w_in_c, w_out_c=w_out_c, hgrn_lb_gamma=hgrn_lb_gamma,
             hgrn_norm_w=hgrn_norm_w, ffn_w_up=ffn_w_up, ffn_conv_w=ffn_conv_w, ffn_conv_b=ffn_conv_b,
             ffn_w_down=ffn_w_down)
    bp, sp, _ = x_prompt.shape
    db, ds, _ = x_sample.shape
    na, nc = (DEPTH + 1) // 2, DEPTH // 2
    zeros = lambda *s: jnp.zeros(s, F32)
    prompt_caches = (zeros(na, bp, SSD_CONV - 1, SSD_CONV_DIM), zeros(na, bp, SSD_HEADS, SSD_HEAD_DIM, SSD_STATE),
                     zeros(nc, bp, HGRN_HEADS, HGRN_KEY_DIM, HGRN_VAL_DIM), zeros(DEPTH, bp, FFN_CONV - 1, FFN_DIM))
    outs_p = _trunk(x_prompt.reshape(bp * sp, D_MODEL), bp, sp, True, prompt_caches, p)
    sample_caches = (state_ssm_conv, state_ssm, state_hgrn, state_ffn_conv, cache_k_pool, cache_v_pool, page_table)
    outs_s = _trunk(x_sample.reshape(db * ds, D_MODEL), db, ds, False, sample_caches, p)
    yp, krp, vrp, ssp, scp, hgp, fcp = outs_p
    ys, krs, vrs, sss, scs, hgs, fcs = outs_s
    return (yp, ys, krp, vrp, ssp, scp, hgp, fcp, krs, vrs, sss, scs, hgs, fcs)
```

```python
import functools
import math

import numpy as np
import jax
import jax.numpy as jnp
from jax import lax
from jax.experimental import pallas as pl
from jax.experimental.pallas import tpu as pltpu

F32 = jnp.float32
BF16 = jnp.bfloat16
HIGHEST = lax.Precision.HIGHEST

D_MODEL = 1024
DEPTH = 2
PAGE_SIZE = 128
SSD_HEADS = 8
SSD_HEAD_DIM = 64
SSD_INNER = SSD_HEADS * SSD_HEAD_DIM
SSD_STATE = 64
SSD_GROUPS = 2
SSD_CONV = 4
SSD_CHUNK = 128
SSD_GN = SSD_GROUPS * SSD_STATE
SSD_CONV_DIM = SSD_INNER + 2 * SSD_GN
MOBA_HEADS = 8
MOBA_HEAD_DIM = 64
MOBA_INNER = MOBA_HEADS * MOBA_HEAD_DIM
MOBA_BLOCK = 256
MOBA_TOPK = 3
OFF_XBC = SSD_INNER
OFF_DT = OFF_XBC + SSD_CONV_DIM
OFF_Q = OFF_DT + SSD_HEADS
OFF_K = OFF_Q + MOBA_INNER
OFF_V = OFF_K + MOBA_INNER
HGRN_HEADS = 8
HGRN_KEY_DIM = 128
HGRN_VAL_DIM = D_MODEL // HGRN_HEADS
HGRN_K = HGRN_HEADS * HGRN_KEY_DIM
HGRN_V = HGRN_HEADS * HGRN_VAL_DIM
HGRN_CHUNK = 64
FFN_DIM = 2816
FFN_CONV = 3
RMS_EPS = 1e-6

LANES = 128
SUBLANES = 8
VMEM_LIMIT_BYTES = 56 * 1024 * 1024

ROW_TILE = 256
FFN_COL_CHUNK = 256
NEG_BIG = -1e30
LOG2E = math.log2(math.e)
ALIBI_PARTS = 3
MOBA_VT_ROWS = MOBA_HEAD_DIM + 16
MOBA_AHEAD = 2
MOBA_SLOTS = MOBA_AHEAD + 1
MOBA_UNROLL = 4 * MOBA_SLOTS


def _cparams(sem, flags=None):
    return pltpu.CompilerParams(dimension_semantics=sem, vmem_limit_bytes=VMEM_LIMIT_BYTES, flags=flags)


def _const_spec(shape):
    nd = len(shape)
    return pl.BlockSpec(shape, lambda *_: (0,) * nd)


def _rms(x, w):
    y = x * lax.rsqrt(jnp.mean(x * x, axis=-1, keepdims=True) + RMS_EPS)
    return y * w


def _rms_bf16(x, w):
    return _rms(x, w).astype(BF16)


def _dot(a, b):
    return jnp.dot(a, b, preferred_element_type=F32)


def _dot_split(a, b):
    a_hi = a.astype(BF16)
    a_lo = (a - a_hi.astype(F32)).astype(BF16)
    b_hi = b.astype(BF16)
    b_lo = (b - b_hi.astype(F32)).astype(BF16)
    return _dot(jnp.concatenate([a_hi, a_lo, a_hi], axis=1), jnp.concatenate([b_hi, b_hi, b_lo], axis=0))


def _dot_nt(a, b):
    return lax.dot_general(a, b, (((1,), (1,)), ((), ())), preferred_element_type=F32)


def _dot_tn(a, b):
    return lax.dot_general(a, b, (((0,), (0,)), ((), ())), preferred_element_type=F32)


def _tril_f32(n):
    r = lax.broadcasted_iota(jnp.int32, (n, n), 0)
    c = lax.broadcasted_iota(jnp.int32, (n, n), 1)
    return r >= c


def _cumsum_rows(x, q, split):
    if not split:
        return jnp.dot(_tril_f32(q).astype(F32), x, precision=HIGHEST, preferred_element_type=F32)
    tri = _tril_f32(q).astype(BF16)
    acc = None
    rest = x
    for _ in range(3):
        term = rest.astype(BF16)
        part = _dot(tri, term)
        acc = part if acc is None else acc + part
        rest = rest - term.astype(F32)
    return acc


def _silu(x):
    return x * jax.nn.sigmoid(x)


def _inproj_a_kernel(x_ref, nw_ref, wz, wxbc, wdt, wq, wk, wv, *outs, prompt):
    h = _rms(x_ref[...], nw_ref[...])
    hb = h.astype(BF16)
    z_o, xbc_o, dt_o, q_o = outs[:4]
    z_o[...] = _dot(hb, wz[...])
    xbc_o[...] = _dot(hb, wxbc[...])
    dt_o[...] = _dot(hb, wdt[...])
    if wq.dtype == F32:
        q_o[...] = jnp.dot(h, wq[...], precision=HIGHEST, preferred_element_type=F32)
    else:
        q_o[...] = _dot(hb, wq[...])
    k = _dot(hb, wk[...])
    v = _dot(hb, wv[...])
    if not prompt:
        k_o, v_o = outs[4:]
        k_o[...] = k
        v_o[...] = v
        return
    kr_o, vr_o, ka_o, vt_o, km_o = outs[4:]
    tm = k.shape[0]
    lane = lax.broadcasted_iota(jnp.int32, (tm, LANES), 1)
    key_aug = _key_aug_columns(tm)
    for hp in range(MOBA_HEADS // 2):
        pair = k[:, hp * LANES:(hp + 1) * LANES]
        for j, kh in enumerate((pair, pltpu.roll(pair, MOBA_HEAD_DIM, 1))):
            ka_o[2 * hp + j, 0] = jnp.where(lane < MOBA_HEAD_DIM, kh, key_aug).astype(BF16)
    k_t = k.T
    v_t = v.T
    for p in range(tm // PAGE_SIZE):
        toks = slice(p * PAGE_SIZE, (p + 1) * PAGE_SIZE)
        for h in range(MOBA_HEADS):
            dims = slice(h * MOBA_HEAD_DIM, (h + 1) * MOBA_HEAD_DIM)
            kr_o[p, h] = k_t[dims, toks]
            vr_o[p, h] = v_t[dims, toks]
    ones_rows = (lax.broadcasted_iota(jnp.int32, (MOBA_VT_ROWS - MOBA_HEAD_DIM, tm), 0) == 0).astype(F32)
    pieces = []
    for h in range(MOBA_HEADS):
        pieces += [v_t[h * MOBA_HEAD_DIM:(h + 1) * MOBA_HEAD_DIM, :], ones_rows]
    vt_o[0] = jnp.concatenate(pieces, axis=0).astype(BF16)
    km_o[0] = jnp.mean(k, axis=0, keepdims=True)


def _inproj_a(x, nw, ws, prompt):
    m = x.shape[0]
    tm = ROW_TILE if prompt else m
    assert m % tm == 0
    if prompt:
        assert tm == MOBA_BLOCK
    nt = m // tm
    widths = [w.shape[1] for w in ws]
    row = lambda i: (i, 0)
    in_specs = [pl.BlockSpec((tm, D_MODEL), row), _const_spec((1, D_MODEL))]
    in_specs += [_const_spec(w.shape) for w in ws]
    out_shape = [jax.ShapeDtypeStruct((m, n), F32) for n in widths[:4]]
    out_specs = [pl.BlockSpec((tm, n), row) for n in widths[:4]]
    if prompt:
        npg = m // PAGE_SIZE
        ppt = tm // PAGE_SIZE
        rows_shape = (npg, MOBA_HEADS, MOBA_HEAD_DIM, PAGE_SIZE)
        rows_spec = pl.BlockSpec((ppt, MOBA_HEADS, MOBA_HEAD_DIM, PAGE_SIZE), lambda i: (i, 0, 0, 0))
        out_shape += [jax.ShapeDtypeStruct(rows_shape, F32)] * 2
        out_specs += [rows_spec, rows_spec]
        out_shape += [jax.ShapeDtypeStruct((MOBA_HEADS, nt, tm, LANES), BF16),
                      jax.ShapeDtypeStruct((nt, MOBA_HEADS * MOBA_VT_ROWS, tm), BF16),
                      jax.ShapeDtypeStruct((nt, 1, MOBA_INNER), F32)]
        out_specs += [pl.BlockSpec((MOBA_HEADS, 1, tm, LANES), lambda i: (0, i, 0, 0)),
                      pl.BlockSpec((1, MOBA_HEADS * MOBA_VT_ROWS, tm), lambda i: (i, 0, 0)),
                      pl.BlockSpec((1, 1, MOBA_INNER), lambda i: (i, 0, 0))]
    else:
        out_shape += [jax.ShapeDtypeStruct((m, MOBA_INNER), F32)] * 2
        out_specs += [pl.BlockSpec((tm, MOBA_INNER), row)] * 2
    return pl.pallas_call(
        functools.partial(_inproj_a_kernel, prompt=prompt),
        grid=(nt,), in_specs=in_specs, out_specs=out_specs, out_shape=out_shape,
        compiler_params=_cparams(("arbitrary",)), name="inproj_a",
    )(x, nw, *ws)


def _norm_matmul_kernel(x_ref, nw_ref, *refs):
    n = len(refs) // 2
    hb = _rms_bf16(x_ref[...], nw_ref[...])
    for w, o in zip(refs[:n], refs[n:]):
        o[...] = _dot(hb, w[...])


def _norm_matmul(x, nw, ws):
    m = x.shape[0]
    tm = min(ROW_TILE, m)
    assert m % tm == 0
    row = lambda i: (i, 0)
    in_specs = [pl.BlockSpec((tm, D_MODEL), row), _const_spec((1, D_MODEL))]
    in_specs += [_const_spec(w.shape) for w in ws]
    return pl.pallas_call(
        _norm_matmul_kernel, grid=(m // tm,), in_specs=in_specs,
        out_specs=[pl.BlockSpec((tm, w.shape[1]), row) for w in ws],
        out_shape=[jax.ShapeDtypeStruct((m, w.shape[1]), F32) for w in ws],
        compiler_params=_cparams(("arbitrary",)), name="norm_matmul",
    )(x, nw, *ws)


def _ffn_kernel(*refs, tm, seq_len, final_norm, n_mix):
    it = iter(refs)
    x_ref = next(it)
    mix_refs = [next(it) for _ in range(2 * n_mix)]
    nw_ref, wup_ref, cw_ref, cb_ref, wdn_ref = (next(it) for _ in range(5))
    if seq_len is None:
        hist_ref = next(it)
    else:
        h1_ref, h2_ref = next(it), next(it)
    nf_ref = next(it) if final_norm else None
    o_ref = next(it)
    tail_ref = next(it)
    up_s, act_s = next(it), next(it)

    i = pl.program_id(0)
    f = FFN_DIM
    if seq_len is None:
        @pl.when(i == 0)
        def _():
            up_s[0:SUBLANES, 0:f] = hist_ref[...]
    else:
        up_s[0:SUBLANES, 0:f] = jnp.zeros((SUBLANES, f), F32)

    x = x_ref[...]
    for a_ref, w_ref in zip(mix_refs[:n_mix], mix_refs[n_mix:]):
        x = x + _dot(a_ref[...], w_ref[...])
    hb = _rms_bf16(x, nw_ref[...])
    up_s[SUBLANES:SUBLANES + tm, :] = _dot(hb, wup_ref[...])

    if seq_len is not None:
        t = lax.broadcasted_iota(jnp.int32, (tm, 1), 0) % seq_len
    for c in range(0, f, FFN_COL_CHUNK):
        cols = slice(c, c + FFN_COL_CHUNK)
        g0 = up_s[SUBLANES:SUBLANES + tm, cols]
        g1 = up_s[SUBLANES - 1:SUBLANES - 1 + tm, cols]
        g2 = up_s[SUBLANES - 2:SUBLANES - 2 + tm, cols]
        if seq_len is not None:
            g1 = jnp.where(t >= 1, g1, h1_ref[:, cols])
            g2 = jnp.where(t >= 2, g2, h2_ref[:, cols])
        val = up_s[SUBLANES:SUBLANES + tm, f + c:f + c + FFN_COL_CHUNK]
        conv = g2 * cw_ref[0:1, cols]
        conv = conv + g1 * cw_ref[1:2, cols]
        conv = conv + g0 * cw_ref[2:3, cols]
        conv = conv + cb_ref[:, cols]
        act_s[:, cols] = (_silu(conv) * val).astype(BF16)

    out = x + _dot(act_s[...], wdn_ref[...])
    o_ref[...] = _rms(out, nf_ref[...]) if final_norm else out

    if seq_len is None:
        last = up_s[tm:tm + SUBLANES, 0:f]
        up_s[0:SUBLANES, 0:f] = last

        @pl.when(i == pl.num_programs(0) - 1)
        def _():
            tail_ref[...] = last
    else:
        tail_ref[...] = up_s[SUBLANES:SUBLANES + tm, 0:f]


def _ffn(x, mix_acts, mix_ws, nw, wup, cw, cb, wdn, hist, seq_len, nf):
    assert FFN_CONV == 3
    m = x.shape[0]
    f = FFN_DIM
    final_norm = nf is not None
    row = lambda i: (i, 0)
    if seq_len is None:
        tm = ROW_TILE
        hist8 = jnp.zeros((SUBLANES, f), F32).at[SUBLANES - 2:].set(hist)
        extra = [hist8]
        extra_specs = [_const_spec((SUBLANES, f))]
        tail_shape = (SUBLANES, f)
    else:
        tm = m
        nb = m // seq_len
        assert seq_len >= 2 and nb * seq_len == m
        h1 = jnp.zeros((nb, seq_len, f), F32).at[:, 0].set(hist[:, 1]).reshape(m, f)
        h2 = jnp.zeros((nb, seq_len, f), F32).at[:, 0:2].set(hist).reshape(m, f)
        extra = [h1, h2]
        extra_specs = [_const_spec((m, f))] * 2
        tail_shape = (m, f)
    assert m % tm == 0
    ins = [x] + list(mix_acts) + list(mix_ws) + [nw, wup, cw, cb, wdn] + extra
    in_specs = [pl.BlockSpec((tm, D_MODEL), row)]
    in_specs += [pl.BlockSpec((tm, a.shape[1]), row) for a in mix_acts]
    in_specs += [_const_spec(w.shape) for w in mix_ws]
    in_specs += [_const_spec((1, D_MODEL)), _const_spec(wup.shape), _const_spec(cw.shape),
                 _const_spec(cb.shape), _const_spec(wdn.shape)] + extra_specs
    if final_norm:
        ins.append(nf)
        in_specs.append(_const_spec((1, D_MODEL)))
    xo, tail = pl.pallas_call(
        functools.partial(_ffn_kernel, tm=tm, seq_len=seq_len, final_norm=final_norm, n_mix=len(mix_acts)),
        grid=(m // tm,), in_specs=in_specs,
        out_specs=[pl.BlockSpec((tm, D_MODEL), row), _const_spec(tail_shape)],
        out_shape=[jax.ShapeDtypeStruct((m, D_MODEL), F32), jax.ShapeDtypeStruct(tail_shape, F32)],
        scratch_shapes=[pltpu.VMEM((SUBLANES + tm, 2 * f), F32), pltpu.VMEM((tm, f), BF16)],
        compiler_params=_cparams(("arbitrary",)), name="conv_ffn",
    )(*ins)
    if seq_len is None:
        new_hist = tail[SUBLANES - 2:]
    else:
        new_hist = tail.reshape(m // seq_len, seq_len, f)[:, seq_len - 2:]
    return xo, new_hist


def _ssd_kernel(z_ref, xbc_ref, dt_ref, hist_ref, s0_ref, cw_ref, cb_ref, dtb_ref, alog_ref, dsk_ref,
                nw_ref, y_ref, hist_o, s_o, cbuf, st, ybuf, *, q, seq_len, n_chunks):
    c = pl.program_id(1)
    hrows = SSD_CONV - 1
    h0 = SUBLANES - hrows

    @pl.when(c == 0)
    def _():
        cbuf[0:SUBLANES, :] = hist_ref[0]
        st[...] = s0_ref[0]

    cbuf[SUBLANES:SUBLANES + q, :] = xbc_ref[0]
    conv = cbuf[h0:h0 + q, :] * cw_ref[0:1, :]
    for j in range(1, SSD_CONV):
        conv = conv + cbuf[h0 + j:h0 + j + q, :] * cw_ref[j:j + 1, :]
    xc = _silu(conv + cb_ref[...])

    xdt_raw = dt_ref[0] + dtb_ref[...]
    dt = jnp.maximum(xdt_raw, 0.0) + jnp.log1p(jnp.exp(-jnp.abs(xdt_raw)))
    if seq_len % q != 0:
        row = c * q + lax.broadcasted_iota(jnp.int32, (q, 1), 0)
        dt = jnp.where(row < seq_len, dt, 0.0)
    a = -jnp.exp(alog_ref[...])
    acum = _cumsum_rows(dt * a, q, split=False)
    acum_t = acum.T
    a_last = acum[q - 1:q, :]
    e_acum = jnp.exp(acum)
    w_in = jnp.exp(a_last - acum) * dt
    e_last = jnp.exp(a_last)
    tril = _tril_f32(q)

    rep = SSD_HEADS // SSD_GROUPS
    cbs, bms, cms = [], [], []
    for g in range(SSD_GROUPS):
        bm = xc[:, SSD_INNER + g * SSD_STATE:SSD_INNER + (g + 1) * SSD_STATE]
        cm = xc[:, SSD_INNER + SSD_GN + g * SSD_STATE:SSD_INNER + SSD_GN + (g + 1) * SSD_STATE]
        bms.append(bm)
        cms.append(cm)
        cbs.append(_dot_nt(cm.astype(BF16), bm.astype(BF16)))

    for h in range(SSD_HEADS):
        g = h // rep
        xs = xc[:, h * SSD_HEAD_DIM:(h + 1) * SSD_HEAD_DIM]
        seg = acum[:, h:h + 1] - acum_t[h:h + 1, :]
        decay = jnp.exp(jnp.where(tril, seg, -jnp.inf))
        y = _dot((cbs[g] * decay).astype(BF16), (xs * dt[:, h:h + 1]).astype(BF16))
        s_h = st[h]
        y = y + _dot_nt((cms[g] * e_acum[:, h:h + 1]).astype(BF16), s_h.astype(BF16))
        st[h] = s_h * e_last[:, h:h + 1] + _dot_tn((xs * w_in[:, h:h + 1]).astype(BF16),
                                                    bms[g].astype(BF16))
        ybuf[:, h * SSD_HEAD_DIM:(h + 1) * SSD_HEAD_DIM] = y + xs * dsk_ref[:, h:h + 1]

    yz = ybuf[...] * _silu(z_ref[0])
    gw = SSD_INNER // SSD_GROUPS
    for g in range(SSD_GROUPS):
        yg = yz[:, g * gw:(g + 1) * gw]
        yn = yg * lax.rsqrt(jnp.mean(yg * yg, axis=-1, keepdims=True) + RMS_EPS)
        y_ref[0, :, g * gw:(g + 1) * gw] = (yn * nw_ref[:, g * gw:(g + 1) * gw]).astype(BF16)

    @pl.when(c == n_chunks - 1)
    def _():
        l_last = seq_len - (n_chunks - 1) * q
        hist_o[0] = cbuf[l_last:l_last + SUBLANES, :]
        s_o[0] = st[...]

    cbuf[0:SUBLANES, :] = cbuf[q:q + SUBLANES, :]


def _pad_lanes(v, fill=0.0):
    return jnp.full((1, LANES), fill, F32).at[0, :v.shape[0]].set(v.astype(F32))


def _ssd(z, xbc, dtr, hist, s0, conv_w, conv_b, dt_bias, a_log, d_skip, norm_w, seq_len):
    b, lp, _ = z.shape
    q = SSD_CHUNK
    assert lp % q == 0 and lp - seq_len < q
    hrows = SSD_CONV - 1
    hist8 = jnp.zeros((b, SUBLANES, SSD_CONV_DIM), F32).at[:, SUBLANES - hrows:].set(hist)
    cw8 = jnp.zeros((SUBLANES, SSD_CONV_DIM), F32).at[:SSD_CONV].set(conv_w)
    seq = lambda i, c: (i, c, 0)
    per_b3 = lambda i, c: (i, 0, 0)
    per_b4 = lambda i, c: (i, 0, 0, 0)
    in_specs = [pl.BlockSpec((1, q, SSD_INNER), seq), pl.BlockSpec((1, q, SSD_CONV_DIM), seq),
                pl.BlockSpec((1, q, LANES), seq), pl.BlockSpec((1, SUBLANES, SSD_CONV_DIM), per_b3),
                pl.BlockSpec((1, SSD_HEADS, SSD_HEAD_DIM, SSD_STATE), per_b4),
                _const_spec((SUBLANES, SSD_CONV_DIM)), _const_spec((1, SSD_CONV_DIM)),
                _const_spec((1, LANES)), _const_spec((1, LANES)), _const_spec((1, LANES)),
                _const_spec((1, SSD_INNER))]
    out_shape = [jax.ShapeDtypeStruct((b, lp, SSD_INNER), BF16),
                 jax.ShapeDtypeStruct((b, SUBLANES, SSD_CONV_DIM), F32),
                 jax.ShapeDtypeStruct((b, SSD_HEADS, SSD_HEAD_DIM, SSD_STATE), F32)]
    out_specs = [pl.BlockSpec((1, q, SSD_INNER), seq), pl.BlockSpec((1, SUBLANES, SSD_CONV_DIM), per_b3),
                 pl.BlockSpec((1, SSD_HEADS, SSD_HEAD_DIM, SSD_STATE), per_b4)]
    y, hist_o, s_o = pl.pallas_call(
        functools.partial(_ssd_kernel, q=q, seq_len=seq_len, n_chunks=lp // q),
        grid=(b, lp // q), in_specs=in_specs, out_specs=out_specs, out_shape=out_shape,
        scratch_shapes=[pltpu.VMEM((SUBLANES + q, SSD_CONV_DIM), F32),
                        pltpu.VMEM((SSD_HEADS, SSD_HEAD_DIM, SSD_STATE), F32),
                        pltpu.VMEM((q, SSD_INNER), F32)],
        compiler_params=_cparams(("arbitrary", "arbitrary")), name="ssd",
    )(z, xbc, dtr, hist8, s0, cw8, conv_b.reshape(1, -1), _pad_lanes(dt_bias), _pad_lanes(a_log),
      _pad_lanes(d_skip), norm_w.reshape(1, -1))
    return y, hist_o[:, SUBLANES - hrows:], s_o


def _gla_kernel(q_ref, fx_ref, iv_ref, g_ref, gam_ref, nw_ref, s0_ref, o_ref, s_o, st_t, *, q, seq_len,
                layer):
    c = pl.program_id(1)
    n_chunks = pl.num_programs(1)

    @pl.when(c == 0)
    def _():
        for h in range(HGRN_HEADS):
            st_t[h] = s0_ref[0, h].T

    rows = [gam_ref[l:l + 1, :] for l in range(DEPTH)]
    mx = functools.reduce(jnp.maximum, rows)
    es = [jnp.exp(r - mx) for r in rows]
    lb = sum(es[1:layer + 1]) / sum(es) if layer >= 1 else jnp.zeros_like(mx)

    f = lb + (1.0 - lb) * jax.nn.sigmoid(fx_ref[0])
    if seq_len % q != 0:
        row = c * q + lax.broadcasted_iota(jnp.int32, (q, 1), 0)
        f = jnp.where(row < seq_len, f, 1.0)
    kk = 1.0 - f
    b = _cumsum_rows(jnp.log(f), q, split=True)
    b_last = b[q - 1:q, :]
    qd = q_ref[0] * jnp.exp(b)
    kd = kk * jnp.exp(-b)
    kdec = kk * jnp.exp(b_last - b)
    e_last = jnp.exp(b_last)
    tril = _tril_f32(q)
    iv = iv_ref[0]

    ks = [slice(h * HGRN_KEY_DIM, (h + 1) * HGRN_KEY_DIM) for h in range(HGRN_HEADS)]
    vs = [slice(h * HGRN_VAL_DIM, (h + 1) * HGRN_VAL_DIM) for h in range(HGRN_HEADS)]
    qd_b = [qd[:, s].astype(BF16) for s in ks]
    v_b = [iv[:, s].astype(BF16) for s in vs]
    att = [_dot_nt(qd_b[h], kd[:, ks[h]].astype(BF16)) for h in range(HGRN_HEADS)]
    o_state = [_dot_nt(qd_b[h], st_t[h].astype(BF16)) for h in range(HGRN_HEADS)]
    kv = [_dot_tn(v_b[h], kdec[:, ks[h]].astype(BF16)) for h in range(HGRN_HEADS)]
    for h in range(HGRN_HEADS):
        o = _dot(jnp.where(tril, att[h], 0.0).astype(BF16), v_b[h]) + o_state[h]
        st_t[h] = st_t[h] * e_last[:, ks[h]] + kv[h]
        on = o * lax.rsqrt(jnp.mean(o * o, axis=-1, keepdims=True) + RMS_EPS) * nw_ref[...]
        o_ref[0, :, vs[h]] = (on * _silu(g_ref[0, :, vs[h]])).astype(BF16)

    @pl.when(c == n_chunks - 1)
    def _():
        for h in range(HGRN_HEADS):
            s_o[0, h] = st_t[h].T


def _gla(qa, fx, iv, g, gamma, norm_w, s0, seq_len, layer):
    b, lp, _ = qa.shape
    q = HGRN_CHUNK
    assert lp % q == 0 and lp - seq_len < q
    gam8 = jnp.zeros((SUBLANES, HGRN_K), F32).at[:DEPTH].set(gamma)
    seq = lambda i, c: (i, c, 0)
    per_b4 = lambda i, c: (i, 0, 0, 0)
    st_shape = (HGRN_HEADS, HGRN_KEY_DIM, HGRN_VAL_DIM)
    in_specs = [pl.BlockSpec((1, q, HGRN_K), seq), pl.BlockSpec((1, q, HGRN_K), seq),
                pl.BlockSpec((1, q, HGRN_V), seq), pl.BlockSpec((1, q, HGRN_V), seq),
                _const_spec((SUBLANES, HGRN_K)), _const_spec((1, HGRN_VAL_DIM)),
                pl.BlockSpec((1,) + st_shape, per_b4)]
    o, s_o = pl.pallas_call(
        functools.partial(_gla_kernel, q=q, seq_len=seq_len, layer=layer),
        grid=(b, lp // q), in_specs=in_specs,
        out_specs=[pl.BlockSpec((1, q, HGRN_V), seq), pl.BlockSpec((1,) + st_shape, per_b4)],
        out_shape=[jax.ShapeDtypeStruct((b, lp, HGRN_V), BF16), jax.ShapeDtypeStruct((b,) + st_shape, F32)],
        scratch_shapes=[pltpu.VMEM((HGRN_HEADS, HGRN_VAL_DIM, HGRN_KEY_DIM), F32)],
        compiler_params=_cparams(("arbitrary", "arbitrary")), name="gla",
    )(qa, fx, iv, g, gam8, norm_w.reshape(1, -1), s0)
    return o, s_o


def _select_topk_rows(gate_t, n_valid, nblk, width):
    blk = lax.broadcasted_iota(jnp.int32, (nblk, width), 0)
    blk_f = blk.astype(F32)
    g = jnp.where(blk < n_valid, gate_t, -jnp.inf)
    sel = jnp.zeros((nblk, width), F32)
    for _ in range(MOBA_TOPK):
        mx = jnp.max(g, axis=0, keepdims=True)
        first = jnp.min(jnp.where(g == mx, blk_f, float(nblk)), axis=0, keepdims=True)
        pick = (blk_f == first) & (mx > -jnp.inf)
        sel = jnp.where(pick, 1.0, sel)
        g = jnp.where(pick, -jnp.inf, g)
    return jnp.where(sel > 0.0, 0.0, -jnp.inf)


def _key_aug_columns(tm):
    lane = lax.broadcasted_iota(jnp.int32, (tm, LANES), 1)
    r = lax.broadcasted_iota(jnp.int32, (tm, LANES), 0).astype(F32)
    return jnp.where((lane >= MOBA_HEAD_DIM) & (lane < MOBA_HEAD_DIM + ALIBI_PARTS), r, 0.0)


def _moba_prompt_kernel(slopes_ref, q_ref, ka_ref, vt_ref, km_ref, o_ref, brow_s, ubuf, *, nblk):
    hp = pl.program_id(0)
    i = pl.program_id(1)
    tq = MOBA_BLOCK
    d = MOBA_HEAD_DIM
    q = q_ref[...]
    lane_k = lax.broadcasted_iota(jnp.int32, (nblk, 2 * d), 1)
    causal =(lax.broadcasted_iota(jnp.int32, (MOBA_BLOCK, tq), 0)
              <= lax.broadcasted_iota(jnp.int32, (MOBA_BLOCK, tq), 1))
    blk_f = lax.broadcasted_iota(jnp.int32, (nblk, tq), 0).astype(F32)
    col_f = lax.broadcasted_iota(jnp.int32, (nblk, tq), 1).astype(F32)
    aug_row = lax.broadcasted_iota(jnp.int32, (d, tq), 0)
    i_f = i.astype(F32)

    q_t = q.T
    km = km_ref[...]
    km2 = jnp.concatenate([jnp.where(lane_k < d, km, 0.0), jnp.where(lane_k >= d, km, 0.0)], axis=0)
    gate_both = _dot_split(km2, q_t)

    rhs = []
    for j in range(2):
        a2 = slopes_ref[2 * hp + j] * LOG2E
        sel = _select_topk_rows(gate_both[j * nblk:(j + 1) * nblk, :], i, nblk, tq)
        brow_s[j] = sel + a2 * (MOBA_BLOCK * (blk_f - i_f) - col_f)
        rest = jnp.full((d, tq), a2, F32)
        aug = jnp.zeros((d, tq), F32)
        for part in range(ALIBI_PARTS):
            term = rest.astype(BF16).astype(F32)
            aug = jnp.where(aug_row == part, term, aug)
            rest = rest - term
        q_h = q_t[j * d:(j + 1) * d, :] * (d ** -0.5 * LOG2E)
        rhs.append(jnp.concatenate([q_h, aug], axis=0).astype(BF16))

    def block_update(st, u, b_row, v_t):
        m_run, o_t = st
        m_new = jnp.maximum(m_run, jnp.max(u, axis=0, keepdims=True) + b_row)
        alpha = jnp.exp2(m_run - m_new)
        p = jnp.exp2(u - (m_new - b_row))
        return m_new, alpha * o_t + _dot(v_t, p.astype(BF16))

    def score(n, slot):
        for j in range(2):
            ubuf[slot, j] = _dot(ka_ref[j, n], rhs[j])

    def consume(n, slot, sts, own):
        new = []
        for j in range(2):
            u = ubuf[slot, j]
            if own:
                u = jnp.where(causal, u, -jnp.inf)
                b_row = -(slopes_ref[2 * hp + j] * LOG2E) * col_f[0:1, :]
            else:
                b_row = brow_s[j, pl.ds(n, 1), :]
            new.append(block_update(sts[j], u, b_row, vt_ref[n, j * MOBA_VT_ROWS:(j + 1) * MOBA_VT_ROWS, :]))
        return tuple(new)

    def step(n, slot, slot_ahead, sts):
        score(jnp.minimum(n + MOBA_AHEAD, i), slot_ahead)
        return consume(n, slot, sts, False)

    def group(base, unroll, sts):
        for k in range(unroll):
            sts = step(base + k, k % MOBA_SLOTS, (k + MOBA_AHEAD) % MOBA_SLOTS, sts)
        return sts

    init = (jnp.full((1, tq), NEG_BIG, F32), jnp.zeros((MOBA_VT_ROWS, tq), F32))
    for n0 in range(MOBA_AHEAD):
        score(jnp.minimum(n0, i), n0)
    n_big = i // MOBA_UNROLL
    sts = lax.fori_loop(0, n_big, lambda g, s: group(g * MOBA_UNROLL, MOBA_UNROLL, s), (init, init))
    base = n_big * MOBA_UNROLL
    n_small = (i - base) // MOBA_SLOTS
    sts = lax.fori_loop(0, n_small, lambda g, s: group(base + g * MOBA_SLOTS, MOBA_SLOTS, s), sts)
    base = base + n_small * MOBA_SLOTS
    rem = i - base
    for k in range(MOBA_SLOTS - 1):
        sts = lax.cond(rem > k, lambda s, k=k: step(base + k, k, (k + MOBA_AHEAD) % MOBA_SLOTS, s),
                       lambda s: s, sts)
    outs = []
    for _, o_t in consume(i, rem, sts, True):
        outs.append(o_t[0:d, :] / o_t[d:d + 1, :])
    o_ref[...] = jnp.concatenate(outs, axis=0).T.astype(BF16)


def _alibi_slopes():
    return jnp.asarray(np.exp2(-8.0 * np.arange(1, MOBA_HEADS + 1) / MOBA_HEADS), dtype=F32)


def _moba_prompt(q, ka, vt, km):
    s = q.shape[0]
    nblk = s // MOBA_BLOCK
    assert nblk * MOBA_BLOCK == s and 2 * MOBA_HEAD_DIM == LANES
    pairs = MOBA_HEADS // 2
    return pl.pallas_call(
        functools.partial(_moba_prompt_kernel, nblk=nblk),
        grid=(pairs, nblk),
        in_specs=[pl.BlockSpec(memory_space=pltpu.SMEM),
                  pl.BlockSpec((MOBA_BLOCK, LANES), lambda hp, i: (i, hp)),
                  pl.BlockSpec((2, nblk, MOBA_BLOCK, LANES), lambda hp, i: (hp, 0, 0, 0)),
                  pl.BlockSpec((nblk, 2 * MOBA_VT_ROWS, MOBA_BLOCK), lambda hp, i: (0, hp, 0)),
                  pl.BlockSpec((nblk, LANES), lambda hp, i: (0, hp))],
        out_specs=pl.BlockSpec((MOBA_BLOCK, LANES), lambda hp, i: (i, hp)),
        out_shape=jax.ShapeDtypeStruct((s, MOBA_INNER), BF16),
        scratch_shapes=[pltpu.VMEM((2, nblk, MOBA_BLOCK), F32),
                        pltpu.VMEM((MOBA_SLOTS, 2, MOBA_BLOCK, MOBA_BLOCK), F32)],
        compiler_params=_cparams(("arbitrary", "arbitrary")), name="moba_prompt",
    )(_alibi_slopes(), q, ka, vt, km)


PAGES_PER_STEP = 32


def _kmean_kernel(pt_ref, *refs):
    del pt_ref
    o_ref = refs[-1]
    s = pl.program_id(1)
    ppb = MOBA_BLOCK // PAGE_SIZE
    bps = PAGES_PER_STEP // ppb

    @pl.when(s == 0)
    def _():
        o_ref[...] = jnp.zeros(o_ref.shape, F32)

    acc = o_ref[0]
    lane = lax.broadcasted_iota(jnp.int32, acc.shape, 2)
    for m in range(bps):
        x = refs[m * ppb][0]
        for p in range(1, ppb):
            x = x + refs[m * ppb + p][0]
        mean = jnp.sum(x, axis=-1, keepdims=True) * (1.0 / MOBA_BLOCK)
        acc = jnp.where(lane == s * bps + m, mean, acc)
    o_ref[0] = acc


def _sample_kmean(k_pool_t, page_table):
    db, n_pages = page_table.shape
    ppb = MOBA_BLOCK // PAGE_SIZE
    n_full = n_pages // ppb
    assert n_full * ppb == n_pages and n_pages % PAGES_PER_STEP == 0
    steps = n_pages // PAGES_PER_STEP
    blk = (1, MOBA_HEADS, MOBA_HEAD_DIM, PAGE_SIZE)

    def page_spec(p):
        return pl.BlockSpec(blk, lambda b, s, pt: (pt[b * n_pages + s * PAGES_PER_STEP + p], 0, 0, 0))

    out_blk = (1, MOBA_HEADS, MOBA_HEAD_DIM, n_full)
    grid_spec = pltpu.PrefetchScalarGridSpec(
        num_scalar_prefetch=1, grid=(db, steps),
        in_specs=[page_spec(p) for p in range(PAGES_PER_STEP)],
        out_specs=pl.BlockSpec(out_blk, lambda b, s, pt: (b, 0, 0, 0)))
    return pl.pallas_call(
        _kmean_kernel, grid_spec=grid_spec,
        out_shape=jax.ShapeDtypeStruct((db,) + out_blk[1:], F32),
        compiler_params=_cparams(("arbitrary", "arbitrary")), name="sample_kmean",
    )(page_table.reshape(-1), *([k_pool_t] * PAGES_PER_STEP))


def _sample_select_kernel(q_ref, km_ref, idx_ref, *, n_full):
    t8 = q_ref.shape[2]
    blk = lax.broadcasted_iota(jnp.int32, (t8, n_full), 1).astype(F32)
    lane = lax.broadcasted_iota(jnp.int32, (t8, LANES), 1)
    for h in range(MOBA_HEADS):
        g = jnp.dot(q_ref[0, h], km_ref[0, h], precision=HIGHEST, preferred_element_type=F32)
        out = jnp.zeros((t8, LANES), jnp.int32)
        for k in range(MOBA_TOPK):
            mx = jnp.max(g, axis=-1, keepdims=True)
            first = jnp.min(jnp.where(g == mx, blk, float(n_full)), axis=-1, keepdims=True)
            out = jnp.where(lane == k, first.astype(jnp.int32), out)
            g = jnp.where(blk == first, -jnp.inf, g)
        idx_ref[0, h] = out


def _sample_select(qh, km):
    db, h, t8, d = qh.shape
    n_full = km.shape[3]
    assert n_full >= MOBA_TOPK
    b4 = lambda b: (b, 0, 0, 0)
    return pl.pallas_call(
        functools.partial(_sample_select_kernel, n_full=n_full), grid=(db,),
        in_specs=[pl.BlockSpec((1, h, t8, d), b4), pl.BlockSpec((1, h, d, n_full), b4)],
        out_specs=pl.BlockSpec((1, h, t8, LANES), b4),
        out_shape=jax.ShapeDtypeStruct((db, h, t8, LANES), jnp.int32),
        compiler_params=_cparams(("arbitrary",)), name="sample_select",
    )(qh, km)


def _sample_attn_kernel(idx_ref, phys_ref, slopes_ref, q_ref, kn_ref, vn_ref, kpool, vpool, o_ref,
                        kbuf, vbuf, sem, *, ds, past):
    ppb = MOBA_BLOCK // PAGE_SIZE
    n_sel = MOBA_TOPK * ppb
    n_pg = ds * n_sel
    b = pl.program_id(0)
    h = pl.program_id(1)
    n_heads = pl.num_programs(1)
    step = b * n_heads + h
    n_steps = pl.num_programs(0) * n_heads

    def page_copies(st, half):
        hh = st % n_heads
        out = []
        for k in range(n_pg):
            page = phys_ref[st * n_pg + k]
            out.append(pltpu.make_async_copy(kpool.at[page, hh], kbuf.at[half, k], sem.at[0, half]))
            out.append(pltpu.make_async_copy(vpool.at[page, hh], vbuf.at[half, k], sem.at[1, half]))
        return out

    @pl.when(step == 0)
    def _():
        for cp in page_copies(step, 0):
            cp.start()

    @pl.when(step + 1 < n_steps)
    def _():
        for cp in page_copies(step + 1, (step + 1) % 2):
            cp.start()

    half = step % 2
    for cp in page_copies(step, half):
        cp.wait()
    k_refs = [kbuf.at[half, k] for k in range(n_pg)]
    v_refs = [vbuf.at[half, k] for k in range(n_pg)]
    slope = slopes_ref[h]
    t8 = q_ref.shape[3]
    c_page = lax.broadcasted_iota(jnp.int32, (1, PAGE_SIZE), 1).astype(F32)
    c_new = lax.broadcasted_iota(jnp.int32, (1, t8), 1)
    qf = q_ref[0, 0] * (MOBA_HEAD_DIM ** -0.5)
    k_new = kn_ref[0, 0]
    v_new = vn_ref[0, 0]
    o_ref[0, 0] = jnp.zeros((MOBA_HEAD_DIM, t8), F32)
    for t in range(ds):
        q_t = qf[:, t:t + 1]
        t_pos = float(past + t)
        scores = []
        for sp in range(n_sel):
            slot, p = divmod(sp, ppb)
            blk_idx = idx_ref[((b * MOBA_HEADS + h) * ds + t) * MOBA_TOPK + slot]
            pos0 = (blk_idx * MOBA_BLOCK + p * PAGE_SIZE).astype(F32)
            s = jnp.sum(k_refs[t * n_sel + sp][...] * q_t, axis=0, keepdims=True)
            scores.append(s - slope * (t_pos - (pos0 + c_page)))
        s_new = jnp.sum(k_new * q_t, axis=0, keepdims=True)
        s_new = s_new - slope * (t_pos - (float(past) + c_new.astype(F32)))
        s_new = jnp.where(c_new <= t, s_new, -jnp.inf)
        mx = jnp.max(s_new, axis=-1, keepdims=True)
        for s in scores:
            mx = jnp.maximum(mx, jnp.max(s, axis=-1, keepdims=True))
        p_new = jnp.exp(s_new - mx)
        l = jnp.sum(p_new, axis=-1, keepdims=True)
        o = jnp.sum(p_new * v_new, axis=-1, keepdims=True)
        acc = jnp.zeros((MOBA_HEAD_DIM, PAGE_SIZE), F32)
        for sp in range(n_sel):
            pr = jnp.exp(scores[sp] - mx)
            l = l + jnp.sum(pr, axis=-1, keepdims=True)
            acc = acc + pr * v_refs[t * n_sel + sp][...]
        o = o + jnp.sum(acc, axis=-1, keepdims=True)
        o_ref[0, 0, :, t:t + 1] = o / l


def _sample_attn(qh, kh, vh, idx, k_pool, v_pool, page_table, ds):
    db, h, d, t8 = qh.shape
    n_pages = page_table.shape[1]
    ppb = MOBA_BLOCK // PAGE_SIZE
    assert n_pages % ppb == 0
    past = n_pages * PAGE_SIZE
    n_sel = MOBA_TOPK * ppb
    logical = idx[..., None] * ppb + jnp.arange(ppb, dtype=jnp.int32)
    phys = page_table[jnp.arange(db)[:, None, None, None], logical.reshape(db, h, ds, n_sel)]
    tok = pl.BlockSpec((1, 1, d, t8), lambda b, hh, *_: (b, hh, 0, 0))
    hbm = pl.BlockSpec(memory_space=pl.ANY)
    page_buf = pltpu.VMEM((2, ds * n_sel, d, PAGE_SIZE), F32)
    grid_spec = pltpu.PrefetchScalarGridSpec(
        num_scalar_prefetch=2, grid=(db, h),
        in_specs=[pl.BlockSpec(memory_space=pltpu.SMEM), tok, tok, tok, hbm, hbm],
        out_specs=tok,
        scratch_shapes=[page_buf, page_buf, pltpu.SemaphoreType.DMA((2, 2))])
    return pl.pallas_call(
        functools.partial(_sample_attn_kernel, ds=ds, past=past), grid_spec=grid_spec,
        out_shape=jax.ShapeDtypeStruct((db, h, d, t8), F32),
        compiler_params=_cparams(("arbitrary", "arbitrary")), name="sample_attn",
    )(idx.reshape(-1), phys.reshape(-1), _alibi_slopes(), qh, kh, vh, k_pool, v_pool)


def _split_w_in_a(w, f32_query):
    wb = w.astype(BF16)
    wdt = jnp.zeros((D_MODEL, LANES), BF16).at[:, :SSD_HEADS].set(wb[:, OFF_DT:OFF_Q])
    wq = w[:, OFF_Q:OFF_K] if f32_query else wb[:, OFF_Q:OFF_K]
    return [wb[:, :OFF_XBC], wb[:, OFF_XBC:OFF_DT], wdt, wq, wb[:, OFF_K:OFF_V], wb[:, OFF_V:]]


def _pad_seq(a, b, l, lp):
    a = a.reshape(b, l, a.shape[-1])
    return a if lp == l else jnp.pad(a, ((0, 0), (0, lp - l), (0, 0)))


def _round_up(n, m):
    return -(-n // m) * m


def _heads(a, b, l):
    return a.reshape(b, l, MOBA_HEADS, MOBA_HEAD_DIM).transpose(0, 2, 1, 3)


def _pad_tokens(a):
    l = a.shape[-2]
    return jnp.pad(a, ((0, 0),) * (a.ndim - 2) + ((0, _round_up(l, SUBLANES) - l), (0, 0)))


def _trunk(x, bsz, length, prompt, caches, p):
    ssm_conv0, ssm0, hgrn0, ffn_conv0 = caches[:4]
    m = bsz * length
    k_rows = v_rows = ssm_c = ssm_s = hgrn_s = None
    ffn_c = []
    for layer in range(DEPTH):
        nw = p['norm_mix'][layer].reshape(1, -1)
        if layer % 2 == 0:
            ia = layer // 2
            assert ia == 0
            ws = _split_w_in_a(p['w_in_a'][ia], f32_query=not prompt)
            outs = _inproj_a(x, nw, ws, prompt)
            z, xbc, dtr, q = outs[:4]
            lp = _round_up(length, SSD_CHUNK)
            y_ssd, hist, s_new = _ssd(
                _pad_seq(z, bsz, length, lp), _pad_seq(xbc, bsz, length, lp), _pad_seq(dtr, bsz, length, lp),
                ssm_conv0[ia], ssm0[ia], p['ssd_conv_w'][ia], p['ssd_conv_b'][ia], p['ssd_dt_bias'][ia],
                p['ssd_a_log'][ia], p['ssd_d'][ia], p['ssd_norm_w'][ia], length)
            y_ssd = y_ssd[:, :length].reshape(m, SSD_INNER)
            if prompt:
                assert bsz == 1
                kr, vr, ka, vt, km = outs[4:]
                o_att = _moba_prompt(q, ka, vt, km.reshape(-1, MOBA_INNER))
                k_rows, v_rows = jnp.swapaxes(kr, -1, -2)[None], jnp.swapaxes(vr, -1, -2)[None]
            else:
                k, v = outs[4:]
                k_pool, v_pool, page_table = caches[4:]
                k_pool_t = jnp.swapaxes(k_pool[ia], -1, -2)
                v_pool_t = jnp.swapaxes(v_pool[ia], -1, -2)
                qh, kh, vh = (_pad_tokens(_heads(t, bsz, length)) for t in (q, k, v))
                qt, kt, vt = (jnp.swapaxes(t, -1, -2) for t in (qh, kh, vh))
                km = _sample_kmean(k_pool_t, page_table)
                idx = _sample_select(qh, km)[:, :, :length, :MOBA_TOPK]
                o = _sample_attn(qt, kt, vt, idx, k_pool_t, v_pool_t, page_table, length)
                o_att = o[..., :length].transpose(0, 3, 1, 2).reshape(m, MOBA_INNER).astype(BF16)
                k_rows, v_rows = kh[:, :, :length], vh[:, :, :length]
            wo = p['w_out_a'][ia].astype(BF16)
            mix_acts, mix_ws = [y_ssd, o_att], [wo[:SSD_INNER], wo[SSD_INNER:]]
            ssm_c, ssm_s = hist, s_new
        else:
            ic = layer // 2
            assert ic == 0
            wc = p['w_in_c'][ic].astype(BF16)
            ws = [wc[:, :HGRN_K], wc[:, HGRN_K:2 * HGRN_K], wc[:, 2 * HGRN_K:2 * HGRN_K + HGRN_V],
                  wc[:, 2 * HGRN_K + HGRN_V:]]
            qa, fx, iv, g = _norm_matmul(x, nw, ws)
            lp = _round_up(length, HGRN_CHUNK)
            o, s_new = _gla(*(_pad_seq(t, bsz, length, lp) for t in (qa, fx, iv, g)),
                            p['hgrn_lb_gamma'], p['hgrn_norm_w'][ic], hgrn0[ic], length, layer)
            mix_acts, mix_ws = [o[:, :length].reshape(m, HGRN_V)], [p['w_out_c'][ic].astype(BF16)]
            hgrn_s = s_new
        nf = p['norm_final'].reshape(1, -1) if layer == DEPTH - 1 else None
        hist = ffn_conv0[layer, 0] if prompt else ffn_conv0[layer]
        x, fh = _ffn(x, mix_acts, mix_ws, p['norm_ffn'][layer].reshape(1, -1),
                     p['ffn_w_up'][layer].astype(BF16), p['ffn_conv_w'][layer],
                     p['ffn_conv_b'][layer].reshape(1, -1), p['ffn_w_down'][layer].astype(BF16), hist,
                     None if prompt else length, nf)
        ffn_c.append(fh[None] if prompt else fh)
    return (x.reshape(bsz, length, D_MODEL), k_rows[None], v_rows[None], ssm_s[None], ssm_c[None],
            hgrn_s[None], jnp.stack(ffn_c))


def kernel(x_prompt, x_sample, cache_k_pool, cache_v_pool, page_table, state_ssm, state_ssm_conv, state_hgrn,
           state_ffn_conv, norm_mix, norm_ffn, norm_final, w_in_a, w_out_a, ssd_conv_w, ssd_conv_b,
           ssd_dt_bias, ssd_a_log, ssd_d, ssd_norm_w, w_in_c, w_out_c, hgrn_lb_gamma, hgrn_norm_w, ffn_w_up,
           ffn_conv_w, ffn_conv_b, ffn_w_down):
    p = dict(norm_mix=norm_mix, norm_ffn=norm_ffn, norm_final=norm_final, w_in_a=w_in_a, w_out_a=w_out_a,
             ssd_conv_w=ssd_conv_w, ssd_conv_b=ssd_conv_b, ssd_dt_bias=ssd_dt_bias, ssd_a_log=ssd_a_log,
             ssd_d=ssd_d, ssd_norm_w=ssd_norm_w, w_in_c=w_in_c, w_out_c=w_out_c, hgrn_lb_gamma=hgrn_lb_gamma,
             hgrn_norm_w=hgrn_norm_w, ffn_w_up=ffn_w_up, ffn_conv_w=ffn_conv_w, ffn_conv_b=ffn_conv_b,
             ffn_w_down=ffn_w_down)
    bp, sp, _ = x_prompt.shape
    db, ds, _ = x_sample.shape
    na, nc = (DEPTH + 1) // 2, DEPTH // 2
    zeros = lambda *s: jnp.zeros(s, F32)
    prompt_caches = (zeros(na, bp, SSD_CONV - 1, SSD_CONV_DIM), zeros(na, bp, SSD_HEADS, SSD_HEAD_DIM, SSD_STATE),
                     zeros(nc, bp, HGRN_HEADS, HGRN_KEY_DIM, HGRN_VAL_DIM), zeros(DEPTH, bp, FFN_CONV - 1, FFN_DIM))
    outs_p = _trunk(x_prompt.reshape(bp * sp, D_MODEL), bp, sp, True, prompt_caches, p)
    sample_caches = (state_ssm_conv, state_ssm, state_hgrn, state_ffn_conv, cache_k_pool, cache_v_pool, page_table)
    outs_s = _trunk(x_sample.reshape(db * ds, D_MODEL), db, ds, False, sample_caches, p)
    yp, krp, vrp, ssp, scp, hgp, fcp = outs_p
    ys, krs, vrs, sss, scs, hgs, fcs = outs_s
    return (yp, ys, krp, vrp, ssp, scp, hgp, fcp, krs, vrs, sss, scs, hgs, fcs)
```

```python
import functools
import math

import numpy as np
import jax
import jax.numpy as jnp
from jax import lax
from jax.experimental import pallas as pl
from jax.experimental.pallas import tpu as pltpu

F32 = jnp.float32
BF16 = jnp.bfloat16
HIGHEST = lax.Precision.HIGHEST

D_MODEL = 1024
DEPTH = 2
PAGE_SIZE = 128
SSD_HEADS = 8
SSD_HEAD_DIM = 64
SSD_INNER = SSD_HEADS * SSD_HEAD_DIM
SSD_STATE = 64
SSD_GROUPS = 2
SSD_CONV = 4
SSD_CHUNK = 128
SSD_GN = SSD_GROUPS * SSD_STATE
SSD_CONV_DIM = SSD_INNER + 2 * SSD_GN
MOBA_HEADS = 8
MOBA_HEAD_DIM = 64
MOBA_INNER = MOBA_HEADS * MOBA_HEAD_DIM
MOBA_BLOCK = 256
MOBA_TOPK = 3
OFF_XBC = SSD_INNER
OFF_DT = OFF_XBC + SSD_CONV_DIM
OFF_Q = OFF_DT + SSD_HEADS
OFF_K = OFF_Q + MOBA_INNER
OFF_V = OFF_K + MOBA_INNER
HGRN_HEADS = 8
HGRN_KEY_DIM = 128
HGRN_VAL_DIM = D_MODEL // HGRN_HEADS
HGRN_K = HGRN_HEADS * HGRN_KEY_DIM
HGRN_V = HGRN_HEADS * HGRN_VAL_DIM
HGRN_CHUNK = 64
FFN_DIM = 2816
FFN_CONV = 3
RMS_EPS = 1e-6

LANES = 128
SUBLANES = 8
VMEM_LIMIT_BYTES = 56 * 1024 * 1024

ROW_TILE = 256
FFN_COL_CHUNK = 256
NEG_BIG = -1e30
LOG2E = math.log2(math.e)
ALIBI_PARTS = 3
MOBA_VT_ROWS = MOBA_HEAD_DIM + 16
BOUND_SLACK = 0.02
UNDERFLOW_LOG2 = 160.0
MOBA_AHEAD = 2
MOBA_SLOTS = MOBA_AHEAD + 1
MOBA_UNROLL = 4 * MOBA_SLOTS


def _cparams(sem, flags=None):
    return pltpu.CompilerParams(dimension_semantics=sem, vmem_limit_bytes=VMEM_LIMIT_BYTES, flags=flags)


def _const_spec(shape):
    nd = len(shape)
    return pl.BlockSpec(shape, lambda *_: (0,) * nd)


def _rms(x, w):
    y = x * lax.rsqrt(jnp.mean(x * x, axis=-1, keepdims=True) + RMS_EPS)
    return y * w


def _rms_bf16(x, w):
    return _rms(x, w).astype(BF16)


def _dot(a, b):
    return jnp.dot(a, b, preferred_element_type=F32)


def _dot_split(a, b):
    a_hi = a.astype(BF16)
    a_lo = (a - a_hi.astype(F32)).astype(BF16)
    b_hi = b.astype(BF16)
    b_lo = (b - b_hi.astype(F32)).astype(BF16)
    return _dot(jnp.concatenate([a_hi, a_lo, a_hi], axis=1), jnp.concatenate([b_hi, b_hi, b_lo], axis=0))


def _dot_nt(a, b):
    return lax.dot_general(a, b, (((1,), (1,)), ((), ())), preferred_element_type=F32)


def _dot_tn(a, b):
    return lax.dot_general(a, b, (((0,), (0,)), ((), ())), preferred_element_type=F32)


def _tril_f32(n):
    r = lax.broadcasted_iota(jnp.int32, (n, n), 0)
    c = lax.broadcasted_iota(jnp.int32, (n, n), 1)
    return r >= c


def _cumsum_rows(x, q, split):
    if not split:
        return jnp.dot(_tril_f32(q).astype(F32), x, precision=HIGHEST, preferred_element_type=F32)
    tri = _tril_f32(q).astype(BF16)
    acc = None
    rest = x
    for _ in range(3):
        term = rest.astype(BF16)
        part = _dot(tri, term)
        acc = part if acc is None else acc + part
        rest = rest - term.astype(F32)
    return acc


def _silu(x):
    return x * jax.nn.sigmoid(x)


def _inproj_a_kernel(x_ref, nw_ref, wz, wxbc, wdt, wq, wk, wv, *rest, prompt):
    outs = rest[1:] if prompt else rest
    h = _rms(x_ref[...], nw_ref[...])
    hb = h.astype(BF16)
    z_o, xbc_o, dt_o, q_o = outs[:4]
    z_o[...] = _dot(hb, wz[...])
    xbc_o[...] = _dot(hb, wxbc[...])
    dt_o[...] = _dot(hb, wdt[...])
    if wq.dtype == F32:
        q = jnp.dot(h, wq[...], precision=HIGHEST, preferred_element_type=F32)
    else:
        q = _dot(hb, wq[...])
    q_o[...] = q
    k = _dot(hb, wk[...])
    v = _dot(hb, wv[...])
    if not prompt:
        k_o, v_o = outs[4:]
        k_o[...] = k
        v_o[...] = v
        return
    kr_o, vr_o, ka_o, vt_o, km_o, bnd_o = outs[4:]
    tm = k.shape[0]
    head_sel = rest[0][...]
    k_norm = jnp.sqrt(jnp.max(_dot((k * k).astype(BF16), head_sel), axis=0, keepdims=True))
    q_norm = jnp.sqrt(jnp.max(_dot((q * q).astype(BF16), head_sel), axis=0, keepdims=True))
    qk_min = jnp.min(_dot((q * k).astype(BF16), head_sel), axis=0, keepdims=True)
    row8 = lax.broadcasted_iota(jnp.int32, (SUBLANES, LANES), 0)
    bnd_o[0] = jnp.where(row8 == 0, k_norm, jnp.where(row8 == 1, q_norm, jnp.where(row8 == 2, qk_min, 0.0)))
    lane = lax.broadcasted_iota(jnp.int32, (tm, LANES), 1)
    key_aug = _key_aug_columns(tm)
    for hp in range(MOBA_HEADS // 2):
        pair = k[:, hp * LANES:(hp + 1) * LANES]
        for j, kh in enumerate((pair, pltpu.roll(pair, MOBA_HEAD_DIM, 1))):
            ka_o[2 * hp + j, 0] = jnp.where(lane < MOBA_HEAD_DIM, kh, key_aug).astype(BF16)
    k_t = k.T
    v_t = v.T
    for p in range(tm // PAGE_SIZE):
        toks = slice(p * PAGE_SIZE, (p + 1) * PAGE_SIZE)
        for h in range(MOBA_HEADS):
            dims = slice(h * MOBA_HEAD_DIM, (h + 1) * MOBA_HEAD_DIM)
            kr_o[p, h] = k_t[dims, toks]
            vr_o[p, h] = v_t[dims, toks]
    ones_rows = (lax.broadcasted_iota(jnp.int32, (MOBA_VT_ROWS - MOBA_HEAD_DIM, tm), 0) == 0).astype(F32)
    pieces = []
    for h in range(MOBA_HEADS):
        pieces += [v_t[h * MOBA_HEAD_DIM:(h + 1) * MOBA_HEAD_DIM, :], ones_rows]
    vt_o[0] = jnp.concatenate(pieces, axis=0).astype(BF16)
    km_o[0] = jnp.mean(k, axis=0, keepdims=True)


def _inproj_a(x, nw, ws, prompt):
    m = x.shape[0]
    tm = ROW_TILE if prompt else m
    assert m % tm == 0
    if prompt:
        assert tm == MOBA_BLOCK
    nt = m // tm
    widths = [w.shape[1] for w in ws]
    row = lambda i: (i, 0)
    in_specs = [pl.BlockSpec((tm, D_MODEL), row), _const_spec((1, D_MODEL))]
    in_specs += [_const_spec(w.shape) for w in ws]
    out_shape = [jax.ShapeDtypeStruct((m, n), F32) for n in widths[:4]]
    out_specs = [pl.BlockSpec((tm, n), row) for n in widths[:4]]
    if prompt:
        npg = m // PAGE_SIZE
        ppt = tm // PAGE_SIZE
        rows_shape = (npg, MOBA_HEADS, MOBA_HEAD_DIM, PAGE_SIZE)
        rows_spec = pl.BlockSpec((ppt, MOBA_HEADS, MOBA_HEAD_DIM, PAGE_SIZE), lambda i: (i, 0, 0, 0))
        out_shape += [jax.ShapeDtypeStruct(rows_shape, F32)] * 2
        out_specs += [rows_spec, rows_spec]
        out_shape += [jax.ShapeDtypeStruct((MOBA_HEADS, nt, tm, LANES), BF16),
                      jax.ShapeDtypeStruct((nt, MOBA_HEADS * MOBA_VT_ROWS, tm), BF16),
                      jax.ShapeDtypeStruct((nt, 1, MOBA_INNER), F32),
                      jax.ShapeDtypeStruct((nt, SUBLANES, LANES), F32)]
        out_specs += [pl.BlockSpec((MOBA_HEADS, 1, tm, LANES), lambda i: (0, i, 0, 0)),
                      pl.BlockSpec((1, MOBA_HEADS * MOBA_VT_ROWS, tm), lambda i: (i, 0, 0)),
                      pl.BlockSpec((1, 1, MOBA_INNER), lambda i: (i, 0, 0)),
                      pl.BlockSpec((1, SUBLANES, LANES), lambda i: (i, 0, 0))]
        head_sel = (jnp.arange(MOBA_INNER)[:, None] // MOBA_HEAD_DIM == jnp.arange(LANES)[None, :]).astype(BF16)
        extra, extra_specs = [head_sel], [_const_spec(head_sel.shape)]
    else:
        out_shape += [jax.ShapeDtypeStruct((m, MOBA_INNER), F32)] * 2
        out_specs += [pl.BlockSpec((tm, MOBA_INNER), row)] * 2
        extra, extra_specs = [], []
    return pl.pallas_call(
        functools.partial(_inproj_a_kernel, prompt=prompt),
        grid=(nt,), in_specs=in_specs + extra_specs, out_specs=out_specs, out_shape=out_shape,
        compiler_params=_cparams(("arbitrary",)), name="inproj_a",
    )(x, nw, *ws, *extra)


def _norm_matmul_kernel(x_ref, nw_ref, *refs):
    n = len(refs) // 2
    hb = _rms_bf16(x_ref[...], nw_ref[...])
    for w, o in zip(refs[:n], refs[n:]):
        o[...] = _dot(hb, w[...])


def _norm_matmul(x, nw, ws):
    m = x.shape[0]
    tm = min(ROW_TILE, m)
    assert m % tm == 0
    row = lambda i: (i, 0)
    in_specs = [pl.BlockSpec((tm, D_MODEL), row), _const_spec((1, D_MODEL))]
    in_specs += [_const_spec(w.shape) for w in ws]
    return pl.pallas_call(
        _norm_matmul_kernel, grid=(m // tm,), in_specs=in_specs,
        out_specs=[pl.BlockSpec((tm, w.shape[1]), row) for w in ws],
        out_shape=[jax.ShapeDtypeStruct((m, w.shape[1]), F32) for w in ws],
        compiler_params=_cparams(("arbitrary",)), name="norm_matmul",
    )(x, nw, *ws)


def _ffn_kernel(*refs, tm, seq_len, final_norm, n_mix):
    it = iter(refs)
    x_ref = next(it)
    mix_refs = [next(it) for _ in range(2 * n_mix)]
    nw_ref, wup_ref, cw_ref, cb_ref, wdn_ref = (next(it) for _ in range(5))
    if seq_len is None:
        hist_ref = next(it)
    else:
        h1_ref, h2_ref = next(it), next(it)
    nf_ref = next(it) if final_norm else None
    o_ref = next(it)
    tail_ref = next(it)
    up_s, act_s = next(it), next(it)

    i = pl.program_id(0)
    f = FFN_DIM
    if seq_len is None:
        @pl.when(i == 0)
        def _():
            up_s[0:SUBLANES, 0:f] = hist_ref[...]
    else:
        up_s[0:SUBLANES, 0:f] = jnp.zeros((SUBLANES, f), F32)

    x = x_ref[...]
    for a_ref, w_ref in zip(mix_refs[:n_mix], mix_refs[n_mix:]):
        x = x + _dot(a_ref[...], w_ref[...])
    hb = _rms_bf16(x, nw_ref[...])
    up_s[SUBLANES:SUBLANES + tm, :] = _dot(hb, wup_ref[...])

    if seq_len is not None:
        t = lax.broadcasted_iota(jnp.int32, (tm, 1), 0) % seq_len
    for c in range(0, f, FFN_COL_CHUNK):
        cols = slice(c, c + FFN_COL_CHUNK)
        g0 = up_s[SUBLANES:SUBLANES + tm, cols]
        g1 = up_s[SUBLANES - 1:SUBLANES - 1 + tm, cols]
        g2 = up_s[SUBLANES - 2:SUBLANES - 2 + tm, cols]
        if seq_len is not None:
            g1 = jnp.where(t >= 1, g1, h1_ref[:, cols])
            g2 = jnp.where(t >= 2, g2, h2_ref[:, cols])
        val = up_s[SUBLANES:SUBLANES + tm, f + c:f + c + FFN_COL_CHUNK]
        conv = g2 * cw_ref[0:1, cols]
        conv = conv + g1 * cw_ref[1:2, cols]
        conv = conv + g0 * cw_ref[2:3, cols]
        conv = conv + cb_ref[:, cols]
        act_s[:, cols] = (_silu(conv) * val).astype(BF16)

    out = x + _dot(act_s[...], wdn_ref[...])
    o_ref[...] = _rms(out, nf_ref[...]) if final_norm else out

    if seq_len is None:
        last = up_s[tm:tm + SUBLANES, 0:f]
        up_s[0:SUBLANES, 0:f] = last

        @pl.when(i == pl.num_programs(0) - 1)
        def _():
            tail_ref[...] = last
    else:
        tail_ref[...] = up_s[SUBLANES:SUBLANES + tm, 0:f]


def _ffn(x, mix_acts, mix_ws, nw, wup, cw, cb, wdn, hist, seq_len, nf):
    assert FFN_CONV == 3
    m = x.shape[0]
    f = FFN_DIM
    final_norm = nf is not None
    row = lambda i: (i, 0)
    if seq_len is None:
        tm = ROW_TILE
        hist8 = jnp.zeros((SUBLANES, f), F32).at[SUBLANES - 2:].set(hist)
        extra = [hist8]
        extra_specs = [_const_spec((SUBLANES, f))]
        tail_shape = (SUBLANES, f)
    else:
        tm = m
        nb = m // seq_len
        assert seq_len >= 2 and nb * seq_len == m
        h1 = jnp.zeros((nb, seq_len, f), F32).at[:, 0].set(hist[:, 1]).reshape(m, f)
        h2 = jnp.zeros((nb, seq_len, f), F32).at[:, 0:2].set(hist).reshape(m, f)
        extra = [h1, h2]
        extra_specs = [_const_spec((m, f))] * 2
        tail_shape = (m, f)
    assert m % tm == 0
    ins = [x] + list(mix_acts) + list(mix_ws) + [nw, wup, cw, cb, wdn] + extra
    in_specs = [pl.BlockSpec((tm, D_MODEL), row)]
    in_specs += [pl.BlockSpec((tm, a.shape[1]), row) for a in mix_acts]
    in_specs += [_const_spec(w.shape) for w in mix_ws]
    in_specs += [_const_spec((1, D_MODEL)), _const_spec(wup.shape), _const_spec(cw.shape),
                 _const_spec(cb.shape), _const_spec(wdn.shape)] + extra_specs
    if final_norm:
        ins.append(nf)
        in_specs.append(_const_spec((1, D_MODEL)))
    xo, tail = pl.pallas_call(
        functools.partial(_ffn_kernel, tm=tm, seq_len=seq_len, final_norm=final_norm, n_mix=len(mix_acts)),
        grid=(m // tm,), in_specs=in_specs,
        out_specs=[pl.BlockSpec((tm, D_MODEL), row), _const_spec(tail_shape)],
        out_shape=[jax.ShapeDtypeStruct((m, D_MODEL), F32), jax.ShapeDtypeStruct(tail_shape, F32)],
        scratch_shapes=[pltpu.VMEM((SUBLANES + tm, 2 * f), F32), pltpu.VMEM((tm, f), BF16)],
        compiler_params=_cparams(("arbitrary",)), name="conv_ffn",
    )(*ins)
    if seq_len is None:
        new_hist = tail[SUBLANES - 2:]
    else:
        new_hist = tail.reshape(m // seq_len, seq_len, f)[:, seq_len - 2:]
    return xo, new_hist


def _ssd_kernel(z_ref, xbc_ref, dt_ref, hist_ref, s0_ref, cw_ref, cb_ref, dtb_ref, alog_ref, dsk_ref,
                nw_ref, y_ref, hist_o, s_o, cbuf, st, ybuf, *, q, seq_len, n_chunks):
    c = pl.program_id(1)
    hrows = SSD_CONV - 1
    h0 = SUBLANES - hrows

    @pl.when(c == 0)
    def _():
        cbuf[0:SUBLANES, :] = hist_ref[0]
        st[...] = s0_ref[0]

    cbuf[SUBLANES:SUBLANES + q, :] = xbc_ref[0]
    conv = cbuf[h0:h0 + q, :] * cw_ref[0:1, :]
    for j in range(1, SSD_CONV):
        conv = conv + cbuf[h0 + j:h0 + j + q, :] * cw_ref[j:j + 1, :]
    xc = _silu(conv + cb_ref[...])

    xdt_raw = dt_ref[0] + dtb_ref[...]
    dt = jnp.maximum(xdt_raw, 0.0) + jnp.log1p(jnp.exp(-jnp.abs(xdt_raw)))
    if seq_len % q != 0:
        row = c * q + lax.broadcasted_iota(jnp.int32, (q, 1), 0)
        dt = jnp.where(row < seq_len, dt, 0.0)
    a = -jnp.exp(alog_ref[...])
    acum = _cumsum_rows(dt * a, q, split=False)
    acum_t = acum.T
    a_last = acum[q - 1:q, :]
    e_acum = jnp.exp(acum)
    w_in = jnp.exp(a_last - acum) * dt
    e_last = jnp.exp(a_last)
    tril = _tril_f32(q)

    rep = SSD_HEADS // SSD_GROUPS
    cbs, bms, cms = [], [], []
    for g in range(SSD_GROUPS):
        bm = xc[:, SSD_INNER + g * SSD_STATE:SSD_INNER + (g + 1) * SSD_STATE]
        cm = xc[:, SSD_INNER + SSD_GN + g * SSD_STATE:SSD_INNER + SSD_GN + (g + 1) * SSD_STATE]
        bms.append(bm)
        cms.append(cm)
        cbs.append(_dot_nt(cm.astype(BF16), bm.astype(BF16)))

    for h in range(SSD_HEADS):
        g = h // rep
        xs = xc[:, h * SSD_HEAD_DIM:(h + 1) * SSD_HEAD_DIM]
        seg = acum[:, h:h + 1] - acum_t[h:h + 1, :]
        decay = jnp.exp(jnp.where(tril, seg, -jnp.inf))
        y = _dot((cbs[g] * decay).astype(BF16), (xs * dt[:, h:h + 1]).astype(BF16))
        s_h = st[h]
        y = y + _dot_nt((cms[g] * e_acum[:, h:h + 1]).astype(BF16), s_h.astype(BF16))
        st[h] = s_h * e_last[:, h:h + 1] + _dot_tn((xs * w_in[:, h:h + 1]).astype(BF16),
                                                    bms[g].astype(BF16))
        ybuf[:, h * SSD_HEAD_DIM:(h + 1) * SSD_HEAD_DIM] = y + xs * dsk_ref[:, h:h + 1]

    yz = ybuf[...] * _silu(z_ref[0])
    gw = SSD_INNER // SSD_GROUPS
    for g in range(SSD_GROUPS):
        yg = yz[:, g * gw:(g + 1) * gw]
        yn = yg * lax.rsqrt(jnp.mean(yg * yg, axis=-1, keepdims=True) + RMS_EPS)
        y_ref[0, :, g * gw:(g + 1) * gw] = (yn * nw_ref[:, g * gw:(g + 1) * gw]).astype(BF16)

    @pl.when(c == n_chunks - 1)
    def _():
        l_last = seq_len - (n_chunks - 1) * q
        hist_o[0] = cbuf[l_last:l_last + SUBLANES, :]
        s_o[0] = st[...]

    cbuf[0:SUBLANES, :] = cbuf[q:q + SUBLANES, :]


def _pad_lanes(v, fill=0.0):
    return jnp.full((1, LANES), fill, F32).at[0, :v.shape[0]].set(v.astype(F32))


def _ssd(z, xbc, dtr, hist, s0, conv_w, conv_b, dt_bias, a_log, d_skip, norm_w, seq_len):
    b, lp, _ = z.shape
    q = SSD_CHUNK
    assert lp % q == 0 and lp - seq_len < q
    hrows = SSD_CONV - 1
    hist8 = jnp.zeros((b, SUBLANES, SSD_CONV_DIM), F32).at[:, SUBLANES - hrows:].set(hist)
    cw8 = jnp.zeros((SUBLANES, SSD_CONV_DIM), F32).at[:SSD_CONV].set(conv_w)
    seq = lambda i, c: (i, c, 0)
    per_b3 = lambda i, c: (i, 0, 0)
    per_b4 = lambda i, c: (i, 0, 0, 0)
    in_specs = [pl.BlockSpec((1, q, SSD_INNER), seq), pl.BlockSpec((1, q, SSD_CONV_DIM), seq),
                pl.BlockSpec((1, q, LANES), seq), pl.BlockSpec((1, SUBLANES, SSD_CONV_DIM), per_b3),
                pl.BlockSpec((1, SSD_HEADS, SSD_HEAD_DIM, SSD_STATE), per_b4),
                _const_spec((SUBLANES, SSD_CONV_DIM)), _const_spec((1, SSD_CONV_DIM)),
                _const_spec((1, LANES)), _const_spec((1, LANES)), _const_spec((1, LANES)),
                _const_spec((1, SSD_INNER))]
    out_shape = [jax.ShapeDtypeStruct((b, lp, SSD_INNER), BF16),
                 jax.ShapeDtypeStruct((b, SUBLANES, SSD_CONV_DIM), F32),
                 jax.ShapeDtypeStruct((b, SSD_HEADS, SSD_HEAD_DIM, SSD_STATE), F32)]
    out_specs = [pl.BlockSpec((1, q, SSD_INNER), seq), pl.BlockSpec((1, SUBLANES, SSD_CONV_DIM), per_b3),
                 pl.BlockSpec((1, SSD_HEADS, SSD_HEAD_DIM, SSD_STATE), per_b4)]
    y, hist_o, s_o = pl.pallas_call(
        functools.partial(_ssd_kernel, q=q, seq_len=seq_len, n_chunks=lp // q),
        grid=(b, lp // q), in_specs=in_specs, out_specs=out_specs, out_shape=out_shape,
        scratch_shapes=[pltpu.VMEM((SUBLANES + q, SSD_CONV_DIM), F32),
                        pltpu.VMEM((SSD_HEADS, SSD_HEAD_DIM, SSD_STATE), F32),
                        pltpu.VMEM((q, SSD_INNER), F32)],
        compiler_params=_cparams(("arbitrary", "arbitrary")), name="ssd",
    )(z, xbc, dtr, hist8, s0, cw8, conv_b.reshape(1, -1), _pad_lanes(dt_bias), _pad_lanes(a_log),
      _pad_lanes(d_skip), norm_w.reshape(1, -1))
    return y, hist_o[:, SUBLANES - hrows:], s_o


def _gla_kernel(q_ref, fx_ref, iv_ref, g_ref, gam_ref, nw_ref, s0_ref, o_ref, s_o, st_t, *, q, seq_len,
                layer):
    c = pl.program_id(1)
    n_chunks = pl.num_programs(1)

    @pl.when(c == 0)
    def _():
        for h in range(HGRN_HEADS):
            st_t[h] = s0_ref[0, h].T

    rows = [gam_ref[l:l + 1, :] for l in range(DEPTH)]
    mx = functools.reduce(jnp.maximum, rows)
    es = [jnp.exp(r - mx) for r in rows]
    lb = sum(es[1:layer + 1]) / sum(es) if layer >= 1 else jnp.zeros_like(mx)

    f = lb + (1.0 - lb) * jax.nn.sigmoid(fx_ref[0])
    if seq_len % q != 0:
        row = c * q + lax.broadcasted_iota(jnp.int32, (q, 1), 0)
        f = jnp.where(row < seq_len, f, 1.0)
    kk = 1.0 - f
    b = _cumsum_rows(jnp.log(f), q, split=True)
    b_last = b[q - 1:q, :]
    qd = q_ref[0] * jnp.exp(b)
    kd = kk * jnp.exp(-b)
    kdec = kk * jnp.exp(b_last - b)
    e_last = jnp.exp(b_last)
    tril = _tril_f32(q)
    iv = iv_ref[0]

    ks = [slice(h * HGRN_KEY_DIM, (h + 1) * HGRN_KEY_DIM) for h in range(HGRN_HEADS)]
    vs = [slice(h * HGRN_VAL_DIM, (h + 1) * HGRN_VAL_DIM) for h in range(HGRN_HEADS)]
    qd_b = [qd[:, s].astype(BF16) for s in ks]
    v_b = [iv[:, s].astype(BF16) for s in vs]
    att = [_dot_nt(qd_b[h], kd[:, ks[h]].astype(BF16)) for h in range(HGRN_HEADS)]
    o_state = [_dot_nt(qd_b[h], st_t[h].astype(BF16)) for h in range(HGRN_HEADS)]
    kv = [_dot_tn(v_b[h], kdec[:, ks[h]].astype(BF16)) for h in range(HGRN_HEADS)]
    for h in range(HGRN_HEADS):
        o = _dot(jnp.where(tril, att[h], 0.0).astype(BF16), v_b[h]) + o_state[h]
        st_t[h] = st_t[h] * e_last[:, ks[h]] + kv[h]
        on = o * lax.rsqrt(jnp.mean(o * o, axis=-1, keepdims=True) + RMS_EPS) * nw_ref[...]
        o_ref[0, :, vs[h]] = (on * _silu(g_ref[0, :, vs[h]])).astype(BF16)

    @pl.when(c == n_chunks - 1)
    def _():
        for h in range(HGRN_HEADS):
            s_o[0, h] = st_t[h].T


def _gla(qa, fx, iv, g, gamma, norm_w, s0, seq_len, layer):
    b, lp, _ = qa.shape
    q = HGRN_CHUNK
    assert lp % q == 0 and lp - seq_len < q
    gam8 = jnp.zeros((SUBLANES, HGRN_K), F32).at[:DEPTH].set(gamma)
    seq = lambda i, c: (i, c, 0)
    per_b4 = lambda i, c: (i, 0, 0, 0)
    st_shape = (HGRN_HEADS, HGRN_KEY_DIM, HGRN_VAL_DIM)
    in_specs = [pl.BlockSpec((1, q, HGRN_K), seq), pl.BlockSpec((1, q, HGRN_K), seq),
                pl.BlockSpec((1, q, HGRN_V), seq), pl.BlockSpec((1, q, HGRN_V), seq),
                _const_spec((SUBLANES, HGRN_K)), _const_spec((1, HGRN_VAL_DIM)),
                pl.BlockSpec((1,) + st_shape, per_b4)]
    o, s_o = pl.pallas_call(
        functools.partial(_gla_kernel, q=q, seq_len=seq_len, layer=layer),
        grid=(b, lp // q), in_specs=in_specs,
        out_specs=[pl.BlockSpec((1, q, HGRN_V), seq), pl.BlockSpec((1,) + st_shape, per_b4)],
        out_shape=[jax.ShapeDtypeStruct((b, lp, HGRN_V), BF16), jax.ShapeDtypeStruct((b,) + st_shape, F32)],
        scratch_shapes=[pltpu.VMEM((HGRN_HEADS, HGRN_VAL_DIM, HGRN_KEY_DIM), F32)],
        compiler_params=_cparams(("arbitrary", "arbitrary")), name="gla",
    )(qa, fx, iv, g, gam8, norm_w.reshape(1, -1), s0)
    return o, s_o


def _select_topk_rows(gate_t, n_valid, nblk, width):
    blk = lax.broadcasted_iota(jnp.int32, (nblk, width), 0)
    blk_f = blk.astype(F32)
    g = jnp.where(blk < n_valid, gate_t, -jnp.inf)
    sel = jnp.zeros((nblk, width), F32)
    for _ in range(MOBA_TOPK):
        mx = jnp.max(g, axis=0, keepdims=True)
        first = jnp.min(jnp.where(g == mx, blk_f, float(nblk)), axis=0, keepdims=True)
        pick = (blk_f == first) & (mx > -jnp.inf)
        sel = jnp.where(pick, 1.0, sel)
        g = jnp.where(pick, -jnp.inf, g)
    return jnp.where(sel > 0.0, 0.0, -jnp.inf)


def _key_aug_columns(tm):
    lane = lax.broadcasted_iota(jnp.int32, (tm, LANES), 1)
    r = lax.broadcasted_iota(jnp.int32, (tm, LANES), 0).astype(F32)
    return jnp.where((lane >= MOBA_HEAD_DIM) & (lane < MOBA_HEAD_DIM + ALIBI_PARTS), r, 0.0)


def _moba_prompt_kernel(slopes_ref, kn_ref, qn_ref, dmin_ref, q_ref, ka_ref, vt_ref, km_ref, o_ref, brow_s,
                        ubuf, *, nblk):
    hp = pl.program_id(0)
    i = pl.program_id(1)
    tq = MOBA_BLOCK
    d = MOBA_HEAD_DIM
    q = q_ref[...]
    lane_k = lax.broadcasted_iota(jnp.int32, (nblk, 2 * d), 1)
    causal =(lax.broadcasted_iota(jnp.int32, (MOBA_BLOCK, tq), 0)
              <= lax.broadcasted_iota(jnp.int32, (MOBA_BLOCK, tq), 1))
    blk_f = lax.broadcasted_iota(jnp.int32, (nblk, tq), 0).astype(F32)
    col_f = lax.broadcasted_iota(jnp.int32, (nblk, tq), 1).astype(F32)
    aug_row = lax.broadcasted_iota(jnp.int32, (d, tq), 0)
    i_f = i.astype(F32)

    q_t = q.T
    km = km_ref[...]
    km2 = jnp.concatenate([jnp.where(lane_k < d, km, 0.0), jnp.where(lane_k >= d, km, 0.0)], axis=0)
    gate_both = _dot_split(km2, q_t)

    rhs = []
    for j in range(2):
        a2 = slopes_ref[2 * hp + j] * LOG2E
        sel = _select_topk_rows(gate_both[j * nblk:(j + 1) * nblk, :], i, nblk, tq)
        brow_s[j] = sel + a2 * (MOBA_BLOCK * (blk_f - i_f) - col_f)
        rest = jnp.full((d, tq), a2, F32)
        aug = jnp.zeros((d, tq), F32)
        for part in range(ALIBI_PARTS):
            term = rest.astype(BF16).astype(F32)
            aug = jnp.where(aug_row == part, term, aug)
            rest = rest - term
        q_h = q_t[j * d:(j + 1) * d, :] * (d ** -0.5 * LOG2E)
        rhs.append(jnp.concatenate([q_h, aug], axis=0).astype(BF16))

    def block_update(st, u, b_row, v_t):
        m_run, o_t = st
        m_new = jnp.maximum(m_run, jnp.max(u, axis=0, keepdims=True) + b_row)
        alpha = jnp.exp2(m_run - m_new)
        p = jnp.exp2(u - (m_new - b_row))
        return m_new, alpha * o_t + _dot(v_t, p.astype(BF16))

    def score(n, slot):
        for j in range(2):
            ubuf[slot, j] = _dot(ka_ref[j, n], rhs[j])

    def consume(n, slot, sts, own):
        new = []
        for j in range(2):
            u = ubuf[slot, j]
            if own:
                u = jnp.where(causal, u, -jnp.inf)
                b_row = -(slopes_ref[2 * hp + j] * LOG2E) * col_f[0:1, :]
            else:
                b_row = brow_s[j, pl.ds(n, 1), :]
            new.append(block_update(sts[j], u, b_row, vt_ref[n, j * MOBA_VT_ROWS:(j + 1) * MOBA_VT_ROWS, :]))
        return tuple(new)

    def step(n, slot, slot_ahead, sts):
        score(jnp.minimum(n + MOBA_AHEAD, i), slot_ahead)
        return consume(n, slot, sts, False)

    def group(base, unroll, sts):
        for k in range(unroll):
            sts = step(base + k, k % MOBA_SLOTS, (k + MOBA_AHEAD) % MOBA_SLOTS, sts)
        return sts

    def first_needed(j):
        hd = 2 * hp + j
        a2 = slopes_ref[hd] * LOG2E
        c = d ** -0.5 * LOG2E
        qn = qn_ref[i * MOBA_HEADS + hd]
        floor = c * (dmin_ref[i * MOBA_HEADS + hd] - BOUND_SLACK * qn * kn_ref[i * MOBA_HEADS + hd])

        def skippable(n):
            ceil = (c * (1.0 + BOUND_SLACK) * qn * kn_ref[n * MOBA_HEADS + hd]
                    + a2 * (MOBA_BLOCK * (n - i).astype(F32) + (MOBA_BLOCK - 1)))
            return (n < i) & (ceil < floor - UNDERFLOW_LOG2)

        return lax.while_loop(skippable, lambda n: n + 1, jnp.int32(0))

    start = (jnp.minimum(first_needed(0), first_needed(1)) // MOBA_SLOTS) * MOBA_SLOTS

    init = (jnp.full((1, tq), NEG_BIG, F32), jnp.zeros((MOBA_VT_ROWS, tq), F32))
    for k in range(MOBA_AHEAD):
        score(jnp.minimum(start + k, i), k)
    n_big = (i - start) // MOBA_UNROLL
    sts = lax.fori_loop(0, n_big, lambda g, s: group(start + g * MOBA_UNROLL, MOBA_UNROLL, s), (init, init))
    base = start + n_big * MOBA_UNROLL
    n_small = (i - base) // MOBA_SLOTS
    sts = lax.fori_loop(0, n_small, lambda g, s: group(base + g * MOBA_SLOTS, MOBA_SLOTS, s), sts)
    base = base + n_small * MOBA_SLOTS
    rem = i - base
    for k in range(MOBA_SLOTS - 1):
        sts = lax.cond(rem > k, lambda s, k=k: step(base + k, k, (k + MOBA_AHEAD) % MOBA_SLOTS, s),
                       lambda s: s, sts)
    outs = []
    for _, o_t in consume(i, rem, sts, True):
        outs.append(o_t[0:d, :] / o_t[d:d + 1, :])
    o_ref[...] = jnp.concatenate(outs, axis=0).T.astype(BF16)


def _alibi_slopes():
    return jnp.asarray(np.exp2(-8.0 * np.arange(1, MOBA_HEADS + 1) / MOBA_HEADS), dtype=F32)


def _moba_prompt(q, ka, vt, km, bounds):
    s = q.shape[0]
    nblk = s // MOBA_BLOCK
    assert nblk * MOBA_BLOCK == s and 2 * MOBA_HEAD_DIM == LANES
    pairs = MOBA_HEADS // 2
    tables = [bounds[:, r, :MOBA_HEADS].reshape(-1) for r in range(3)]
    smem = pl.BlockSpec(memory_space=pltpu.SMEM)
    return pl.pallas_call(
        functools.partial(_moba_prompt_kernel, nblk=nblk),
        grid=(pairs, nblk),
        in_specs=[smem, smem, smem, smem,
                  pl.BlockSpec((MOBA_BLOCK, LANES), lambda hp, i: (i, hp)),
                  pl.BlockSpec((2, nblk, MOBA_BLOCK, LANES), lambda hp, i: (hp, 0, 0, 0)),
                  pl.BlockSpec((nblk, 2 * MOBA_VT_ROWS, MOBA_BLOCK), lambda hp, i: (0, hp, 0)),
                  pl.BlockSpec((nblk, LANES), lambda hp, i: (0, hp))],
        out_specs=pl.BlockSpec((MOBA_BLOCK, LANES), lambda hp, i: (i, hp)),
        out_shape=jax.ShapeDtypeStruct((s, MOBA_INNER), BF16),
        scratch_shapes=[pltpu.VMEM((2, nblk, MOBA_BLOCK), F32),
                        pltpu.VMEM((MOBA_SLOTS, 2, MOBA_BLOCK, MOBA_BLOCK), F32)],
        compiler_params=_cparams(("arbitrary", "arbitrary")), name="moba_prompt",
    )(_alibi_slopes(), *tables, q, ka, vt, km)


PAGES_PER_STEP = 32


def _kmean_kernel(pt_ref, *refs):
    del pt_ref
    o_ref = refs[-1]
    s = pl.program_id(1)
    ppb = MOBA_BLOCK // PAGE_SIZE
    bps = PAGES_PER_STEP // ppb

    @pl.when(s == 0)
    def _():
        o_ref[...] = jnp.zeros(o_ref.shape, F32)

    acc = o_ref[0]
    lane = lax.broadcasted_iota(jnp.int32, acc.shape, 2)
    for m in range(bps):
        x = refs[m * ppb][0]
        for p in range(1, ppb):
            x = x + refs[m * ppb + p][0]
        mean = jnp.sum(x, axis=-1, keepdims=True) * (1.0 / MOBA_BLOCK)
        acc = jnp.where(lane == s * bps + m, mean, acc)
    o_ref[0] = acc


def _sample_kmean(k_pool_t, page_table):
    db, n_pages = page_table.shape
    ppb = MOBA_BLOCK // PAGE_SIZE
    n_full = n_pages // ppb
    assert n_full * ppb == n_pages and n_pages % PAGES_PER_STEP == 0
    steps = n_pages // PAGES_PER_STEP
    blk = (1, MOBA_HEADS, MOBA_HEAD_DIM, PAGE_SIZE)

    def page_spec(p):
        return pl.BlockSpec(blk, lambda b, s, pt: (pt[b * n_pages + s * PAGES_PER_STEP + p], 0, 0, 0))

    out_blk = (1, MOBA_HEADS, MOBA_HEAD_DIM, n_full)
    grid_spec = pltpu.PrefetchScalarGridSpec(
        num_scalar_prefetch=1, grid=(db, steps),
        in_specs=[page_spec(p) for p in range(PAGES_PER_STEP)],
        out_specs=pl.BlockSpec(out_blk, lambda b, s, pt: (b, 0, 0, 0)))
    return pl.pallas_call(
        _kmean_kernel, grid_spec=grid_spec,
        out_shape=jax.ShapeDtypeStruct((db,) + out_blk[1:], F32),
        compiler_params=_cparams(("arbitrary", "arbitrary")), name="sample_kmean",
    )(page_table.reshape(-1), *([k_pool_t] * PAGES_PER_STEP))


def _sample_select_kernel(q_ref, km_ref, idx_ref, *, n_full):
    t8 = q_ref.shape[2]
    blk = lax.broadcasted_iota(jnp.int32, (t8, n_full), 1).astype(F32)
    lane = lax.broadcasted_iota(jnp.int32, (t8, LANES), 1)
    for h in range(MOBA_HEADS):
        g = jnp.dot(q_ref[0, h], km_ref[0, h], precision=HIGHEST, preferred_element_type=F32)
        out = jnp.zeros((t8, LANES), jnp.int32)
        for k in range(MOBA_TOPK):
            mx = jnp.max(g, axis=-1, keepdims=True)
            first = jnp.min(jnp.where(g == mx, blk, float(n_full)), axis=-1, keepdims=True)
            out = jnp.where(lane == k, first.astype(jnp.int32), out)
            g = jnp.where(blk == first, -jnp.inf, g)
        idx_ref[0, h] = out


def _sample_select(qh, km):
    db, h, t8, d = qh.shape
    n_full = km.shape[3]
    assert n_full >= MOBA_TOPK
    b4 = lambda b: (b, 0, 0, 0)
    return pl.pallas_call(
        functools.partial(_sample_select_kernel, n_full=n_full), grid=(db,),
        in_specs=[pl.BlockSpec((1, h, t8, d), b4), pl.BlockSpec((1, h, d, n_full), b4)],
        out_specs=pl.BlockSpec((1, h, t8, LANES), b4),
        out_shape=jax.ShapeDtypeStruct((db, h, t8, LANES), jnp.int32),
        compiler_params=_cparams(("arbitrary",)), name="sample_select",
    )(qh, km)


def _sample_attn_kernel(idx_ref, phys_ref, slopes_ref, q_ref, kn_ref, vn_ref, kpool, vpool, o_ref,
                        kbuf, vbuf, sem, *, ds, past):
    ppb = MOBA_BLOCK // PAGE_SIZE
    n_sel = MOBA_TOPK * ppb
    n_pg = ds * n_sel
    b = pl.program_id(0)
    h = pl.program_id(1)
    n_heads = pl.num_programs(1)
    step = b * n_heads + h
    n_steps = pl.num_programs(0) * n_heads

    def page_copies(st, half):
        hh = st % n_heads
        out = []
        for k in range(n_pg):
            page = phys_ref[st * n_pg + k]
            out.append(pltpu.make_async_copy(kpool.at[page, hh], kbuf.at[half, k], sem.at[0, half]))
            out.append(pltpu.make_async_copy(vpool.at[page, hh], vbuf.at[half, k], sem.at[1, half]))
        return out

    @pl.when(step == 0)
    def _():
        for cp in page_copies(step, 0):
            cp.start()

    @pl.when(step + 1 < n_steps)
    def _():
        for cp in page_copies(step + 1, (step + 1) % 2):
            cp.start()

    half = step % 2
    for cp in page_copies(step, half):
        cp.wait()
    k_refs = [kbuf.at[half, k] for k in range(n_pg)]
    v_refs = [vbuf.at[half, k] for k in range(n_pg)]
    slope = slopes_ref[h]
    t8 = q_ref.shape[3]
    c_page = lax.broadcasted_iota(jnp.int32, (1, PAGE_SIZE), 1).astype(F32)
    c_new = lax.broadcasted_iota(jnp.int32, (1, t8), 1)
    qf = q_ref[0, 0] * (MOBA_HEAD_DIM ** -0.5)
    k_new = kn_ref[0, 0]
    v_new = vn_ref[0, 0]
    o_ref[0, 0] = jnp.zeros((MOBA_HEAD_DIM, t8), F32)
    for t in range(ds):
        q_t = qf[:, t:t + 1]
        t_pos = float(past + t)
        scores = []
        for sp in range(n_sel):
            slot, p = divmod(sp, ppb)
            blk_idx = idx_ref[((b * MOBA_HEADS + h) * ds + t) * MOBA_TOPK + slot]
            pos0 = (blk_idx * MOBA_BLOCK + p * PAGE_SIZE).astype(F32)
            s = jnp.sum(k_refs[t * n_sel + sp][...] * q_t, axis=0, keepdims=True)
            scores.append(s - slope * (t_pos - (pos0 + c_page)))
        s_new = jnp.sum(k_new * q_t, axis=0, keepdims=True)
        s_new = s_new - slope * (t_pos - (float(past) + c_new.astype(F32)))
        s_new = jnp.where(c_new <= t, s_new, -jnp.inf)
        mx = jnp.max(s_new, axis=-1, keepdims=True)
        for s in scores:
            mx = jnp.maximum(mx, jnp.max(s, axis=-1, keepdims=True))
        p_new = jnp.exp(s_new - mx)
        l = jnp.sum(p_new, axis=-1, keepdims=True)
        o = jnp.sum(p_new * v_new, axis=-1, keepdims=True)
        acc = jnp.zeros((MOBA_HEAD_DIM, PAGE_SIZE), F32)
        for sp in range(n_sel):
            pr = jnp.exp(scores[sp] - mx)
            l = l + jnp.sum(pr, axis=-1, keepdims=True)
            acc = acc + pr * v_refs[t * n_sel + sp][...]
        o = o + jnp.sum(acc, axis=-1, keepdims=True)
        o_ref[0, 0, :, t:t + 1] = o / l


def _sample_attn(qh, kh, vh, idx, k_pool, v_pool, page_table, ds):
    db, h, d, t8 = qh.shape
    n_pages = page_table.shape[1]
    ppb = MOBA_BLOCK // PAGE_SIZE
    assert n_pages % ppb == 0
    past = n_pages * PAGE_SIZE
    n_sel = MOBA_TOPK * ppb
    logical = idx[..., None] * ppb + jnp.arange(ppb, dtype=jnp.int32)
    phys = page_table[jnp.arange(db)[:, None, None, None], logical.reshape(db, h, ds, n_sel)]
    tok = pl.BlockSpec((1, 1, d, t8), lambda b, hh, *_: (b, hh, 0, 0))
    hbm = pl.BlockSpec(memory_space=pl.ANY)
    page_buf = pltpu.VMEM((2, ds * n_sel, d, PAGE_SIZE), F32)
    grid_spec = pltpu.PrefetchScalarGridSpec(
        num_scalar_prefetch=2, grid=(db, h),
        in_specs=[pl.BlockSpec(memory_space=pltpu.SMEM), tok, tok, tok, hbm, hbm],
        out_specs=tok,
        scratch_shapes=[page_buf, page_buf, pltpu.SemaphoreType.DMA((2, 2))])
    return pl.pallas_call(
        functools.partial(_sample_attn_kernel, ds=ds, past=past), grid_spec=grid_spec,
        out_shape=jax.ShapeDtypeStruct((db, h, d, t8), F32),
        compiler_params=_cparams(("arbitrary", "arbitrary")), name="sample_attn",
    )(idx.reshape(-1), phys.reshape(-1), _alibi_slopes(), qh, kh, vh, k_pool, v_pool)


def _split_w_in_a(w, f32_query):
    wb = w.astype(BF16)
    wdt = jnp.zeros((D_MODEL, LANES), BF16).at[:, :SSD_HEADS].set(wb[:, OFF_DT:OFF_Q])
    wq = w[:, OFF_Q:OFF_K] if f32_query else wb[:, OFF_Q:OFF_K]
    return [wb[:, :OFF_XBC], wb[:, OFF_XBC:OFF_DT], wdt, wq, wb[:, OFF_K:OFF_V], wb[:, OFF_V:]]


def _pad_seq(a, b, l, lp):
    a = a.reshape(b, l, a.shape[-1])
    return a if lp == l else jnp.pad(a, ((0, 0), (0, lp - l), (0, 0)))


def _round_up(n, m):
    return -(-n // m) * m


def _heads(a, b, l):
    return a.reshape(b, l, MOBA_HEADS, MOBA_HEAD_DIM).transpose(0, 2, 1, 3)


def _pad_tokens(a):
    l = a.shape[-2]
    return jnp.pad(a, ((0, 0),) * (a.ndim - 2) + ((0, _round_up(l, SUBLANES) - l), (0, 0)))


def _trunk(x, bsz, length, prompt, caches, p):
    ssm_conv0, ssm0, hgrn0, ffn_conv0 = caches[:4]
    m = bsz * length
    k_rows = v_rows = ssm_c = ssm_s = hgrn_s = None
    ffn_c = []
    for layer in range(DEPTH):
        nw = p['norm_mix'][layer].reshape(1, -1)
        if layer % 2 == 0:
            ia = layer // 2
            assert ia == 0
            ws = _split_w_in_a(p['w_in_a'][ia], f32_query=not prompt)
            outs = _inproj_a(x, nw, ws, prompt)
            z, xbc, dtr, q = outs[:4]
            lp = _round_up(length, SSD_CHUNK)
            y_ssd, hist, s_new = _ssd(
                _pad_seq(z, bsz, length, lp), _pad_seq(xbc, bsz, length, lp), _pad_seq(dtr, bsz, length, lp),
                ssm_conv0[ia], ssm0[ia], p['ssd_conv_w'][ia], p['ssd_conv_b'][ia], p['ssd_dt_bias'][ia],
                p['ssd_a_log'][ia], p['ssd_d'][ia], p['ssd_norm_w'][ia], length)
            y_ssd = y_ssd[:, :length].reshape(m, SSD_INNER)
            if prompt:
                assert bsz == 1
                kr, vr, ka, vt, km, bounds = outs[4:]
                o_att = _moba_prompt(q, ka, vt, km.reshape(-1, MOBA_INNER), bounds)
                k_rows, v_rows = jnp.swapaxes(kr, -1, -2)[None], jnp.swapaxes(vr, -1, -2)[None]
            else:
                k, v = outs[4:]
                k_pool, v_pool, page_table = caches[4:]
                k_pool_t = jnp.swapaxes(k_pool[ia], -1, -2)
                v_pool_t = jnp.swapaxes(v_pool[ia], -1, -2)
                qh, kh, vh = (_pad_tokens(_heads(t, bsz, length)) for t in (q, k, v))
                qt, kt, vt = (jnp.swapaxes(t, -1, -2) for t in (qh, kh, vh))
                km = _sample_kmean(k_pool_t, page_table)
                idx = _sample_select(qh, km)[:, :, :length, :MOBA_TOPK]
                o = _sample_attn(qt, kt, vt, idx, k_pool_t, v_pool_t, page_table, length)
                o_att = o[..., :length].transpose(0, 3, 1, 2).reshape(m, MOBA_INNER).astype(BF16)
                k_rows, v_rows = kh[:, :, :length], vh[:, :, :length]
            wo = p['w_out_a'][ia].astype(BF16)
            mix_acts, mix_ws = [y_ssd, o_att], [wo[:SSD_INNER], wo[SSD_INNER:]]
            ssm_c, ssm_s = hist, s_new
        else:
            ic = layer // 2
            assert ic == 0
            wc = p['w_in_c'][ic].astype(BF16)
            ws = [wc[:, :HGRN_K], wc[:, HGRN_K:2 * HGRN_K], wc[:, 2 * HGRN_K:2 * HGRN_K + HGRN_V],
                  wc[:, 2 * HGRN_K + HGRN_V:]]
            qa, fx, iv, g = _norm_matmul(x, nw, ws)
            lp = _round_up(length, HGRN_CHUNK)
            o, s_new = _gla(*(_pad_seq(t, bsz, length, lp) for t in (qa, fx, iv, g)),
                            p['hgrn_lb_gamma'], p['hgrn_norm_w'][ic], hgrn0[ic], length, layer)
            mix_acts, mix_ws = [o[:, :length].reshape(m, HGRN_V)], [p['w_out_c'][ic].astype(BF16)]
            hgrn_s = s_new
        nf = p['norm_final'].reshape(1, -1) if layer == DEPTH - 1 else None
        hist = ffn_conv0[layer, 0] if prompt else ffn_conv0[layer]
        x, fh = _ffn(x, mix_acts, mix_ws, p['norm_ffn'][layer].reshape(1, -1),
                     p['ffn_w_up'][layer].astype(BF16), p['ffn_conv_w'][layer],
                     p['ffn_conv_b'][layer].reshape(1, -1), p['ffn_w_down'][layer].astype(BF16), hist,
                     None if prompt else length, nf)
        ffn_c.append(fh[None] if prompt else fh)
    return (x.reshape(bsz, length, D_MODEL), k_rows[None], v_rows[None], ssm_s[None], ssm_c[None],
            hgrn_s[None], jnp.stack(ffn_c))


def kernel(x_prompt, x_sample, cache_k_pool, cache_v_pool, page_table, state_ssm, state_ssm_conv, state_hgrn,
           state_ffn_conv, norm_mix, norm_ffn, norm_final, w_in_a, w_out_a, ssd_conv_w, ssd_conv_b,
           ssd_dt_bias, ssd_a_log, ssd_d, ssd_norm_w, w_in_c, w_out_c, hgrn_lb_gamma, hgrn_norm_w, ffn_w_up,
           ffn_conv_w, ffn_conv_b, ffn_w_down):
    p = dict(norm_mix=norm_mix, norm_ffn=norm_ffn, norm_final=norm_final, w_in_a=w_in_a, w_out_a=w_out_a,
             ssd_conv_w=ssd_conv_w, ssd_conv_b=ssd_conv_b, ssd_dt_bias=ssd_dt_bias, ssd_a_log=ssd_a_log,
             ssd_d=ssd_d, ssd_norm_w=ssd_norm_w, w_in_c=w_in_c, w_out_c=w_out_c, hgrn_lb_gamma=hgrn_lb_gamma,
             hgrn_norm_w=hgrn_norm_w, ffn_w_up=ffn_w_up, ffn_conv_w=ffn_conv_w, ffn_conv_b=ffn_conv_b,
             ffn_w_down=ffn_w_down)
    bp, sp, _ = x_prompt.shape
    db, ds, _ = x_sample.shape
    na, nc = (DEPTH + 1) // 2, DEPTH // 2
    zeros = lambda *s: jnp.zeros(s, F32)
    prompt_caches = (zeros(na, bp, SSD_CONV - 1, SSD_CONV_DIM), zeros(na, bp, SSD_HEADS, SSD_HEAD_DIM, SSD_STATE),
                     zeros(nc, bp, HGRN_HEADS, HGRN_KEY_DIM, HGRN_VAL_DIM), zeros(DEPTH, bp, FFN_CONV - 1, FFN_DIM))
    outs_p = _trunk(x_prompt.reshape(bp * sp, D_MODEL), bp, sp, True, prompt_caches, p)
    sample_caches = (state_ssm_conv, state_ssm, state_hgrn, state_ffn_conv, cache_k_pool, cache_v_pool, page_table)
    outs_s = _trunk(x_sample.reshape(db * ds, D_MODEL), db, ds, False, sample_caches, p)
    yp, krp, vrp, ssp, scp, hgp, fcp = outs_p
    ys, krs, vrs, sss, scs, hgs, fcs = outs_s
    return (yp, ys, krp, vrp, ssp, scp, hgp, fcp, krs, vrs, sss, scs, hgs, fcs)
```

```python
import functools
import math

import numpy as np
import jax
import jax.numpy as jnp
from jax import lax
from jax.experimental import pallas as pl
from jax.experimental.pallas import tpu as pltpu

F32 = jnp.float32
BF16 = jnp.bfloat16
HIGHEST = lax.Precision.HIGHEST

D_MODEL = 1024
DEPTH = 2
PAGE_SIZE = 128
SSD_HEADS = 8
SSD_HEAD_DIM = 64
SSD_INNER = SSD_HEADS * SSD_HEAD_DIM
SSD_STATE = 64
SSD_GROUPS = 2
SSD_CONV = 4
SSD_CHUNK = 128
SSD_GN = SSD_GROUPS * SSD_STATE
SSD_CONV_DIM = SSD_INNER + 2 * SSD_GN
MOBA_HEADS = 8
MOBA_HEAD_DIM = 64
MOBA_INNER = MOBA_HEADS * MOBA_HEAD_DIM
MOBA_BLOCK = 256
MOBA_TOPK = 3
OFF_XBC = SSD_INNER
OFF_DT = OFF_XBC + SSD_CONV_DIM
OFF_Q = OFF_DT + SSD_HEADS
OFF_K = OFF_Q + MOBA_INNER
OFF_V = OFF_K + MOBA_INNER
HGRN_HEADS = 8
HGRN_KEY_DIM = 128
HGRN_VAL_DIM = D_MODEL // HGRN_HEADS
HGRN_K = HGRN_HEADS * HGRN_KEY_DIM
HGRN_V = HGRN_HEADS * HGRN_VAL_DIM
HGRN_CHUNK = 64
FFN_DIM = 2816
FFN_CONV = 3
RMS_EPS = 1e-6

LANES = 128
SUBLANES = 8
VMEM_LIMIT_BYTES = 56 * 1024 * 1024

ROW_TILE = 256
FFN_COL_CHUNK = 256
NEG_BIG = -1e30
LOG2E = math.log2(math.e)
ALIBI_PARTS = 3
MOBA_VT_ROWS = MOBA_HEAD_DIM + 16
BOUND_SLACK = 0.02
UNDERFLOW_LOG2 = 160.0
MOBA_AHEAD = 2
MOBA_SLOTS = MOBA_AHEAD + 1
MOBA_UNROLL = 4 * MOBA_SLOTS


def _cparams(sem, flags=None):
    return pltpu.CompilerParams(dimension_semantics=sem, vmem_limit_bytes=VMEM_LIMIT_BYTES, flags=flags)


def _const_spec(shape):
    nd = len(shape)
    return pl.BlockSpec(shape, lambda *_: (0,) * nd)


def _rms(x, w):
    y = x * lax.rsqrt(jnp.mean(x * x, axis=-1, keepdims=True) + RMS_EPS)
    return y * w


def _rms_bf16(x, w):
    return _rms(x, w).astype(BF16)


def _dot(a, b):
    return jnp.dot(a, b, preferred_element_type=F32)


def _dot_split(a, b):
    a_hi = a.astype(BF16)
    a_lo = (a - a_hi.astype(F32)).astype(BF16)
    b_hi = b.astype(BF16)
    b_lo = (b - b_hi.astype(F32)).astype(BF16)
    return _dot(jnp.concatenate([a_hi, a_lo, a_hi], axis=1), jnp.concatenate([b_hi, b_hi, b_lo], axis=0))


def _dot_nt(a, b):
    return lax.dot_general(a, b, (((1,), (1,)), ((), ())), preferred_element_type=F32)


def _dot_tn(a, b):
    return lax.dot_general(a, b, (((0,), (0,)), ((), ())), preferred_element_type=F32)


def _tril_f32(n):
    r = lax.broadcasted_iota(jnp.int32, (n, n), 0)
    c = lax.broadcasted_iota(jnp.int32, (n, n), 1)
    return r >= c


def _cumsum_rows(x, q, split):
    if not split:
        return jnp.dot(_tril_f32(q).astype(F32), x, precision=HIGHEST, preferred_element_type=F32)
    tri = _tril_f32(q).astype(BF16)
    acc = None
    rest = x
    for _ in range(3):
        term = rest.astype(BF16)
        part = _dot(tri, term)
        acc = part if acc is None else acc + part
        rest = rest - term.astype(F32)
    return acc


def _silu(x):
    return x * jax.nn.sigmoid(x)


def _inproj_a_kernel(x_ref, nw_ref, wz, wxbc, wdt, wq, wk, wv, *rest, prompt):
    outs = rest[1:] if prompt else rest
    h = _rms(x_ref[...], nw_ref[...])
    hb = h.astype(BF16)
    z_o, xbc_o, dt_o, q_o = outs[:4]
    z_o[...] = _dot(hb, wz[...])
    xbc_o[...] = _dot(hb, wxbc[...])
    dt_o[...] = _dot(hb, wdt[...])
    if wq.dtype == F32:
        q = jnp.dot(h, wq[...], precision=HIGHEST, preferred_element_type=F32)
    else:
        q = _dot(hb, wq[...])
    q_o[...] = q
    k = _dot(hb, wk[...])
    v = _dot(hb, wv[...])
    if not prompt:
        k_o, v_o = outs[4:]
        k_o[...] = k
        v_o[...] = v
        return
    kr_o, vr_o, ka_o, vt_o, km_o, bnd_o = outs[4:]
    tm = k.shape[0]
    head_sel = rest[0][...]
    k_norm = jnp.sqrt(jnp.max(_dot((k * k).astype(BF16), head_sel), axis=0, keepdims=True))
    q_norm = jnp.sqrt(jnp.max(_dot((q * q).astype(BF16), head_sel), axis=0, keepdims=True))
    qk_min = jnp.min(_dot((q * k).astype(BF16), head_sel), axis=0, keepdims=True)
    row8 = lax.broadcasted_iota(jnp.int32, (SUBLANES, LANES), 0)
    bnd_o[0] = jnp.where(row8 == 0, k_norm, jnp.where(row8 == 1, q_norm, jnp.where(row8 == 2, qk_min, 0.0)))
    lane = lax.broadcasted_iota(jnp.int32, (tm, LANES), 1)
    key_aug = _key_aug_columns(tm)
    for hp in range(MOBA_HEADS // 2):
        pair = k[:, hp * LANES:(hp + 1) * LANES]
        for j, kh in enumerate((pair, pltpu.roll(pair, MOBA_HEAD_DIM, 1))):
            ka_o[2 * hp + j, 0] = jnp.where(lane < MOBA_HEAD_DIM, kh, key_aug).astype(BF16)
    k_t = k.T
    v_t = v.T
    for p in range(tm // PAGE_SIZE):
        toks = slice(p * PAGE_SIZE, (p + 1) * PAGE_SIZE)
        for h in range(MOBA_HEADS):
            dims = slice(h * MOBA_HEAD_DIM, (h + 1) * MOBA_HEAD_DIM)
            kr_o[p, h] = k_t[dims, toks]
            vr_o[p, h] = v_t[dims, toks]
    ones_rows = (lax.broadcasted_iota(jnp.int32, (MOBA_VT_ROWS - MOBA_HEAD_DIM, tm), 0) == 0).astype(F32)
    pieces = []
    for h in range(MOBA_HEADS):
        pieces += [v_t[h * MOBA_HEAD_DIM:(h + 1) * MOBA_HEAD_DIM, :], ones_rows]
    vt_o[0] = jnp.concatenate(pieces, axis=0).astype(BF16)
    km_o[0] = jnp.mean(k, axis=0, keepdims=True)


def _inproj_a(x, nw, ws, prompt):
    m = x.shape[0]
    tm = ROW_TILE if prompt else m
    assert m % tm == 0
    if prompt:
        assert tm == MOBA_BLOCK
    nt = m // tm
    widths = [w.shape[1] for w in ws]
    row = lambda i: (i, 0)
    in_specs = [pl.BlockSpec((tm, D_MODEL), row), _const_spec((1, D_MODEL))]
    in_specs += [_const_spec(w.shape) for w in ws]
    out_shape = [jax.ShapeDtypeStruct((m, n), F32) for n in widths[:4]]
    out_specs = [pl.BlockSpec((tm, n), row) for n in widths[:4]]
    if prompt:
        npg = m // PAGE_SIZE
        ppt = tm // PAGE_SIZE
        rows_shape = (npg, MOBA_HEADS, MOBA_HEAD_DIM, PAGE_SIZE)
        rows_spec = pl.BlockSpec((ppt, MOBA_HEADS, MOBA_HEAD_DIM, PAGE_SIZE), lambda i: (i, 0, 0, 0))
        out_shape += [jax.ShapeDtypeStruct(rows_shape, F32)] * 2
        out_specs += [rows_spec, rows_spec]
        out_shape += [jax.ShapeDtypeStruct((MOBA_HEADS, nt, tm, LANES), BF16),
                      jax.ShapeDtypeStruct((nt, MOBA_HEADS * MOBA_VT_ROWS, tm), BF16),
                      jax.ShapeDtypeStruct((nt, 1, MOBA_INNER), F32),
                      jax.ShapeDtypeStruct((nt, SUBLANES, LANES), F32)]
        out_specs += [pl.BlockSpec((MOBA_HEADS, 1, tm, LANES), lambda i: (0, i, 0, 0)),
                      pl.BlockSpec((1, MOBA_HEADS * MOBA_VT_ROWS, tm), lambda i: (i, 0, 0)),
                      pl.BlockSpec((1, 1, MOBA_INNER), lambda i: (i, 0, 0)),
                      pl.BlockSpec((1, SUBLANES, LANES), lambda i: (i, 0, 0))]
        head_sel = (jnp.arange(MOBA_INNER)[:, None] // MOBA_HEAD_DIM == jnp.arange(LANES)[None, :]).astype(BF16)
        extra, extra_specs = [head_sel], [_const_spec(head_sel.shape)]
    else:
        out_shape += [jax.ShapeDtypeStruct((m, MOBA_INNER), F32)] * 2
        out_specs += [pl.BlockSpec((tm, MOBA_INNER), row)] * 2
        extra, extra_specs = [], []
    return pl.pallas_call(
        functools.partial(_inproj_a_kernel, prompt=prompt),
        grid=(nt,), in_specs=in_specs + extra_specs, out_specs=out_specs, out_shape=out_shape,
        compiler_params=_cparams(("arbitrary",)), name="inproj_a",
    )(x, nw, *ws, *extra)


def _norm_matmul_kernel(x_ref, nw_ref, *refs):
    n = len(refs) // 2
    hb = _rms_bf16(x_ref[...], nw_ref[...])
    for w, o in zip(refs[:n], refs[n:]):
        o[...] = _dot(hb, w[...])


def _norm_matmul(x, nw, ws):
    m = x.shape[0]
    tm = min(ROW_TILE, m)
    assert m % tm == 0
    row = lambda i: (i, 0)
    in_specs = [pl.BlockSpec((tm, D_MODEL), row), _const_spec((1, D_MODEL))]
    in_specs += [_const_spec(w.shape) for w in ws]
    return pl.pallas_call(
        _norm_matmul_kernel, grid=(m // tm,), in_specs=in_specs,
        out_specs=[pl.BlockSpec((tm, w.shape[1]), row) for w in ws],
        out_shape=[jax.ShapeDtypeStruct((m, w.shape[1]), F32) for w in ws],
        compiler_params=_cparams(("arbitrary",)), name="norm_matmul",
    )(x, nw, *ws)


def _ffn_kernel(*refs, tm, seq_len, final_norm, n_mix):
    it = iter(refs)
    x_ref = next(it)
    mix_refs = [next(it) for _ in range(2 * n_mix)]
    nw_ref, wup_ref, cw_ref, cb_ref, wdn_ref = (next(it) for _ in range(5))
    if seq_len is None:
        hist_ref = next(it)
    else:
        h1_ref, h2_ref = next(it), next(it)
    nf_ref = next(it) if final_norm else None
    o_ref = next(it)
    tail_ref = next(it)
    up_s, act_s = next(it), next(it)

    i = pl.program_id(0)
    f = FFN_DIM
    if seq_len is None:
        @pl.when(i == 0)
        def _():
            up_s[0:SUBLANES, 0:f] = hist_ref[...]
    else:
        up_s[0:SUBLANES, 0:f] = jnp.zeros((SUBLANES, f), F32)

    x = x_ref[...]
    for a_ref, w_ref in zip(mix_refs[:n_mix], mix_refs[n_mix:]):
        x = x + _dot(a_ref[...], w_ref[...])
    hb = _rms_bf16(x, nw_ref[...])
    up_s[SUBLANES:SUBLANES + tm, :] = _dot(hb, wup_ref[...])

    if seq_len is not None:
        t = lax.broadcasted_iota(jnp.int32, (tm, 1), 0) % seq_len
    for c in range(0, f, FFN_COL_CHUNK):
        cols = slice(c, c + FFN_COL_CHUNK)
        g0 = up_s[SUBLANES:SUBLANES + tm, cols]
        g1 = up_s[SUBLANES - 1:SUBLANES - 1 + tm, cols]
        g2 = up_s[SUBLANES - 2:SUBLANES - 2 + tm, cols]
        if seq_len is not None:
            g1 = jnp.where(t >= 1, g1, h1_ref[:, cols])
            g2 = jnp.where(t >= 2, g2, h2_ref[:, cols])
        val = up_s[SUBLANES:SUBLANES + tm, f + c:f + c + FFN_COL_CHUNK]
        conv = g2 * cw_ref[0:1, cols]
        conv = conv + g1 * cw_ref[1:2, cols]
        conv = conv + g0 * cw_ref[2:3, cols]
        conv = conv + cb_ref[:, cols]
        act_s[:, cols] = (_silu(conv) * val).astype(BF16)

    out = x + _dot(act_s[...], wdn_ref[...])
    o_ref[...] = _rms(out, nf_ref[...]) if final_norm else out

    if seq_len is None:
        last = up_s[tm:tm + SUBLANES, 0:f]
        up_s[0:SUBLANES, 0:f] = last

        @pl.when(i == pl.num_programs(0) - 1)
        def _():
            tail_ref[...] = last
    else:
        tail_ref[...] = up_s[SUBLANES:SUBLANES + tm, 0:f]


def _ffn(x, mix_acts, mix_ws, nw, wup, cw, cb, wdn, hist, seq_len, nf):
    assert FFN_CONV == 3
    m = x.shape[0]
    f = FFN_DIM
    final_norm = nf is not None
    row = lambda i: (i, 0)
    if seq_len is None:
        tm = ROW_TILE
        hist8 = jnp.zeros((SUBLANES, f), F32).at[SUBLANES - 2:].set(hist)
        extra = [hist8]
        extra_specs = [_const_spec((SUBLANES, f))]
        tail_shape = (SUBLANES, f)
    else:
        tm = m
        nb = m // seq_len
        assert seq_len >= 2 and nb * seq_len == m
        h1 = jnp.zeros((nb, seq_len, f), F32).at[:, 0].set(hist[:, 1]).reshape(m, f)
        h2 = jnp.zeros((nb, seq_len, f), F32).at[:, 0:2].set(hist).reshape(m, f)
        extra = [h1, h2]
        extra_specs = [_const_spec((m, f))] * 2
        tail_shape = (m, f)
    assert m % tm == 0
    ins = [x] + list(mix_acts) + list(mix_ws) + [nw, wup, cw, cb, wdn] + extra
    in_specs = [pl.BlockSpec((tm, D_MODEL), row)]
    in_specs += [pl.BlockSpec((tm, a.shape[1]), row) for a in mix_acts]
    in_specs += [_const_spec(w.shape) for w in mix_ws]
    in_specs += [_const_spec((1, D_MODEL)), _const_spec(wup.shape), _const_spec(cw.shape),
                 _const_spec(cb.shape), _const_spec(wdn.shape)] + extra_specs
    if final_norm:
        ins.append(nf)
        in_specs.append(_const_spec((1, D_MODEL)))
    xo, tail = pl.pallas_call(
        functools.partial(_ffn_kernel, tm=tm, seq_len=seq_len, final_norm=final_norm, n_mix=len(mix_acts)),
        grid=(m // tm,), in_specs=in_specs,
        out_specs=[pl.BlockSpec((tm, D_MODEL), row), _const_spec(tail_shape)],
        out_shape=[jax.ShapeDtypeStruct((m, D_MODEL), F32), jax.ShapeDtypeStruct(tail_shape, F32)],
        scratch_shapes=[pltpu.VMEM((SUBLANES + tm, 2 * f), F32), pltpu.VMEM((tm, f), BF16)],
        compiler_params=_cparams(("arbitrary",)), name="conv_ffn",
    )(*ins)
    if seq_len is None:
        new_hist = tail[SUBLANES - 2:]
    else:
        new_hist = tail.reshape(m // seq_len, seq_len, f)[:, seq_len - 2:]
    return xo, new_hist


def _ssd_kernel(z_ref, xbc_ref, dt_ref, hist_ref, s0_ref, cw_ref, cb_ref, dtb_ref, alog_ref, dsk_ref,
                nw_ref, y_ref, hist_o, s_o, cbuf, st, ybuf, m1_s, xdt_s, ce_s, xw_s, bm_s, xsd_s, zs_s, el_s,
                *, q, seq_len, n_chunks):
    c = pl.program_id(1)
    pipelined = n_chunks > 1
    stage = (m1_s, xdt_s, ce_s, xw_s, bm_s, xsd_s, zs_s, el_s)

    def state_stage(rd):
        _ssd_state_stage(rd, nw_ref, y_ref, st, ybuf, *stage)

    def free_stage(wr):
        _ssd_free_stage(c, wr, z_ref, xbc_ref, dt_ref, cw_ref, cb_ref, dtb_ref, alog_ref, dsk_ref, hist_o, cbuf,
                        *stage, q=q, seq_len=seq_len, n_chunks=n_chunks)

    @pl.when(c == 0)
    def _():
        cbuf[0:SUBLANES, :] = hist_ref[0]
        st[...] = s0_ref[0]
        if pipelined:
            for ref in stage[:-1]:
                ref[1] = jnp.zeros(ref.shape[1:], ref.dtype)
            el_s[1] = jnp.ones(el_s.shape[1:], F32)

    if pipelined:
        @pl.when(c % 2 == 0)
        def _():
            state_stage(1)
            free_stage(0)

        @pl.when(c % 2 == 1)
        def _():
            state_stage(0)
            free_stage(1)
    else:
        free_stage(0)
        state_stage(0)

    @pl.when(c == (n_chunks if pipelined else n_chunks - 1))
    def _():
        s_o[0] = st[...]


def _ssd_state_stage(rd, nw_ref, y_ref, st, ybuf, m1_s, xdt_s, ce_s, xw_s, bm_s, xsd_s, zs_s, el_s):
    rep = SSD_HEADS // SSD_GROUPS
    hd = SSD_HEAD_DIM
    e_last = el_s[rd, 0:1, :]
    y_state = [_dot_nt(ce_s[rd, h], st[h].astype(BF16)) for h in range(SSD_HEADS)]
    s_inc = [_dot_tn(xw_s[rd, h], bm_s[rd, h // rep]) for h in range(SSD_HEADS)]
    y_intra = [_dot(m1_s[rd, h], xdt_s[rd, h]) for h in range(SSD_HEADS)]
    for h in range(SSD_HEADS):
        st[h] = st[h] * e_last[:, h:h + 1] + s_inc[h]
        ybuf[:, h * hd:(h + 1) * hd] = y_intra[h] + y_state[h]
    yz = (ybuf[...] + xsd_s[rd]) * zs_s[rd]
    gw = SSD_INNER // SSD_GROUPS
    for g in range(SSD_GROUPS):
        yg = yz[:, g * gw:(g + 1) * gw]
        yn = yg * lax.rsqrt(jnp.mean(yg * yg, axis=-1, keepdims=True) + RMS_EPS)
        y_ref[0, :, g * gw:(g + 1) * gw] = (yn * nw_ref[:, g * gw:(g + 1) * gw]).astype(BF16)


def _ssd_free_stage(c, wr, z_ref, xbc_ref, dt_ref, cw_ref, cb_ref, dtb_ref, alog_ref, dsk_ref, hist_o, cbuf,
                    m1_s, xdt_s, ce_s, xw_s, bm_s, xsd_s, zs_s, el_s, *, q, seq_len, n_chunks):
    rep = SSD_HEADS // SSD_GROUPS
    hd = SSD_HEAD_DIM
    hrows = SSD_CONV - 1
    h0 = SUBLANES - hrows
    cbuf[SUBLANES:SUBLANES + q, :] = xbc_ref[0]
    conv = cbuf[h0:h0 + q, :] * cw_ref[0:1, :]
    for j in range(1, SSD_CONV):
        conv = conv + cbuf[h0 + j:h0 + j + q, :] * cw_ref[j:j + 1, :]
    xc = _silu(conv + cb_ref[...])

    xdt_raw = dt_ref[0] + dtb_ref[...]
    dt = jnp.maximum(xdt_raw, 0.0) + jnp.log1p(jnp.exp(-jnp.abs(xdt_raw)))
    if seq_len % q != 0:
        row = jnp.minimum(c, n_chunks - 1) * q + lax.broadcasted_iota(jnp.int32, (q, 1), 0)
        dt = jnp.where(row < seq_len, dt, 0.0)
    a = -jnp.exp(alog_ref[...])
    acum = _cumsum_rows(dt * a, q, split=False)
    acum_t = acum.T
    a_last = acum[q - 1:q, :]
    e_acum = jnp.exp(acum)
    w_in = jnp.exp(a_last - acum) * dt
    el_s[wr] = jnp.broadcast_to(jnp.exp(a_last), (SUBLANES, LANES))
    tril = _tril_f32(q)

    cbs, cms = [], []
    for g in range(SSD_GROUPS):
        bm = xc[:, SSD_INNER + g * SSD_STATE:SSD_INNER + (g + 1) * SSD_STATE].astype(BF16)
        cm = xc[:, SSD_INNER + SSD_GN + g * SSD_STATE:SSD_INNER + SSD_GN + (g + 1) * SSD_STATE]
        bm_s[wr, g] = bm
        cms.append(cm)
        cbs.append(_dot_nt(cm.astype(BF16), bm))

    for h in range(SSD_HEADS):
        g = h // rep
        xs = xc[:, h * hd:(h + 1) * hd]
        seg = acum[:, h:h + 1] - acum_t[h:h + 1, :]
        decay = jnp.exp(jnp.where(tril, seg, -jnp.inf))
        m1_s[wr, h] = (cbs[g] * decay).astype(BF16)
        xdt_s[wr, h] = (xs * dt[:, h:h + 1]).astype(BF16)
        ce_s[wr, h] = (cms[g] * e_acum[:, h:h + 1]).astype(BF16)
        xw_s[wr, h] = (xs * w_in[:, h:h + 1]).astype(BF16)
        xsd_s[wr, :, h * hd:(h + 1) * hd] = xs * dsk_ref[:, h:h + 1]
    zs_s[wr] = _silu(z_ref[0])

    @pl.when(c == n_chunks - 1)
    def _():
        l_last = seq_len - (n_chunks - 1) * q
        hist_o[0] = cbuf[l_last:l_last + SUBLANES, :]

    cbuf[0:SUBLANES, :] = cbuf[q:q + SUBLANES, :]


def _stage_grid(n_chunks):
    if n_chunks == 1:
        same = lambda i, c: (i, c, 0)
        return 1, same, same
    return (n_chunks + 1, lambda i, c: (i, jnp.minimum(c, n_chunks - 1), 0),
            lambda i, c: (i, jnp.maximum(c - 1, 0), 0))


def _pad_lanes(v, fill=0.0):
    return jnp.full((1, LANES), fill, F32).at[0, :v.shape[0]].set(v.astype(F32))


def _ssd(z, xbc, dtr, hist, s0, conv_w, conv_b, dt_bias, a_log, d_skip, norm_w, seq_len):
    b, lp, _ = z.shape
    q = SSD_CHUNK
    assert lp % q == 0 and lp - seq_len < q
    hrows = SSD_CONV - 1
    hist8 = jnp.zeros((b, SUBLANES, SSD_CONV_DIM), F32).at[:, SUBLANES - hrows:].set(hist)
    cw8 = jnp.zeros((SUBLANES, SSD_CONV_DIM), F32).at[:SSD_CONV].set(conv_w)
    n_chunks = lp // q
    n_steps, seq, seq_out = _stage_grid(n_chunks)
    per_b3 = lambda i, c: (i, 0, 0)
    per_b4 = lambda i, c: (i, 0, 0, 0)
    in_specs = [pl.BlockSpec((1, q, SSD_INNER), seq), pl.BlockSpec((1, q, SSD_CONV_DIM), seq),
                pl.BlockSpec((1, q, LANES), seq), pl.BlockSpec((1, SUBLANES, SSD_CONV_DIM), per_b3),
                pl.BlockSpec((1, SSD_HEADS, SSD_HEAD_DIM, SSD_STATE), per_b4),
                _const_spec((SUBLANES, SSD_CONV_DIM)), _const_spec((1, SSD_CONV_DIM)),
                _const_spec((1, LANES)), _const_spec((1, LANES)), _const_spec((1, LANES)),
                _const_spec((1, SSD_INNER))]
    out_shape = [jax.ShapeDtypeStruct((b, lp, SSD_INNER), BF16),
                 jax.ShapeDtypeStruct((b, SUBLANES, SSD_CONV_DIM), F32),
                 jax.ShapeDtypeStruct((b, SSD_HEADS, SSD_HEAD_DIM, SSD_STATE), F32)]
    out_specs = [pl.BlockSpec((1, q, SSD_INNER), seq_out), pl.BlockSpec((1, SUBLANES, SSD_CONV_DIM), per_b3),
                 pl.BlockSpec((1, SSD_HEADS, SSD_HEAD_DIM, SSD_STATE), per_b4)]
    per_head = pltpu.VMEM((2, SSD_HEADS, q, SSD_HEAD_DIM), BF16)
    y, hist_o, s_o = pl.pallas_call(
        functools.partial(_ssd_kernel, q=q, seq_len=seq_len, n_chunks=n_chunks),
        grid=(b, n_steps), in_specs=in_specs, out_specs=out_specs, out_shape=out_shape,
        scratch_shapes=[pltpu.VMEM((SUBLANES + q, SSD_CONV_DIM), F32),
                        pltpu.VMEM((SSD_HEADS, SSD_HEAD_DIM, SSD_STATE), F32),
                        pltpu.VMEM((q, SSD_INNER), F32),
                        pltpu.VMEM((2, SSD_HEADS, q, q), BF16), per_head, per_head, per_head,
                        pltpu.VMEM((2, SSD_GROUPS, q, SSD_STATE), BF16),
                        pltpu.VMEM((2, q, SSD_INNER), F32), pltpu.VMEM((2, q, SSD_INNER), F32),
                        pltpu.VMEM((2, SUBLANES, LANES), F32)],
        compiler_params=_cparams(("arbitrary", "arbitrary")), name="ssd",
    )(z, xbc, dtr, hist8, s0, cw8, conv_b.reshape(1, -1), _pad_lanes(dt_bias), _pad_lanes(a_log),
      _pad_lanes(d_skip), norm_w.reshape(1, -1))
    return y, hist_o[:, SUBLANES - hrows:], s_o


def _gla_kernel(q_ref, fx_ref, iv_ref, g_ref, gam_ref, nw_ref, s0_ref, o_ref, s_o, st_t, qd_s, kd_s, kdec_s,
                v_s, gs_s, el_s, *, q, seq_len, layer, n_chunks):
    c = pl.program_id(1)
    pipelined = n_chunks > 1
    stage = (qd_s, kd_s, kdec_s, v_s, gs_s, el_s)

    def state_stage(rd):
        _gla_state_stage(rd, nw_ref, o_ref, st_t, *stage, q=q)

    def free_stage(wr):
        _gla_free_stage(c, wr, q_ref, fx_ref, iv_ref, g_ref, gam_ref, *stage, q=q, seq_len=seq_len, layer=layer,
                        n_chunks=n_chunks)

    @pl.when(c == 0)
    def _():
        for h in range(HGRN_HEADS):
            st_t[h] = s0_ref[0, h].T
        if pipelined:
            for ref in stage[:-1]:
                ref[1] = jnp.zeros(ref.shape[1:], ref.dtype)
            el_s[1] = jnp.ones(el_s.shape[1:], F32)

    if pipelined:
        @pl.when(c % 2 == 0)
        def _():
            state_stage(1)
            free_stage(0)

        @pl.when(c % 2 == 1)
        def _():
            state_stage(0)
            free_stage(1)
    else:
        free_stage(0)
        state_stage(0)

    @pl.when(c == (n_chunks if pipelined else n_chunks - 1))
    def _():
        for h in range(HGRN_HEADS):
            s_o[0, h] = st_t[h].T


def _gla_state_stage(rd, nw_ref, o_ref, st_t, qd_s, kd_s, kdec_s, v_s, gs_s, el_s, *, q):
    tril = _tril_f32(q)
    ks = [slice(h * HGRN_KEY_DIM, (h + 1) * HGRN_KEY_DIM) for h in range(HGRN_HEADS)]
    vs = [slice(h * HGRN_VAL_DIM, (h + 1) * HGRN_VAL_DIM) for h in range(HGRN_HEADS)]
    qd_b = [qd_s[rd, :, s] for s in ks]
    v_b = [v_s[rd, :, s] for s in vs]
    e_last = el_s[rd, 0:1, :]
    att = [_dot_nt(qd_b[h], kd_s[rd, :, ks[h]]) for h in range(HGRN_HEADS)]
    o_state = [_dot_nt(qd_b[h], st_t[h].astype(BF16)) for h in range(HGRN_HEADS)]
    kv = [_dot_tn(v_b[h], kdec_s[rd, :, ks[h]]) for h in range(HGRN_HEADS)]
    for h in range(HGRN_HEADS):
        o = _dot(jnp.where(tril, att[h], 0.0).astype(BF16), v_b[h]) + o_state[h]
        st_t[h] = st_t[h] * e_last[:, ks[h]] + kv[h]
        on = o * lax.rsqrt(jnp.mean(o * o, axis=-1, keepdims=True) + RMS_EPS) * nw_ref[...]
        o_ref[0, :, vs[h]] = (on * gs_s[rd, :, vs[h]]).astype(BF16)


def _gla_free_stage(c, wr, q_ref, fx_ref, iv_ref, g_ref, gam_ref, qd_s, kd_s, kdec_s, v_s, gs_s, el_s, *, q,
                    seq_len, layer, n_chunks):
    rows = [gam_ref[l:l + 1, :] for l in range(DEPTH)]
    mx = functools.reduce(jnp.maximum, rows)
    es = [jnp.exp(r - mx) for r in rows]
    lb = sum(es[1:layer + 1]) / sum(es) if layer >= 1 else jnp.zeros_like(mx)

    f = lb + (1.0 - lb) * jax.nn.sigmoid(fx_ref[0])
    if seq_len % q != 0:
        row = jnp.minimum(c, n_chunks - 1) * q + lax.broadcasted_iota(jnp.int32, (q, 1), 0)
        f = jnp.where(row < seq_len, f, 1.0)
    kk = 1.0 - f
    b = _cumsum_rows(jnp.log(f), q, split=True)
    b_last = b[q - 1:q, :]
    qd_s[wr] = (q_ref[0] * jnp.exp(b)).astype(BF16)
    kd_s[wr] = (kk * jnp.exp(-b)).astype(BF16)
    kdec_s[wr] = (kk * jnp.exp(b_last - b)).astype(BF16)
    v_s[wr] = iv_ref[0].astype(BF16)
    gs_s[wr] = _silu(g_ref[0])
    el_s[wr] = jnp.broadcast_to(jnp.exp(b_last), (SUBLANES, HGRN_K))


def _gla(qa, fx, iv, g, gamma, norm_w, s0, seq_len, layer):
    b, lp, _ = qa.shape
    q = HGRN_CHUNK
    assert lp % q == 0 and lp - seq_len < q
    gam8 = jnp.zeros((SUBLANES, HGRN_K), F32).at[:DEPTH].set(gamma)
    n_chunks = lp // q
    n_steps, seq, seq_out = _stage_grid(n_chunks)
    per_b4 = lambda i, c: (i, 0, 0, 0)
    st_shape = (HGRN_HEADS, HGRN_KEY_DIM, HGRN_VAL_DIM)
    in_specs = [pl.BlockSpec((1, q, HGRN_K), seq), pl.BlockSpec((1, q, HGRN_K), seq),
                pl.BlockSpec((1, q, HGRN_V), seq), pl.BlockSpec((1, q, HGRN_V), seq),
                _const_spec((SUBLANES, HGRN_K)), _const_spec((1, HGRN_VAL_DIM)),
                pl.BlockSpec((1,) + st_shape, per_b4)]
    stage_k = pltpu.VMEM((2, q, HGRN_K), BF16)
    o, s_o = pl.pallas_call(
        functools.partial(_gla_kernel, q=q, seq_len=seq_len, layer=layer, n_chunks=n_chunks),
        grid=(b, n_steps), in_specs=in_specs,
        out_specs=[pl.BlockSpec((1, q, HGRN_V), seq_out), pl.BlockSpec((1,) + st_shape, per_b4)],
        out_shape=[jax.ShapeDtypeStruct((b, lp, HGRN_V), BF16), jax.ShapeDtypeStruct((b,) + st_shape, F32)],
        scratch_shapes=[pltpu.VMEM((HGRN_HEADS, HGRN_VAL_DIM, HGRN_KEY_DIM), F32),
                        stage_k, stage_k, stage_k, pltpu.VMEM((2, q, HGRN_V), BF16),
                        pltpu.VMEM((2, q, HGRN_V), F32), pltpu.VMEM((2, SUBLANES, HGRN_K), F32)],
        compiler_params=_cparams(("arbitrary", "arbitrary")), name="gla",
    )(qa, fx, iv, g, gam8, norm_w.reshape(1, -1), s0)
    return o, s_o


def _select_topk_rows(gate_t, n_valid, nblk, width):
    blk = lax.broadcasted_iota(jnp.int32, (nblk, width), 0)
    blk_f = blk.astype(F32)
    g = jnp.where(blk < n_valid, gate_t, -jnp.inf)
    sel = jnp.zeros((nblk, width), F32)
    for _ in range(MOBA_TOPK):
        mx = jnp.max(g, axis=0, keepdims=True)
        first = jnp.min(jnp.where(g == mx, blk_f, float(nblk)), axis=0, keepdims=True)
        pick = (blk_f == first) & (mx > -jnp.inf)
        sel = jnp.where(pick, 1.0, sel)
        g = jnp.where(pick, -jnp.inf, g)
    return jnp.where(sel > 0.0, 0.0, -jnp.inf)


def _key_aug_columns(tm):
    lane = lax.broadcasted_iota(jnp.int32, (tm, LANES), 1)
    r = lax.broadcasted_iota(jnp.int32, (tm, LANES), 0).astype(F32)
    return jnp.where((lane >= MOBA_HEAD_DIM) & (lane < MOBA_HEAD_DIM + ALIBI_PARTS), r, 0.0)


def _moba_prompt_kernel(slopes_ref, kn_ref, qn_ref, dmin_ref, q_ref, ka_ref, vt_ref, km_ref, o_ref, brow_s,
                        ubuf, *, nblk):
    hp = pl.program_id(0)
    i = pl.program_id(1)
    tq = MOBA_BLOCK
    d = MOBA_HEAD_DIM
    q = q_ref[...]
    lane_k = lax.broadcasted_iota(jnp.int32, (nblk, 2 * d), 1)
    causal =(lax.broadcasted_iota(jnp.int32, (MOBA_BLOCK, tq), 0)
              <= lax.broadcasted_iota(jnp.int32, (MOBA_BLOCK, tq), 1))
    blk_f = lax.broadcasted_iota(jnp.int32, (nblk, tq), 0).astype(F32)
    col_f = lax.broadcasted_iota(jnp.int32, (nblk, tq), 1).astype(F32)
    aug_row = lax.broadcasted_iota(jnp.int32, (d, tq), 0)
    i_f = i.astype(F32)

    q_t = q.T
    km = km_ref[...]
    km2 = jnp.concatenate([jnp.where(lane_k < d, km, 0.0), jnp.where(lane_k >= d, km, 0.0)], axis=0)
    gate_both = _dot_split(km2, q_t)

    rhs = []
    for j in range(2):
        a2 = slopes_ref[2 * hp + j] * LOG2E
        sel = _select_topk_rows(gate_both[j * nblk:(j + 1) * nblk, :], i, nblk, tq)
        brow_s[j] = sel + a2 * (MOBA_BLOCK * (blk_f - i_f) - col_f)
        rest = jnp.full((d, tq), a2, F32)
        aug = jnp.zeros((d, tq), F32)
        for part in range(ALIBI_PARTS):
            term = rest.astype(BF16).astype(F32)
            aug = jnp.where(aug_row == part, term, aug)
            rest = rest - term
        q_h = q_t[j * d:(j + 1) * d, :] * (d ** -0.5 * LOG2E)
        rhs.append(jnp.concatenate([q_h, aug], axis=0).astype(BF16))

    def block_update(st, u, b_row, v_t):
        m_run, o_t = st
        m_new = jnp.maximum(m_run, jnp.max(u, axis=0, keepdims=True) + b_row)
        alpha = jnp.exp2(m_run - m_new)
        p = jnp.exp2(u - (m_new - b_row))
        return m_new, alpha * o_t + _dot(v_t, p.astype(BF16))

    def score(n, slot):
        for j in range(2):
            ubuf[slot, j] = _dot(ka_ref[j, n], rhs[j])

    def consume(n, slot, sts, own):
        new = []
        for j in range(2):
            u = ubuf[slot, j]
            if own:
                u = jnp.where(causal, u, -jnp.inf)
                b_row = -(slopes_ref[2 * hp + j] * LOG2E) * col_f[0:1, :]
            else:
                b_row = brow_s[j, pl.ds(n, 1), :]
            new.append(block_update(sts[j], u, b_row, vt_ref[n, j * MOBA_VT_ROWS:(j + 1) * MOBA_VT_ROWS, :]))
        return tuple(new)

    def step(n, slot, slot_ahead, sts):
        score(jnp.minimum(n + MOBA_AHEAD, i), slot_ahead)
        return consume(n, slot, sts, False)

    def group(base, unroll, sts):
        for k in range(unroll):
            sts = step(base + k, k % MOBA_SLOTS, (k + MOBA_AHEAD) % MOBA_SLOTS, sts)
        return sts

    def first_needed(j):
        hd = 2 * hp + j
        a2 = slopes_ref[hd] * LOG2E
        c = d ** -0.5 * LOG2E
        qn = qn_ref[i * MOBA_HEADS + hd]
        floor = c * (dmin_ref[i * MOBA_HEADS + hd] - BOUND_SLACK * qn * kn_ref[i * MOBA_HEADS + hd])

        def skippable(n):
            ceil = (c * (1.0 + BOUND_SLACK) * qn * kn_ref[n * MOBA_HEADS + hd]
                    + a2 * (MOBA_BLOCK * (n - i).astype(F32) + (MOBA_BLOCK - 1)))
            return (n < i) & (ceil < floor - UNDERFLOW_LOG2)

        return lax.while_loop(skippable, lambda n: n + 1, jnp.int32(0))

    start = (jnp.minimum(first_needed(0), first_needed(1)) // MOBA_SLOTS) * MOBA_SLOTS

    init = (jnp.full((1, tq), NEG_BIG, F32), jnp.zeros((MOBA_VT_ROWS, tq), F32))
    for k in range(MOBA_AHEAD):
        score(jnp.minimum(start + k, i), k)
    n_big = (i - start) // MOBA_UNROLL
    sts = lax.fori_loop(0, n_big, lambda g, s: group(start + g * MOBA_UNROLL, MOBA_UNROLL, s), (init, init))
    base = start + n_big * MOBA_UNROLL
    n_small = (i - base) // MOBA_SLOTS
    sts = lax.fori_loop(0, n_small, lambda g, s: group(base + g * MOBA_SLOTS, MOBA_SLOTS, s), sts)
    base = base + n_small * MOBA_SLOTS
    rem = i - base
    for k in range(MOBA_SLOTS - 1):
        sts = lax.cond(rem > k, lambda s, k=k: step(base + k, k, (k + MOBA_AHEAD) % MOBA_SLOTS, s),
                       lambda s: s, sts)
    outs = []
    for _, o_t in consume(i, rem, sts, True):
        outs.append(o_t[0:d, :] / o_t[d:d + 1, :])
    o_ref[...] = jnp.concatenate(outs, axis=0).T.astype(BF16)


def _alibi_slopes():
    return jnp.asarray(np.exp2(-8.0 * np.arange(1, MOBA_HEADS + 1) / MOBA_HEADS), dtype=F32)


def _moba_prompt(q, ka, vt, km, bounds):
    s = q.shape[0]
    nblk = s // MOBA_BLOCK
    assert nblk * MOBA_BLOCK == s and 2 * MOBA_HEAD_DIM == LANES
    pairs = MOBA_HEADS // 2
    tables = [bounds[:, r, :MOBA_HEADS].reshape(-1) for r in range(3)]
    smem = pl.BlockSpec(memory_space=pltpu.SMEM)
    return pl.pallas_call(
        functools.partial(_moba_prompt_kernel, nblk=nblk),
        grid=(pairs, nblk),
        in_specs=[smem, smem, smem, smem,
                  pl.BlockSpec((MOBA_BLOCK, LANES), lambda hp, i: (i, hp)),
                  pl.BlockSpec((2, nblk, MOBA_BLOCK, LANES), lambda hp, i: (hp, 0, 0, 0)),
                  pl.BlockSpec((nblk, 2 * MOBA_VT_ROWS, MOBA_BLOCK), lambda hp, i: (0, hp, 0)),
                  pl.BlockSpec((nblk, LANES), lambda hp, i: (0, hp))],
        out_specs=pl.BlockSpec((MOBA_BLOCK, LANES), lambda hp, i: (i, hp)),
        out_shape=jax.ShapeDtypeStruct((s, MOBA_INNER), BF16),
        scratch_shapes=[pltpu.VMEM((2, nblk, MOBA_BLOCK), F32),
                        pltpu.VMEM((MOBA_SLOTS, 2, MOBA_BLOCK, MOBA_BLOCK), F32)],
        compiler_params=_cparams(("arbitrary", "arbitrary")), name="moba_prompt",
    )(_alibi_slopes(), *tables, q, ka, vt, km)


PAGES_PER_STEP = 32


def _kmean_kernel(pt_ref, *refs):
    del pt_ref
    o_ref = refs[-1]
    s = pl.program_id(1)
    ppb = MOBA_BLOCK // PAGE_SIZE
    bps = PAGES_PER_STEP // ppb

    @pl.when(s == 0)
    def _():
        o_ref[...] = jnp.zeros(o_ref.shape, F32)

    acc = o_ref[0]
    lane = lax.broadcasted_iota(jnp.int32, acc.shape, 2)
    for m in range(bps):
        x = refs[m * ppb][0]
        for p in range(1, ppb):
            x = x + refs[m * ppb + p][0]
        mean = jnp.sum(x, axis=-1, keepdims=True) * (1.0 / MOBA_BLOCK)
        acc = jnp.where(lane == s * bps + m, mean, acc)
    o_ref[0] = acc


def _sample_kmean(k_pool_t, page_table):
    db, n_pages = page_table.shape
    ppb = MOBA_BLOCK // PAGE_SIZE
    n_full = n_pages // ppb
    assert n_full * ppb == n_pages and n_pages % PAGES_PER_STEP == 0
    steps = n_pages // PAGES_PER_STEP
    blk = (1, MOBA_HEADS, MOBA_HEAD_DIM, PAGE_SIZE)

    def page_spec(p):
        return pl.BlockSpec(blk, lambda b, s, pt: (pt[b * n_pages + s * PAGES_PER_STEP + p], 0, 0, 0))

    out_blk = (1, MOBA_HEADS, MOBA_HEAD_DIM, n_full)
    grid_spec = pltpu.PrefetchScalarGridSpec(
        num_scalar_prefetch=1, grid=(db, steps),
        in_specs=[page_spec(p) for p in range(PAGES_PER_STEP)],
        out_specs=pl.BlockSpec(out_blk, lambda b, s, pt: (b, 0, 0, 0)))
    return pl.pallas_call(
        _kmean_kernel, grid_spec=grid_spec,
        out_shape=jax.ShapeDtypeStruct((db,) + out_blk[1:], F32),
        compiler_params=_cparams(("arbitrary", "arbitrary")), name="sample_kmean",
    )(page_table.reshape(-1), *([k_pool_t] * PAGES_PER_STEP))


def _sample_select_kernel(q_ref, km_ref, idx_ref, *, n_full):
    t8 = q_ref.shape[2]
    blk = lax.broadcasted_iota(jnp.int32, (t8, n_full), 1).astype(F32)
    lane = lax.broadcasted_iota(jnp.int32, (t8, LANES), 1)
    for h in range(MOBA_HEADS):
        g = jnp.dot(q_ref[0, h], km_ref[0, h], precision=HIGHEST, preferred_element_type=F32)
        out = jnp.zeros((t8, LANES), jnp.int32)
        for k in range(MOBA_TOPK):
            mx = jnp.max(g, axis=-1, keepdims=True)
            first = jnp.min(jnp.where(g == mx, blk, float(n_full)), axis=-1, keepdims=True)
            out = jnp.where(lane == k, first.astype(jnp.int32), out)
            g = jnp.where(blk == first, -jnp.inf, g)
        idx_ref[0, h] = out


def _sample_select(qh, km):
    db, h, t8, d = qh.shape
    n_full = km.shape[3]
    assert n_full >= MOBA_TOPK
    b4 = lambda b: (b, 0, 0, 0)
    return pl.pallas_call(
        functools.partial(_sample_select_kernel, n_full=n_full), grid=(db,),
        in_specs=[pl.BlockSpec((1, h, t8, d), b4), pl.BlockSpec((1, h, d, n_full), b4)],
        out_specs=pl.BlockSpec((1, h, t8, LANES), b4),
        out_shape=jax.ShapeDtypeStruct((db, h, t8, LANES), jnp.int32),
        compiler_params=_cparams(("arbitrary",)), name="sample_select",
    )(qh, km)


def _sample_attn_kernel(idx_ref, phys_ref, slopes_ref, q_ref, kn_ref, vn_ref, kpool, vpool, o_ref,
                        kbuf, vbuf, sem, *, ds, past):
    ppb = MOBA_BLOCK // PAGE_SIZE
    n_sel = MOBA_TOPK * ppb
    n_pg = ds * n_sel
    b = pl.program_id(0)
    h = pl.program_id(1)
    n_heads = pl.num_programs(1)
    step = b * n_heads + h
    n_steps = pl.num_programs(0) * n_heads

    def page_copies(st, half):
        hh = st % n_heads
        out = []
        for k in range(n_pg):
            page = phys_ref[st * n_pg + k]
            out.append(pltpu.make_async_copy(kpool.at[page, hh], kbuf.at[half, k], sem.at[0, half]))
            out.append(pltpu.make_async_copy(vpool.at[page, hh], vbuf.at[half, k], sem.at[1, half]))
        return out

    @pl.when(step == 0)
    def _():
        for cp in page_copies(step, 0):
            cp.start()

    @pl.when(step + 1 < n_steps)
    def _():
        for cp in page_copies(step + 1, (step + 1) % 2):
            cp.start()

    half = step % 2
    for cp in page_copies(step, half):
        cp.wait()
    k_refs = [kbuf.at[half, k] for k in range(n_pg)]
    v_refs = [vbuf.at[half, k] for k in range(n_pg)]
    slope = slopes_ref[h]
    t8 = q_ref.shape[3]
    c_page = lax.broadcasted_iota(jnp.int32, (1, PAGE_SIZE), 1).astype(F32)
    c_new = lax.broadcasted_iota(jnp.int32, (1, t8), 1)
    qf = q_ref[0, 0] * (MOBA_HEAD_DIM ** -0.5)
    k_new = kn_ref[0, 0]
    v_new = vn_ref[0, 0]
    o_ref[0, 0] = jnp.zeros((MOBA_HEAD_DIM, t8), F32)
    for t in range(ds):
        q_t = qf[:, t:t + 1]
        t_pos = float(past + t)
        scores = []
        for sp in range(n_sel):
            slot, p = divmod(sp, ppb)
            blk_idx = idx_ref[((b * MOBA_HEADS + h) * ds + t) * MOBA_TOPK + slot]
            pos0 = (blk_idx * MOBA_BLOCK + p * PAGE_SIZE).astype(F32)
            s = jnp.sum(k_refs[t * n_sel + sp][...] * q_t, axis=0, keepdims=True)
            scores.append(s - slope * (t_pos - (pos0 + c_page)))
        s_new = jnp.sum(k_new * q_t, axis=0, keepdims=True)
        s_new = s_new - slope * (t_pos - (float(past) + c_new.astype(F32)))
        s_new = jnp.where(c_new <= t, s_new, -jnp.inf)
        mx = jnp.max(s_new, axis=-1, keepdims=True)
        for s in scores:
            mx = jnp.maximum(mx, jnp.max(s, axis=-1, keepdims=True))
        p_new = jnp.exp(s_new - mx)
        l = jnp.sum(p_new, axis=-1, keepdims=True)
        o = jnp.sum(p_new * v_new, axis=-1, keepdims=True)
        acc = jnp.zeros((MOBA_HEAD_DIM, PAGE_SIZE), F32)
        for sp in range(n_sel):
            pr = jnp.exp(scores[sp] - mx)
            l = l + jnp.sum(pr, axis=-1, keepdims=True)
            acc = acc + pr * v_refs[t * n_sel + sp][...]
        o = o + jnp.sum(acc, axis=-1, keepdims=True)
        o_ref[0, 0, :, t:t + 1] = o / l


def _sample_attn(qh, kh, vh, idx, k_pool, v_pool, page_table, ds):
    db, h, d, t8 = qh.shape
    n_pages = page_table.shape[1]
    ppb = MOBA_BLOCK // PAGE_SIZE
    assert n_pages % ppb == 0
    past = n_pages * PAGE_SIZE
    n_sel = MOBA_TOPK * ppb
    logical = idx[..., None] * ppb + jnp.arange(ppb, dtype=jnp.int32)
    phys = page_table[jnp.arange(db)[:, None, None, None], logical.reshape(db, h, ds, n_sel)]
    tok = pl.BlockSpec((1, 1, d, t8), lambda b, hh, *_: (b, hh, 0, 0))
    hbm = pl.BlockSpec(memory_space=pl.ANY)
    page_buf = pltpu.VMEM((2, ds * n_sel, d, PAGE_SIZE), F32)
    grid_spec = pltpu.PrefetchScalarGridSpec(
        num_scalar_prefetch=2, grid=(db, h),
        in_specs=[pl.BlockSpec(memory_space=pltpu.SMEM), tok, tok, tok, hbm, hbm],
        out_specs=tok,
        scratch_shapes=[page_buf, page_buf, pltpu.SemaphoreType.DMA((2, 2))])
    return pl.pallas_call(
        functools.partial(_sample_attn_kernel, ds=ds, past=past), grid_spec=grid_spec,
        out_shape=jax.ShapeDtypeStruct((db, h, d, t8), F32),
        compiler_params=_cparams(("arbitrary", "arbitrary")), name="sample_attn",
    )(idx.reshape(-1), phys.reshape(-1), _alibi_slopes(), qh, kh, vh, k_pool, v_pool)


def _split_w_in_a(w, f32_query):
    wb = w.astype(BF16)
    wdt = jnp.zeros((D_MODEL, LANES), BF16).at[:, :SSD_HEADS].set(wb[:, OFF_DT:OFF_Q])
    wq = w[:, OFF_Q:OFF_K] if f32_query else wb[:, OFF_Q:OFF_K]
    return [wb[:, :OFF_XBC], wb[:, OFF_XBC:OFF_DT], wdt, wq, wb[:, OFF_K:OFF_V], wb[:, OFF_V:]]


def _pad_seq(a, b, l, lp):
    a = a.reshape(b, l, a.shape[-1])
    return a if lp == l else jnp.pad(a, ((0, 0), (0, lp - l), (0, 0)))


def _round_up(n, m):
    return -(-n // m) * m


def _heads(a, b, l):
    return a.reshape(b, l, MOBA_HEADS, MOBA_HEAD_DIM).transpose(0, 2, 1, 3)


def _pad_tokens(a):
    l = a.shape[-2]
    return jnp.pad(a, ((0, 0),) * (a.ndim - 2) + ((0, _round_up(l, SUBLANES) - l), (0, 0)))


def _trunk(x, bsz, length, prompt, caches, p):
    ssm_conv0, ssm0, hgrn0, ffn_conv0 = caches[:4]
    m = bsz * length
    k_rows = v_rows = ssm_c = ssm_s = hgrn_s = None
    ffn_c = []
    for layer in range(DEPTH):
        nw = p['norm_mix'][layer].reshape(1, -1)
        if layer % 2 == 0:
            ia = layer // 2
            assert ia == 0
            ws = _split_w_in_a(p['w_in_a'][ia], f32_query=not prompt)
            outs = _inproj_a(x, nw, ws, prompt)
            z, xbc, dtr, q = outs[:4]
            lp = _round_up(length, SSD_CHUNK)
            y_ssd, hist, s_new = _ssd(
                _pad_seq(z, bsz, length, lp), _pad_seq(xbc, bsz, length, lp), _pad_seq(dtr, bsz, length, lp),
                ssm_conv0[ia], ssm0[ia], p['ssd_conv_w'][ia], p['ssd_conv_b'][ia], p['ssd_dt_bias'][ia],
                p['ssd_a_log'][ia], p['ssd_d'][ia], p['ssd_norm_w'][ia], length)
            y_ssd = y_ssd[:, :length].reshape(m, SSD_INNER)
            if prompt:
                assert bsz == 1
                kr, vr, ka, vt, km, bounds = outs[4:]
                o_att = _moba_prompt(q, ka, vt, km.reshape(-1, MOBA_INNER), bounds)
                k_rows, v_rows = jnp.swapaxes(kr, -1, -2)[None], jnp.swapaxes(vr, -1, -2)[None]
            else:
                k, v = outs[4:]
                k_pool, v_pool, page_table = caches[4:]
                k_pool_t = jnp.swapaxes(k_pool[ia], -1, -2)
                v_pool_t = jnp.swapaxes(v_pool[ia], -1, -2)
                qh, kh, vh = (_pad_tokens(_heads(t, bsz, length)) for t in (q, k, v))
                qt, kt, vt = (jnp.swapaxes(t, -1, -2) for t in (qh, kh, vh))
                km = _sample_kmean(k_pool_t, page_table)
                idx = _sample_select(qh, km)[:, :, :length, :MOBA_TOPK]
                o = _sample_attn(qt, kt, vt, idx, k_pool_t, v_pool_t, page_table, length)
                o_att = o[..., :length].transpose(0, 3, 1, 2).reshape(m, MOBA_INNER).astype(BF16)
                k_rows, v_rows = kh[:, :, :length], vh[:, :, :length]
            wo = p['w_out_a'][ia].astype(BF16)
            mix_acts, mix_ws = [y_ssd, o_att], [wo[:SSD_INNER], wo[SSD_INNER:]]
            ssm_c, ssm_s = hist, s_new
        else:
            ic = layer // 2
            assert ic == 0
            wc = p['w_in_c'][ic].astype(BF16)
            ws = [wc[:, :HGRN_K], wc[:, HGRN_K:2 * HGRN_K], wc[:, 2 * HGRN_K:2 * HGRN_K + HGRN_V],
                  wc[:, 2 * HGRN_K + HGRN_V:]]
            qa, fx, iv, g = _norm_matmul(x, nw, ws)
            lp = _round_up(length, HGRN_CHUNK)
            o, s_new = _gla(*(_pad_seq(t, bsz, length, lp) for t in (qa, fx, iv, g)),
                            p['hgrn_lb_gamma'], p['hgrn_norm_w'][ic], hgrn0[ic], length, layer)
            mix_acts, mix_ws = [o[:, :length].reshape(m, HGRN_V)], [p['w_out_c'][ic].astype(BF16)]
            hgrn_s = s_new
        nf = p['norm_final'].reshape(1, -1) if layer == DEPTH - 1 else None
        hist = ffn_conv0[layer, 0] if prompt else ffn_conv0[layer]
        x, fh = _ffn(x, mix_acts, mix_ws, p['norm_ffn'][layer].reshape(1, -1),
                     p['ffn_w_up'][layer].astype(BF16), p['ffn_conv_w'][layer],
                     p['ffn_conv_b'][layer].reshape(1, -1), p['ffn_w_down'][layer].astype(BF16), hist,
                     None if prompt else length, nf)
        ffn_c.append(fh[None] if prompt else fh)
    return (x.reshape(bsz, length, D_MODEL), k_rows[None], v_rows[None], ssm_s[None], ssm_c[None],
            hgrn_s[None], jnp.stack(ffn_c))


def kernel(x_prompt, x_sample, cache_k_pool, cache_v_pool, page_table, state_ssm, state_ssm_conv, state_hgrn,
           state_ffn_conv, norm_mix, norm_ffn, norm_final, w_in_a, w_out_a, ssd_conv_w, ssd_conv_b,
           ssd_dt_bias, ssd_a_log, ssd_d, ssd_norm_w, w_in_c, w_out_c, hgrn_lb_gamma, hgrn_norm_w, ffn_w_up,
           ffn_conv_w, ffn_conv_b, ffn_w_down):
    p = dict(norm_mix=norm_mix, norm_ffn=norm_ffn, norm_final=norm_final, w_in_a=w_in_a, w_out_a=w_out_a,
             ssd_conv_w=ssd_conv_w, ssd_conv_b=ssd_conv_b, ssd_dt_bias=ssd_dt_bias, ssd_a_log=ssd_a_log,
             ssd_d=ssd_d, ssd_norm_w=ssd_norm_w, w_in_c=w_in_c, w_out_c=w_out_c, hgrn_lb_gamma=hgrn_lb_gamma,
             hgrn_norm_w=hgrn_norm_w, ffn_w_up=ffn_w_up, ffn_conv_w=ffn_conv_w, ffn_conv_b=ffn_conv_b,
             ffn_w_down=ffn_w_down)
    bp, sp, _ = x_prompt.shape
    db, ds, _ = x_sample.shape
    na, nc = (DEPTH + 1) // 2, DEPTH // 2
    zeros = lambda *s: jnp.zeros(s, F32)
    prompt_caches = (zeros(na, bp, SSD_CONV - 1, SSD_CONV_DIM), zeros(na, bp, SSD_HEADS, SSD_HEAD_DIM, SSD_STATE),
                     zeros(nc, bp, HGRN_HEADS, HGRN_KEY_DIM, HGRN_VAL_DIM), zeros(DEPTH, bp, FFN_CONV - 1, FFN_DIM))
    outs_p = _trunk(x_prompt.reshape(bp * sp, D_MODEL), bp, sp, True, prompt_caches, p)
    sample_caches = (state_ssm_conv, state_ssm, state_hgrn, state_ffn_conv, cache_k_pool, cache_v_pool, page_table)
    outs_s = _trunk(x_sample.reshape(db * ds, D_MODEL), db, ds, False, sample_caches, p)
    yp, krp, vrp, ssp, scp, hgp, fcp = outs_p
    ys, krs, vrs, sss, scs, hgs, fcs = outs_s
    return (yp, ys, krp, vrp, ssp, scp, hgp, fcp, krs, vrs, sss, scs, hgs, fcs)
```

```python
import functools
import math

import numpy as np
import jax
import jax.numpy as jnp
from jax import lax
from jax.experimental import pallas as pl
from jax.experimental.pallas import tpu as pltpu

F32 = jnp.float32
BF16 = jnp.bfloat16
HIGHEST = lax.Precision.HIGHEST

D_MODEL = 1024
DEPTH = 2
PAGE_SIZE = 128
SSD_HEADS = 8
SSD_HEAD_DIM = 64
SSD_INNER = SSD_HEADS * SSD_HEAD_DIM
SSD_STATE = 64
SSD_GROUPS = 2
SSD_CONV = 4
SSD_CHUNK = 128
SSD_GN = SSD_GROUPS * SSD_STATE
SSD_CONV_DIM = SSD_INNER + 2 * SSD_GN
MOBA_HEADS = 8
MOBA_HEAD_DIM = 64
MOBA_INNER = MOBA_HEADS * MOBA_HEAD_DIM
MOBA_BLOCK = 256
MOBA_TOPK = 3
OFF_XBC = SSD_INNER
OFF_DT = OFF_XBC + SSD_CONV_DIM
OFF_Q = OFF_DT + SSD_HEADS
OFF_K = OFF_Q + MOBA_INNER
OFF_V = OFF_K + MOBA_INNER
HGRN_HEADS = 8
HGRN_KEY_DIM = 128
HGRN_VAL_DIM = D_MODEL // HGRN_HEADS
HGRN_K = HGRN_HEADS * HGRN_KEY_DIM
HGRN_V = HGRN_HEADS * HGRN_VAL_DIM
HGRN_CHUNK = 64
FFN_DIM = 2816
FFN_CONV = 3
RMS_EPS = 1e-6

LANES = 128
SUBLANES = 8
VMEM_LIMIT_BYTES = 56 * 1024 * 1024

ROW_TILE = 256
FFN_COL_CHUNK = 256
NEG_BIG = -1e30
LOG2E = math.log2(math.e)
ALIBI_PARTS = 3
MOBA_VT_ROWS = MOBA_HEAD_DIM + 16
BOUND_SLACK = 0.02
UNDERFLOW_LOG2 = 160.0
MOBA_AHEAD = 2
MOBA_SLOTS = MOBA_AHEAD + 1
MOBA_UNROLL = 4 * MOBA_SLOTS


def _cparams(sem, flags=None):
    return pltpu.CompilerParams(dimension_semantics=sem, vmem_limit_bytes=VMEM_LIMIT_BYTES, flags=flags)


def _const_spec(shape):
    nd = len(shape)
    return pl.BlockSpec(shape, lambda *_: (0,) * nd)


def _rms(x, w):
    y = x * lax.rsqrt(jnp.mean(x * x, axis=-1, keepdims=True) + RMS_EPS)
    return y * w


def _rms_bf16(x, w):
    return _rms(x, w).astype(BF16)


def _dot(a, b):
    return jnp.dot(a, b, preferred_element_type=F32)


def _dot_split(a, b):
    a_hi = a.astype(BF16)
    a_lo = (a - a_hi.astype(F32)).astype(BF16)
    b_hi = b.astype(BF16)
    b_lo = (b - b_hi.astype(F32)).astype(BF16)
    return _dot(jnp.concatenate([a_hi, a_lo, a_hi], axis=1), jnp.concatenate([b_hi, b_hi, b_lo], axis=0))


def _dot_nt(a, b):
    return lax.dot_general(a, b, (((1,), (1,)), ((), ())), preferred_element_type=F32)


def _dot_tn(a, b):
    return lax.dot_general(a, b, (((0,), (0,)), ((), ())), preferred_element_type=F32)


def _tril_f32(n):
    r = lax.broadcasted_iota(jnp.int32, (n, n), 0)
    c = lax.broadcasted_iota(jnp.int32, (n, n), 1)
    return r >= c


def _cumsum_rows(x, q, split):
    if not split:
        return jnp.dot(_tril_f32(q).astype(F32), x, precision=HIGHEST, preferred_element_type=F32)
    tri = _tril_f32(q).astype(BF16)
    acc = None
    rest = x
    for _ in range(3):
        term = rest.astype(BF16)
        part = _dot(tri, term)
        acc = part if acc is None else acc + part
        rest = rest - term.astype(F32)
    return acc


def _silu(x):
    return x * jax.nn.sigmoid(x)


def _inproj_a_kernel(x_ref, nw_ref, wz, wxbc, wdt, wq, wk, wv, *rest, prompt):
    outs = rest[1:] if prompt else rest
    h = _rms(x_ref[...], nw_ref[...])
    hb = h.astype(BF16)
    z_o, xbc_o, dt_o, q_o = outs[:4]
    z_o[...] = _dot(hb, wz[...])
    xbc_o[...] = _dot(hb, wxbc[...])
    dt_o[...] = _dot(hb, wdt[...])
    if wq.dtype == F32:
        q = jnp.dot(h, wq[...], precision=HIGHEST, preferred_element_type=F32)
    else:
        q = _dot(hb, wq[...])
    q_o[...] = q
    k = _dot(hb, wk[...])
    v = _dot(hb, wv[...])
    if not prompt:
        k_o, v_o = outs[4:]
        k_o[...] = k
        v_o[...] = v
        return
    kr_o, vr_o, ka_o, vt_o, km_o, bnd_o = outs[4:]
    tm = k.shape[0]
    head_sel = rest[0][...]
    k_norm = jnp.sqrt(jnp.max(_dot((k * k).astype(BF16), head_sel), axis=0, keepdims=True))
    q_norm = jnp.sqrt(jnp.max(_dot((q * q).astype(BF16), head_sel), axis=0, keepdims=True))
    qk_min = jnp.min(_dot((q * k).astype(BF16), head_sel), axis=0, keepdims=True)
    row8 = lax.broadcasted_iota(jnp.int32, (SUBLANES, LANES), 0)
    bnd_o[0] = jnp.where(row8 == 0, k_norm, jnp.where(row8 == 1, q_norm, jnp.where(row8 == 2, qk_min, 0.0)))
    lane = lax.broadcasted_iota(jnp.int32, (tm, LANES), 1)
    key_aug = _key_aug_columns(tm)
    for hp in range(MOBA_HEADS // 2):
        pair = k[:, hp * LANES:(hp + 1) * LANES]
        for j, kh in enumerate((pair, pltpu.roll(pair, MOBA_HEAD_DIM, 1))):
            ka_o[2 * hp + j, 0] = jnp.where(lane < MOBA_HEAD_DIM, kh, key_aug).astype(BF16)
    k_t = k.T
    v_t = v.T
    for p in range(tm // PAGE_SIZE):
        toks = slice(p * PAGE_SIZE, (p + 1) * PAGE_SIZE)
        for h in range(MOBA_HEADS):
            dims = slice(h * MOBA_HEAD_DIM, (h + 1) * MOBA_HEAD_DIM)
            kr_o[p, h] = k_t[dims, toks]
            vr_o[p, h] = v_t[dims, toks]
    ones_rows = (lax.broadcasted_iota(jnp.int32, (MOBA_VT_ROWS - MOBA_HEAD_DIM, tm), 0) == 0).astype(F32)
    pieces = []
    for h in range(MOBA_HEADS):
        pieces += [v_t[h * MOBA_HEAD_DIM:(h + 1) * MOBA_HEAD_DIM, :], ones_rows]
    vt_o[0] = jnp.concatenate(pieces, axis=0).astype(BF16)
    km_o[0] = jnp.mean(k, axis=0, keepdims=True)


def _inproj_a(x, nw, ws, prompt):
    m = x.shape[0]
    tm = ROW_TILE if prompt else m
    assert m % tm == 0
    if prompt:
        assert tm == MOBA_BLOCK
    nt = m // tm
    widths = [w.shape[1] for w in ws]
    row = lambda i: (i, 0)
    in_specs = [pl.BlockSpec((tm, D_MODEL), row), _const_spec((1, D_MODEL))]
    in_specs += [_const_spec(w.shape) for w in ws]
    out_shape = [jax.ShapeDtypeStruct((m, n), F32) for n in widths[:4]]
    out_specs = [pl.BlockSpec((tm, n), row) for n in widths[:4]]
    if prompt:
        npg = m // PAGE_SIZE
        ppt = tm // PAGE_SIZE
        rows_shape = (npg, MOBA_HEADS, MOBA_HEAD_DIM, PAGE_SIZE)
        rows_spec = pl.BlockSpec((ppt, MOBA_HEADS, MOBA_HEAD_DIM, PAGE_SIZE), lambda i: (i, 0, 0, 0))
        out_shape += [jax.ShapeDtypeStruct(rows_shape, F32)] * 2
        out_specs += [rows_spec, rows_spec]
        out_shape += [jax.ShapeDtypeStruct((MOBA_HEADS, nt, tm, LANES), BF16),
                      jax.ShapeDtypeStruct((nt, MOBA_HEADS * MOBA_VT_ROWS, tm), BF16),
                      jax.ShapeDtypeStruct((nt, 1, MOBA_INNER), F32),
                      jax.ShapeDtypeStruct((nt, SUBLANES, LANES), F32)]
        out_specs += [pl.BlockSpec((MOBA_HEADS, 1, tm, LANES), lambda i: (0, i, 0, 0)),
                      pl.BlockSpec((1, MOBA_HEADS * MOBA_VT_ROWS, tm), lambda i: (i, 0, 0)),
                      pl.BlockSpec((1, 1, MOBA_INNER), lambda i: (i, 0, 0)),
                      pl.BlockSpec((1, SUBLANES, LANES), lambda i: (i, 0, 0))]
        head_sel = (jnp.arange(MOBA_INNER)[:, None] // MOBA_HEAD_DIM == jnp.arange(LANES)[None, :]).astype(BF16)
        extra, extra_specs = [head_sel], [_const_spec(head_sel.shape)]
    else:
        out_shape += [jax.ShapeDtypeStruct((m, MOBA_INNER), F32)] * 2
        out_specs += [pl.BlockSpec((tm, MOBA_INNER), row)] * 2
        extra, extra_specs = [], []
    return pl.pallas_call(
        functools.partial(_inproj_a_kernel, prompt=prompt),
        grid=(nt,), in_specs=in_specs + extra_specs, out_specs=out_specs, out_shape=out_shape,
        compiler_params=_cparams(("arbitrary",)), name="inproj_a",
    )(x, nw, *ws, *extra)


def _norm_matmul_kernel(x_ref, nw_ref, *refs):
    n = len(refs) // 2
    hb = _rms_bf16(x_ref[...], nw_ref[...])
    for w, o in zip(refs[:n], refs[n:]):
        o[...] = _dot(hb, w[...])


def _norm_matmul(x, nw, ws):
    m = x.shape[0]
    tm = min(ROW_TILE, m)
    assert m % tm == 0
    row = lambda i: (i, 0)
    in_specs = [pl.BlockSpec((tm, D_MODEL), row), _const_spec((1, D_MODEL))]
    in_specs += [_const_spec(w.shape) for w in ws]
    return pl.pallas_call(
        _norm_matmul_kernel, grid=(m // tm,), in_specs=in_specs,
        out_specs=[pl.BlockSpec((tm, w.shape[1]), row) for w in ws],
        out_shape=[jax.ShapeDtypeStruct((m, w.shape[1]), F32) for w in ws],
        compiler_params=_cparams(("arbitrary",)), name="norm_matmul",
    )(x, nw, *ws)


def _ffn_kernel(*refs, tm, seq_len, final_norm, n_mix):
    it = iter(refs)
    x_ref = next(it)
    mix_refs = [next(it) for _ in range(2 * n_mix)]
    nw_ref, wup_ref, cw_ref, cb_ref, wdn_ref = (next(it) for _ in range(5))
    if seq_len is None:
        hist_ref = next(it)
    else:
        h1_ref, h2_ref = next(it), next(it)
    nf_ref = next(it) if final_norm else None
    o_ref = next(it)
    tail_ref = next(it)
    up_s, act_s = next(it), next(it)

    i = pl.program_id(0)
    f = FFN_DIM
    if seq_len is None:
        @pl.when(i == 0)
        def _():
            up_s[0:SUBLANES, 0:f] = hist_ref[...]
    else:
        up_s[0:SUBLANES, 0:f] = jnp.zeros((SUBLANES, f), F32)

    x = x_ref[...]
    for a_ref, w_ref in zip(mix_refs[:n_mix], mix_refs[n_mix:]):
        x = x + _dot(a_ref[...], w_ref[...])
    hb = _rms_bf16(x, nw_ref[...])
    up_s[SUBLANES:SUBLANES + tm, :] = _dot(hb, wup_ref[0])

    if seq_len is not None:
        t = lax.broadcasted_iota(jnp.int32, (tm, 1), 0) % seq_len
    for c in range(0, f, FFN_COL_CHUNK):
        cols = slice(c, c + FFN_COL_CHUNK)
        g0 = up_s[SUBLANES:SUBLANES + tm, cols]
        g1 = up_s[SUBLANES - 1:SUBLANES - 1 + tm, cols]
        g2 = up_s[SUBLANES - 2:SUBLANES - 2 + tm, cols]
        if seq_len is not None:
            g1 = jnp.where(t >= 1, g1, h1_ref[:, cols])
            g2 = jnp.where(t >= 2, g2, h2_ref[:, cols])
        val = up_s[SUBLANES:SUBLANES + tm, f + c:f + c + FFN_COL_CHUNK]
        conv = g2 * cw_ref[0:1, cols]
        conv = conv + g1 * cw_ref[1:2, cols]
        conv = conv + g0 * cw_ref[2:3, cols]
        conv = conv + cb_ref[:, cols]
        act_s[:, cols] = (_silu(conv) * val).astype(BF16)

    out = x + _dot(act_s[...], wdn_ref[0])
    o_ref[...] = _rms(out, nf_ref[...]) if final_norm else out

    if seq_len is None:
        last = up_s[tm:tm + SUBLANES, 0:f]
        up_s[0:SUBLANES, 0:f] = last

        @pl.when(i == pl.num_programs(0) - 1)
        def _():
            tail_ref[...] = last
    else:
        tail_ref[...] = up_s[SUBLANES:SUBLANES + tm, 0:f]


def _ffn(x, mix_acts, mix_ws, nw, wup, cw, cb, wdn, layer, hist, seq_len, nf):
    assert FFN_CONV == 3
    m = x.shape[0]
    f = FFN_DIM
    final_norm = nf is not None
    row = lambda i: (i, 0)
    if seq_len is None:
        tm = ROW_TILE
        hist8 = jnp.zeros((SUBLANES, f), F32).at[SUBLANES - 2:].set(hist)
        extra = [hist8]
        extra_specs = [_const_spec((SUBLANES, f))]
        tail_shape = (SUBLANES, f)
    else:
        tm = m
        nb = m // seq_len
        assert seq_len >= 2 and nb * seq_len == m
        h1 = jnp.zeros((nb, seq_len, f), F32).at[:, 0].set(hist[:, 1]).reshape(m, f)
        h2 = jnp.zeros((nb, seq_len, f), F32).at[:, 0:2].set(hist).reshape(m, f)
        extra = [h1, h2]
        extra_specs = [_const_spec((m, f))] * 2
        tail_shape = (m, f)
    assert m % tm == 0
    ins = [x] + list(mix_acts) + list(mix_ws) + [nw, wup, cw, cb, wdn] + extra
    in_specs = [pl.BlockSpec((tm, D_MODEL), row)]
    in_specs += [pl.BlockSpec((tm, a.shape[1]), row) for a in mix_acts]
    in_specs += [_const_spec(w.shape) for w in mix_ws]
    layer_slab = lambda w: pl.BlockSpec((1,) + w.shape[1:], lambda i: (layer, 0, 0))
    in_specs += [_const_spec((1, D_MODEL)), layer_slab(wup), _const_spec(cw.shape),
                 _const_spec(cb.shape), layer_slab(wdn)] + extra_specs
    if final_norm:
        ins.append(nf)
        in_specs.append(_const_spec((1, D_MODEL)))
    xo, tail = pl.pallas_call(
        functools.partial(_ffn_kernel, tm=tm, seq_len=seq_len, final_norm=final_norm, n_mix=len(mix_acts)),
        grid=(m // tm,), in_specs=in_specs,
        out_specs=[pl.BlockSpec((tm, D_MODEL), row), _const_spec(tail_shape)],
        out_shape=[jax.ShapeDtypeStruct((m, D_MODEL), F32), jax.ShapeDtypeStruct(tail_shape, F32)],
        scratch_shapes=[pltpu.VMEM((SUBLANES + tm, 2 * f), F32), pltpu.VMEM((tm, f), BF16)],
        compiler_params=_cparams(("arbitrary",)), name="conv_ffn",
    )(*ins)
    if seq_len is None:
        new_hist = tail[SUBLANES - 2:]
    else:
        new_hist = tail.reshape(m // seq_len, seq_len, f)[:, seq_len - 2:]
    return xo, new_hist


def _ssd_kernel(z_ref, xbc_ref, dt_ref, hist_ref, s0_ref, cw_ref, cb_ref, dtb_ref, alog_ref, dsk_ref,
                nw_ref, y_ref, hist_o, s_o, cbuf, st, ybuf, m1_s, xdt_s, ce_s, xw_s, bm_s, xsd_s, zs_s, el_s,
                *, q, seq_len, n_chunks):
    c = pl.program_id(1)
    pipelined = n_chunks > 1
    stage = (m1_s, xdt_s, ce_s, xw_s, bm_s, xsd_s, zs_s, el_s)

    def state_stage(rd):
        _ssd_state_stage(rd, nw_ref, y_ref, st, ybuf, *stage)

    def free_stage(wr):
        _ssd_free_stage(c, wr, z_ref, xbc_ref, dt_ref, cw_ref, cb_ref, dtb_ref, alog_ref, dsk_ref, hist_o, cbuf,
                        *stage, q=q, seq_len=seq_len, n_chunks=n_chunks)

    @pl.when(c == 0)
    def _():
        cbuf[0:SUBLANES, :] = hist_ref[0]
        st[...] = s0_ref[0]
        if pipelined:
            for ref in stage[:-1]:
                ref[1] = jnp.zeros(ref.shape[1:], ref.dtype)
            el_s[1] = jnp.ones(el_s.shape[1:], F32)

    if pipelined:
        @pl.when(c % 2 == 0)
        def _():
            state_stage(1)
            free_stage(0)

        @pl.when(c % 2 == 1)
        def _():
            state_stage(0)
            free_stage(1)
    else:
        free_stage(0)
        state_stage(0)

    @pl.when(c == (n_chunks if pipelined else n_chunks - 1))
    def _():
        s_o[0] = st[...]


def _ssd_state_stage(rd, nw_ref, y_ref, st, ybuf, m1_s, xdt_s, ce_s, xw_s, bm_s, xsd_s, zs_s, el_s):
    rep = SSD_HEADS // SSD_GROUPS
    hd = SSD_HEAD_DIM
    e_last = el_s[rd, 0:1, :]
    y_state = [_dot_nt(ce_s[rd, h], st[h].astype(BF16)) for h in range(SSD_HEADS)]
    s_inc = [_dot_tn(xw_s[rd, h], bm_s[rd, h // rep]) for h in range(SSD_HEADS)]
    y_intra = [_dot(m1_s[rd, h], xdt_s[rd, h]) for h in range(SSD_HEADS)]
    for h in range(SSD_HEADS):
        st[h] = st[h] * e_last[:, h:h + 1] + s_inc[h]
        ybuf[:, h * hd:(h + 1) * hd] = y_intra[h] + y_state[h]
    yz = (ybuf[...] + xsd_s[rd]) * zs_s[rd]
    gw = SSD_INNER // SSD_GROUPS
    for g in range(SSD_GROUPS):
        yg = yz[:, g * gw:(g + 1) * gw]
        yn = yg * lax.rsqrt(jnp.mean(yg * yg, axis=-1, keepdims=True) + RMS_EPS)
        y_ref[0, :, g * gw:(g + 1) * gw] = (yn * nw_ref[:, g * gw:(g + 1) * gw]).astype(BF16)


def _ssd_free_stage(c, wr, z_ref, xbc_ref, dt_ref, cw_ref, cb_ref, dtb_ref, alog_ref, dsk_ref, hist_o, cbuf,
                    m1_s, xdt_s, ce_s, xw_s, bm_s, xsd_s, zs_s, el_s, *, q, seq_len, n_chunks):
    rep = SSD_HEADS // SSD_GROUPS
    hd = SSD_HEAD_DIM
    hrows = SSD_CONV - 1
    h0 = SUBLANES - hrows
    cbuf[SUBLANES:SUBLANES + q, :] = xbc_ref[0]
    conv = cbuf[h0:h0 + q, :] * cw_ref[0:1, :]
    for j in range(1, SSD_CONV):
        conv = conv + cbuf[h0 + j:h0 + j + q, :] * cw_ref[j:j + 1, :]
    xc = _silu(conv + cb_ref[...])

    xdt_raw = dt_ref[0] + dtb_ref[...]
    dt = jnp.maximum(xdt_raw, 0.0) + jnp.log1p(jnp.exp(-jnp.abs(xdt_raw)))
    if seq_len % q != 0:
        row = jnp.minimum(c, n_chunks - 1) * q + lax.broadcasted_iota(jnp.int32, (q, 1), 0)
        dt = jnp.where(row < seq_len, dt, 0.0)
    a = -jnp.exp(alog_ref[...])
    acum = _cumsum_rows(dt * a, q, split=False)
    acum_t = acum.T
    a_last = acum[q - 1:q, :]
    e_acum = jnp.exp(acum)
    w_in = jnp.exp(a_last - acum) * dt
    el_s[wr] = jnp.broadcast_to(jnp.exp(a_last), (SUBLANES, LANES))
    tril = _tril_f32(q)

    cbs, cms = [], []
    for g in range(SSD_GROUPS):
        bm = xc[:, SSD_INNER + g * SSD_STATE:SSD_INNER + (g + 1) * SSD_STATE].astype(BF16)
        cm = xc[:, SSD_INNER + SSD_GN + g * SSD_STATE:SSD_INNER + SSD_GN + (g + 1) * SSD_STATE]
        bm_s[wr, g] = bm
        cms.append(cm)
        cbs.append(_dot_nt(cm.astype(BF16), bm))

    for h in range(SSD_HEADS):
        g = h // rep
        xs = xc[:, h * hd:(h + 1) * hd]
        seg = acum[:, h:h + 1] - acum_t[h:h + 1, :]
        decay = jnp.exp(jnp.where(tril, seg, -jnp.inf))
        m1_s[wr, h] = (cbs[g] * decay).astype(BF16)
        xdt_s[wr, h] = (xs * dt[:, h:h + 1]).astype(BF16)
        ce_s[wr, h] = (cms[g] * e_acum[:, h:h + 1]).astype(BF16)
        xw_s[wr, h] = (xs * w_in[:, h:h + 1]).astype(BF16)
        xsd_s[wr, :, h * hd:(h + 1) * hd] = xs * dsk_ref[:, h:h + 1]
    zs_s[wr] = _silu(z_ref[0])

    @pl.when(c == n_chunks - 1)
    def _():
        l_last = seq_len - (n_chunks - 1) * q
        hist_o[0] = cbuf[l_last:l_last + SUBLANES, :]

    cbuf[0:SUBLANES, :] = cbuf[q:q + SUBLANES, :]


BF16_SUBLANES = 16


def _chunk_len(seq_len, max_chunk):
    return min(max_chunk, _round_up(seq_len, BF16_SUBLANES))


def _stage_grid(n_chunks):
    if n_chunks == 1:
        same = lambda i, c: (i, c, 0)
        return 1, same, same
    return (n_chunks + 1, lambda i, c: (i, jnp.minimum(c, n_chunks - 1), 0),
            lambda i, c: (i, jnp.maximum(c - 1, 0), 0))


def _pad_lanes(v, fill=0.0):
    return jnp.full((1, LANES), fill, F32).at[0, :v.shape[0]].set(v.astype(F32))


def _ssd(z, xbc, dtr, hist, s0, conv_w, conv_b, dt_bias, a_log, d_skip, norm_w, seq_len):
    b, lp, _ = z.shape
    q = _chunk_len(seq_len, SSD_CHUNK)
    assert lp % q == 0 and lp - seq_len < q
    hrows = SSD_CONV - 1
    hist8 = jnp.zeros((b, SUBLANES, SSD_CONV_DIM), F32).at[:, SUBLANES - hrows:].set(hist)
    cw8 = jnp.zeros((SUBLANES, SSD_CONV_DIM), F32).at[:SSD_CONV].set(conv_w)
    n_chunks = lp // q
    n_steps, seq, seq_out = _stage_grid(n_chunks)
    per_b3 = lambda i, c: (i, 0, 0)
    per_b4 = lambda i, c: (i, 0, 0, 0)
    in_specs = [pl.BlockSpec((1, q, SSD_INNER), seq), pl.BlockSpec((1, q, SSD_CONV_DIM), seq),
                pl.BlockSpec((1, q, LANES), seq), pl.BlockSpec((1, SUBLANES, SSD_CONV_DIM), per_b3),
                pl.BlockSpec((1, SSD_HEADS, SSD_HEAD_DIM, SSD_STATE), per_b4),
                _const_spec((SUBLANES, SSD_CONV_DIM)), _const_spec((1, SSD_CONV_DIM)),
                _const_spec((1, LANES)), _const_spec((1, LANES)), _const_spec((1, LANES)),
                _const_spec((1, SSD_INNER))]
    out_shape = [jax.ShapeDtypeStruct((b, lp, SSD_INNER), BF16),
                 jax.ShapeDtypeStruct((b, SUBLANES, SSD_CONV_DIM), F32),
                 jax.ShapeDtypeStruct((b, SSD_HEADS, SSD_HEAD_DIM, SSD_STATE), F32)]
    out_specs = [pl.BlockSpec((1, q, SSD_INNER), seq_out), pl.BlockSpec((1, SUBLANES, SSD_CONV_DIM), per_b3),
                 pl.BlockSpec((1, SSD_HEADS, SSD_HEAD_DIM, SSD_STATE), per_b4)]
    per_head = pltpu.VMEM((2, SSD_HEADS, q, SSD_HEAD_DIM), BF16)
    y, hist_o, s_o = pl.pallas_call(
        functools.partial(_ssd_kernel, q=q, seq_len=seq_len, n_chunks=n_chunks),
        grid=(b, n_steps), in_specs=in_specs, out_specs=out_specs, out_shape=out_shape,
        scratch_shapes=[pltpu.VMEM((SUBLANES + q, SSD_CONV_DIM), F32),
                        pltpu.VMEM((SSD_HEADS, SSD_HEAD_DIM, SSD_STATE), F32),
                        pltpu.VMEM((q, SSD_INNER), F32),
                        pltpu.VMEM((2, SSD_HEADS, q, q), BF16), per_head, per_head, per_head,
                        pltpu.VMEM((2, SSD_GROUPS, q, SSD_STATE), BF16),
                        pltpu.VMEM((2, q, SSD_INNER), F32), pltpu.VMEM((2, q, SSD_INNER), F32),
                        pltpu.VMEM((2, SUBLANES, LANES), F32)],
        compiler_params=_cparams(("arbitrary", "arbitrary")), name="ssd",
    )(z, xbc, dtr, hist8, s0, cw8, conv_b.reshape(1, -1), _pad_lanes(dt_bias), _pad_lanes(a_log),
      _pad_lanes(d_skip), norm_w.reshape(1, -1))
    return y, hist_o[:, SUBLANES - hrows:], s_o


def _gla_kernel(q_ref, fx_ref, iv_ref, g_ref, gam_ref, nw_ref, s0_ref, o_ref, s_o, st_t, qd_s, kd_s, kdec_s,
                v_s, gs_s, el_s, *, q, seq_len, layer, n_chunks):
    c = pl.program_id(1)
    pipelined = n_chunks > 1
    stage = (qd_s, kd_s, kdec_s, v_s, gs_s, el_s)

    def state_stage(rd):
        _gla_state_stage(rd, nw_ref, o_ref, st_t, *stage, q=q)

    def free_stage(wr):
        _gla_free_stage(c, wr, q_ref, fx_ref, iv_ref, g_ref, gam_ref, *stage, q=q, seq_len=seq_len, layer=layer,
                        n_chunks=n_chunks)

    @pl.when(c == 0)
    def _():
        for h in range(HGRN_HEADS):
            st_t[h] = s0_ref[0, h].T
        if pipelined:
            for ref in stage[:-1]:
                ref[1] = jnp.zeros(ref.shape[1:], ref.dtype)
            el_s[1] = jnp.ones(el_s.shape[1:], F32)

    if pipelined:
        @pl.when(c % 2 == 0)
        def _():
            state_stage(1)
            free_stage(0)

        @pl.when(c % 2 == 1)
        def _():
            state_stage(0)
            free_stage(1)
    else:
        free_stage(0)
        state_stage(0)

    @pl.when(c == (n_chunks if pipelined else n_chunks - 1))
    def _():
        for h in range(HGRN_HEADS):
            s_o[0, h] = st_t[h].T


def _gla_state_stage(rd, nw_ref, o_ref, st_t, qd_s, kd_s, kdec_s, v_s, gs_s, el_s, *, q):
    tril = _tril_f32(q)
    ks = [slice(h * HGRN_KEY_DIM, (h + 1) * HGRN_KEY_DIM) for h in range(HGRN_HEADS)]
    vs = [slice(h * HGRN_VAL_DIM, (h + 1) * HGRN_VAL_DIM) for h in range(HGRN_HEADS)]
    qd_b = [qd_s[rd, :, s] for s in ks]
    v_b = [v_s[rd, :, s] for s in vs]
    e_last = el_s[rd, 0:1, :]
    att = [_dot_nt(qd_b[h], kd_s[rd, :, ks[h]]) for h in range(HGRN_HEADS)]
    o_state = [_dot_nt(qd_b[h], st_t[h].astype(BF16)) for h in range(HGRN_HEADS)]
    kv = [_dot_tn(v_b[h], kdec_s[rd, :, ks[h]]) for h in range(HGRN_HEADS)]
    for h in range(HGRN_HEADS):
        o = _dot(jnp.where(tril, att[h], 0.0).astype(BF16), v_b[h]) + o_state[h]
        st_t[h] = st_t[h] * e_last[:, ks[h]] + kv[h]
        on = o * lax.rsqrt(jnp.mean(o * o, axis=-1, keepdims=True) + RMS_EPS) * nw_ref[...]
        o_ref[0, :, vs[h]] = (on * gs_s[rd, :, vs[h]]).astype(BF16)


def _gla_free_stage(c, wr, q_ref, fx_ref, iv_ref, g_ref, gam_ref, qd_s, kd_s, kdec_s, v_s, gs_s, el_s, *, q,
                    seq_len, layer, n_chunks):
    rows = [gam_ref[l:l + 1, :] for l in range(DEPTH)]
    mx = functools.reduce(jnp.maximum, rows)
    es = [jnp.exp(r - mx) for r in rows]
    lb = sum(es[1:layer + 1]) / sum(es) if layer >= 1 else jnp.zeros_like(mx)

    f = lb + (1.0 - lb) * jax.nn.sigmoid(fx_ref[0])
    if seq_len % q != 0:
        row = jnp.minimum(c, n_chunks - 1) * q + lax.broadcasted_iota(jnp.int32, (q, 1), 0)
        f = jnp.where(row < seq_len, f, 1.0)
    kk = 1.0 - f
    b = _cumsum_rows(jnp.log(f), q, split=True)
    b_last = b[q - 1:q, :]
    qd_s[wr] = (q_ref[0] * jnp.exp(b)).astype(BF16)
    kd_s[wr] = (kk * jnp.exp(-b)).astype(BF16)
    kdec_s[wr] = (kk * jnp.exp(b_last - b)).astype(BF16)
    v_s[wr] = iv_ref[0].astype(BF16)
    gs_s[wr] = _silu(g_ref[0])
    el_s[wr] = jnp.broadcast_to(jnp.exp(b_last), (SUBLANES, HGRN_K))


def _gla(qa, fx, iv, g, gamma, norm_w, s0, seq_len, layer):
    b, lp, _ = qa.shape
    q = _chunk_len(seq_len, HGRN_CHUNK)
    assert lp % q == 0 and lp - seq_len < q
    gam8 = jnp.zeros((SUBLANES, HGRN_K), F32).at[:DEPTH].set(gamma)
    n_chunks = lp // q
    n_steps, seq, seq_out = _stage_grid(n_chunks)
    per_b4 = lambda i, c: (i, 0, 0, 0)
    st_shape = (HGRN_HEADS, HGRN_KEY_DIM, HGRN_VAL_DIM)
    in_specs = [pl.BlockSpec((1, q, HGRN_K), seq), pl.BlockSpec((1, q, HGRN_K), seq),
                pl.BlockSpec((1, q, HGRN_V), seq), pl.BlockSpec((1, q, HGRN_V), seq),
                _const_spec((SUBLANES, HGRN_K)), _const_spec((1, HGRN_VAL_DIM)),
                pl.BlockSpec((1,) + st_shape, per_b4)]
    stage_k = pltpu.VMEM((2, q, HGRN_K), BF16)
    o, s_o = pl.pallas_call(
        functools.partial(_gla_kernel, q=q, seq_len=seq_len, layer=layer, n_chunks=n_chunks),
        grid=(b, n_steps), in_specs=in_specs,
        out_specs=[pl.BlockSpec((1, q, HGRN_V), seq_out), pl.BlockSpec((1,) + st_shape, per_b4)],
        out_shape=[jax.ShapeDtypeStruct((b, lp, HGRN_V), BF16), jax.ShapeDtypeStruct((b,) + st_shape, F32)],
        scratch_shapes=[pltpu.VMEM((HGRN_HEADS, HGRN_VAL_DIM, HGRN_KEY_DIM), F32),
                        stage_k, stage_k, stage_k, pltpu.VMEM((2, q, HGRN_V), BF16),
                        pltpu.VMEM((2, q, HGRN_V), F32), pltpu.VMEM((2, SUBLANES, HGRN_K), F32)],
        compiler_params=_cparams(("arbitrary", "arbitrary")), name="gla",
    )(qa, fx, iv, g, gam8, norm_w.reshape(1, -1), s0)
    return o, s_o


def _select_topk_rows(gate_t, n_valid, nblk, width):
    blk = lax.broadcasted_iota(jnp.int32, (nblk, width), 0)
    blk_f = blk.astype(F32)
    g = jnp.where(blk < n_valid, gate_t, -jnp.inf)
    sel = jnp.zeros((nblk, width), F32)
    for _ in range(MOBA_TOPK):
        mx = jnp.max(g, axis=0, keepdims=True)
        first = jnp.min(jnp.where(g == mx, blk_f, float(nblk)), axis=0, keepdims=True)
        pick = (blk_f == first) & (mx > -jnp.inf)
        sel = jnp.where(pick, 1.0, sel)
        g = jnp.where(pick, -jnp.inf, g)
    return jnp.where(sel > 0.0, 0.0, -jnp.inf)


def _key_aug_columns(tm):
    lane = lax.broadcasted_iota(jnp.int32, (tm, LANES), 1)
    r = lax.broadcasted_iota(jnp.int32, (tm, LANES), 0).astype(F32)
    return jnp.where((lane >= MOBA_HEAD_DIM) & (lane < MOBA_HEAD_DIM + ALIBI_PARTS), r, 0.0)


def _moba_prompt_kernel(slopes_ref, kn_ref, qn_ref, dmin_ref, q_ref, ka_ref, vt_ref, km_ref, o_ref, brow_s,
                        ubuf, *, nblk):
    hp = pl.program_id(0)
    i = pl.program_id(1)
    tq = MOBA_BLOCK
    d = MOBA_HEAD_DIM
    q = q_ref[...]
    lane_k = lax.broadcasted_iota(jnp.int32, (nblk, 2 * d), 1)
    causal =(lax.broadcasted_iota(jnp.int32, (MOBA_BLOCK, tq), 0)
              <= lax.broadcasted_iota(jnp.int32, (MOBA_BLOCK, tq), 1))
    blk_f = lax.broadcasted_iota(jnp.int32, (nblk, tq), 0).astype(F32)
    col_f = lax.broadcasted_iota(jnp.int32, (nblk, tq), 1).astype(F32)
    aug_row = lax.broadcasted_iota(jnp.int32, (d, tq), 0)
    i_f = i.astype(F32)

    q_t = q.T
    km = km_ref[...]
    km2 = jnp.concatenate([jnp.where(lane_k < d, km, 0.0), jnp.where(lane_k >= d, km, 0.0)], axis=0)
    gate_both = _dot_split(km2, q_t)

    rhs = []
    for j in range(2):
        a2 = slopes_ref[2 * hp + j] * LOG2E
        sel = _select_topk_rows(gate_both[j * nblk:(j + 1) * nblk, :], i, nblk, tq)
        brow_s[j] = sel + a2 * (MOBA_BLOCK * (blk_f - i_f) - col_f)
        rest = jnp.full((d, tq), a2, F32)
        aug = jnp.zeros((d, tq), F32)
        for part in range(ALIBI_PARTS):
            term = rest.astype(BF16).astype(F32)
            aug = jnp.where(aug_row == part, term, aug)
            rest = rest - term
        q_h = q_t[j * d:(j + 1) * d, :] * (d ** -0.5 * LOG2E)
        rhs.append(jnp.concatenate([q_h, aug], axis=0).astype(BF16))

    def block_update(st, u, b_row, v_t):
        m_run, o_t = st
        m_new = jnp.maximum(m_run, jnp.max(u, axis=0, keepdims=True) + b_row)
        alpha = jnp.exp2(m_run - m_new)
        p = jnp.exp2(u - (m_new - b_row))
        return m_new, alpha * o_t + _dot(v_t, p.astype(BF16))

    def score(n, slot):
        for j in range(2):
            ubuf[slot, j] = _dot(ka_ref[j, n], rhs[j])

    def consume(n, slot, sts, own):
        new = []
        for j in range(2):
            u = ubuf[slot, j]
            if own:
                u = jnp.where(causal, u, -jnp.inf)
                b_row = -(slopes_ref[2 * hp + j] * LOG2E) * col_f[0:1, :]
            else:
                b_row = brow_s[j, pl.ds(n, 1), :]
            new.append(block_update(sts[j], u, b_row, vt_ref[n, j * MOBA_VT_ROWS:(j + 1) * MOBA_VT_ROWS, :]))
        return tuple(new)

    def step(n, slot, slot_ahead, sts):
        score(jnp.minimum(n + MOBA_AHEAD, i), slot_ahead)
        return consume(n, slot, sts, False)

    def group(base, unroll, sts):
        for k in range(unroll):
            sts = step(base + k, k % MOBA_SLOTS, (k + MOBA_AHEAD) % MOBA_SLOTS, sts)
        return sts

    def first_needed(j):
        hd = 2 * hp + j
        a2 = slopes_ref[hd] * LOG2E
        c = d ** -0.5 * LOG2E
        qn = qn_ref[i * MOBA_HEADS + hd]
        floor = c * (dmin_ref[i * MOBA_HEADS + hd] - BOUND_SLACK * qn * kn_ref[i * MOBA_HEADS + hd])

        def skippable(n):
            ceil = (c * (1.0 + BOUND_SLACK) * qn * kn_ref[n * MOBA_HEADS + hd]
                    + a2 * (MOBA_BLOCK * (n - i).astype(F32) + (MOBA_BLOCK - 1)))
            return (n < i) & (ceil < floor - UNDERFLOW_LOG2)

        return lax.while_loop(skippable, lambda n: n + 1, jnp.int32(0))

    start = (jnp.minimum(first_needed(0), first_needed(1)) // MOBA_SLOTS) * MOBA_SLOTS

    init = (jnp.full((1, tq), NEG_BIG, F32), jnp.zeros((MOBA_VT_ROWS, tq), F32))
    for k in range(MOBA_AHEAD):
        score(jnp.minimum(start + k, i), k)
    n_big = (i - start) // MOBA_UNROLL
    sts = lax.fori_loop(0, n_big, lambda g, s: group(start + g * MOBA_UNROLL, MOBA_UNROLL, s), (init, init))
    base = start + n_big * MOBA_UNROLL
    n_small = (i - base) // MOBA_SLOTS
    sts = lax.fori_loop(0, n_small, lambda g, s: group(base + g * MOBA_SLOTS, MOBA_SLOTS, s), sts)
    base = base + n_small * MOBA_SLOTS
    rem = i - base
    for k in range(MOBA_SLOTS - 1):
        sts = lax.cond(rem > k, lambda s, k=k: step(base + k, k, (k + MOBA_AHEAD) % MOBA_SLOTS, s),
                       lambda s: s, sts)
    outs = []
    for _, o_t in consume(i, rem, sts, True):
        outs.append(o_t[0:d, :] / o_t[d:d + 1, :])
    o_ref[...] = jnp.concatenate(outs, axis=0).T.astype(BF16)


def _alibi_slopes():
    return jnp.asarray(np.exp2(-8.0 * np.arange(1, MOBA_HEADS + 1) / MOBA_HEADS), dtype=F32)


def _moba_prompt(q, ka, vt, km, bounds):
    s = q.shape[0]
    nblk = s // MOBA_BLOCK
    assert nblk * MOBA_BLOCK == s and 2 * MOBA_HEAD_DIM == LANES
    pairs = MOBA_HEADS // 2
    tables = [bounds[:, r, :MOBA_HEADS].reshape(-1) for r in range(3)]
    smem = pl.BlockSpec(memory_space=pltpu.SMEM)
    return pl.pallas_call(
        functools.partial(_moba_prompt_kernel, nblk=nblk),
        grid=(pairs, nblk),
        in_specs=[smem, smem, smem, smem,
                  pl.BlockSpec((MOBA_BLOCK, LANES), lambda hp, i: (i, hp)),
                  pl.BlockSpec((2, nblk, MOBA_BLOCK, LANES), lambda hp, i: (hp, 0, 0, 0)),
                  pl.BlockSpec((nblk, 2 * MOBA_VT_ROWS, MOBA_BLOCK), lambda hp, i: (0, hp, 0)),
                  pl.BlockSpec((nblk, LANES), lambda hp, i: (0, hp))],
        out_specs=pl.BlockSpec((MOBA_BLOCK, LANES), lambda hp, i: (i, hp)),
        out_shape=jax.ShapeDtypeStruct((s, MOBA_INNER), BF16),
        scratch_shapes=[pltpu.VMEM((2, nblk, MOBA_BLOCK), F32),
                        pltpu.VMEM((MOBA_SLOTS, 2, MOBA_BLOCK, MOBA_BLOCK), F32)],
        compiler_params=_cparams(("arbitrary", "arbitrary")), name="moba_prompt",
    )(_alibi_slopes(), *tables, q, ka, vt, km)


PAGES_PER_STEP = 32
SAMPLE_AHEAD = 3
SAMPLE_BUFS = SAMPLE_AHEAD + 1


def _kmean_kernel(pt_ref, *refs):
    del pt_ref
    o_ref = refs[-1]
    s = pl.program_id(1)
    ppb = MOBA_BLOCK // PAGE_SIZE
    bps = PAGES_PER_STEP // ppb

    @pl.when(s == 0)
    def _():
        o_ref[...] = jnp.zeros(o_ref.shape, F32)

    acc = o_ref[0]
    lane = lax.broadcasted_iota(jnp.int32, acc.shape, 2)
    for m in range(bps):
        x = refs[m * ppb][0]
        for p in range(1, ppb):
            x = x + refs[m * ppb + p][0]
        mean = jnp.sum(x, axis=-1, keepdims=True) * (1.0 / MOBA_BLOCK)
        acc = jnp.where(lane == s * bps + m, mean, acc)
    o_ref[0] = acc


def _sample_kmean(k_pool_t, page_table):
    db, n_pages = page_table.shape
    ppb = MOBA_BLOCK // PAGE_SIZE
    n_full = n_pages // ppb
    assert n_full * ppb == n_pages and n_pages % PAGES_PER_STEP == 0
    steps = n_pages // PAGES_PER_STEP
    blk = (1, MOBA_HEADS, MOBA_HEAD_DIM, PAGE_SIZE)

    def page_spec(p):
        return pl.BlockSpec(blk, lambda b, s, pt: (pt[b * n_pages + s * PAGES_PER_STEP + p], 0, 0, 0))

    out_blk = (1, MOBA_HEADS, MOBA_HEAD_DIM, n_full)
    grid_spec = pltpu.PrefetchScalarGridSpec(
        num_scalar_prefetch=1, grid=(db, steps),
        in_specs=[page_spec(p) for p in range(PAGES_PER_STEP)],
        out_specs=pl.BlockSpec(out_blk, lambda b, s, pt: (b, 0, 0, 0)))
    return pl.pallas_call(
        _kmean_kernel, grid_spec=grid_spec,
        out_shape=jax.ShapeDtypeStruct((db,) + out_blk[1:], F32),
        compiler_params=_cparams(("arbitrary", "arbitrary")), name="sample_kmean",
    )(page_table.reshape(-1), *([k_pool_t] * PAGES_PER_STEP))


def _sample_select_kernel(q_ref, km_ref, idx_ref, *, n_full):
    t8 = q_ref.shape[2]
    blk = lax.broadcasted_iota(jnp.int32, (t8, n_full), 1).astype(F32)
    lane = lax.broadcasted_iota(jnp.int32, (t8, LANES), 1)
    for h in range(MOBA_HEADS):
        g = jnp.dot(q_ref[0, h], km_ref[0, h], precision=HIGHEST, preferred_element_type=F32)
        out = jnp.zeros((t8, LANES), jnp.int32)
        for k in range(MOBA_TOPK):
            mx = jnp.max(g, axis=-1, keepdims=True)
            first = jnp.min(jnp.where(g == mx, blk, float(n_full)), axis=-1, keepdims=True)
            out = jnp.where(lane == k, first.astype(jnp.int32), out)
            g = jnp.where(blk == first, -jnp.inf, g)
        idx_ref[0, h] = out


def _sample_select(qh, km):
    db, h, t8, d = qh.shape
    n_full = km.shape[3]
    assert n_full >= MOBA_TOPK
    b4 = lambda b: (b, 0, 0, 0)
    return pl.pallas_call(
        functools.partial(_sample_select_kernel, n_full=n_full), grid=(db,),
        in_specs=[pl.BlockSpec((1, h, t8, d), b4), pl.BlockSpec((1, h, d, n_full), b4)],
        out_specs=pl.BlockSpec((1, h, t8, LANES), b4),
        out_shape=jax.ShapeDtypeStruct((db, h, t8, LANES), jnp.int32),
        compiler_params=_cparams(("arbitrary",)), name="sample_select",
    )(qh, km)


def _sample_attn_kernel(idx_ref, phys_ref, slopes_ref, q_ref, kn_ref, vn_ref, kpool, vpool, o_ref,
                        kbuf, vbuf, sem, *, ds, past):
    ppb = MOBA_BLOCK // PAGE_SIZE
    n_sel = MOBA_TOPK * ppb
    n_pg = ds * n_sel
    b = pl.program_id(0)
    h = pl.program_id(1)
    n_heads = pl.num_programs(1)
    step = b * n_heads + h
    n_steps = pl.num_programs(0) * n_heads

    def page_copies(st, half):
        hh = st % n_heads
        out = []
        for k in range(n_pg):
            page = phys_ref[st * n_pg + k]
            out.append(pltpu.make_async_copy(kpool.at[page, hh], kbuf.at[half, k], sem.at[0, half]))
            out.append(pltpu.make_async_copy(vpool.at[page, hh], vbuf.at[half, k], sem.at[1, half]))
        return out

    @pl.when(step == 0)
    def _():
        for st in range(SAMPLE_AHEAD):
            for cp in page_copies(st, st):
                cp.start()

    @pl.when(step + SAMPLE_AHEAD < n_steps)
    def _():
        for cp in page_copies(step + SAMPLE_AHEAD, (step + SAMPLE_AHEAD) % SAMPLE_BUFS):
            cp.start()

    half = step % SAMPLE_BUFS
    for cp in page_copies(step, half):
        cp.wait()
    k_refs = [kbuf.at[half, k] for k in range(n_pg)]
    v_refs = [vbuf.at[half, k] for k in range(n_pg)]
    slope = slopes_ref[h]
    t8 = q_ref.shape[3]
    c_page = lax.broadcasted_iota(jnp.int32, (1, PAGE_SIZE), 1).astype(F32)
    c_new = lax.broadcasted_iota(jnp.int32, (1, t8), 1)
    qf = q_ref[0, 0] * (MOBA_HEAD_DIM ** -0.5)
    k_new = kn_ref[0, 0]
    v_new = vn_ref[0, 0]
    o_ref[0, 0] = jnp.zeros((MOBA_HEAD_DIM, t8), F32)
    for t in range(ds):
        q_t = qf[:, t:t + 1]
        t_pos = float(past + t)
        scores = []
        for sp in range(n_sel):
            slot, p = divmod(sp, ppb)
            blk_idx = idx_ref[((b * MOBA_HEADS + h) * ds + t) * MOBA_TOPK + slot]
            pos0 = (blk_idx * MOBA_BLOCK + p * PAGE_SIZE).astype(F32)
            s = jnp.sum(k_refs[t * n_sel + sp][...] * q_t, axis=0, keepdims=True)
            scores.append(s - slope * (t_pos - (pos0 + c_page)))
        s_new = jnp.sum(k_new * q_t, axis=0, keepdims=True)
        s_new = s_new - slope * (t_pos - (float(past) + c_new.astype(F32)))
        s_new = jnp.where(c_new <= t, s_new, -jnp.inf)
        mx = jnp.max(s_new, axis=-1, keepdims=True)
        for s in scores:
            mx = jnp.maximum(mx, jnp.max(s, axis=-1, keepdims=True))
        p_new = jnp.exp(s_new - mx)
        l = jnp.sum(p_new, axis=-1, keepdims=True)
        o = jnp.sum(p_new * v_new, axis=-1, keepdims=True)
        acc = jnp.zeros((MOBA_HEAD_DIM, PAGE_SIZE), F32)
        for sp in range(n_sel):
            pr = jnp.exp(scores[sp] - mx)
            l = l + jnp.sum(pr, axis=-1, keepdims=True)
            acc = acc + pr * v_refs[t * n_sel + sp][...]
        o = o + jnp.sum(acc, axis=-1, keepdims=True)
        o_ref[0, 0, :, t:t + 1] = o / l


def _sample_attn(qh, kh, vh, idx, k_pool, v_pool, page_table, ds):
    db, h, d, t8 = qh.shape
    n_pages = page_table.shape[1]
    ppb = MOBA_BLOCK // PAGE_SIZE
    assert n_pages % ppb == 0
    past = n_pages * PAGE_SIZE
    n_sel = MOBA_TOPK * ppb
    logical = idx[..., None] * ppb + jnp.arange(ppb, dtype=jnp.int32)
    phys = page_table[jnp.arange(db)[:, None, None, None], logical.reshape(db, h, ds, n_sel)]
    tok = pl.BlockSpec((1, 1, d, t8), lambda b, hh, *_: (b, hh, 0, 0))
    hbm = pl.BlockSpec(memory_space=pl.ANY)
    assert db * h >= SAMPLE_AHEAD
    page_buf = pltpu.VMEM((SAMPLE_BUFS, ds * n_sel, d, PAGE_SIZE), F32)
    grid_spec = pltpu.PrefetchScalarGridSpec(
        num_scalar_prefetch=2, grid=(db, h),
        in_specs=[pl.BlockSpec(memory_space=pltpu.SMEM), tok, tok, tok, hbm, hbm],
        out_specs=tok,
        scratch_shapes=[page_buf, page_buf, pltpu.SemaphoreType.DMA((2, SAMPLE_BUFS))])
    return pl.pallas_call(
        functools.partial(_sample_attn_kernel, ds=ds, past=past), grid_spec=grid_spec,
        out_shape=jax.ShapeDtypeStruct((db, h, d, t8), F32),
        compiler_params=_cparams(("arbitrary", "arbitrary")), name="sample_attn",
    )(idx.reshape(-1), phys.reshape(-1), _alibi_slopes(), qh, kh, vh, k_pool, v_pool)


def _split_w_in_a(w, f32_query):
    wb = w.astype(BF16)
    wdt = jnp.zeros((D_MODEL, LANES), BF16).at[:, :SSD_HEADS].set(wb[:, OFF_DT:OFF_Q])
    wq = w[:, OFF_Q:OFF_K] if f32_query else wb[:, OFF_Q:OFF_K]
    return [wb[:, :OFF_XBC], wb[:, OFF_XBC:OFF_DT], wdt, wq, wb[:, OFF_K:OFF_V], wb[:, OFF_V:]]


def _pad_seq(a, b, l, lp):
    a = a.reshape(b, l, a.shape[-1])
    return a if lp == l else jnp.pad(a, ((0, 0), (0, lp - l), (0, 0)))


def _round_up(n, m):
    return -(-n // m) * m


def _heads(a, b, l):
    return a.reshape(b, l, MOBA_HEADS, MOBA_HEAD_DIM).transpose(0, 2, 1, 3)


def _pad_tokens(a):
    l = a.shape[-2]
    return jnp.pad(a, ((0, 0),) * (a.ndim - 2) + ((0, _round_up(l, SUBLANES) - l), (0, 0)))


def _trunk(x, bsz, length, prompt, caches, p):
    ssm_conv0, ssm0, hgrn0, ffn_conv0 = caches[:4]
    m = bsz * length
    k_rows = v_rows = ssm_c = ssm_s = hgrn_s = None
    ffn_c = []
    for layer in range(DEPTH):
        nw = p['norm_mix'][layer].reshape(1, -1)
        if layer % 2 == 0:
            ia = layer // 2
            assert ia == 0
            ws = _split_w_in_a(p['w_in_a'][ia], f32_query=not prompt)
            outs = _inproj_a(x, nw, ws, prompt)
            z, xbc, dtr, q = outs[:4]
            lp = _round_up(length, _chunk_len(length, SSD_CHUNK))
            y_ssd, hist, s_new = _ssd(
                _pad_seq(z, bsz, length, lp), _pad_seq(xbc, bsz, length, lp), _pad_seq(dtr, bsz, length, lp),
                ssm_conv0[ia], ssm0[ia], p['ssd_conv_w'][ia], p['ssd_conv_b'][ia], p['ssd_dt_bias'][ia],
                p['ssd_a_log'][ia], p['ssd_d'][ia], p['ssd_norm_w'][ia], length)
            y_ssd = y_ssd[:, :length].reshape(m, SSD_INNER)
            if prompt:
                assert bsz == 1
                kr, vr, ka, vt, km, bounds = outs[4:]
                o_att = _moba_prompt(q, ka, vt, km.reshape(-1, MOBA_INNER), bounds)
                k_rows, v_rows = jnp.swapaxes(kr, -1, -2)[None], jnp.swapaxes(vr, -1, -2)[None]
            else:
                k, v = outs[4:]
                k_pool, v_pool, page_table = caches[4:]
                k_pool_t = jnp.swapaxes(k_pool[ia], -1, -2)
                v_pool_t = jnp.swapaxes(v_pool[ia], -1, -2)
                qh, kh, vh = (_pad_tokens(_heads(t, bsz, length)) for t in (q, k, v))
                qt, kt, vt = (jnp.swapaxes(t, -1, -2) for t in (qh, kh, vh))
                km = _sample_kmean(k_pool_t, page_table)
                idx = _sample_select(qh, km)[:, :, :length, :MOBA_TOPK]
                o = _sample_attn(qt, kt, vt, idx, k_pool_t, v_pool_t, page_table, length)
                o_att = o[..., :length].transpose(0, 3, 1, 2).reshape(m, MOBA_INNER).astype(BF16)
                k_rows, v_rows = kh[:, :, :length], vh[:, :, :length]
            wo = p['w_out_a'][ia].astype(BF16)
            mix_acts, mix_ws = [y_ssd, o_att], [wo[:SSD_INNER], wo[SSD_INNER:]]
            ssm_c, ssm_s = hist, s_new
        else:
            ic = layer // 2
            assert ic == 0
            wc = p['w_in_c'][ic].astype(BF16)
            ws = [wc[:, :HGRN_K], wc[:, HGRN_K:2 * HGRN_K], wc[:, 2 * HGRN_K:2 * HGRN_K + HGRN_V],
                  wc[:, 2 * HGRN_K + HGRN_V:]]
            qa, fx, iv, g = _norm_matmul(x, nw, ws)
            lp = _round_up(length, _chunk_len(length, HGRN_CHUNK))
            o, s_new = _gla(*(_pad_seq(t, bsz, length, lp) for t in (qa, fx, iv, g)),
                            p['hgrn_lb_gamma'], p['hgrn_norm_w'][ic], hgrn0[ic], length, layer)
            mix_acts, mix_ws = [o[:, :length].reshape(m, HGRN_V)], [p['w_out_c'][ic].astype(BF16)]
            hgrn_s = s_new
        nf = p['norm_final'].reshape(1, -1) if layer == DEPTH - 1 else None
        hist = ffn_conv0[layer, 0] if prompt else ffn_conv0[layer]
        x, fh = _ffn(x, mix_acts, mix_ws, p['norm_ffn'][layer].reshape(1, -1),
                     p['ffn_w_up'].astype(BF16), p['ffn_conv_w'][layer],
                     p['ffn_conv_b'][layer].reshape(1, -1), p['ffn_w_down'].astype(BF16), layer, hist,
                     None if prompt else length, nf)
        ffn_c.append(fh[None] if prompt else fh)
    return (x.reshape(bsz, length, D_MODEL), k_rows[None], v_rows[None], ssm_s[None], ssm_c[None],
            hgrn_s[None], jnp.stack(ffn_c))


def kernel(x_prompt, x_sample, cache_k_pool, cache_v_pool, page_table, state_ssm, state_ssm_conv, state_hgrn,
           state_ffn_conv, norm_mix, norm_ffn, norm_final, w_in_a, w_out_a, ssd_conv_w, ssd_conv_b,
           ssd_dt_bias, ssd_a_log, ssd_d, ssd_norm_w, w_in_c, w_out_c, hgrn_lb_gamma, hgrn_norm_w, ffn_w_up,
           ffn_conv_w, ffn_conv_b, ffn_w_down):
    p = dict(norm_mix=norm_mix, norm_ffn=norm_ffn, norm_final=norm_final, w_in_a=w_in_a, w_out_a=w_out_a,
             ssd_conv_w=ssd_conv_w, ssd_conv_b=ssd_conv_b, ssd_dt_bias=ssd_dt_bias, ssd_a_log=ssd_a_log,
             ssd_d=ssd_d, ssd_norm_w=ssd_norm_w, w_in_c=w_in_c, w_out_c=w_out_c, hgrn_lb_gamma=hgrn_lb_gamma,
             hgrn_norm_w=hgrn_norm_w, ffn_w_up=ffn_w_up, ffn_conv_w=ffn_conv_w, ffn_conv_b=ffn_conv_b,
             ffn_w_down=ffn_w_down)
    bp, sp, _ = x_prompt.shape
    db, ds, _ = x_sample.shape
    na, nc = (DEPTH + 1) // 2, DEPTH // 2
    zeros = lambda *s: jnp.zeros(s, F32)
    prompt_caches = (zeros(na, bp, SSD_CONV - 1, SSD_CONV_DIM), zeros(na, bp, SSD_HEADS, SSD_HEAD_DIM, SSD_STATE),
                     zeros(nc, bp, HGRN_HEADS, HGRN_KEY_DIM, HGRN_VAL_DIM), zeros(DEPTH, bp, FFN_CONV - 1, FFN_DIM))
    outs_p = _trunk(x_prompt.reshape(bp * sp, D_MODEL), bp, sp, True, prompt_caches, p)
    sample_caches = (state_ssm_conv, state_ssm, state_hgrn, state_ffn_conv, cache_k_pool, cache_v_pool, page_table)
    outs_s = _trunk(x_sample.reshape(db * ds, D_MODEL), db, ds, False, sample_caches, p)
    yp, krp, vrp, ssp, scp, hgp, fcp = outs_p
    ys, krs, vrs, sss, scs, hgs, fcs = outs_s
    return (yp, ys, krp, vrp, ssp, scp, hgp, fcp, krs, vrs, sss, scs, hgs, fcs)
```

```python
import functools
import math

import numpy as np
import jax
import jax.numpy as jnp
from jax import lax
from jax.experimental import pallas as pl
from jax.experimental.pallas import tpu as pltpu

F32 = jnp.float32
BF16 = jnp.bfloat16
HIGHEST = lax.Precision.HIGHEST

D_MODEL = 1024
DEPTH = 2
PAGE_SIZE = 128
SSD_HEADS = 8
SSD_HEAD_DIM = 64
SSD_INNER = SSD_HEADS * SSD_HEAD_DIM
SSD_STATE = 64
SSD_GROUPS = 2
SSD_CONV = 4
SSD_CHUNK = 128
SSD_GN = SSD_GROUPS * SSD_STATE
SSD_CONV_DIM = SSD_INNER + 2 * SSD_GN
MOBA_HEADS = 8
MOBA_HEAD_DIM = 64
MOBA_INNER = MOBA_HEADS * MOBA_HEAD_DIM
MOBA_BLOCK = 256
MOBA_TOPK = 3
OFF_XBC = SSD_INNER
OFF_DT = OFF_XBC + SSD_CONV_DIM
OFF_Q = OFF_DT + SSD_HEADS
OFF_K = OFF_Q + MOBA_INNER
OFF_V = OFF_K + MOBA_INNER
HGRN_HEADS = 8
HGRN_KEY_DIM = 128
HGRN_VAL_DIM = D_MODEL // HGRN_HEADS
HGRN_K = HGRN_HEADS * HGRN_KEY_DIM
HGRN_V = HGRN_HEADS * HGRN_VAL_DIM
HGRN_CHUNK = 64
FFN_DIM = 2816
FFN_CONV = 3
RMS_EPS = 1e-6

LANES = 128
SUBLANES = 8
VMEM_LIMIT_BYTES = 56 * 1024 * 1024

ROW_TILE = 256
FFN_COL_CHUNK = 256
NEG_BIG = -1e30
LOG2E = math.log2(math.e)
ALIBI_PARTS = 3
MOBA_VT_ROWS = MOBA_HEAD_DIM + 16
BOUND_SLACK = 0.02
UNDERFLOW_LOG2 = 160.0
MOBA_AHEAD = 2
MOBA_SLOTS = MOBA_AHEAD + 1
MOBA_UNROLL = 4 * MOBA_SLOTS


def _cparams(sem, flags=None):
    return pltpu.CompilerParams(dimension_semantics=sem, vmem_limit_bytes=VMEM_LIMIT_BYTES, flags=flags)


def _const_spec(shape):
    nd = len(shape)
    return pl.BlockSpec(shape, lambda *_: (0,) * nd)


def _rms(x, w):
    y = x * lax.rsqrt(jnp.mean(x * x, axis=-1, keepdims=True) + RMS_EPS)
    return y * w


def _rms_bf16(x, w):
    return _rms(x, w).astype(BF16)


def _dot(a, b):
    return jnp.dot(a, b, preferred_element_type=F32)


def _dot_split(a, b):
    a_hi = a.astype(BF16)
    a_lo = (a - a_hi.astype(F32)).astype(BF16)
    b_hi = b.astype(BF16)
    b_lo = (b - b_hi.astype(F32)).astype(BF16)
    return _dot(jnp.concatenate([a_hi, a_lo, a_hi], axis=1), jnp.concatenate([b_hi, b_hi, b_lo], axis=0))


def _dot_nt(a, b):
    return lax.dot_general(a, b, (((1,), (1,)), ((), ())), preferred_element_type=F32)


def _dot_tn(a, b):
    return lax.dot_general(a, b, (((0,), (0,)), ((), ())), preferred_element_type=F32)


def _tril_f32(n):
    r = lax.broadcasted_iota(jnp.int32, (n, n), 0)
    c = lax.broadcasted_iota(jnp.int32, (n, n), 1)
    return r >= c


def _cumsum_rows(x, q, split):
    if not split:
        return jnp.dot(_tril_f32(q).astype(F32), x, precision=HIGHEST, preferred_element_type=F32)
    tri = _tril_f32(q).astype(BF16)
    acc = None
    rest = x
    for _ in range(3):
        term = rest.astype(BF16)
        part = _dot(tri, term)
        acc = part if acc is None else acc + part
        rest = rest - term.astype(F32)
    return acc


def _silu(x):
    return x * jax.nn.sigmoid(x)


def _inproj_a_kernel(x_ref, nw_ref, wz, wxbc, wdt, wq, wk, wv, *rest, prompt):
    outs = rest[1:] if prompt else rest
    h = _rms(x_ref[...], nw_ref[...])
    hb = h.astype(BF16)
    z_o, xbc_o, dt_o, q_o = outs[:4]
    z_o[...] = _dot(hb, wz[...])
    xbc_o[...] = _dot(hb, wxbc[...])
    dt_o[...] = _dot(hb, wdt[...])
    if wq.dtype == F32:
        q = jnp.dot(h, wq[...], precision=HIGHEST, preferred_element_type=F32)
    else:
        q = _dot(hb, wq[...])
    q_o[...] = q
    k = _dot(hb, wk[...])
    v = _dot(hb, wv[...])
    if not prompt:
        k_o, v_o = outs[4:]
        k_o[...] = k
        v_o[...] = v
        return
    kr_o, vr_o, ka_o, vt_o, km_o, bnd_o = outs[4:]
    tm = k.shape[0]
    head_sel = rest[0][...]
    k_norm = jnp.sqrt(jnp.max(_dot((k * k).astype(BF16), head_sel), axis=0, keepdims=True))
    q_norm = jnp.sqrt(jnp.max(_dot((q * q).astype(BF16), head_sel), axis=0, keepdims=True))
    qk_min = jnp.min(_dot((q * k).astype(BF16), head_sel), axis=0, keepdims=True)
    row8 = lax.broadcasted_iota(jnp.int32, (SUBLANES, LANES), 0)
    bnd_o[0] = jnp.where(row8 == 0, k_norm, jnp.where(row8 == 1, q_norm, jnp.where(row8 == 2, qk_min, 0.0)))
    lane = lax.broadcasted_iota(jnp.int32, (tm, LANES), 1)
    key_aug = _key_aug_columns(tm)
    for hp in range(MOBA_HEADS // 2):
        pair = k[:, hp * LANES:(hp + 1) * LANES]
        for j, kh in enumerate((pair, pltpu.roll(pair, MOBA_HEAD_DIM, 1))):
            ka_o[2 * hp + j, 0] = jnp.where(lane < MOBA_HEAD_DIM, kh, key_aug).astype(BF16)
    k_t = k.T
    v_t = v.T
    for p in range(tm // PAGE_SIZE):
        toks = slice(p * PAGE_SIZE, (p + 1) * PAGE_SIZE)
        for h in range(MOBA_HEADS):
            dims = slice(h * MOBA_HEAD_DIM, (h + 1) * MOBA_HEAD_DIM)
            kr_o[p, h] = k_t[dims, toks]
            vr_o[p, h] = v_t[dims, toks]
    ones_rows = (lax.broadcasted_iota(jnp.int32, (MOBA_VT_ROWS - MOBA_HEAD_DIM, tm), 0) == 0).astype(F32)
    pieces = []
    for h in range(MOBA_HEADS):
        pieces += [v_t[h * MOBA_HEAD_DIM:(h + 1) * MOBA_HEAD_DIM, :], ones_rows]
    vt_o[0] = jnp.concatenate(pieces, axis=0).astype(BF16)
    km_o[0] = jnp.mean(k, axis=0, keepdims=True)


def _inproj_a(x, nw, ws, prompt):
    m = x.shape[0]
    tm = ROW_TILE if prompt else m
    assert m % tm == 0
    if prompt:
        assert tm == MOBA_BLOCK
    nt = m // tm
    widths = [w.shape[1] for w in ws]
    row = lambda i: (i, 0)
    in_specs = [pl.BlockSpec((tm, D_MODEL), row), _const_spec((1, D_MODEL))]
    in_specs += [_const_spec(w.shape) for w in ws]
    out_shape = [jax.ShapeDtypeStruct((m, n), F32) for n in widths[:4]]
    out_specs = [pl.BlockSpec((tm, n), row) for n in widths[:4]]
    if prompt:
        npg = m // PAGE_SIZE
        ppt = tm // PAGE_SIZE
        rows_shape = (npg, MOBA_HEADS, MOBA_HEAD_DIM, PAGE_SIZE)
        rows_spec = pl.BlockSpec((ppt, MOBA_HEADS, MOBA_HEAD_DIM, PAGE_SIZE), lambda i: (i, 0, 0, 0))
        out_shape += [jax.ShapeDtypeStruct(rows_shape, F32)] * 2
        out_specs += [rows_spec, rows_spec]
        out_shape += [jax.ShapeDtypeStruct((MOBA_HEADS, nt, tm, LANES), BF16),
                      jax.ShapeDtypeStruct((nt, MOBA_HEADS * MOBA_VT_ROWS, tm), BF16),
                      jax.ShapeDtypeStruct((nt, 1, MOBA_INNER), F32),
                      jax.ShapeDtypeStruct((nt, SUBLANES, LANES), F32)]
        out_specs += [pl.BlockSpec((MOBA_HEADS, 1, tm, LANES), lambda i: (0, i, 0, 0)),
                      pl.BlockSpec((1, MOBA_HEADS * MOBA_VT_ROWS, tm), lambda i: (i, 0, 0)),
                      pl.BlockSpec((1, 1, MOBA_INNER), lambda i: (i, 0, 0)),
                      pl.BlockSpec((1, SUBLANES, LANES), lambda i: (i, 0, 0))]
        head_sel = (jnp.arange(MOBA_INNER)[:, None] // MOBA_HEAD_DIM == jnp.arange(LANES)[None, :]).astype(BF16)
        extra, extra_specs = [head_sel], [_const_spec(head_sel.shape)]
    else:
        out_shape += [jax.ShapeDtypeStruct((m, MOBA_INNER), F32)] * 2
        out_specs += [pl.BlockSpec((tm, MOBA_INNER), row)] * 2
        extra, extra_specs = [], []
    return pl.pallas_call(
        functools.partial(_inproj_a_kernel, prompt=prompt),
        grid=(nt,), in_specs=in_specs + extra_specs, out_specs=out_specs, out_shape=out_shape,
        compiler_params=_cparams(("arbitrary",)), name="inproj_a",
    )(x, nw, *ws, *extra)


def _norm_matmul_kernel(x_ref, nw_ref, *refs):
    n = len(refs) // 2
    hb = _rms_bf16(x_ref[...], nw_ref[...])
    for w, o in zip(refs[:n], refs[n:]):
        o[...] = _dot(hb, w[...])


def _norm_matmul(x, nw, ws):
    m = x.shape[0]
    tm = min(ROW_TILE, m)
    assert m % tm == 0
    row = lambda i: (i, 0)
    in_specs = [pl.BlockSpec((tm, D_MODEL), row), _const_spec((1, D_MODEL))]
    in_specs += [_const_spec(w.shape) for w in ws]
    return pl.pallas_call(
        _norm_matmul_kernel, grid=(m // tm,), in_specs=in_specs,
        out_specs=[pl.BlockSpec((tm, w.shape[1]), row) for w in ws],
        out_shape=[jax.ShapeDtypeStruct((m, w.shape[1]), F32) for w in ws],
        compiler_params=_cparams(("arbitrary",)), name="norm_matmul",
    )(x, nw, *ws)


def _ffn_kernel(*refs, tm, seq_len, final_norm, n_mix):
    it = iter(refs)
    x_ref = next(it)
    mix_refs = [next(it) for _ in range(2 * n_mix)]
    nw_ref, wup_ref, cw_ref, cb_ref, wdn_ref = (next(it) for _ in range(5))
    if seq_len is None:
        hist_ref = next(it)
    else:
        h1_ref, h2_ref = next(it), next(it)
    nf_ref = next(it) if final_norm else None
    o_ref = next(it)
    tail_ref = next(it)
    up_s, act_s = next(it), next(it)

    i = pl.program_id(0)
    f = FFN_DIM
    if seq_len is None:
        @pl.when(i == 0)
        def _():
            up_s[0:SUBLANES, 0:f] = hist_ref[...]
    else:
        up_s[0:SUBLANES, 0:f] = jnp.zeros((SUBLANES, f), F32)

    x = x_ref[...]
    for a_ref, w_ref in zip(mix_refs[:n_mix], mix_refs[n_mix:]):
        x = x + _dot(a_ref[...], w_ref[...])
    hb = _rms_bf16(x, nw_ref[...])
    up_s[SUBLANES:SUBLANES + tm, :] = _dot(hb, wup_ref[0])

    if seq_len is not None:
        t = lax.broadcasted_iota(jnp.int32, (tm, 1), 0) % seq_len
    for c in range(0, f, FFN_COL_CHUNK):
        cols = slice(c, c + FFN_COL_CHUNK)
        g0 = up_s[SUBLANES:SUBLANES + tm, cols]
        g1 = up_s[SUBLANES - 1:SUBLANES - 1 + tm, cols]
        g2 = up_s[SUBLANES - 2:SUBLANES - 2 + tm, cols]
        if seq_len is not None:
            g1 = jnp.where(t >= 1, g1, h1_ref[:, cols])
            g2 = jnp.where(t >= 2, g2, h2_ref[:, cols])
        val = up_s[SUBLANES:SUBLANES + tm, f + c:f + c + FFN_COL_CHUNK]
        conv = g2 * cw_ref[0:1, cols]
        conv = conv + g1 * cw_ref[1:2, cols]
        conv = conv + g0 * cw_ref[2:3, cols]
        conv = conv + cb_ref[:, cols]
        act_s[:, cols] = (_silu(conv) * val).astype(BF16)

    out = x + _dot(act_s[...], wdn_ref[0])
    o_ref[...] = _rms(out, nf_ref[...]) if final_norm else out

    if seq_len is None:
        last = up_s[tm:tm + SUBLANES, 0:f]
        up_s[0:SUBLANES, 0:f] = last

        @pl.when(i == pl.num_programs(0) - 1)
        def _():
            tail_ref[...] = last
    else:
        tail_ref[...] = up_s[SUBLANES:SUBLANES + tm, 0:f]


def _ffn(x, mix_acts, mix_ws, nw, wup, cw, cb, wdn, layer, hist, seq_len, nf):
    assert FFN_CONV == 3
    m = x.shape[0]
    f = FFN_DIM
    final_norm = nf is not None
    row = lambda i: (i, 0)
    if seq_len is None:
        tm = ROW_TILE
        hist8 = jnp.zeros((SUBLANES, f), F32).at[SUBLANES - 2:].set(hist)
        extra = [hist8]
        extra_specs = [_const_spec((SUBLANES, f))]
        tail_shape = (SUBLANES, f)
    else:
        tm = m
        nb = m // seq_len
        assert seq_len >= 2 and nb * seq_len == m
        h1 = jnp.zeros((nb, seq_len, f), F32).at[:, 0].set(hist[:, 1]).reshape(m, f)
        h2 = jnp.zeros((nb, seq_len, f), F32).at[:, 0:2].set(hist).reshape(m, f)
        extra = [h1, h2]
        extra_specs = [_const_spec((m, f))] * 2
        tail_shape = (m, f)
    assert m % tm == 0
    ins = [x] + list(mix_acts) + list(mix_ws) + [nw, wup, cw, cb, wdn] + extra
    in_specs = [pl.BlockSpec((tm, D_MODEL), row)]
    in_specs += [pl.BlockSpec((tm, a.shape[1]), row) for a in mix_acts]
    in_specs += [_const_spec(w.shape) for w in mix_ws]
    layer_slab = lambda w: pl.BlockSpec((1,) + w.shape[1:], lambda i: (layer, 0, 0))
    in_specs += [_const_spec((1, D_MODEL)), layer_slab(wup), _const_spec(cw.shape),
                 _const_spec(cb.shape), layer_slab(wdn)] + extra_specs
    if final_norm:
        ins.append(nf)
        in_specs.append(_const_spec((1, D_MODEL)))
    xo, tail = pl.pallas_call(
        functools.partial(_ffn_kernel, tm=tm, seq_len=seq_len, final_norm=final_norm, n_mix=len(mix_acts)),
        grid=(m // tm,), in_specs=in_specs,
        out_specs=[pl.BlockSpec((tm, D_MODEL), row), _const_spec(tail_shape)],
        out_shape=[jax.ShapeDtypeStruct((m, D_MODEL), F32), jax.ShapeDtypeStruct(tail_shape, F32)],
        scratch_shapes=[pltpu.VMEM((SUBLANES + tm, 2 * f), F32), pltpu.VMEM((tm, f), BF16)],
        compiler_params=_cparams(("arbitrary",)), name="conv_ffn",
    )(*ins)
    if seq_len is None:
        new_hist = tail[SUBLANES - 2:]
    else:
        new_hist = tail.reshape(m // seq_len, seq_len, f)[:, seq_len - 2:]
    return xo, new_hist


def _ssd_kernel(z_ref, xbc_ref, dt_ref, hist_ref, s0_ref, cw_ref, cb_ref, dtb_ref, alog_ref, dsk_ref,
                nw_ref, y_ref, hist_o, s_o, cbuf, st, ybuf, m1_s, xdt_s, ce_s, xw_s, bm_s, xsd_s, zs_s, el_s,
                *, q, seq_len, n_chunks, pipelined):
    c = pl.program_id(1)
    stage = (m1_s, xdt_s, ce_s, xw_s, bm_s, xsd_s, zs_s, el_s)

    def state_stage(rd):
        _ssd_state_stage(rd, nw_ref, y_ref, st, ybuf, *stage)

    def free_stage(wr):
        _ssd_free_stage(c, wr, z_ref, xbc_ref, dt_ref, cw_ref, cb_ref, dtb_ref, alog_ref, dsk_ref, hist_o, cbuf,
                        *stage, q=q, seq_len=seq_len, n_chunks=n_chunks)

    @pl.when(c == 0)
    def _():
        cbuf[0:SUBLANES, :] = hist_ref[0]
        st[...] = s0_ref[0]
        if pipelined:
            for ref in stage[:-1]:
                ref[1] = jnp.zeros(ref.shape[1:], ref.dtype)
            el_s[1] = jnp.ones(el_s.shape[1:], F32)

    if pipelined:
        @pl.when(c % 2 == 0)
        def _():
            state_stage(1)
            free_stage(0)

        @pl.when(c % 2 == 1)
        def _():
            state_stage(0)
            free_stage(1)
    else:
        free_stage(0)
        state_stage(0)

    @pl.when(c == (n_chunks if pipelined else n_chunks - 1))
    def _():
        s_o[0] = st[...]


def _ssd_state_stage(rd, nw_ref, y_ref, st, ybuf, m1_s, xdt_s, ce_s, xw_s, bm_s, xsd_s, zs_s, el_s):
    rep = SSD_HEADS // SSD_GROUPS
    hd = SSD_HEAD_DIM
    e_last = el_s[rd, 0:1, :]
    y_state = [_dot_nt(ce_s[rd, h], st[h].astype(BF16)) for h in range(SSD_HEADS)]
    s_inc = [_dot_tn(xw_s[rd, h], bm_s[rd, h // rep]) for h in range(SSD_HEADS)]
    y_intra = [_dot(m1_s[rd, h], xdt_s[rd, h]) for h in range(SSD_HEADS)]
    for h in range(SSD_HEADS):
        st[h] = st[h] * e_last[:, h:h + 1] + s_inc[h]
        ybuf[:, h * hd:(h + 1) * hd] = y_intra[h] + y_state[h]
    yz = (ybuf[...] + xsd_s[rd]) * zs_s[rd]
    gw = SSD_INNER // SSD_GROUPS
    for g in range(SSD_GROUPS):
        yg = yz[:, g * gw:(g + 1) * gw]
        yn = yg * lax.rsqrt(jnp.mean(yg * yg, axis=-1, keepdims=True) + RMS_EPS)
        y_ref[0, :, g * gw:(g + 1) * gw] = (yn * nw_ref[:, g * gw:(g + 1) * gw]).astype(BF16)


def _ssd_free_stage(c, wr, z_ref, xbc_ref, dt_ref, cw_ref, cb_ref, dtb_ref, alog_ref, dsk_ref, hist_o, cbuf,
                    m1_s, xdt_s, ce_s, xw_s, bm_s, xsd_s, zs_s, el_s, *, q, seq_len, n_chunks):
    rep = SSD_HEADS // SSD_GROUPS
    hd = SSD_HEAD_DIM
    hrows = SSD_CONV - 1
    h0 = SUBLANES - hrows
    cbuf[SUBLANES:SUBLANES + q, :] = xbc_ref[0]
    conv = cbuf[h0:h0 + q, :] * cw_ref[0:1, :]
    for j in range(1, SSD_CONV):
        conv = conv + cbuf[h0 + j:h0 + j + q, :] * cw_ref[j:j + 1, :]
    xc = _silu(conv + cb_ref[...])

    xdt_raw = dt_ref[0] + dtb_ref[...]
    dt = jnp.maximum(xdt_raw, 0.0) + jnp.log1p(jnp.exp(-jnp.abs(xdt_raw)))
    if seq_len % q != 0:
        row = jnp.minimum(c, n_chunks - 1) * q + lax.broadcasted_iota(jnp.int32, (q, 1), 0)
        dt = jnp.where(row < seq_len, dt, 0.0)
    a = -jnp.exp(alog_ref[...])
    acum = _cumsum_rows(dt * a, q, split=False)
    acum_t = acum.T
    a_last = acum[q - 1:q, :]
    e_acum = jnp.exp(acum)
    w_in = jnp.exp(a_last - acum) * dt
    el_s[wr] = jnp.broadcast_to(jnp.exp(a_last), (SUBLANES, LANES))
    tril = _tril_f32(q)

    cbs, cms = [], []
    for g in range(SSD_GROUPS):
        bm = xc[:, SSD_INNER + g * SSD_STATE:SSD_INNER + (g + 1) * SSD_STATE].astype(BF16)
        cm = xc[:, SSD_INNER + SSD_GN + g * SSD_STATE:SSD_INNER + SSD_GN + (g + 1) * SSD_STATE]
        bm_s[wr, g] = bm
        cms.append(cm)
        cbs.append(_dot_nt(cm.astype(BF16), bm))

    for h in range(SSD_HEADS):
        g = h // rep
        xs = xc[:, h * hd:(h + 1) * hd]
        seg = acum[:, h:h + 1] - acum_t[h:h + 1, :]
        decay = jnp.exp(jnp.where(tril, seg, -jnp.inf))
        m1_s[wr, h] = (cbs[g] * decay).astype(BF16)
        xdt_s[wr, h] = (xs * dt[:, h:h + 1]).astype(BF16)
        ce_s[wr, h] = (cms[g] * e_acum[:, h:h + 1]).astype(BF16)
        xw_s[wr, h] = (xs * w_in[:, h:h + 1]).astype(BF16)
        xsd_s[wr, :, h * hd:(h + 1) * hd] = xs * dsk_ref[:, h:h + 1]
    zs_s[wr] = _silu(z_ref[0])

    @pl.when(c == n_chunks - 1)
    def _():
        l_last = seq_len - (n_chunks - 1) * q
        hist_o[0] = cbuf[l_last:l_last + SUBLANES, :]

    cbuf[0:SUBLANES, :] = cbuf[q:q + SUBLANES, :]


BF16_SUBLANES = 16


def _chunk_len(seq_len, max_chunk):
    return min(max_chunk, _round_up(seq_len, BF16_SUBLANES))


def _stage_grid(n_chunks, pipelined):
    if not pipelined:
        same = lambda i, c: (i, c, 0)
        return n_chunks, same, same
    return (n_chunks + 1, lambda i, c: (i, jnp.minimum(c, n_chunks - 1), 0),
            lambda i, c: (i, jnp.maximum(c - 1, 0), 0))


def _pad_lanes(v, fill=0.0):
    return jnp.full((1, LANES), fill, F32).at[0, :v.shape[0]].set(v.astype(F32))


def _ssd(z, xbc, dtr, hist, s0, conv_w, conv_b, dt_bias, a_log, d_skip, norm_w, seq_len):
    b, lp, _ = z.shape
    q = _chunk_len(seq_len, SSD_CHUNK)
    assert lp % q == 0 and lp - seq_len < q
    hrows = SSD_CONV - 1
    hist8 = jnp.zeros((b, SUBLANES, SSD_CONV_DIM), F32).at[:, SUBLANES - hrows:].set(hist)
    cw8 = jnp.zeros((SUBLANES, SSD_CONV_DIM), F32).at[:SSD_CONV].set(conv_w)
    n_chunks = lp // q
    pipelined = False
    n_steps, seq, seq_out = _stage_grid(n_chunks, pipelined)
    per_b3 = lambda i, c: (i, 0, 0)
    per_b4 = lambda i, c: (i, 0, 0, 0)
    in_specs = [pl.BlockSpec((1, q, SSD_INNER), seq), pl.BlockSpec((1, q, SSD_CONV_DIM), seq),
                pl.BlockSpec((1, q, LANES), seq), pl.BlockSpec((1, SUBLANES, SSD_CONV_DIM), per_b3),
                pl.BlockSpec((1, SSD_HEADS, SSD_HEAD_DIM, SSD_STATE), per_b4),
                _const_spec((SUBLANES, SSD_CONV_DIM)), _const_spec((1, SSD_CONV_DIM)),
                _const_spec((1, LANES)), _const_spec((1, LANES)), _const_spec((1, LANES)),
                _const_spec((1, SSD_INNER))]
    out_shape = [jax.ShapeDtypeStruct((b, lp, SSD_INNER), BF16),
                 jax.ShapeDtypeStruct((b, SUBLANES, SSD_CONV_DIM), F32),
                 jax.ShapeDtypeStruct((b, SSD_HEADS, SSD_HEAD_DIM, SSD_STATE), F32)]
    out_specs = [pl.BlockSpec((1, q, SSD_INNER), seq_out), pl.BlockSpec((1, SUBLANES, SSD_CONV_DIM), per_b3),
                 pl.BlockSpec((1, SSD_HEADS, SSD_HEAD_DIM, SSD_STATE), per_b4)]
    per_head = pltpu.VMEM((2, SSD_HEADS, q, SSD_HEAD_DIM), BF16)
    y, hist_o, s_o = pl.pallas_call(
        functools.partial(_ssd_kernel, q=q, seq_len=seq_len, n_chunks=n_chunks, pipelined=pipelined),
        grid=(b, n_steps), in_specs=in_specs, out_specs=out_specs, out_shape=out_shape,
        scratch_shapes=[pltpu.VMEM((SUBLANES + q, SSD_CONV_DIM), F32),
                        pltpu.VMEM((SSD_HEADS, SSD_HEAD_DIM, SSD_STATE), F32),
                        pltpu.VMEM((q, SSD_INNER), F32),
                        pltpu.VMEM((2, SSD_HEADS, q, q), BF16), per_head, per_head, per_head,
                        pltpu.VMEM((2, SSD_GROUPS, q, SSD_STATE), BF16),
                        pltpu.VMEM((2, q, SSD_INNER), F32), pltpu.VMEM((2, q, SSD_INNER), F32),
                        pltpu.VMEM((2, SUBLANES, LANES), F32)],
        compiler_params=_cparams(("arbitrary", "arbitrary")), name="ssd",
    )(z, xbc, dtr, hist8, s0, cw8, conv_b.reshape(1, -1), _pad_lanes(dt_bias), _pad_lanes(a_log),
      _pad_lanes(d_skip), norm_w.reshape(1, -1))
    return y, hist_o[:, SUBLANES - hrows:], s_o


def _gla_kernel(q_ref, fx_ref, iv_ref, g_ref, gam_ref, nw_ref, s0_ref, o_ref, s_o, st_t, qd_s, kd_s, kdec_s,
                v_s, gs_s, el_s, *, q, seq_len, layer, n_chunks):
    c = pl.program_id(1)
    pipelined = n_chunks > 1
    stage = (qd_s, kd_s, kdec_s, v_s, gs_s, el_s)

    def state_stage(rd):
        _gla_state_stage(rd, nw_ref, o_ref, st_t, *stage, q=q)

    def free_stage(wr):
        _gla_free_stage(c, wr, q_ref, fx_ref, iv_ref, g_ref, gam_ref, *stage, q=q, seq_len=seq_len, layer=layer,
                        n_chunks=n_chunks)

    @pl.when(c == 0)
    def _():
        for h in range(HGRN_HEADS):
            st_t[h] = s0_ref[0, h].T
        if pipelined:
            for ref in stage[:-1]:
                ref[1] = jnp.zeros(ref.shape[1:], ref.dtype)
            el_s[1] = jnp.ones(el_s.shape[1:], F32)

    if pipelined:
        @pl.when(c % 2 == 0)
        def _():
            state_stage(1)
            free_stage(0)

        @pl.when(c % 2 == 1)
        def _():
            state_stage(0)
            free_stage(1)
    else:
        free_stage(0)
        state_stage(0)

    @pl.when(c == (n_chunks if pipelined else n_chunks - 1))
    def _():
        for h in range(HGRN_HEADS):
            s_o[0, h] = st_t[h].T


def _gla_state_stage(rd, nw_ref, o_ref, st_t, qd_s, kd_s, kdec_s, v_s, gs_s, el_s, *, q):
    tril = _tril_f32(q)
    ks = [slice(h * HGRN_KEY_DIM, (h + 1) * HGRN_KEY_DIM) for h in range(HGRN_HEADS)]
    vs = [slice(h * HGRN_VAL_DIM, (h + 1) * HGRN_VAL_DIM) for h in range(HGRN_HEADS)]
    qd_b = [qd_s[rd, :, s] for s in ks]
    v_b = [v_s[rd, :, s] for s in vs]
    e_last = el_s[rd, 0:1, :]
    att = [_dot_nt(qd_b[h], kd_s[rd, :, ks[h]]) for h in range(HGRN_HEADS)]
    o_state = [_dot_nt(qd_b[h], st_t[h].astype(BF16)) for h in range(HGRN_HEADS)]
    kv = [_dot_tn(v_b[h], kdec_s[rd, :, ks[h]]) for h in range(HGRN_HEADS)]
    for h in range(HGRN_HEADS):
        o = _dot(jnp.where(tril, att[h], 0.0).astype(BF16), v_b[h]) + o_state[h]
        st_t[h] = st_t[h] * e_last[:, ks[h]] + kv[h]
        on = o * lax.rsqrt(jnp.mean(o * o, axis=-1, keepdims=True) + RMS_EPS) * nw_ref[...]
        o_ref[0, :, vs[h]] = (on * gs_s[rd, :, vs[h]]).astype(BF16)


def _gla_free_stage(c, wr, q_ref, fx_ref, iv_ref, g_ref, gam_ref, qd_s, kd_s, kdec_s, v_s, gs_s, el_s, *, q,
                    seq_len, layer, n_chunks):
    rows = [gam_ref[l:l + 1, :] for l in range(DEPTH)]
    mx = functools.reduce(jnp.maximum, rows)
    es = [jnp.exp(r - mx) for r in rows]
    lb = sum(es[1:layer + 1]) / sum(es) if layer >= 1 else jnp.zeros_like(mx)

    f = lb + (1.0 - lb) * jax.nn.sigmoid(fx_ref[0])
    if seq_len % q != 0:
        row = jnp.minimum(c, n_chunks - 1) * q + lax.broadcasted_iota(jnp.int32, (q, 1), 0)
        f = jnp.where(row < seq_len, f, 1.0)
    kk = 1.0 - f
    b = _cumsum_rows(jnp.log(f), q, split=True)
    b_last = b[q - 1:q, :]
    qd_s[wr] = (q_ref[0] * jnp.exp(b)).astype(BF16)
    kd_s[wr] = (kk * jnp.exp(-b)).astype(BF16)
    kdec_s[wr] = (kk * jnp.exp(b_last - b)).astype(BF16)
    v_s[wr] = iv_ref[0].astype(BF16)
    gs_s[wr] = _silu(g_ref[0])
    el_s[wr] = jnp.broadcast_to(jnp.exp(b_last), (SUBLANES, HGRN_K))


def _gla(qa, fx, iv, g, gamma, norm_w, s0, seq_len, layer):
    b, lp, _ = qa.shape
    q = _chunk_len(seq_len, HGRN_CHUNK)
    assert lp % q == 0 and lp - seq_len < q
    gam8 = jnp.zeros((SUBLANES, HGRN_K), F32).at[:DEPTH].set(gamma)
    n_chunks = lp // q
    n_steps, seq, seq_out = _stage_grid(n_chunks, n_chunks > 1)
    per_b4 = lambda i, c: (i, 0, 0, 0)
    st_shape = (HGRN_HEADS, HGRN_KEY_DIM, HGRN_VAL_DIM)
    in_specs = [pl.BlockSpec((1, q, HGRN_K), seq), pl.BlockSpec((1, q, HGRN_K), seq),
                pl.BlockSpec((1, q, HGRN_V), seq), pl.BlockSpec((1, q, HGRN_V), seq),
                _const_spec((SUBLANES, HGRN_K)), _const_spec((1, HGRN_VAL_DIM)),
                pl.BlockSpec((1,) + st_shape, per_b4)]
    stage_k = pltpu.VMEM((2, q, HGRN_K), BF16)
    o, s_o = pl.pallas_call(
        functools.partial(_gla_kernel, q=q, seq_len=seq_len, layer=layer, n_chunks=n_chunks),
        grid=(b, n_steps), in_specs=in_specs,
        out_specs=[pl.BlockSpec((1, q, HGRN_V), seq_out), pl.BlockSpec((1,) + st_shape, per_b4)],
        out_shape=[jax.ShapeDtypeStruct((b, lp, HGRN_V), BF16), jax.ShapeDtypeStruct((b,) + st_shape, F32)],
        scratch_shapes=[pltpu.VMEM((HGRN_HEADS, HGRN_VAL_DIM, HGRN_KEY_DIM), F32),
                        stage_k, stage_k, stage_k, pltpu.VMEM((2, q, HGRN_V), BF16),
                        pltpu.VMEM((2, q, HGRN_V), F32), pltpu.VMEM((2, SUBLANES, HGRN_K), F32)],
        compiler_params=_cparams(("arbitrary", "arbitrary")), name="gla",
    )(qa, fx, iv, g, gam8, norm_w.reshape(1, -1), s0)
    return o, s_o


def _select_topk_rows(gate_t, n_valid, nblk, width):
    blk = lax.broadcasted_iota(jnp.int32, (nblk, width), 0)
    blk_f = blk.astype(F32)
    g = jnp.where(blk < n_valid, gate_t, -jnp.inf)
    sel = jnp.zeros((nblk, width), F32)
    for _ in range(MOBA_TOPK):
        mx = jnp.max(g, axis=0, keepdims=True)
        first = jnp.min(jnp.where(g == mx, blk_f, float(nblk)), axis=0, keepdims=True)
        pick = (blk_f == first) & (mx > -jnp.inf)
        sel = jnp.where(pick, 1.0, sel)
        g = jnp.where(pick, -jnp.inf, g)
    return jnp.where(sel > 0.0, 0.0, -jnp.inf)


def _key_aug_columns(tm):
    lane = lax.broadcasted_iota(jnp.int32, (tm, LANES), 1)
    r = lax.broadcasted_iota(jnp.int32, (tm, LANES), 0).astype(F32)
    return jnp.where((lane >= MOBA_HEAD_DIM) & (lane < MOBA_HEAD_DIM + ALIBI_PARTS), r, 0.0)


def _moba_prompt_kernel(slopes_ref, kn_ref, qn_ref, dmin_ref, knt_ref, q_ref, ka_ref, vt_ref, km_ref, o_ref,
                        brow_s, ubuf, *, nblk):
    hp = pl.program_id(0)
    i = pl.program_id(1)
    tq = MOBA_BLOCK
    d = MOBA_HEAD_DIM
    q = q_ref[...]
    lane_k = lax.broadcasted_iota(jnp.int32, (nblk, 2 * d), 1)
    causal =(lax.broadcasted_iota(jnp.int32, (MOBA_BLOCK, tq), 0)
              <= lax.broadcasted_iota(jnp.int32, (MOBA_BLOCK, tq), 1))
    blk_f = lax.broadcasted_iota(jnp.int32, (nblk, tq), 0).astype(F32)
    col_f = lax.broadcasted_iota(jnp.int32, (nblk, tq), 1).astype(F32)
    aug_row = lax.broadcasted_iota(jnp.int32, (d, tq), 0)
    i_f = i.astype(F32)

    q_t = q.T
    km = km_ref[...]
    km2 = jnp.concatenate([jnp.where(lane_k < d, km, 0.0), jnp.where(lane_k >= d, km, 0.0)], axis=0)
    gate_both = _dot_split(km2, q_t)

    rhs = []
    for j in range(2):
        a2 = slopes_ref[2 * hp + j] * LOG2E
        sel = _select_topk_rows(gate_both[j * nblk:(j + 1) * nblk, :], i, nblk, tq)
        brow_s[j] = sel + a2 * (MOBA_BLOCK * (blk_f - i_f) - col_f)
        rest = jnp.full((d, tq), a2, F32)
        aug = jnp.zeros((d, tq), F32)
        for part in range(ALIBI_PARTS):
            term = rest.astype(BF16).astype(F32)
            aug = jnp.where(aug_row == part, term, aug)
            rest = rest - term
        q_h = q_t[j * d:(j + 1) * d, :] * (d ** -0.5 * LOG2E)
        rhs.append(jnp.concatenate([q_h, aug], axis=0).astype(BF16))

    def block_update(st, u, b_row, v_t):
        m_run, o_t = st
        m_new = jnp.maximum(m_run, jnp.max(u, axis=0, keepdims=True) + b_row)
        alpha = jnp.exp2(m_run - m_new)
        p = jnp.exp2(u - (m_new - b_row))
        return m_new, alpha * o_t + _dot(v_t, p.astype(BF16))

    def score(n, slot):
        for j in range(2):
            ubuf[slot, j] = _dot(ka_ref[j, n], rhs[j])

    def consume(n, slot, sts, own):
        new = []
        for j in range(2):
            u = ubuf[slot, j]
            if own:
                u = jnp.where(causal, u, -jnp.inf)
                b_row = -(slopes_ref[2 * hp + j] * LOG2E) * col_f[0:1, :]
            else:
                b_row = brow_s[j, pl.ds(n, 1), :]
            new.append(block_update(sts[j], u, b_row, vt_ref[n, j * MOBA_VT_ROWS:(j + 1) * MOBA_VT_ROWS, :]))
        return tuple(new)

    def step(n, slot, slot_ahead, sts):
        score(jnp.minimum(n + MOBA_AHEAD, i), slot_ahead)
        return consume(n, slot, sts, False)

    def group(base, unroll, sts):
        for k in range(unroll):
            sts = step(base + k, k % MOBA_SLOTS, (k + MOBA_AHEAD) % MOBA_SLOTS, sts)
        return sts

    def first_needed(j):
        hd = 2 * hp + j
        a2 = slopes_ref[hd] * LOG2E
        c = d ** -0.5 * LOG2E
        qn = qn_ref[i * MOBA_HEADS + hd]
        floor = c * (dmin_ref[i * MOBA_HEADS + hd] - BOUND_SLACK * qn * kn_ref[i * MOBA_HEADS + hd])

        blk_row = lax.broadcasted_iota(jnp.int32, (1, nblk), 1).astype(F32)
        ceil = (c * (1.0 + BOUND_SLACK) * qn * knt_ref[pl.ds(hd, 1), :]
                + a2 * (MOBA_BLOCK * (blk_row - i_f) + (MOBA_BLOCK - 1)))
        needed = (blk_row < i_f) & (ceil >= floor - UNDERFLOW_LOG2)
        return jnp.min(jnp.where(needed, blk_row, i_f)).astype(jnp.int32)

    start = (jnp.minimum(first_needed(0), first_needed(1)) // MOBA_SLOTS) * MOBA_SLOTS

    init = (jnp.full((1, tq), NEG_BIG, F32), jnp.zeros((MOBA_VT_ROWS, tq), F32))
    for k in range(MOBA_AHEAD):
        score(jnp.minimum(start + k, i), k)
    n_big = (i - start) // MOBA_UNROLL
    sts = lax.fori_loop(0, n_big, lambda g, s: group(start + g * MOBA_UNROLL, MOBA_UNROLL, s), (init, init))
    base = start + n_big * MOBA_UNROLL
    n_small = (i - base) // MOBA_SLOTS
    sts = lax.fori_loop(0, n_small, lambda g, s: group(base + g * MOBA_SLOTS, MOBA_SLOTS, s), sts)
    base = base + n_small * MOBA_SLOTS
    rem = i - base
    for k in range(MOBA_SLOTS - 1):
        sts = lax.cond(rem > k, lambda s, k=k: step(base + k, k, (k + MOBA_AHEAD) % MOBA_SLOTS, s),
                       lambda s: s, sts)
    outs = []
    for _, o_t in consume(i, rem, sts, True):
        outs.append(o_t[0:d, :] / o_t[d:d + 1, :])
    o_ref[...] = jnp.concatenate(outs, axis=0).T.astype(BF16)


def _alibi_slopes():
    return jnp.asarray(np.exp2(-8.0 * np.arange(1, MOBA_HEADS + 1) / MOBA_HEADS), dtype=F32)


def _moba_prompt(q, ka, vt, km, bounds):
    s = q.shape[0]
    nblk = s // MOBA_BLOCK
    assert nblk * MOBA_BLOCK == s and 2 * MOBA_HEAD_DIM == LANES
    pairs = MOBA_HEADS // 2
    tables = [bounds[:, r, :MOBA_HEADS].reshape(-1) for r in range(3)]
    kn_t = bounds[:, 0, :MOBA_HEADS].T
    smem = pl.BlockSpec(memory_space=pltpu.SMEM)
    return pl.pallas_call(
        functools.partial(_moba_prompt_kernel, nblk=nblk),
        grid=(pairs, nblk),
        in_specs=[smem, smem, smem, smem, _const_spec((MOBA_HEADS, nblk)),
                  pl.BlockSpec((MOBA_BLOCK, LANES), lambda hp, i: (i, hp)),
                  pl.BlockSpec((2, nblk, MOBA_BLOCK, LANES), lambda hp, i: (hp, 0, 0, 0)),
                  pl.BlockSpec((nblk, 2 * MOBA_VT_ROWS, MOBA_BLOCK), lambda hp, i: (0, hp, 0)),
                  pl.BlockSpec((nblk, LANES), lambda hp, i: (0, hp))],
        out_specs=pl.BlockSpec((MOBA_BLOCK, LANES), lambda hp, i: (i, hp)),
        out_shape=jax.ShapeDtypeStruct((s, MOBA_INNER), BF16),
        scratch_shapes=[pltpu.VMEM((2, nblk, MOBA_BLOCK), F32),
                        pltpu.VMEM((MOBA_SLOTS, 2, MOBA_BLOCK, MOBA_BLOCK), F32)],
        compiler_params=_cparams(("arbitrary", "arbitrary")), name="moba_prompt",
    )(_alibi_slopes(), *tables, kn_t, q, ka, vt, km)


PAGES_PER_STEP = 64
SAMPLE_AHEAD = 3
SAMPLE_BUFS = SAMPLE_AHEAD + 1


def _kmean_kernel(pt_ref, *refs):
    del pt_ref
    o_ref = refs[-1]
    s = pl.program_id(1)
    ppb = MOBA_BLOCK // PAGE_SIZE
    bps = PAGES_PER_STEP // ppb

    @pl.when(s == 0)
    def _():
        o_ref[...] = jnp.zeros(o_ref.shape, F32)

    acc = o_ref[0]
    lane = lax.broadcasted_iota(jnp.int32, acc.shape, 2)
    for m in range(bps):
        x = refs[m * ppb][0]
        for p in range(1, ppb):
            x = x + refs[m * ppb + p][0]
        mean = jnp.sum(x, axis=-1, keepdims=True) * (1.0 / MOBA_BLOCK)
        acc = jnp.where(lane == s * bps + m, mean, acc)
    o_ref[0] = acc


def _sample_kmean(k_pool_t, page_table):
    db, n_pages = page_table.shape
    ppb = MOBA_BLOCK // PAGE_SIZE
    n_full = n_pages // ppb
    assert n_full * ppb == n_pages and n_pages % PAGES_PER_STEP == 0
    steps = n_pages // PAGES_PER_STEP
    blk = (1, MOBA_HEADS, MOBA_HEAD_DIM, PAGE_SIZE)

    def page_spec(p):
        return pl.BlockSpec(blk, lambda b, s, pt: (pt[b * n_pages + s * PAGES_PER_STEP + p], 0, 0, 0))

    out_blk = (1, MOBA_HEADS, MOBA_HEAD_DIM, n_full)
    grid_spec = pltpu.PrefetchScalarGridSpec(
        num_scalar_prefetch=1, grid=(db, steps),
        in_specs=[page_spec(p) for p in range(PAGES_PER_STEP)],
        out_specs=pl.BlockSpec(out_blk, lambda b, s, pt: (b, 0, 0, 0)))
    return pl.pallas_call(
        _kmean_kernel, grid_spec=grid_spec,
        out_shape=jax.ShapeDtypeStruct((db,) + out_blk[1:], F32),
        compiler_params=_cparams(("arbitrary", "arbitrary")), name="sample_kmean",
    )(page_table.reshape(-1), *([k_pool_t] * PAGES_PER_STEP))


def _sample_select_kernel(q_ref, km_ref, idx_ref, *, n_full):
    t8 = q_ref.shape[2]
    blk = lax.broadcasted_iota(jnp.int32, (t8, n_full), 1).astype(F32)
    lane = lax.broadcasted_iota(jnp.int32, (t8, LANES), 1)
    for h in range(MOBA_HEADS):
        g = jnp.dot(q_ref[0, h], km_ref[0, h], precision=HIGHEST, preferred_element_type=F32)
        out = jnp.zeros((t8, LANES), jnp.int32)
        for k in range(MOBA_TOPK):
            mx = jnp.max(g, axis=-1, keepdims=True)
            first = jnp.min(jnp.where(g == mx, blk, float(n_full)), axis=-1, keepdims=True)
            out = jnp.where(lane == k, first.astype(jnp.int32), out)
            g = jnp.where(blk == first, -jnp.inf, g)
        idx_ref[0, h] = out


def _sample_select(qh, km):
    db, h, t8, d = qh.shape
    n_full = km.shape[3]
    assert n_full >= MOBA_TOPK
    b4 = lambda b: (b, 0, 0, 0)
    return pl.pallas_call(
        functools.partial(_sample_select_kernel, n_full=n_full), grid=(db,),
        in_specs=[pl.BlockSpec((1, h, t8, d), b4), pl.BlockSpec((1, h, d, n_full), b4)],
        out_specs=pl.BlockSpec((1, h, t8, LANES), b4),
        out_shape=jax.ShapeDtypeStruct((db, h, t8, LANES), jnp.int32),
        compiler_params=_cparams(("arbitrary",)), name="sample_select",
    )(qh, km)


def _sample_attn_kernel(idx_ref, phys_ref, slopes_ref, q_ref, kn_ref, vn_ref, kpool, vpool, o_ref,
                        kbuf, vbuf, sem, *, ds, past):
    ppb = MOBA_BLOCK // PAGE_SIZE
    n_sel = MOBA_TOPK * ppb
    n_pg = ds * n_sel
    b = pl.program_id(0)
    h = pl.program_id(1)
    n_heads = pl.num_programs(1)
    step = b * n_heads + h
    n_steps = pl.num_programs(0) * n_heads

    def page_copies(st, half):
        hh = st % n_heads
        out = []
        for k in range(n_pg):
            page = phys_ref[st * n_pg + k]
            out.append(pltpu.make_async_copy(kpool.at[page, hh], kbuf.at[half, k], sem.at[0, half]))
            out.append(pltpu.make_async_copy(vpool.at[page, hh], vbuf.at[half, k], sem.at[1, half]))
        return out

    @pl.when(step == 0)
    def _():
        for st in range(SAMPLE_AHEAD):
            for cp in page_copies(st, st):
                cp.start()

    @pl.when(step + SAMPLE_AHEAD < n_steps)
    def _():
        for cp in page_copies(step + SAMPLE_AHEAD, (step + SAMPLE_AHEAD) % SAMPLE_BUFS):
            cp.start()

    half = step % SAMPLE_BUFS
    for cp in page_copies(step, half):
        cp.wait()
    k_refs = [kbuf.at[half, k] for k in range(n_pg)]
    v_refs = [vbuf.at[half, k] for k in range(n_pg)]
    slope = slopes_ref[h]
    t8 = q_ref.shape[3]
    c_page = lax.broadcasted_iota(jnp.int32, (1, PAGE_SIZE), 1).astype(F32)
    c_new = lax.broadcasted_iota(jnp.int32, (1, t8), 1)
    qf = q_ref[0, 0] * (MOBA_HEAD_DIM ** -0.5)
    k_new = kn_ref[0, 0]
    v_new = vn_ref[0, 0]
    o_ref[0, 0] = jnp.zeros((MOBA_HEAD_DIM, t8), F32)
    for t in range(ds):
        q_t = qf[:, t:t + 1]
        t_pos = float(past + t)
        scores = []
        for sp in range(n_sel):
            slot, p = divmod(sp, ppb)
            blk_idx = idx_ref[((b * MOBA_HEADS + h) * ds + t) * MOBA_TOPK + slot]
            pos0 = (blk_idx * MOBA_BLOCK + p * PAGE_SIZE).astype(F32)
            s = jnp.sum(k_refs[t * n_sel + sp][...] * q_t, axis=0, keepdims=True)
            scores.append(s - slope * (t_pos - (pos0 + c_page)))
        s_new = jnp.sum(k_new * q_t, axis=0, keepdims=True)
        s_new = s_new - slope * (t_pos - (float(past) + c_new.astype(F32)))
        s_new = jnp.where(c_new <= t, s_new, -jnp.inf)
        mx = jnp.max(s_new, axis=-1, keepdims=True)
        for s in scores:
            mx = jnp.maximum(mx, jnp.max(s, axis=-1, keepdims=True))
        p_new = jnp.exp(s_new - mx)
        l = jnp.sum(p_new, axis=-1, keepdims=True)
        o = jnp.sum(p_new * v_new, axis=-1, keepdims=True)
        acc = jnp.zeros((MOBA_HEAD_DIM, PAGE_SIZE), F32)
        for sp in range(n_sel):
            pr = jnp.exp(scores[sp] - mx)
            l = l + jnp.sum(pr, axis=-1, keepdims=True)
            acc = acc + pr * v_refs[t * n_sel + sp][...]
        o = o + jnp.sum(acc, axis=-1, keepdims=True)
        o_ref[0, 0, :, t:t + 1] = o / l


def _sample_attn(qh, kh, vh, idx, k_pool, v_pool, page_table, ds):
    db, h, d, t8 = qh.shape
    n_pages = page_table.shape[1]
    ppb = MOBA_BLOCK // PAGE_SIZE
    assert n_pages % ppb == 0
    past = n_pages * PAGE_SIZE
    n_sel = MOBA_TOPK * ppb
    logical = idx[..., None] * ppb + jnp.arange(ppb, dtype=jnp.int32)
    phys = page_table[jnp.arange(db)[:, None, None, None], logical.reshape(db, h, ds, n_sel)]
    tok = pl.BlockSpec((1, 1, d, t8), lambda b, hh, *_: (b, hh, 0, 0))
    hbm = pl.BlockSpec(memory_space=pl.ANY)
    assert db * h >= SAMPLE_AHEAD
    page_buf = pltpu.VMEM((SAMPLE_BUFS, ds * n_sel, d, PAGE_SIZE), F32)
    grid_spec = pltpu.PrefetchScalarGridSpec(
        num_scalar_prefetch=2, grid=(db, h),
        in_specs=[pl.BlockSpec(memory_space=pltpu.SMEM), tok, tok, tok, hbm, hbm],
        out_specs=tok,
        scratch_shapes=[page_buf, page_buf, pltpu.SemaphoreType.DMA((2, SAMPLE_BUFS))])
    return pl.pallas_call(
        functools.partial(_sample_attn_kernel, ds=ds, past=past), grid_spec=grid_spec,
        out_shape=jax.ShapeDtypeStruct((db, h, d, t8), F32),
        compiler_params=_cparams(("arbitrary", "arbitrary")), name="sample_attn",
    )(idx.reshape(-1), phys.reshape(-1), _alibi_slopes(), qh, kh, vh, k_pool, v_pool)


def _split_w_in_a(w, f32_query):
    wb = w.astype(BF16)
    wdt = jnp.zeros((D_MODEL, LANES), BF16).at[:, :SSD_HEADS].set(wb[:, OFF_DT:OFF_Q])
    wq = w[:, OFF_Q:OFF_K] if f32_query else wb[:, OFF_Q:OFF_K]
    return [wb[:, :OFF_XBC], wb[:, OFF_XBC:OFF_DT], wdt, wq, wb[:, OFF_K:OFF_V], wb[:, OFF_V:]]


def _pad_seq(a, b, l, lp):
    a = a.reshape(b, l, a.shape[-1])
    return a if lp == l else jnp.pad(a, ((0, 0), (0, lp - l), (0, 0)))


def _round_up(n, m):
    return -(-n // m) * m


def _heads(a, b, l):
    return a.reshape(b, l, MOBA_HEADS, MOBA_HEAD_DIM).transpose(0, 2, 1, 3)


def _pad_tokens(a):
    l = a.shape[-2]
    return jnp.pad(a, ((0, 0),) * (a.ndim - 2) + ((0, _round_up(l, SUBLANES) - l), (0, 0)))


def _trunk(x, bsz, length, prompt, caches, p):
    ssm_conv0, ssm0, hgrn0, ffn_conv0 = caches[:4]
    m = bsz * length
    k_rows = v_rows = ssm_c = ssm_s = hgrn_s = None
    ffn_c = []
    for layer in range(DEPTH):
        nw = p['norm_mix'][layer].reshape(1, -1)
        if layer % 2 == 0:
            ia = layer // 2
            assert ia == 0
            ws = _split_w_in_a(p['w_in_a'][ia], f32_query=not prompt)
            outs = _inproj_a(x, nw, ws, prompt)
            z, xbc, dtr, q = outs[:4]
            lp = _round_up(length, _chunk_len(length, SSD_CHUNK))
            y_ssd, hist, s_new = _ssd(
                _pad_seq(z, bsz, length, lp), _pad_seq(xbc, bsz, length, lp), _pad_seq(dtr, bsz, length, lp),
                ssm_conv0[ia], ssm0[ia], p['ssd_conv_w'][ia], p['ssd_conv_b'][ia], p['ssd_dt_bias'][ia],
                p['ssd_a_log'][ia], p['ssd_d'][ia], p['ssd_norm_w'][ia], length)
            y_ssd = y_ssd[:, :length].reshape(m, SSD_INNER)
            if prompt:
                assert bsz == 1
                kr, vr, ka, vt, km, bounds = outs[4:]
                o_att = _moba_prompt(q, ka, vt, km.reshape(-1, MOBA_INNER), bounds)
                k_rows, v_rows = jnp.swapaxes(kr, -1, -2)[None], jnp.swapaxes(vr, -1, -2)[None]
            else:
                k, v = outs[4:]
                k_pool, v_pool, page_table = caches[4:]
                k_pool_t = jnp.swapaxes(k_pool[ia], -1, -2)
                v_pool_t = jnp.swapaxes(v_pool[ia], -1, -2)
                qh, kh, vh = (_pad_tokens(_heads(t, bsz, length)) for t in (q, k, v))
                qt, kt, vt = (jnp.swapaxes(t, -1, -2) for t in (qh, kh, vh))
                km = _sample_kmean(k_pool_t, page_table)
                idx = _sample_select(qh, km)[:, :, :length, :MOBA_TOPK]
                o = _sample_attn(qt, kt, vt, idx, k_pool_t, v_pool_t, page_table, length)
                o_att = o[..., :length].transpose(0, 3, 1, 2).reshape(m, MOBA_INNER).astype(BF16)
                k_rows, v_rows = kh[:, :, :length], vh[:, :, :length]
            wo = p['w_out_a'][ia].astype(BF16)
            mix_acts, mix_ws = [y_ssd, o_att], [wo[:SSD_INNER], wo[SSD_INNER:]]
            ssm_c, ssm_s = hist, s_new
        else:
            ic = layer // 2
            assert ic == 0
            wc = p['w_in_c'][ic].astype(BF16)
            ws = [wc[:, :HGRN_K], wc[:, HGRN_K:2 * HGRN_K], wc[:, 2 * HGRN_K:2 * HGRN_K + HGRN_V],
                  wc[:, 2 * HGRN_K + HGRN_V:]]
            qa, fx, iv, g = _norm_matmul(x, nw, ws)
            lp = _round_up(length, _chunk_len(length, HGRN_CHUNK))
            o, s_new = _gla(*(_pad_seq(t, bsz, length, lp) for t in (qa, fx, iv, g)),
                            p['hgrn_lb_gamma'], p['hgrn_norm_w'][ic], hgrn0[ic], length, layer)
            mix_acts, mix_ws = [o[:, :length].reshape(m, HGRN_V)], [p['w_out_c'][ic].astype(BF16)]
            hgrn_s = s_new
        nf = p['norm_final'].reshape(1, -1) if layer == DEPTH - 1 else None
        hist = ffn_conv0[layer, 0] if prompt else ffn_conv0[layer]
        x, fh = _ffn(x, mix_acts, mix_ws, p['norm_ffn'][layer].reshape(1, -1),
                     p['ffn_w_up'].astype(BF16), p['ffn_conv_w'][layer],
                     p['ffn_conv_b'][layer].reshape(1, -1), p['ffn_w_down'].astype(BF16), layer, hist,
                     None if prompt else length, nf)
        ffn_c.append(fh[None] if prompt else fh)
    return (x.reshape(bsz, length, D_MODEL), k_rows[None], v_rows[None], ssm_s[None], ssm_c[None],
            hgrn_s[None], jnp.stack(ffn_c))


def kernel(x_prompt, x_sample, cache_k_pool, cache_v_pool, page_table, state_ssm, state_ssm_conv, state_hgrn,
           state_ffn_conv, norm_mix, norm_ffn, norm_final, w_in_a, w_out_a, ssd_conv_w, ssd_conv_b,
           ssd_dt_bias, ssd_a_log, ssd_d, ssd_norm_w, w_in_c, w_out_c, hgrn_lb_gamma, hgrn_norm_w, ffn_w_up,
           ffn_conv_w, ffn_conv_b, ffn_w_down):
    p = dict(norm_mix=norm_mix, norm_ffn=norm_ffn, norm_final=norm_final, w_in_a=w_in_a, w_out_a=w_out_a,
             ssd_conv_w=ssd_conv_w, ssd_conv_b=ssd_conv_b, ssd_dt_bias=ssd_dt_bias, ssd_a_log=ssd_a_log,
             ssd_d=ssd_d, ssd_norm_w=ssd_norm_w, w_in_c=w_in_c, w_out_c=w_out_c, hgrn_lb_gamma=hgrn_lb_gamma,
             hgrn_norm_w=hgrn_norm_w, ffn_w_up=ffn_w_up, ffn_conv_w=ffn_conv_w, ffn_conv_b=ffn_conv_b,
             ffn_w_down=ffn_w_down)
    bp, sp, _ = x_prompt.shape
    db, ds, _ = x_sample.shape
    na, nc = (DEPTH + 1) // 2, DEPTH // 2
    zeros = lambda *s: jnp.zeros(s, F32)
    prompt_caches = (zeros(na, bp, SSD_CONV - 1, SSD_CONV_DIM), zeros(na, bp, SSD_HEADS, SSD_HEAD_DIM, SSD_STATE),
                     zeros(nc, bp, HGRN_HEADS, HGRN_KEY_DIM, HGRN_VAL_DIM), zeros(DEPTH, bp, FFN_CONV - 1, FFN_DIM))
    outs_p = _trunk(x_prompt.reshape(bp * sp, D_MODEL), bp, sp, True, prompt_caches, p)
    sample_caches = (state_ssm_conv, state_ssm, state_hgrn, state_ffn_conv, cache_k_pool, cache_v_pool, page_table)
    outs_s = _trunk(x_sample.reshape(db * ds, D_MODEL), db, ds, False, sample_caches, p)
    yp, krp, vrp, ssp, scp, hgp, fcp = outs_p
    ys, krs, vrs, sss, scs, hgs, fcs = outs_s
    return (yp, ys, krp, vrp, ssp, scp, hgp, fcp, krs, vrs, sss, scs, hgs, fcs)
```

```python
import functools
import math

import numpy as np
import jax
import jax.numpy as jnp
from jax import lax
from jax.experimental import pallas as pl
from jax.experimental.pallas import tpu as pltpu

F32 = jnp.float32
BF16 = jnp.bfloat16
HIGHEST = lax.Precision.HIGHEST

D_MODEL = 1024
DEPTH = 2
PAGE_SIZE = 128
SSD_HEADS = 8
SSD_HEAD_DIM = 64
SSD_INNER = SSD_HEADS * SSD_HEAD_DIM
SSD_STATE = 64
SSD_GROUPS = 2
SSD_CONV = 4
SSD_CHUNK = 128
SSD_GN = SSD_GROUPS * SSD_STATE
SSD_CONV_DIM = SSD_INNER + 2 * SSD_GN
MOBA_HEADS = 8
MOBA_HEAD_DIM = 64
MOBA_INNER = MOBA_HEADS * MOBA_HEAD_DIM
MOBA_BLOCK = 256
MOBA_TOPK = 3
OFF_XBC = SSD_INNER
OFF_DT = OFF_XBC + SSD_CONV_DIM
OFF_Q = OFF_DT + SSD_HEADS
OFF_K = OFF_Q + MOBA_INNER
OFF_V = OFF_K + MOBA_INNER
HGRN_HEADS = 8
HGRN_KEY_DIM = 128
HGRN_VAL_DIM = D_MODEL // HGRN_HEADS
HGRN_K = HGRN_HEADS * HGRN_KEY_DIM
HGRN_V = HGRN_HEADS * HGRN_VAL_DIM
HGRN_CHUNK = 64
FFN_DIM = 2816
FFN_CONV = 3
RMS_EPS = 1e-6

LANES = 128
SUBLANES = 8
VMEM_LIMIT_BYTES = 56 * 1024 * 1024

ROW_TILE = 256
FFN_ROW_TILE = 512
FFN_COL_CHUNK = 256
NEG_BIG = -1e30
LOG2E = math.log2(math.e)
ALIBI_PARTS = 3
MOBA_VT_ROWS = MOBA_HEAD_DIM + 16
BOUND_SLACK = 0.02
UNDERFLOW_LOG2 = 160.0
MOBA_AHEAD = 2
MOBA_SLOTS = MOBA_AHEAD + 1
MOBA_UNROLL = 4 * MOBA_SLOTS


def _cparams(sem, flags=None):
    return pltpu.CompilerParams(dimension_semantics=sem, vmem_limit_bytes=VMEM_LIMIT_BYTES, flags=flags)


def _const_spec(shape):
    nd = len(shape)
    return pl.BlockSpec(shape, lambda *_: (0,) * nd)


def _rms(x, w):
    y = x * lax.rsqrt(jnp.mean(x * x, axis=-1, keepdims=True) + RMS_EPS)
    return y * w


def _rms_bf16(x, w):
    return _rms(x, w).astype(BF16)


def _dot(a, b):
    return jnp.dot(a, b, preferred_element_type=F32)


def _dot_split(a, b):
    a_hi = a.astype(BF16)
    a_lo = (a - a_hi.astype(F32)).astype(BF16)
    b_hi = b.astype(BF16)
    b_lo = (b - b_hi.astype(F32)).astype(BF16)
    return _dot(jnp.concatenate([a_hi, a_lo, a_hi], axis=1), jnp.concatenate([b_hi, b_hi, b_lo], axis=0))


def _dot_nt(a, b):
    return lax.dot_general(a, b, (((1,), (1,)), ((), ())), preferred_element_type=F32)


def _dot_tn(a, b):
    return lax.dot_general(a, b, (((0,), (0,)), ((), ())), preferred_element_type=F32)


def _tril_f32(n):
    r = lax.broadcasted_iota(jnp.int32, (n, n), 0)
    c = lax.broadcasted_iota(jnp.int32, (n, n), 1)
    return r >= c


def _cumsum_rows(x, q, split):
    if not split:
        return jnp.dot(_tril_f32(q).astype(F32), x, precision=HIGHEST, preferred_element_type=F32)
    tri = _tril_f32(q).astype(BF16)
    acc = None
    rest = x
    for _ in range(3):
        term = rest.astype(BF16)
        part = _dot(tri, term)
        acc = part if acc is None else acc + part
        rest = rest - term.astype(F32)
    return acc


def _silu(x):
    return x * jax.nn.sigmoid(x)


def _inproj_a_kernel(x_ref, nw_ref, wz, wxbc, wdt, wq, wk, wv, *rest, prompt):
    outs = rest[1:] if prompt else rest
    h = _rms(x_ref[...], nw_ref[...])
    hb = h.astype(BF16)
    z_o, xbc_o, dt_o, q_o = outs[:4]
    z_o[...] = _dot(hb, wz[...])
    xbc_o[...] = _dot(hb, wxbc[...])
    dt_o[...] = _dot(hb, wdt[...])
    if wq.dtype == F32:
        q = jnp.dot(h, wq[...], precision=HIGHEST, preferred_element_type=F32)
    else:
        q = _dot(hb, wq[...])
    q_o[...] = q
    k = _dot(hb, wk[...])
    v = _dot(hb, wv[...])
    if not prompt:
        k_o, v_o = outs[4:]
        k_o[...] = k
        v_o[...] = v
        return
    kr_o, vr_o, ka_o, vt_o, km_o, bnd_o = outs[4:]
    tm = k.shape[0]
    head_sel = rest[0][...]
    k_norm = jnp.sqrt(jnp.max(_dot((k * k).astype(BF16), head_sel), axis=0, keepdims=True))
    q_norm = jnp.sqrt(jnp.max(_dot((q * q).astype(BF16), head_sel), axis=0, keepdims=True))
    qk_min = jnp.min(_dot((q * k).astype(BF16), head_sel), axis=0, keepdims=True)
    row8 = lax.broadcasted_iota(jnp.int32, (SUBLANES, LANES), 0)
    bnd_o[0] = jnp.where(row8 == 0, k_norm, jnp.where(row8 == 1, q_norm, jnp.where(row8 == 2, qk_min, 0.0)))
    lane = lax.broadcasted_iota(jnp.int32, (tm, LANES), 1)
    key_aug = _key_aug_columns(tm)
    for hp in range(MOBA_HEADS // 2):
        pair = k[:, hp * LANES:(hp + 1) * LANES]
        for j, kh in enumerate((pair, pltpu.roll(pair, MOBA_HEAD_DIM, 1))):
            ka_o[2 * hp + j, 0] = jnp.where(lane < MOBA_HEAD_DIM, kh, key_aug).astype(BF16)
    k_t = k.T
    v_t = v.T
    for p in range(tm // PAGE_SIZE):
        toks = slice(p * PAGE_SIZE, (p + 1) * PAGE_SIZE)
        for h in range(MOBA_HEADS):
            dims = slice(h * MOBA_HEAD_DIM, (h + 1) * MOBA_HEAD_DIM)
            kr_o[p, h] = k_t[dims, toks]
            vr_o[p, h] = v_t[dims, toks]
    ones_rows = (lax.broadcasted_iota(jnp.int32, (MOBA_VT_ROWS - MOBA_HEAD_DIM, tm), 0) == 0).astype(F32)
    pieces = []
    for h in range(MOBA_HEADS):
        pieces += [v_t[h * MOBA_HEAD_DIM:(h + 1) * MOBA_HEAD_DIM, :], ones_rows]
    vt_o[0] = jnp.concatenate(pieces, axis=0).astype(BF16)
    km_o[0] = jnp.mean(k, axis=0, keepdims=True)


def _inproj_a(x, nw, ws, prompt):
    m = x.shape[0]
    tm = ROW_TILE if prompt else m
    assert m % tm == 0
    if prompt:
        assert tm == MOBA_BLOCK
    nt = m // tm
    widths = [w.shape[1] for w in ws]
    row = lambda i: (i, 0)
    in_specs = [pl.BlockSpec((tm, D_MODEL), row), _const_spec((1, D_MODEL))]
    in_specs += [_const_spec(w.shape) for w in ws]
    out_shape = [jax.ShapeDtypeStruct((m, n), F32) for n in widths[:4]]
    out_specs = [pl.BlockSpec((tm, n), row) for n in widths[:4]]
    if prompt:
        npg = m // PAGE_SIZE
        ppt = tm // PAGE_SIZE
        rows_shape = (npg, MOBA_HEADS, MOBA_HEAD_DIM, PAGE_SIZE)
        rows_spec = pl.BlockSpec((ppt, MOBA_HEADS, MOBA_HEAD_DIM, PAGE_SIZE), lambda i: (i, 0, 0, 0))
        out_shape += [jax.ShapeDtypeStruct(rows_shape, F32)] * 2
        out_specs += [rows_spec, rows_spec]
        out_shape += [jax.ShapeDtypeStruct((MOBA_HEADS, nt, tm, LANES), BF16),
                      jax.ShapeDtypeStruct((nt, MOBA_HEADS * MOBA_VT_ROWS, tm), BF16),
                      jax.ShapeDtypeStruct((nt, 1, MOBA_INNER), F32),
                      jax.ShapeDtypeStruct((nt, SUBLANES, LANES), F32)]
        out_specs += [pl.BlockSpec((MOBA_HEADS, 1, tm, LANES), lambda i: (0, i, 0, 0)),
                      pl.BlockSpec((1, MOBA_HEADS * MOBA_VT_ROWS, tm), lambda i: (i, 0, 0)),
                      pl.BlockSpec((1, 1, MOBA_INNER), lambda i: (i, 0, 0)),
                      pl.BlockSpec((1, SUBLANES, LANES), lambda i: (i, 0, 0))]
        head_sel = (jnp.arange(MOBA_INNER)[:, None] // MOBA_HEAD_DIM == jnp.arange(LANES)[None, :]).astype(BF16)
        extra, extra_specs = [head_sel], [_const_spec(head_sel.shape)]
    else:
        out_shape += [jax.ShapeDtypeStruct((m, MOBA_INNER), F32)] * 2
        out_specs += [pl.BlockSpec((tm, MOBA_INNER), row)] * 2
        extra, extra_specs = [], []
    return pl.pallas_call(
        functools.partial(_inproj_a_kernel, prompt=prompt),
        grid=(nt,), in_specs=in_specs + extra_specs, out_specs=out_specs, out_shape=out_shape,
        compiler_params=_cparams(("arbitrary",)), name="inproj_a",
    )(x, nw, *ws, *extra)


def _norm_matmul_kernel(x_ref, nw_ref, *refs):
    n = len(refs) // 2
    hb = _rms_bf16(x_ref[...], nw_ref[...])
    for w, o in zip(refs[:n], refs[n:]):
        o[...] = _dot(hb, w[...])


def _norm_matmul(x, nw, ws):
    m = x.shape[0]
    tm = min(ROW_TILE, m)
    assert m % tm == 0
    row = lambda i: (i, 0)
    in_specs = [pl.BlockSpec((tm, D_MODEL), row), _const_spec((1, D_MODEL))]
    in_specs += [_const_spec(w.shape) for w in ws]
    return pl.pallas_call(
        _norm_matmul_kernel, grid=(m // tm,), in_specs=in_specs,
        out_specs=[pl.BlockSpec((tm, w.shape[1]), row) for w in ws],
        out_shape=[jax.ShapeDtypeStruct((m, w.shape[1]), F32) for w in ws],
        compiler_params=_cparams(("arbitrary",)), name="norm_matmul",
    )(x, nw, *ws)


def _ffn_kernel(*refs, tm, seq_len, final_norm, n_mix):
    it = iter(refs)
    x_ref = next(it)
    mix_refs = [next(it) for _ in range(2 * n_mix)]
    nw_ref, wup_ref, cw_ref, cb_ref, wdn_ref = (next(it) for _ in range(5))
    if seq_len is None:
        hist_ref = next(it)
    else:
        h1_ref, h2_ref = next(it), next(it)
    nf_ref = next(it) if final_norm else None
    o_ref = next(it)
    tail_ref = next(it)
    up_s, act_s = next(it), next(it)

    i = pl.program_id(0)
    f = FFN_DIM
    if seq_len is None:
        @pl.when(i == 0)
        def _():
            up_s[0:SUBLANES, 0:f] = hist_ref[...]
    else:
        up_s[0:SUBLANES, 0:f] = jnp.zeros((SUBLANES, f), F32)

    x = x_ref[...]
    for a_ref, w_ref in zip(mix_refs[:n_mix], mix_refs[n_mix:]):
        x = x + _dot(a_ref[...], w_ref[...])
    hb = _rms_bf16(x, nw_ref[...])
    up_s[SUBLANES:SUBLANES + tm, :] = _dot(hb, wup_ref[0])

    if seq_len is not None:
        t = lax.broadcasted_iota(jnp.int32, (tm, 1), 0) % seq_len
    for c in range(0, f, FFN_COL_CHUNK):
        cols = slice(c, c + FFN_COL_CHUNK)
        g0 = up_s[SUBLANES:SUBLANES + tm, cols]
        g1 = up_s[SUBLANES - 1:SUBLANES - 1 + tm, cols]
        g2 = up_s[SUBLANES - 2:SUBLANES - 2 + tm, cols]
        if seq_len is not None:
            g1 = jnp.where(t >= 1, g1, h1_ref[:, cols])
            g2 = jnp.where(t >= 2, g2, h2_ref[:, cols])
        val = up_s[SUBLANES:SUBLANES + tm, f + c:f + c + FFN_COL_CHUNK]
        conv = g2 * cw_ref[0:1, cols]
        conv = conv + g1 * cw_ref[1:2, cols]
        conv = conv + g0 * cw_ref[2:3, cols]
        conv = conv + cb_ref[:, cols]
        act_s[:, cols] = (_silu(conv) * val).astype(BF16)

    out = x + _dot(act_s[...], wdn_ref[0])
    o_ref[...] = _rms(out, nf_ref[...]) if final_norm else out

    if seq_len is None:
        last = up_s[tm:tm + SUBLANES, 0:f]
        up_s[0:SUBLANES, 0:f] = last

        @pl.when(i == pl.num_programs(0) - 1)
        def _():
            tail_ref[...] = last
    else:
        tail_ref[...] = up_s[SUBLANES:SUBLANES + tm, 0:f]


def _ffn(x, mix_acts, mix_ws, nw, wup, cw, cb, wdn, layer, hist, seq_len, nf):
    assert FFN_CONV == 3
    m = x.shape[0]
    f = FFN_DIM
    final_norm = nf is not None
    row = lambda i: (i, 0)
    if seq_len is None:
        tm = FFN_ROW_TILE
        hist8 = jnp.zeros((SUBLANES, f), F32).at[SUBLANES - 2:].set(hist)
        extra = [hist8]
        extra_specs = [_const_spec((SUBLANES, f))]
        tail_shape = (SUBLANES, f)
    else:
        tm = m
        nb = m // seq_len
        assert seq_len >= 2 and nb * seq_len == m
        h1 = jnp.zeros((nb, seq_len, f), F32).at[:, 0].set(hist[:, 1]).reshape(m, f)
        h2 = jnp.zeros((nb, seq_len, f), F32).at[:, 0:2].set(hist).reshape(m, f)
        extra = [h1, h2]
        extra_specs = [_const_spec((m, f))] * 2
        tail_shape = (m, f)
    assert m % tm == 0
    ins = [x] + list(mix_acts) + list(mix_ws) + [nw, wup, cw, cb, wdn] + extra
    in_specs = [pl.BlockSpec((tm, D_MODEL), row)]
    in_specs += [pl.BlockSpec((tm, a.shape[1]), row) for a in mix_acts]
    once = pl.Buffered(1)
    in_specs += [pl.BlockSpec(w.shape, lambda i: (0, 0), pipeline_mode=once) for w in mix_ws]
    layer_slab = lambda w: pl.BlockSpec((1,) + w.shape[1:], lambda i: (layer, 0, 0), pipeline_mode=once)
    in_specs += [_const_spec((1, D_MODEL)), layer_slab(wup), _const_spec(cw.shape),
                 _const_spec(cb.shape), layer_slab(wdn)] + extra_specs
    if final_norm:
        ins.append(nf)
        in_specs.append(_const_spec((1, D_MODEL)))
    xo, tail = pl.pallas_call(
        functools.partial(_ffn_kernel, tm=tm, seq_len=seq_len, final_norm=final_norm, n_mix=len(mix_acts)),
        grid=(m // tm,), in_specs=in_specs,
        out_specs=[pl.BlockSpec((tm, D_MODEL), row), _const_spec(tail_shape)],
        out_shape=[jax.ShapeDtypeStruct((m, D_MODEL), F32), jax.ShapeDtypeStruct(tail_shape, F32)],
        scratch_shapes=[pltpu.VMEM((SUBLANES + tm, 2 * f), F32), pltpu.VMEM((tm, f), BF16)],
        compiler_params=_cparams(("arbitrary",)), name="conv_ffn",
    )(*ins)
    if seq_len is None:
        new_hist = tail[SUBLANES - 2:]
    else:
        new_hist = tail.reshape(m // seq_len, seq_len, f)[:, seq_len - 2:]
    return xo, new_hist


def _ssd_kernel(z_ref, xbc_ref, dt_ref, hist_ref, s0_ref, cw_ref, cb_ref, dtb_ref, alog_ref, dsk_ref,
                nw_ref, y_ref, hist_o, s_o, cbuf, st, ybuf, m1_s, xdt_s, ce_s, xw_s, bm_s, xsd_s, zs_s, el_s,
                *, q, seq_len, n_chunks, pipelined):
    c = pl.program_id(1)
    stage = (m1_s, xdt_s, ce_s, xw_s, bm_s, xsd_s, zs_s, el_s)

    def state_stage(rd):
        _ssd_state_stage(rd, nw_ref, y_ref, st, ybuf, *stage)

    def free_stage(wr):
        _ssd_free_stage(c, wr, z_ref, xbc_ref, dt_ref, cw_ref, cb_ref, dtb_ref, alog_ref, dsk_ref, hist_o, cbuf,
                        *stage, q=q, seq_len=seq_len, n_chunks=n_chunks)

    @pl.when(c == 0)
    def _():
        cbuf[0:SUBLANES, :] = hist_ref[0]
        st[...] = s0_ref[0]
        if pipelined:
            for ref in stage[:-1]:
                ref[1] = jnp.zeros(ref.shape[1:], ref.dtype)
            el_s[1] = jnp.ones(el_s.shape[1:], F32)

    if pipelined:
        @pl.when(c % 2 == 0)
        def _():
            state_stage(1)
            free_stage(0)

        @pl.when(c % 2 == 1)
        def _():
            state_stage(0)
            free_stage(1)
    else:
        free_stage(0)
        state_stage(0)

    @pl.when(c == (n_chunks if pipelined else n_chunks - 1))
    def _():
        s_o[0] = st[...]


def _ssd_state_stage(rd, nw_ref, y_ref, st, ybuf, m1_s, xdt_s, ce_s, xw_s, bm_s, xsd_s, zs_s, el_s):
    rep = SSD_HEADS // SSD_GROUPS
    hd = SSD_HEAD_DIM
    e_last = el_s[rd, 0:1, :]
    y_state = [_dot_nt(ce_s[rd, h], st[h].astype(BF16)) for h in range(SSD_HEADS)]
    s_inc = [_dot_tn(xw_s[rd, h], bm_s[rd, h // rep]) for h in range(SSD_HEADS)]
    y_intra = [_dot(m1_s[rd, h], xdt_s[rd, h]) for h in range(SSD_HEADS)]
    for h in range(SSD_HEADS):
        st[h] = st[h] * e_last[:, h:h + 1] + s_inc[h]
        ybuf[:, h * hd:(h + 1) * hd] = y_intra[h] + y_state[h]
    yz = (ybuf[...] + xsd_s[rd]) * zs_s[rd]
    gw = SSD_INNER // SSD_GROUPS
    for g in range(SSD_GROUPS):
        yg = yz[:, g * gw:(g + 1) * gw]
        yn = yg * lax.rsqrt(jnp.mean(yg * yg, axis=-1, keepdims=True) + RMS_EPS)
        y_ref[0, :, g * gw:(g + 1) * gw] = (yn * nw_ref[:, g * gw:(g + 1) * gw]).astype(BF16)


def _ssd_free_stage(c, wr, z_ref, xbc_ref, dt_ref, cw_ref, cb_ref, dtb_ref, alog_ref, dsk_ref, hist_o, cbuf,
                    m1_s, xdt_s, ce_s, xw_s, bm_s, xsd_s, zs_s, el_s, *, q, seq_len, n_chunks):
    rep = SSD_HEADS // SSD_GROUPS
    hd = SSD_HEAD_DIM
    hrows = SSD_CONV - 1
    h0 = SUBLANES - hrows
    cbuf[SUBLANES:SUBLANES + q, :] = xbc_ref[0]
    conv = cbuf[h0:h0 + q, :] * cw_ref[0:1, :]
    for j in range(1, SSD_CONV):
        conv = conv + cbuf[h0 + j:h0 + j + q, :] * cw_ref[j:j + 1, :]
    xc = _silu(conv + cb_ref[...])

    xdt_raw = dt_ref[0] + dtb_ref[...]
    dt = jnp.maximum(xdt_raw, 0.0) + jnp.log1p(jnp.exp(-jnp.abs(xdt_raw)))
    if seq_len % q != 0:
        row = jnp.minimum(c, n_chunks - 1) * q + lax.broadcasted_iota(jnp.int32, (q, 1), 0)
        dt = jnp.where(row < seq_len, dt, 0.0)
    a = -jnp.exp(alog_ref[...])
    acum = _cumsum_rows(dt * a, q, split=False)
    acum_t = acum.T
    a_last = acum[q - 1:q, :]
    e_acum = jnp.exp(acum)
    w_in = jnp.exp(a_last - acum) * dt
    el_s[wr] = jnp.broadcast_to(jnp.exp(a_last), (SUBLANES, LANES))
    tril = _tril_f32(q)

    cbs, cms = [], []
    for g in range(SSD_GROUPS):
        bm = xc[:, SSD_INNER + g * SSD_STATE:SSD_INNER + (g + 1) * SSD_STATE].astype(BF16)
        cm = xc[:, SSD_INNER + SSD_GN + g * SSD_STATE:SSD_INNER + SSD_GN + (g + 1) * SSD_STATE]
        bm_s[wr, g] = bm
        cms.append(cm)
        cbs.append(_dot_nt(cm.astype(BF16), bm))

    for h in range(SSD_HEADS):
        g = h // rep
        xs = xc[:, h * hd:(h + 1) * hd]
        seg = acum[:, h:h + 1] - acum_t[h:h + 1, :]
        decay = jnp.exp(jnp.where(tril, seg, -jnp.inf))
        m1_s[wr, h] = (cbs[g] * decay).astype(BF16)
        xdt_s[wr, h] = (xs * dt[:, h:h + 1]).astype(BF16)
        ce_s[wr, h] = (cms[g] * e_acum[:, h:h + 1]).astype(BF16)
        xw_s[wr, h] = (xs * w_in[:, h:h + 1]).astype(BF16)
        xsd_s[wr, :, h * hd:(h + 1) * hd] = xs * dsk_ref[:, h:h + 1]
    zs_s[wr] = _silu(z_ref[0])

    @pl.when(c == n_chunks - 1)
    def _():
        l_last = seq_len - (n_chunks - 1) * q
        hist_o[0] = cbuf[l_last:l_last + SUBLANES, :]

    cbuf[0:SUBLANES, :] = cbuf[q:q + SUBLANES, :]


BF16_SUBLANES = 16


def _chunk_len(seq_len, max_chunk):
    return min(max_chunk, _round_up(seq_len, BF16_SUBLANES))


def _stage_grid(n_chunks, pipelined):
    if not pipelined:
        same = lambda i, c: (i, c, 0)
        return n_chunks, same, same
    return (n_chunks + 1, lambda i, c: (i, jnp.minimum(c, n_chunks - 1), 0),
            lambda i, c: (i, jnp.maximum(c - 1, 0), 0))


def _pad_lanes(v, fill=0.0):
    return jnp.full((1, LANES), fill, F32).at[0, :v.shape[0]].set(v.astype(F32))


def _ssd(z, xbc, dtr, hist, s0, conv_w, conv_b, dt_bias, a_log, d_skip, norm_w, seq_len):
    b, lp, _ = z.shape
    q = _chunk_len(seq_len, SSD_CHUNK)
    assert lp % q == 0 and lp - seq_len < q
    hrows = SSD_CONV - 1
    hist8 = jnp.zeros((b, SUBLANES, SSD_CONV_DIM), F32).at[:, SUBLANES - hrows:].set(hist)
    cw8 = jnp.zeros((SUBLANES, SSD_CONV_DIM), F32).at[:SSD_CONV].set(conv_w)
    n_chunks = lp // q
    pipelined = False
    n_steps, seq, seq_out = _stage_grid(n_chunks, pipelined)
    per_b3 = lambda i, c: (i, 0, 0)
    per_b4 = lambda i, c: (i, 0, 0, 0)
    in_specs = [pl.BlockSpec((1, q, SSD_INNER), seq), pl.BlockSpec((1, q, SSD_CONV_DIM), seq),
                pl.BlockSpec((1, q, LANES), seq), pl.BlockSpec((1, SUBLANES, SSD_CONV_DIM), per_b3),
                pl.BlockSpec((1, SSD_HEADS, SSD_HEAD_DIM, SSD_STATE), per_b4),
                _const_spec((SUBLANES, SSD_CONV_DIM)), _const_spec((1, SSD_CONV_DIM)),
                _const_spec((1, LANES)), _const_spec((1, LANES)), _const_spec((1, LANES)),
                _const_spec((1, SSD_INNER))]
    out_shape = [jax.ShapeDtypeStruct((b, lp, SSD_INNER), BF16),
                 jax.ShapeDtypeStruct((b, SUBLANES, SSD_CONV_DIM), F32),
                 jax.ShapeDtypeStruct((b, SSD_HEADS, SSD_HEAD_DIM, SSD_STATE), F32)]
    out_specs = [pl.BlockSpec((1, q, SSD_INNER), seq_out), pl.BlockSpec((1, SUBLANES, SSD_CONV_DIM), per_b3),
                 pl.BlockSpec((1, SSD_HEADS, SSD_HEAD_DIM, SSD_STATE), per_b4)]
    per_head = pltpu.VMEM((2, SSD_HEADS, q, SSD_HEAD_DIM), BF16)
    y, hist_o, s_o = pl.pallas_call(
        functools.partial(_ssd_kernel, q=q, seq_len=seq_len, n_chunks=n_chunks, pipelined=pipelined),
        grid=(b, n_steps), in_specs=in_specs, out_specs=out_specs, out_shape=out_shape,
        scratch_shapes=[pltpu.VMEM((SUBLANES + q, SSD_CONV_DIM), F32),
                        pltpu.VMEM((SSD_HEADS, SSD_HEAD_DIM, SSD_STATE), F32),
                        pltpu.VMEM((q, SSD_INNER), F32),
                        pltpu.VMEM((2, SSD_HEADS, q, q), BF16), per_head, per_head, per_head,
                        pltpu.VMEM((2, SSD_GROUPS, q, SSD_STATE), BF16),
                        pltpu.VMEM((2, q, SSD_INNER), F32), pltpu.VMEM((2, q, SSD_INNER), F32),
                        pltpu.VMEM((2, SUBLANES, LANES), F32)],
        compiler_params=_cparams(("arbitrary", "arbitrary")), name="ssd",
    )(z, xbc, dtr, hist8, s0, cw8, conv_b.reshape(1, -1), _pad_lanes(dt_bias), _pad_lanes(a_log),
      _pad_lanes(d_skip), norm_w.reshape(1, -1))
    return y, hist_o[:, SUBLANES - hrows:], s_o


def _gla_kernel(q_ref, fx_ref, iv_ref, g_ref, gam_ref, nw_ref, s0_ref, o_ref, s_o, st_t, qd_s, kd_s, kdec_s,
                v_s, gs_s, el_s, *, q, seq_len, layer, n_chunks):
    c = pl.program_id(1)
    pipelined = n_chunks > 1
    stage = (qd_s, kd_s, kdec_s, v_s, gs_s, el_s)

    def state_stage(rd):
        _gla_state_stage(rd, nw_ref, o_ref, st_t, *stage, q=q)

    def free_stage(wr):
        _gla_free_stage(c, wr, q_ref, fx_ref, iv_ref, g_ref, gam_ref, *stage, q=q, seq_len=seq_len, layer=layer,
                        n_chunks=n_chunks)

    @pl.when(c == 0)
    def _():
        for h in range(HGRN_HEADS):
            st_t[h] = s0_ref[0, h].T
        if pipelined:
            for ref in stage[:-1]:
                ref[1] = jnp.zeros(ref.shape[1:], ref.dtype)
            el_s[1] = jnp.ones(el_s.shape[1:], F32)

    if pipelined:
        @pl.when(c % 2 == 0)
        def _():
            state_stage(1)
            free_stage(0)

        @pl.when(c % 2 == 1)
        def _():
            state_stage(0)
            free_stage(1)
    else:
        free_stage(0)
        state_stage(0)

    @pl.when(c == (n_chunks if pipelined else n_chunks - 1))
    def _():
        for h in range(HGRN_HEADS):
            s_o[0, h] = st_t[h].T


def _gla_state_stage(rd, nw_ref, o_ref, st_t, qd_s, kd_s, kdec_s, v_s, gs_s, el_s, *, q):
    tril = _tril_f32(q)
    ks = [slice(h * HGRN_KEY_DIM, (h + 1) * HGRN_KEY_DIM) for h in range(HGRN_HEADS)]
    vs = [slice(h * HGRN_VAL_DIM, (h + 1) * HGRN_VAL_DIM) for h in range(HGRN_HEADS)]
    qd_b = [qd_s[rd, :, s] for s in ks]
    v_b = [v_s[rd, :, s] for s in vs]
    e_last = el_s[rd, 0:1, :]
    att = [_dot_nt(qd_b[h], kd_s[rd, :, ks[h]]) for h in range(HGRN_HEADS)]
    o_state = [_dot_nt(qd_b[h], st_t[h].astype(BF16)) for h in range(HGRN_HEADS)]
    kv = [_dot_tn(v_b[h], kdec_s[rd, :, ks[h]]) for h in range(HGRN_HEADS)]
    for h in range(HGRN_HEADS):
        o = _dot(jnp.where(tril, att[h], 0.0).astype(BF16), v_b[h]) + o_state[h]
        st_t[h] = st_t[h] * e_last[:, ks[h]] + kv[h]
        on = o * lax.rsqrt(jnp.mean(o * o, axis=-1, keepdims=True) + RMS_EPS) * nw_ref[...]
        o_ref[0, :, vs[h]] = (on * gs_s[rd, :, vs[h]]).astype(BF16)


def _gla_free_stage(c, wr, q_ref, fx_ref, iv_ref, g_ref, gam_ref, qd_s, kd_s, kdec_s, v_s, gs_s, el_s, *, q,
                    seq_len, layer, n_chunks):
    rows = [gam_ref[l:l + 1, :] for l in range(DEPTH)]
    mx = functools.reduce(jnp.maximum, rows)
    es = [jnp.exp(r - mx) for r in rows]
    lb = sum(es[1:layer + 1]) / sum(es) if layer >= 1 else jnp.zeros_like(mx)

    f = lb + (1.0 - lb) * jax.nn.sigmoid(fx_ref[0])
    if seq_len % q != 0:
        row = jnp.minimum(c, n_chunks - 1) * q + lax.broadcasted_iota(jnp.int32, (q, 1), 0)
        f = jnp.where(row < seq_len, f, 1.0)
    kk = 1.0 - f
    b = _cumsum_rows(jnp.log(f), q, split=True)
    b_last = b[q - 1:q, :]
    qd_s[wr] = (q_ref[0] * jnp.exp(b)).astype(BF16)
    kd_s[wr] = (kk * jnp.exp(-b)).astype(BF16)
    kdec_s[wr] = (kk * jnp.exp(b_last - b)).astype(BF16)
    v_s[wr] = iv_ref[0].astype(BF16)
    gs_s[wr] = _silu(g_ref[0])
    el_s[wr] = jnp.broadcast_to(jnp.exp(b_last), (SUBLANES, HGRN_K))


def _gla(qa, fx, iv, g, gamma, norm_w, s0, seq_len, layer):
    b, lp, _ = qa.shape
    q = _chunk_len(seq_len, HGRN_CHUNK)
    assert lp % q == 0 and lp - seq_len < q
    gam8 = jnp.zeros((SUBLANES, HGRN_K), F32).at[:DEPTH].set(gamma)
    n_chunks = lp // q
    n_steps, seq, seq_out = _stage_grid(n_chunks, n_chunks > 1)
    per_b4 = lambda i, c: (i, 0, 0, 0)
    st_shape = (HGRN_HEADS, HGRN_KEY_DIM, HGRN_VAL_DIM)
    in_specs = [pl.BlockSpec((1, q, HGRN_K), seq), pl.BlockSpec((1, q, HGRN_K), seq),
                pl.BlockSpec((1, q, HGRN_V), seq), pl.BlockSpec((1, q, HGRN_V), seq),
                _const_spec((SUBLANES, HGRN_K)), _const_spec((1, HGRN_VAL_DIM)),
                pl.BlockSpec((1,) + st_shape, per_b4)]
    stage_k = pltpu.VMEM((2, q, HGRN_K), BF16)
    o, s_o = pl.pallas_call(
        functools.partial(_gla_kernel, q=q, seq_len=seq_len, layer=layer, n_chunks=n_chunks),
        grid=(b, n_steps), in_specs=in_specs,
        out_specs=[pl.BlockSpec((1, q, HGRN_V), seq_out), pl.BlockSpec((1,) + st_shape, per_b4)],
        out_shape=[jax.ShapeDtypeStruct((b, lp, HGRN_V), BF16), jax.ShapeDtypeStruct((b,) + st_shape, F32)],
        scratch_shapes=[pltpu.VMEM((HGRN_HEADS, HGRN_VAL_DIM, HGRN_KEY_DIM), F32),
                        stage_k, stage_k, stage_k, pltpu.VMEM((2, q, HGRN_V), BF16),
                        pltpu.VMEM((2, q, HGRN_V), F32), pltpu.VMEM((2, SUBLANES, HGRN_K), F32)],
        compiler_params=_cparams(("arbitrary", "arbitrary")), name="gla",
    )(qa, fx, iv, g, gam8, norm_w.reshape(1, -1), s0)
    return o, s_o


def _select_topk_rows(gate_t, n_valid, nblk, width):
    blk = lax.broadcasted_iota(jnp.int32, (nblk, width), 0)
    blk_f = blk.astype(F32)
    g = jnp.where(blk < n_valid, gate_t, -jnp.inf)
    sel = jnp.zeros((nblk, width), F32)
    for _ in range(MOBA_TOPK):
        mx = jnp.max(g, axis=0, keepdims=True)
        first = jnp.min(jnp.where(g == mx, blk_f, float(nblk)), axis=0, keepdims=True)
        pick = (blk_f == first) & (mx > -jnp.inf)
        sel = jnp.where(pick, 1.0, sel)
        g = jnp.where(pick, -jnp.inf, g)
    return jnp.where(sel > 0.0, 0.0, -jnp.inf)


def _key_aug_columns(tm):
    lane = lax.broadcasted_iota(jnp.int32, (tm, LANES), 1)
    r = lax.broadcasted_iota(jnp.int32, (tm, LANES), 0).astype(F32)
    return jnp.where((lane >= MOBA_HEAD_DIM) & (lane < MOBA_HEAD_DIM + ALIBI_PARTS), r, 0.0)


def _moba_prompt_kernel(slopes_ref, kn_ref, qn_ref, dmin_ref, knt_ref, q_ref, ka_ref, vt_ref, km_ref, o_ref,
                        brow_s, ubuf, *, nblk):
    hp = pl.program_id(0)
    i = pl.program_id(1)
    tq = MOBA_BLOCK
    d = MOBA_HEAD_DIM
    q = q_ref[...]
    lane_k = lax.broadcasted_iota(jnp.int32, (nblk, 2 * d), 1)
    causal =(lax.broadcasted_iota(jnp.int32, (MOBA_BLOCK, tq), 0)
              <= lax.broadcasted_iota(jnp.int32, (MOBA_BLOCK, tq), 1))
    blk_f = lax.broadcasted_iota(jnp.int32, (nblk, tq), 0).astype(F32)
    col_f = lax.broadcasted_iota(jnp.int32, (nblk, tq), 1).astype(F32)
    aug_row = lax.broadcasted_iota(jnp.int32, (d, tq), 0)
    i_f = i.astype(F32)

    q_t = q.T
    km = km_ref[...]
    km2 = jnp.concatenate([jnp.where(lane_k < d, km, 0.0), jnp.where(lane_k >= d, km, 0.0)], axis=0)
    gate_both = _dot_split(km2, q_t)

    rhs = []
    for j in range(2):
        a2 = slopes_ref[2 * hp + j] * LOG2E
        sel = _select_topk_rows(gate_both[j * nblk:(j + 1) * nblk, :], i, nblk, tq)
        brow_s[j] = sel + a2 * (MOBA_BLOCK * (blk_f - i_f) - col_f)
        rest = jnp.full((d, tq), a2, F32)
        aug = jnp.zeros((d, tq), F32)
        for part in range(ALIBI_PARTS):
            term = rest.astype(BF16).astype(F32)
            aug = jnp.where(aug_row == part, term, aug)
            rest = rest - term
        q_h = q_t[j * d:(j + 1) * d, :] * (d ** -0.5 * LOG2E)
        rhs.append(jnp.concatenate([q_h, aug], axis=0).astype(BF16))

    def block_update(st, u, b_row, v_t):
        m_run, o_t = st
        m_new = jnp.maximum(m_run, jnp.max(u, axis=0, keepdims=True) + b_row)
        alpha = jnp.exp2(m_run - m_new)
        p = jnp.exp2(u - (m_new - b_row))
        return m_new, alpha * o_t + _dot(v_t, p.astype(BF16))

    def score(n, slot):
        for j in range(2):
            ubuf[slot, j] = _dot(ka_ref[j, n], rhs[j])

    def consume(n, slot, sts, own):
        new = []
        for j in range(2):
            u = ubuf[slot, j]
            if own:
                u = jnp.where(causal, u, -jnp.inf)
                b_row = -(slopes_ref[2 * hp + j] * LOG2E) * col_f[0:1, :]
            else:
                b_row = brow_s[j, pl.ds(n, 1), :]
            new.append(block_update(sts[j], u, b_row, vt_ref[n, j * MOBA_VT_ROWS:(j + 1) * MOBA_VT_ROWS, :]))
        return tuple(new)

    def step(n, slot, slot_ahead, sts):
        score(jnp.minimum(n + MOBA_AHEAD, i), slot_ahead)
        return consume(n, slot, sts, False)

    def group(base, unroll, sts):
        for k in range(unroll):
            sts = step(base + k, k % MOBA_SLOTS, (k + MOBA_AHEAD) % MOBA_SLOTS, sts)
        return sts

    def first_needed(j):
        hd = 2 * hp + j
        a2 = slopes_ref[hd] * LOG2E
        c = d ** -0.5 * LOG2E
        qn = qn_ref[i * MOBA_HEADS + hd]
        floor = c * (dmin_ref[i * MOBA_HEADS + hd] - BOUND_SLACK * qn * kn_ref[i * MOBA_HEADS + hd])

        blk_row = lax.broadcasted_iota(jnp.int32, (1, nblk), 1).astype(F32)
        ceil = (c * (1.0 + BOUND_SLACK) * qn * knt_ref[pl.ds(hd, 1), :]
                + a2 * (MOBA_BLOCK * (blk_row - i_f) + (MOBA_BLOCK - 1)))
        needed = (blk_row < i_f) & (ceil >= floor - UNDERFLOW_LOG2)
        return jnp.min(jnp.where(needed, blk_row, i_f)).astype(jnp.int32)

    start = (jnp.minimum(first_needed(0), first_needed(1)) // MOBA_SLOTS) * MOBA_SLOTS

    init = (jnp.full((1, tq), NEG_BIG, F32), jnp.zeros((MOBA_VT_ROWS, tq), F32))
    for k in range(MOBA_AHEAD):
        score(jnp.minimum(start + k, i), k)
    n_big = (i - start) // MOBA_UNROLL
    sts = lax.fori_loop(0, n_big, lambda g, s: group(start + g * MOBA_UNROLL, MOBA_UNROLL, s), (init, init))
    base = start + n_big * MOBA_UNROLL
    n_small = (i - base) // MOBA_SLOTS
    sts = lax.fori_loop(0, n_small, lambda g, s: group(base + g * MOBA_SLOTS, MOBA_SLOTS, s), sts)
    base = base + n_small * MOBA_SLOTS
    rem = i - base
    for k in range(MOBA_SLOTS - 1):
        sts = lax.cond(rem > k, lambda s, k=k: step(base + k, k, (k + MOBA_AHEAD) % MOBA_SLOTS, s),
                       lambda s: s, sts)
    outs = []
    for _, o_t in consume(i, rem, sts, True):
        outs.append(o_t[0:d, :] / o_t[d:d + 1, :])
    o_ref[...] = jnp.concatenate(outs, axis=0).T.astype(BF16)


def _alibi_slopes():
    return jnp.asarray(np.exp2(-8.0 * np.arange(1, MOBA_HEADS + 1) / MOBA_HEADS), dtype=F32)


def _moba_prompt(q, ka, vt, km, bounds):
    s = q.shape[0]
    nblk = s // MOBA_BLOCK
    assert nblk * MOBA_BLOCK == s and 2 * MOBA_HEAD_DIM == LANES
    pairs = MOBA_HEADS // 2
    tables = [bounds[:, r, :MOBA_HEADS].reshape(-1) for r in range(3)]
    kn_t = bounds[:, 0, :MOBA_HEADS].T
    smem = pl.BlockSpec(memory_space=pltpu.SMEM)
    return pl.pallas_call(
        functools.partial(_moba_prompt_kernel, nblk=nblk),
        grid=(pairs, nblk),
        in_specs=[smem, smem, smem, smem, _const_spec((MOBA_HEADS, nblk)),
                  pl.BlockSpec((MOBA_BLOCK, LANES), lambda hp, i: (i, hp)),
                  pl.BlockSpec((2, nblk, MOBA_BLOCK, LANES), lambda hp, i: (hp, 0, 0, 0)),
                  pl.BlockSpec((nblk, 2 * MOBA_VT_ROWS, MOBA_BLOCK), lambda hp, i: (0, hp, 0)),
                  pl.BlockSpec((nblk, LANES), lambda hp, i: (0, hp))],
        out_specs=pl.BlockSpec((MOBA_BLOCK, LANES), lambda hp, i: (i, hp)),
        out_shape=jax.ShapeDtypeStruct((s, MOBA_INNER), BF16),
        scratch_shapes=[pltpu.VMEM((2, nblk, MOBA_BLOCK), F32),
                        pltpu.VMEM((MOBA_SLOTS, 2, MOBA_BLOCK, MOBA_BLOCK), F32)],
        compiler_params=_cparams(("arbitrary", "arbitrary")), name="moba_prompt",
    )(_alibi_slopes(), *tables, kn_t, q, ka, vt, km)


PAGES_PER_STEP = 64
SAMPLE_AHEAD = 3
SAMPLE_BUFS = SAMPLE_AHEAD + 1


def _kmean_kernel(pt_ref, *refs):
    del pt_ref
    o_ref = refs[-1]
    s = pl.program_id(1)
    ppb = MOBA_BLOCK // PAGE_SIZE
    bps = PAGES_PER_STEP // ppb

    @pl.when(s == 0)
    def _():
        o_ref[...] = jnp.zeros(o_ref.shape, F32)

    acc = o_ref[0]
    lane = lax.broadcasted_iota(jnp.int32, acc.shape, 2)
    for m in range(bps):
        x = refs[m * ppb][0]
        for p in range(1, ppb):
            x = x + refs[m * ppb + p][0]
        mean = jnp.sum(x, axis=-1, keepdims=True) * (1.0 / MOBA_BLOCK)
        acc = jnp.where(lane == s * bps + m, mean, acc)
    o_ref[0] = acc


def _sample_kmean(k_pool_t, page_table):
    db, n_pages = page_table.shape
    ppb = MOBA_BLOCK // PAGE_SIZE
    n_full = n_pages // ppb
    assert n_full * ppb == n_pages and n_pages % PAGES_PER_STEP == 0
    steps = n_pages // PAGES_PER_STEP
    blk = (1, MOBA_HEADS, MOBA_HEAD_DIM, PAGE_SIZE)

    def page_spec(p):
        return pl.BlockSpec(blk, lambda b, s, pt: (pt[b * n_pages + s * PAGES_PER_STEP + p], 0, 0, 0))

    out_blk = (1, MOBA_HEADS, MOBA_HEAD_DIM, n_full)
    grid_spec = pltpu.PrefetchScalarGridSpec(
        num_scalar_prefetch=1, grid=(db, steps),
        in_specs=[page_spec(p) for p in range(PAGES_PER_STEP)],
        out_specs=pl.BlockSpec(out_blk, lambda b, s, pt: (b, 0, 0, 0)))
    return pl.pallas_call(
        _kmean_kernel, grid_spec=grid_spec,
        out_shape=jax.ShapeDtypeStruct((db,) + out_blk[1:], F32),
        compiler_params=_cparams(("arbitrary", "arbitrary")), name="sample_kmean",
    )(page_table.reshape(-1), *([k_pool_t] * PAGES_PER_STEP))


def _sample_select_kernel(q_ref, km_ref, idx_ref, *, n_full):
    t8 = q_ref.shape[2]
    blk = lax.broadcasted_iota(jnp.int32, (t8, n_full), 1).astype(F32)
    lane = lax.broadcasted_iota(jnp.int32, (t8, LANES), 1)
    for h in range(MOBA_HEADS):
        g = jnp.dot(q_ref[0, h], km_ref[0, h], precision=HIGHEST, preferred_element_type=F32)
        out = jnp.zeros((t8, LANES), jnp.int32)
        for k in range(MOBA_TOPK):
            mx = jnp.max(g, axis=-1, keepdims=True)
            first = jnp.min(jnp.where(g == mx, blk, float(n_full)), axis=-1, keepdims=True)
            out = jnp.where(lane == k, first.astype(jnp.int32), out)
            g = jnp.where(blk == first, -jnp.inf, g)
        idx_ref[0, h] = out


def _sample_select(qh, km):
    db, h, t8, d = qh.shape
    n_full = km.shape[3]
    assert n_full >= MOBA_TOPK
    b4 = lambda b: (b, 0, 0, 0)
    return pl.pallas_call(
        functools.partial(_sample_select_kernel, n_full=n_full), grid=(db,),
        in_specs=[pl.BlockSpec((1, h, t8, d), b4), pl.BlockSpec((1, h, d, n_full), b4)],
        out_specs=pl.BlockSpec((1, h, t8, LANES), b4),
        out_shape=jax.ShapeDtypeStruct((db, h, t8, LANES), jnp.int32),
        compiler_params=_cparams(("arbitrary",)), name="sample_select",
    )(qh, km)


def _sample_attn_kernel(idx_ref, phys_ref, slopes_ref, q_ref, kn_ref, vn_ref, kpool, vpool, o_ref,
                        kbuf, vbuf, sem, *, ds, past):
    ppb = MOBA_BLOCK // PAGE_SIZE
    n_sel = MOBA_TOPK * ppb
    n_pg = ds * n_sel
    b = pl.program_id(0)
    h = pl.program_id(1)
    n_heads = pl.num_programs(1)
    step = b * n_heads + h
    n_steps = pl.num_programs(0) * n_heads

    def page_copies(st, half):
        hh = st % n_heads
        out = []
        for k in range(n_pg):
            page = phys_ref[st * n_pg + k]
            out.append(pltpu.make_async_copy(kpool.at[page, hh], kbuf.at[half, k], sem.at[0, half]))
            out.append(pltpu.make_async_copy(vpool.at[page, hh], vbuf.at[half, k], sem.at[1, half]))
        return out

    @pl.when(step == 0)
    def _():
        for st in range(SAMPLE_AHEAD):
            for cp in page_copies(st, st):
                cp.start()

    @pl.when(step + SAMPLE_AHEAD < n_steps)
    def _():
        for cp in page_copies(step + SAMPLE_AHEAD, (step + SAMPLE_AHEAD) % SAMPLE_BUFS):
            cp.start()

    half = step % SAMPLE_BUFS
    for cp in page_copies(step, half):
        cp.wait()
    k_refs = [kbuf.at[half, k] for k in range(n_pg)]
    v_refs = [vbuf.at[half, k] for k in range(n_pg)]
    slope = slopes_ref[h]
    t8 = q_ref.shape[3]
    c_page = lax.broadcasted_iota(jnp.int32, (1, PAGE_SIZE), 1).astype(F32)
    c_new = lax.broadcasted_iota(jnp.int32, (1, t8), 1)
    qf = q_ref[0, 0] * (MOBA_HEAD_DIM ** -0.5)
    k_new = kn_ref[0, 0]
    v_new = vn_ref[0, 0]
    o_ref[0, 0] = jnp.zeros((MOBA_HEAD_DIM, t8), F32)
    for t in range(ds):
        q_t = qf[:, t:t + 1]
        t_pos = float(past + t)
        scores = []
        for sp in range(n_sel):
            slot, p = divmod(sp, ppb)
            blk_idx = idx_ref[((b * MOBA_HEADS + h) * ds + t) * MOBA_TOPK + slot]
            pos0 = (blk_idx * MOBA_BLOCK + p * PAGE_SIZE).astype(F32)
            s = jnp.sum(k_refs[t * n_sel + sp][...] * q_t, axis=0, keepdims=True)
            scores.append(s - slope * (t_pos - (pos0 + c_page)))
        s_new = jnp.sum(k_new * q_t, axis=0, keepdims=True)
        s_new = s_new - slope * (t_pos - (float(past) + c_new.astype(F32)))
        s_new = jnp.where(c_new <= t, s_new, -jnp.inf)
        mx = jnp.max(s_new, axis=-1, keepdims=True)
        for s in scores:
            mx = jnp.maximum(mx, jnp.max(s, axis=-1, keepdims=True))
        p_new = jnp.exp(s_new - mx)
        l = jnp.sum(p_new, axis=-1, keepdims=True)
        o = jnp.sum(p_new * v_new, axis=-1, keepdims=True)
        acc = jnp.zeros((MOBA_HEAD_DIM, PAGE_SIZE), F32)
        for sp in range(n_sel):
            pr = jnp.exp(scores[sp] - mx)
            l = l + jnp.sum(pr, axis=-1, keepdims=True)
            acc = acc + pr * v_refs[t * n_sel + sp][...]
        o = o + jnp.sum(acc, axis=-1, keepdims=True)
        o_ref[0, 0, :, t:t + 1] = o / l


def _sample_attn(qh, kh, vh, idx, k_pool, v_pool, page_table, ds):
    db, h, d, t8 = qh.shape
    n_pages = page_table.shape[1]
    ppb = MOBA_BLOCK // PAGE_SIZE
    assert n_pages % ppb == 0
    past = n_pages * PAGE_SIZE
    n_sel = MOBA_TOPK * ppb
    logical = idx[..., None] * ppb + jnp.arange(ppb, dtype=jnp.int32)
    phys = page_table[jnp.arange(db)[:, None, None, None], logical.reshape(db, h, ds, n_sel)]
    tok = pl.BlockSpec((1, 1, d, t8), lambda b, hh, *_: (b, hh, 0, 0))
    hbm = pl.BlockSpec(memory_space=pl.ANY)
    assert db * h >= SAMPLE_AHEAD
    page_buf = pltpu.VMEM((SAMPLE_BUFS, ds * n_sel, d, PAGE_SIZE), F32)
    grid_spec = pltpu.PrefetchScalarGridSpec(
        num_scalar_prefetch=2, grid=(db, h),
        in_specs=[pl.BlockSpec(memory_space=pltpu.SMEM), tok, tok, tok, hbm, hbm],
        out_specs=tok,
        scratch_shapes=[page_buf, page_buf, pltpu.SemaphoreType.DMA((2, SAMPLE_BUFS))])
    return pl.pallas_call(
        functools.partial(_sample_attn_kernel, ds=ds, past=past), grid_spec=grid_spec,
        out_shape=jax.ShapeDtypeStruct((db, h, d, t8), F32),
        compiler_params=_cparams(("arbitrary", "arbitrary")), name="sample_attn",
    )(idx.reshape(-1), phys.reshape(-1), _alibi_slopes(), qh, kh, vh, k_pool, v_pool)


def _split_w_in_a(w, f32_query):
    wb = w.astype(BF16)
    wdt = jnp.zeros((D_MODEL, LANES), BF16).at[:, :SSD_HEADS].set(wb[:, OFF_DT:OFF_Q])
    wq = w[:, OFF_Q:OFF_K] if f32_query else wb[:, OFF_Q:OFF_K]
    return [wb[:, :OFF_XBC], wb[:, OFF_XBC:OFF_DT], wdt, wq, wb[:, OFF_K:OFF_V], wb[:, OFF_V:]]


def _pad_seq(a, b, l, lp):
    a = a.reshape(b, l, a.shape[-1])
    return a if lp == l else jnp.pad(a, ((0, 0), (0, lp - l), (0, 0)))


def _round_up(n, m):
    return -(-n // m) * m


def _heads(a, b, l):
    return a.reshape(b, l, MOBA_HEADS, MOBA_HEAD_DIM).transpose(0, 2, 1, 3)


def _pad_tokens(a):
    l = a.shape[-2]
    return jnp.pad(a, ((0, 0),) * (a.ndim - 2) + ((0, _round_up(l, SUBLANES) - l), (0, 0)))


def _trunk(x, bsz, length, prompt, caches, p):
    ssm_conv0, ssm0, hgrn0, ffn_conv0 = caches[:4]
    m = bsz * length
    k_rows = v_rows = ssm_c = ssm_s = hgrn_s = None
    ffn_c = []
    for layer in range(DEPTH):
        nw = p['norm_mix'][layer].reshape(1, -1)
        if layer % 2 == 0:
            ia = layer // 2
            assert ia == 0
            ws = _split_w_in_a(p['w_in_a'][ia], f32_query=not prompt)
            outs = _inproj_a(x, nw, ws, prompt)
            z, xbc, dtr, q = outs[:4]
            lp = _round_up(length, _chunk_len(length, SSD_CHUNK))
            y_ssd, hist, s_new = _ssd(
                _pad_seq(z, bsz, length, lp), _pad_seq(xbc, bsz, length, lp), _pad_seq(dtr, bsz, length, lp),
                ssm_conv0[ia], ssm0[ia], p['ssd_conv_w'][ia], p['ssd_conv_b'][ia], p['ssd_dt_bias'][ia],
                p['ssd_a_log'][ia], p['ssd_d'][ia], p['ssd_norm_w'][ia], length)
            y_ssd = y_ssd[:, :length].reshape(m, SSD_INNER)
            if prompt:
                assert bsz == 1
                kr, vr, ka, vt, km, bounds = outs[4:]
                o_att = _moba_prompt(q, ka, vt, km.reshape(-1, MOBA_INNER), bounds)
                k_rows, v_rows = jnp.swapaxes(kr, -1, -2)[None], jnp.swapaxes(vr, -1, -2)[None]
            else:
                k, v = outs[4:]
                k_pool, v_pool, page_table = caches[4:]
                k_pool_t = jnp.swapaxes(k_pool[ia], -1, -2)
                v_pool_t = jnp.swapaxes(v_pool[ia], -1, -2)
                qh, kh, vh = (_pad_tokens(_heads(t, bsz, length)) for t in (q, k, v))
                qt, kt, vt = (jnp.swapaxes(t, -1, -2) for t in (qh, kh, vh))
                km = _sample_kmean(k_pool_t, page_table)
                idx = _sample_select(qh, km)[:, :, :length, :MOBA_TOPK]
                o = _sample_attn(qt, kt, vt, idx, k_pool_t, v_pool_t, page_table, length)
                o_att = o[..., :length].transpose(0, 3, 1, 2).reshape(m, MOBA_INNER).astype(BF16)
                k_rows, v_rows = kh[:, :, :length], vh[:, :, :length]
            wo = p['w_out_a'][ia].astype(BF16)
            mix_acts, mix_ws = [y_ssd, o_att], [wo[:SSD_INNER], wo[SSD_INNER:]]
            ssm_c, ssm_s = hist, s_new
        else:
            ic = layer // 2
            assert ic == 0
            wc = p['w_in_c'][ic].astype(BF16)
            ws = [wc[:, :HGRN_K], wc[:, HGRN_K:2 * HGRN_K], wc[:, 2 * HGRN_K:2 * HGRN_K + HGRN_V],
                  wc[:, 2 * HGRN_K + HGRN_V:]]
            qa, fx, iv, g = _norm_matmul(x, nw, ws)
            lp = _round_up(length, _chunk_len(length, HGRN_CHUNK))
            o, s_new = _gla(*(_pad_seq(t, bsz, length, lp) for t in (qa, fx, iv, g)),
                            p['hgrn_lb_gamma'], p['hgrn_norm_w'][ic], hgrn0[ic], length, layer)
            mix_acts, mix_ws = [o[:, :length].reshape(m, HGRN_V)], [p['w_out_c'][ic].astype(BF16)]
            hgrn_s = s_new
        nf = p['norm_final'].reshape(1, -1) if layer == DEPTH - 1 else None
        hist = ffn_conv0[layer, 0] if prompt else ffn_conv0[layer]
        x, fh = _ffn(x, mix_acts, mix_ws, p['norm_ffn'][layer].reshape(1, -1),
                     p['ffn_w_up'].astype(BF16), p['ffn_conv_w'][layer],
                     p['ffn_conv_b'][layer].reshape(1, -1), p['ffn_w_down'].astype(BF16), layer, hist,
                     None if prompt else length, nf)
        ffn_c.append(fh[None] if prompt else fh)
    return (x.reshape(bsz, length, D_MODEL), k_rows[None], v_rows[None], ssm_s[None], ssm_c[None],
            hgrn_s[None], jnp.stack(ffn_c))


def kernel(x_prompt, x_sample, cache_k_pool, cache_v_pool, page_table, state_ssm, state_ssm_conv, state_hgrn,
           state_ffn_conv, norm_mix, norm_ffn, norm_final, w_in_a, w_out_a, ssd_conv_w, ssd_conv_b,
           ssd_dt_bias, ssd_a_log, ssd_d, ssd_norm_w, w_in_c, w_out_c, hgrn_lb_gamma, hgrn_norm_w, ffn_w_up,
           ffn_conv_w, ffn_conv_b, ffn_w_down):
    p = dict(norm_mix=norm_mix, norm_ffn=norm_ffn, norm_final=norm_final, w_in_a=w_in_a, w_out_a=w_out_a,
             ssd_conv_w=ssd_conv_w, ssd_conv_b=ssd_conv_b, ssd_dt_bias=ssd_dt_bias, ssd_a_log=ssd_a_log,
             ssd_d=ssd_d, ssd_norm_w=ssd_norm_w, w_in_c=w_in_c, w_out_c=w_out_c, hgrn_lb_gamma=hgrn_lb_gamma,
             hgrn_norm_w=hgrn_norm_w, ffn_w_up=ffn_w_up, ffn_conv_w=ffn_conv_w, ffn_conv_b=ffn_conv_b,
             ffn_w_down=ffn_w_down)
    bp, sp, _ = x_prompt.shape
    db, ds, _ = x_sample.shape
    na, nc = (DEPTH + 1) // 2, DEPTH // 2
    zeros = lambda *s: jnp.zeros(s, F32)
    prompt_caches = (zeros(na, bp, SSD_CONV - 1, SSD_CONV_DIM), zeros(na, bp, SSD_HEADS, SSD_HEAD_DIM, SSD_STATE),
                     zeros(nc, bp, HGRN_HEADS, HGRN_KEY_DIM, HGRN_VAL_DIM), zeros(DEPTH, bp, FFN_CONV - 1, FFN_DIM))
    outs_p = _trunk(x_prompt.reshape(bp * sp, D_MODEL), bp, sp, True, prompt_caches, p)
    sample_caches = (state_ssm_conv, state_ssm, state_hgrn, state_ffn_conv, cache_k_pool, cache_v_pool, page_table)
    outs_s = _trunk(x_sample.reshape(db * ds, D_MODEL), db, ds, False, sample_caches, p)
    yp, krp, vrp, ssp, scp, hgp, fcp = outs_p
    ys, krs, vrs, sss, scs, hgs, fcs = outs_s
    return (yp, ys, krp, vrp, ssp, scp, hgp, fcp, krs, vrs, sss, scs, hgs, fcs)
```

```python
import functools
import math

import numpy as np
import jax
import jax.numpy as jnp
from jax import lax
from jax.experimental import pallas as pl
from jax.experimental.pallas import tpu as pltpu

F32 = jnp.float32
BF16 = jnp.bfloat16
HIGHEST = lax.Precision.HIGHEST

D_MODEL = 1024
DEPTH = 2
PAGE_SIZE = 128
SSD_HEADS = 8
SSD_HEAD_DIM = 64
SSD_INNER = SSD_HEADS * SSD_HEAD_DIM
SSD_STATE = 64
SSD_GROUPS = 2
SSD_CONV = 4
SSD_CHUNK = 128
SSD_GN = SSD_GROUPS * SSD_STATE
SSD_CONV_DIM = SSD_INNER + 2 * SSD_GN
MOBA_HEADS = 8
MOBA_HEAD_DIM = 64
MOBA_INNER = MOBA_HEADS * MOBA_HEAD_DIM
MOBA_BLOCK = 256
MOBA_TOPK = 3
OFF_XBC = SSD_INNER
OFF_DT = OFF_XBC + SSD_CONV_DIM
OFF_Q = OFF_DT + SSD_HEADS
OFF_K = OFF_Q + MOBA_INNER
OFF_V = OFF_K + MOBA_INNER
HGRN_HEADS = 8
HGRN_KEY_DIM = 128
HGRN_VAL_DIM = D_MODEL // HGRN_HEADS
HGRN_K = HGRN_HEADS * HGRN_KEY_DIM
HGRN_V = HGRN_HEADS * HGRN_VAL_DIM
HGRN_CHUNK = 64
FFN_DIM = 2816
FFN_CONV = 3
RMS_EPS = 1e-6

LANES = 128
SUBLANES = 8
VMEM_LIMIT_BYTES = 56 * 1024 * 1024

ROW_TILE = 256
FFN_ROW_TILE = 512
FFN_COL_CHUNK = 256
NEG_BIG = -1e30
LOG2E = math.log2(math.e)
ALIBI_PARTS = 3
MOBA_VT_ROWS = MOBA_HEAD_DIM + 16
BOUND_SLACK = 0.02
UNDERFLOW_LOG2 = 160.0
MOBA_AHEAD = 2
MOBA_SLOTS = MOBA_AHEAD + 1
MOBA_UNROLL = 4 * MOBA_SLOTS


def _cparams(sem, flags=None):
    return pltpu.CompilerParams(dimension_semantics=sem, vmem_limit_bytes=VMEM_LIMIT_BYTES, flags=flags)


def _const_spec(shape):
    nd = len(shape)
    return pl.BlockSpec(shape, lambda *_: (0,) * nd)


def _rms(x, w):
    y = x * lax.rsqrt(jnp.mean(x * x, axis=-1, keepdims=True) + RMS_EPS)
    return y * w


def _rms_bf16(x, w):
    return _rms(x, w).astype(BF16)


def _dot(a, b):
    return jnp.dot(a, b, preferred_element_type=F32)


def _dot_split(a, b):
    a_hi = a.astype(BF16)
    a_lo = (a - a_hi.astype(F32)).astype(BF16)
    b_hi = b.astype(BF16)
    b_lo = (b - b_hi.astype(F32)).astype(BF16)
    return _dot(jnp.concatenate([a_hi, a_lo, a_hi], axis=1), jnp.concatenate([b_hi, b_hi, b_lo], axis=0))


def _dot_nt(a, b):
    return lax.dot_general(a, b, (((1,), (1,)), ((), ())), preferred_element_type=F32)


def _dot_tn(a, b):
    return lax.dot_general(a, b, (((0,), (0,)), ((), ())), preferred_element_type=F32)


def _tril_f32(n):
    r = lax.broadcasted_iota(jnp.int32, (n, n), 0)
    c = lax.broadcasted_iota(jnp.int32, (n, n), 1)
    return r >= c


def _cumsum_rows(x, q, split):
    if not split:
        return jnp.dot(_tril_f32(q).astype(F32), x, precision=HIGHEST, preferred_element_type=F32)
    tri = _tril_f32(q).astype(BF16)
    acc = None
    rest = x
    for _ in range(3):
        term = rest.astype(BF16)
        part = _dot(tri, term)
        acc = part if acc is None else acc + part
        rest = rest - term.astype(F32)
    return acc


def _silu(x):
    return x * jax.nn.sigmoid(x)


def _inproj_a_kernel(x_ref, nw_ref, wz, wxbc, wdt, wq, wk, wv, *rest, prompt):
    outs = rest[1:] if prompt else rest
    h = _rms(x_ref[...], nw_ref[...])
    hb = h.astype(BF16)
    z_o, xbc_o, dt_o, q_o = outs[:4]
    z_o[...] = _dot(hb, wz[...])
    xbc_o[...] = _dot(hb, wxbc[...])
    dt_o[...] = _dot(hb, wdt[...])
    if wq.dtype == F32:
        q = jnp.dot(h, wq[...], precision=HIGHEST, preferred_element_type=F32)
    else:
        q = _dot(hb, wq[...])
    q_o[...] = q
    k = _dot(hb, wk[...])
    v = _dot(hb, wv[...])
    if not prompt:
        k_o, v_o = outs[4:]
        k_o[...] = k
        v_o[...] = v
        return
    kr_o, vr_o, ka_o, vt_o, km_o, bnd_o = outs[4:]
    tm = k.shape[0]
    head_sel = rest[0][...]
    k_norm = jnp.sqrt(jnp.max(_dot((k * k).astype(BF16), head_sel), axis=0, keepdims=True))
    q_norm = jnp.sqrt(jnp.max(_dot((q * q).astype(BF16), head_sel), axis=0, keepdims=True))
    qk_min = jnp.min(_dot((q * k).astype(BF16), head_sel), axis=0, keepdims=True)
    row8 = lax.broadcasted_iota(jnp.int32, (SUBLANES, LANES), 0)
    bnd_o[0] = jnp.where(row8 == 0, k_norm, jnp.where(row8 == 1, q_norm, jnp.where(row8 == 2, qk_min, 0.0)))
    lane = lax.broadcasted_iota(jnp.int32, (tm, LANES), 1)
    key_aug = _key_aug_columns(tm)
    for hp in range(MOBA_HEADS // 2):
        pair = k[:, hp * LANES:(hp + 1) * LANES]
        for j, kh in enumerate((pair, pltpu.roll(pair, MOBA_HEAD_DIM, 1))):
            ka_o[2 * hp + j, 0] = jnp.where(lane < MOBA_HEAD_DIM, kh, key_aug).astype(BF16)
    k_t = k.T
    v_t = v.T
    for p in range(tm // PAGE_SIZE):
        toks = slice(p * PAGE_SIZE, (p + 1) * PAGE_SIZE)
        for h in range(MOBA_HEADS):
            dims = slice(h * MOBA_HEAD_DIM, (h + 1) * MOBA_HEAD_DIM)
            kr_o[p, h] = k_t[dims, toks]
            vr_o[p, h] = v_t[dims, toks]
    ones_rows = (lax.broadcasted_iota(jnp.int32, (MOBA_VT_ROWS - MOBA_HEAD_DIM, tm), 0) == 0).astype(F32)
    pieces = []
    for h in range(MOBA_HEADS):
        pieces += [v_t[h * MOBA_HEAD_DIM:(h + 1) * MOBA_HEAD_DIM, :], ones_rows]
    vt_o[0] = jnp.concatenate(pieces, axis=0).astype(BF16)
    km_o[0] = jnp.mean(k, axis=0, keepdims=True)


def _inproj_a(x, nw, ws, prompt):
    m = x.shape[0]
    tm = ROW_TILE if prompt else m
    assert m % tm == 0
    if prompt:
        assert tm == MOBA_BLOCK
    nt = m // tm
    widths = [w.shape[1] for w in ws]
    row = lambda i: (i, 0)
    in_specs = [pl.BlockSpec((tm, D_MODEL), row), _const_spec((1, D_MODEL))]
    in_specs += [_const_spec(w.shape) for w in ws]
    out_shape = [jax.ShapeDtypeStruct((m, n), F32) for n in widths[:4]]
    out_specs = [pl.BlockSpec((tm, n), row) for n in widths[:4]]
    if prompt:
        npg = m // PAGE_SIZE
        ppt = tm // PAGE_SIZE
        rows_shape = (npg, MOBA_HEADS, MOBA_HEAD_DIM, PAGE_SIZE)
        rows_spec = pl.BlockSpec((ppt, MOBA_HEADS, MOBA_HEAD_DIM, PAGE_SIZE), lambda i: (i, 0, 0, 0))
        out_shape += [jax.ShapeDtypeStruct(rows_shape, F32)] * 2
        out_specs += [rows_spec, rows_spec]
        out_shape += [jax.ShapeDtypeStruct((MOBA_HEADS, nt, tm, LANES), BF16),
                      jax.ShapeDtypeStruct((nt, MOBA_HEADS * MOBA_VT_ROWS, tm), BF16),
                      jax.ShapeDtypeStruct((nt, 1, MOBA_INNER), F32),
                      jax.ShapeDtypeStruct((nt, SUBLANES, LANES), F32)]
        out_specs += [pl.BlockSpec((MOBA_HEADS, 1, tm, LANES), lambda i: (0, i, 0, 0)),
                      pl.BlockSpec((1, MOBA_HEADS * MOBA_VT_ROWS, tm), lambda i: (i, 0, 0)),
                      pl.BlockSpec((1, 1, MOBA_INNER), lambda i: (i, 0, 0)),
                      pl.BlockSpec((1, SUBLANES, LANES), lambda i: (i, 0, 0))]
        head_sel = (jnp.arange(MOBA_INNER)[:, None] // MOBA_HEAD_DIM == jnp.arange(LANES)[None, :]).astype(BF16)
        extra, extra_specs = [head_sel], [_const_spec(head_sel.shape)]
    else:
        out_shape += [jax.ShapeDtypeStruct((m, MOBA_INNER), F32)] * 2
        out_specs += [pl.BlockSpec((tm, MOBA_INNER), row)] * 2
        extra, extra_specs = [], []
    return pl.pallas_call(
        functools.partial(_inproj_a_kernel, prompt=prompt),
        grid=(nt,), in_specs=in_specs + extra_specs, out_specs=out_specs, out_shape=out_shape,
        compiler_params=_cparams(("arbitrary",)), name="inproj_a",
    )(x, nw, *ws, *extra)


def _norm_matmul_kernel(x_ref, nw_ref, *refs):
    n = len(refs) // 2
    hb = _rms_bf16(x_ref[...], nw_ref[...])
    for w, o in zip(refs[:n], refs[n:]):
        o[...] = _dot(hb, w[...]).astype(o.dtype)


def _norm_matmul(x, nw, ws, out_dtypes):
    m = x.shape[0]
    tm = min(FFN_ROW_TILE, m)
    assert m % tm == 0
    row = lambda i: (i, 0)
    in_specs = [pl.BlockSpec((tm, D_MODEL), row), _const_spec((1, D_MODEL))]
    in_specs += [pl.BlockSpec(w.shape, lambda i: (0, 0), pipeline_mode=pl.Buffered(1)) for w in ws]
    return pl.pallas_call(
        _norm_matmul_kernel, grid=(m // tm,), in_specs=in_specs,
        out_specs=[pl.BlockSpec((tm, w.shape[1]), row) for w in ws],
        out_shape=[jax.ShapeDtypeStruct((m, w.shape[1]), dt) for w, dt in zip(ws, out_dtypes)],
        compiler_params=_cparams(("arbitrary",)), name="norm_matmul",
    )(x, nw, *ws)


def _ffn_kernel(*refs, tm, seq_len, final_norm, n_mix):
    it = iter(refs)
    x_ref = next(it)
    mix_refs = [next(it) for _ in range(2 * n_mix)]
    nw_ref, wup_ref, cw_ref, cb_ref, wdn_ref = (next(it) for _ in range(5))
    if seq_len is None:
        hist_ref = next(it)
    else:
        h1_ref, h2_ref = next(it), next(it)
    nf_ref = next(it) if final_norm else None
    o_ref = next(it)
    tail_ref = next(it)
    up_s, act_s = next(it), next(it)

    i = pl.program_id(0)
    f = FFN_DIM
    if seq_len is None:
        @pl.when(i == 0)
        def _():
            up_s[0:SUBLANES, 0:f] = hist_ref[...]
    else:
        up_s[0:SUBLANES, 0:f] = jnp.zeros((SUBLANES, f), F32)

    x = x_ref[...]
    for a_ref, w_ref in zip(mix_refs[:n_mix], mix_refs[n_mix:]):
        x = x + _dot(a_ref[...], w_ref[...])
    hb = _rms_bf16(x, nw_ref[...])
    up_s[SUBLANES:SUBLANES + tm, :] = _dot(hb, wup_ref[0])

    if seq_len is not None:
        t = lax.broadcasted_iota(jnp.int32, (tm, 1), 0) % seq_len
    for c in range(0, f, FFN_COL_CHUNK):
        cols = slice(c, c + FFN_COL_CHUNK)
        g0 = up_s[SUBLANES:SUBLANES + tm, cols]
        g1 = up_s[SUBLANES - 1:SUBLANES - 1 + tm, cols]
        g2 = up_s[SUBLANES - 2:SUBLANES - 2 + tm, cols]
        if seq_len is not None:
            g1 = jnp.where(t >= 1, g1, h1_ref[:, cols])
            g2 = jnp.where(t >= 2, g2, h2_ref[:, cols])
        val = up_s[SUBLANES:SUBLANES + tm, f + c:f + c + FFN_COL_CHUNK]
        conv = g2 * cw_ref[0:1, cols]
        conv = conv + g1 * cw_ref[1:2, cols]
        conv = conv + g0 * cw_ref[2:3, cols]
        conv = conv + cb_ref[:, cols]
        act_s[:, cols] = (_silu(conv) * val).astype(BF16)

    out = x + _dot(act_s[...], wdn_ref[0])
    o_ref[...] = _rms(out, nf_ref[...]) if final_norm else out

    if seq_len is None:
        last = up_s[tm:tm + SUBLANES, 0:f]
        up_s[0:SUBLANES, 0:f] = last

        @pl.when(i == pl.num_programs(0) - 1)
        def _():
            tail_ref[...] = last
    else:
        tail_ref[...] = up_s[SUBLANES:SUBLANES + tm, 0:f]


def _ffn(x, mix_acts, mix_ws, nw, wup, cw, cb, wdn, layer, hist, seq_len, nf):
    assert FFN_CONV == 3
    m = x.shape[0]
    f = FFN_DIM
    final_norm = nf is not None
    row = lambda i: (i, 0)
    if seq_len is None:
        tm = FFN_ROW_TILE
        hist8 = jnp.zeros((SUBLANES, f), F32).at[SUBLANES - 2:].set(hist)
        extra = [hist8]
        extra_specs = [_const_spec((SUBLANES, f))]
        tail_shape = (SUBLANES, f)
    else:
        tm = m
        nb = m // seq_len
        assert seq_len >= 2 and nb * seq_len == m
        h1 = jnp.zeros((nb, seq_len, f), F32).at[:, 0].set(hist[:, 1]).reshape(m, f)
        h2 = jnp.zeros((nb, seq_len, f), F32).at[:, 0:2].set(hist).reshape(m, f)
        extra = [h1, h2]
        extra_specs = [_const_spec((m, f))] * 2
        tail_shape = (m, f)
    assert m % tm == 0
    ins = [x] + list(mix_acts) + list(mix_ws) + [nw, wup, cw, cb, wdn] + extra
    in_specs = [pl.BlockSpec((tm, D_MODEL), row)]
    in_specs += [pl.BlockSpec((tm, a.shape[1]), row) for a in mix_acts]
    once = pl.Buffered(1)
    in_specs += [pl.BlockSpec(w.shape, lambda i: (0, 0), pipeline_mode=once) for w in mix_ws]
    layer_slab = lambda w: pl.BlockSpec((1,) + w.shape[1:], lambda i: (layer, 0, 0), pipeline_mode=once)
    in_specs += [_const_spec((1, D_MODEL)), layer_slab(wup), _const_spec(cw.shape),
                 _const_spec(cb.shape), layer_slab(wdn)] + extra_specs
    if final_norm:
        ins.append(nf)
        in_specs.append(_const_spec((1, D_MODEL)))
    xo, tail = pl.pallas_call(
        functools.partial(_ffn_kernel, tm=tm, seq_len=seq_len, final_norm=final_norm, n_mix=len(mix_acts)),
        grid=(m // tm,), in_specs=in_specs,
        out_specs=[pl.BlockSpec((tm, D_MODEL), row), _const_spec(tail_shape)],
        out_shape=[jax.ShapeDtypeStruct((m, D_MODEL), F32), jax.ShapeDtypeStruct(tail_shape, F32)],
        scratch_shapes=[pltpu.VMEM((SUBLANES + tm, 2 * f), F32), pltpu.VMEM((tm, f), BF16)],
        compiler_params=_cparams(("arbitrary",)), name="conv_ffn",
    )(*ins)
    if seq_len is None:
        new_hist = tail[SUBLANES - 2:]
    else:
        new_hist = tail.reshape(m // seq_len, seq_len, f)[:, seq_len - 2:]
    return xo, new_hist


def _ssd_kernel(z_ref, xbc_ref, dt_ref, hist_ref, s0_ref, cw_ref, cb_ref, dtb_ref, alog_ref, dsk_ref,
                nw_ref, y_ref, hist_o, s_o, cbuf, st, ybuf, m1_s, xdt_s, ce_s, xw_s, bm_s, xsd_s, zs_s, el_s,
                *, q, seq_len, n_chunks, pipelined):
    c = pl.program_id(1)
    stage = (m1_s, xdt_s, ce_s, xw_s, bm_s, xsd_s, zs_s, el_s)

    def state_stage(rd):
        _ssd_state_stage(rd, nw_ref, y_ref, st, ybuf, *stage)

    def free_stage(wr):
        _ssd_free_stage(c, wr, z_ref, xbc_ref, dt_ref, cw_ref, cb_ref, dtb_ref, alog_ref, dsk_ref, hist_o, cbuf,
                        *stage, q=q, seq_len=seq_len, n_chunks=n_chunks)

    @pl.when(c == 0)
    def _():
        cbuf[0:SUBLANES, :] = hist_ref[0]
        st[...] = s0_ref[0]
        if pipelined:
            for ref in stage[:-1]:
                ref[1] = jnp.zeros(ref.shape[1:], ref.dtype)
            el_s[1] = jnp.ones(el_s.shape[1:], F32)

    if pipelined:
        @pl.when(c % 2 == 0)
        def _():
            state_stage(1)
            free_stage(0)

        @pl.when(c % 2 == 1)
        def _():
            state_stage(0)
            free_stage(1)
    else:
        free_stage(0)
        state_stage(0)

    @pl.when(c == (n_chunks if pipelined else n_chunks - 1))
    def _():
        s_o[0] = st[...]


def _ssd_state_stage(rd, nw_ref, y_ref, st, ybuf, m1_s, xdt_s, ce_s, xw_s, bm_s, xsd_s, zs_s, el_s):
    rep = SSD_HEADS // SSD_GROUPS
    hd = SSD_HEAD_DIM
    e_last = el_s[rd, 0:1, :]
    y_state = [_dot_nt(ce_s[rd, h], st[h].astype(BF16)) for h in range(SSD_HEADS)]
    s_inc = [_dot_tn(xw_s[rd, h], bm_s[rd, h // rep]) for h in range(SSD_HEADS)]
    y_intra = [_dot(m1_s[rd, h], xdt_s[rd, h]) for h in range(SSD_HEADS)]
    for h in range(SSD_HEADS):
        st[h] = st[h] * e_last[:, h:h + 1] + s_inc[h]
        ybuf[:, h * hd:(h + 1) * hd] = y_intra[h] + y_state[h]
    yz = (ybuf[...] + xsd_s[rd]) * zs_s[rd]
    gw = SSD_INNER // SSD_GROUPS
    for g in range(SSD_GROUPS):
        yg = yz[:, g * gw:(g + 1) * gw]
        yn = yg * lax.rsqrt(jnp.mean(yg * yg, axis=-1, keepdims=True) + RMS_EPS)
        y_ref[0, :, g * gw:(g + 1) * gw] = (yn * nw_ref[:, g * gw:(g + 1) * gw]).astype(BF16)


def _ssd_free_stage(c, wr, z_ref, xbc_ref, dt_ref, cw_ref, cb_ref, dtb_ref, alog_ref, dsk_ref, hist_o, cbuf,
                    m1_s, xdt_s, ce_s, xw_s, bm_s, xsd_s, zs_s, el_s, *, q, seq_len, n_chunks):
    rep = SSD_HEADS // SSD_GROUPS
    hd = SSD_HEAD_DIM
    hrows = SSD_CONV - 1
    h0 = SUBLANES - hrows
    cbuf[SUBLANES:SUBLANES + q, :] = xbc_ref[0]
    conv = cbuf[h0:h0 + q, :] * cw_ref[0:1, :]
    for j in range(1, SSD_CONV):
        conv = conv + cbuf[h0 + j:h0 + j + q, :] * cw_ref[j:j + 1, :]
    xc = _silu(conv + cb_ref[...])

    xdt_raw = dt_ref[0] + dtb_ref[...]
    dt = jnp.maximum(xdt_raw, 0.0) + jnp.log1p(jnp.exp(-jnp.abs(xdt_raw)))
    if seq_len % q != 0:
        row = jnp.minimum(c, n_chunks - 1) * q + lax.broadcasted_iota(jnp.int32, (q, 1), 0)
        dt = jnp.where(row < seq_len, dt, 0.0)
    a = -jnp.exp(alog_ref[...])
    acum = _cumsum_rows(dt * a, q, split=False)
    acum_t = acum.T
    a_last = acum[q - 1:q, :]
    e_acum = jnp.exp(acum)
    w_in = jnp.exp(a_last - acum) * dt
    el_s[wr] = jnp.broadcast_to(jnp.exp(a_last), (SUBLANES, LANES))
    tril = _tril_f32(q)

    cbs, cms = [], []
    for g in range(SSD_GROUPS):
        bm = xc[:, SSD_INNER + g * SSD_STATE:SSD_INNER + (g + 1) * SSD_STATE].astype(BF16)
        cm = xc[:, SSD_INNER + SSD_GN + g * SSD_STATE:SSD_INNER + SSD_GN + (g + 1) * SSD_STATE]
        bm_s[wr, g] = bm
        cms.append(cm)
        cbs.append(_dot_nt(cm.astype(BF16), bm))

    for h in range(SSD_HEADS):
        g = h // rep
        xs = xc[:, h * hd:(h + 1) * hd]
        seg = acum[:, h:h + 1] - acum_t[h:h + 1, :]
        decay = jnp.exp(jnp.where(tril, seg, -jnp.inf))
        m1_s[wr, h] = (cbs[g] * decay).astype(BF16)
        xdt_s[wr, h] = (xs * dt[:, h:h + 1]).astype(BF16)
        ce_s[wr, h] = (cms[g] * e_acum[:, h:h + 1]).astype(BF16)
        xw_s[wr, h] = (xs * w_in[:, h:h + 1]).astype(BF16)
        xsd_s[wr, :, h * hd:(h + 1) * hd] = xs * dsk_ref[:, h:h + 1]
    zs_s[wr] = _silu(z_ref[0])

    @pl.when(c == n_chunks - 1)
    def _():
        l_last = seq_len - (n_chunks - 1) * q
        hist_o[0] = cbuf[l_last:l_last + SUBLANES, :]

    cbuf[0:SUBLANES, :] = cbuf[q:q + SUBLANES, :]


BF16_SUBLANES = 16


def _chunk_len(seq_len, max_chunk):
    return min(max_chunk, _round_up(seq_len, BF16_SUBLANES))


def _stage_grid(n_chunks, pipelined):
    if not pipelined:
        same = lambda i, c: (i, c, 0)
        return n_chunks, same, same
    return (n_chunks + 1, lambda i, c: (i, jnp.minimum(c, n_chunks - 1), 0),
            lambda i, c: (i, jnp.maximum(c - 1, 0), 0))


def _pad_lanes(v, fill=0.0):
    return jnp.full((1, LANES), fill, F32).at[0, :v.shape[0]].set(v.astype(F32))


def _ssd(z, xbc, dtr, hist, s0, conv_w, conv_b, dt_bias, a_log, d_skip, norm_w, seq_len):
    b, lp, _ = z.shape
    q = _chunk_len(seq_len, SSD_CHUNK)
    assert lp % q == 0 and lp - seq_len < q
    hrows = SSD_CONV - 1
    hist8 = jnp.zeros((b, SUBLANES, SSD_CONV_DIM), F32).at[:, SUBLANES - hrows:].set(hist)
    cw8 = jnp.zeros((SUBLANES, SSD_CONV_DIM), F32).at[:SSD_CONV].set(conv_w)
    n_chunks = lp // q
    pipelined = False
    n_steps, seq, seq_out = _stage_grid(n_chunks, pipelined)
    per_b3 = lambda i, c: (i, 0, 0)
    per_b4 = lambda i, c: (i, 0, 0, 0)
    in_specs = [pl.BlockSpec((1, q, SSD_INNER), seq), pl.BlockSpec((1, q, SSD_CONV_DIM), seq),
                pl.BlockSpec((1, q, LANES), seq), pl.BlockSpec((1, SUBLANES, SSD_CONV_DIM), per_b3),
                pl.BlockSpec((1, SSD_HEADS, SSD_HEAD_DIM, SSD_STATE), per_b4),
                _const_spec((SUBLANES, SSD_CONV_DIM)), _const_spec((1, SSD_CONV_DIM)),
                _const_spec((1, LANES)), _const_spec((1, LANES)), _const_spec((1, LANES)),
                _const_spec((1, SSD_INNER))]
    out_shape = [jax.ShapeDtypeStruct((b, lp, SSD_INNER), BF16),
                 jax.ShapeDtypeStruct((b, SUBLANES, SSD_CONV_DIM), F32),
                 jax.ShapeDtypeStruct((b, SSD_HEADS, SSD_HEAD_DIM, SSD_STATE), F32)]
    out_specs = [pl.BlockSpec((1, q, SSD_INNER), seq_out), pl.BlockSpec((1, SUBLANES, SSD_CONV_DIM), per_b3),
                 pl.BlockSpec((1, SSD_HEADS, SSD_HEAD_DIM, SSD_STATE), per_b4)]
    per_head = pltpu.VMEM((2, SSD_HEADS, q, SSD_HEAD_DIM), BF16)
    y, hist_o, s_o = pl.pallas_call(
        functools.partial(_ssd_kernel, q=q, seq_len=seq_len, n_chunks=n_chunks, pipelined=pipelined),
        grid=(b, n_steps), in_specs=in_specs, out_specs=out_specs, out_shape=out_shape,
        scratch_shapes=[pltpu.VMEM((SUBLANES + q, SSD_CONV_DIM), F32),
                        pltpu.VMEM((SSD_HEADS, SSD_HEAD_DIM, SSD_STATE), F32),
                        pltpu.VMEM((q, SSD_INNER), F32),
                        pltpu.VMEM((2, SSD_HEADS, q, q), BF16), per_head, per_head, per_head,
                        pltpu.VMEM((2, SSD_GROUPS, q, SSD_STATE), BF16),
                        pltpu.VMEM((2, q, SSD_INNER), F32), pltpu.VMEM((2, q, SSD_INNER), F32),
                        pltpu.VMEM((2, SUBLANES, LANES), F32)],
        compiler_params=_cparams(("arbitrary", "arbitrary")), name="ssd",
    )(z, xbc, dtr, hist8, s0, cw8, conv_b.reshape(1, -1), _pad_lanes(dt_bias), _pad_lanes(a_log),
      _pad_lanes(d_skip), norm_w.reshape(1, -1))
    return y, hist_o[:, SUBLANES - hrows:], s_o


def _gla_kernel(q_ref, fx_ref, iv_ref, g_ref, gam_ref, nw_ref, s0_ref, o_ref, s_o, st_t, qd_s, kd_s, kdec_s,
                v_s, gs_s, el_s, *, q, seq_len, layer, n_chunks):
    c = pl.program_id(1)
    pipelined = n_chunks > 1
    stage = (qd_s, kd_s, kdec_s, v_s, gs_s, el_s)

    def state_stage(rd):
        _gla_state_stage(rd, nw_ref, o_ref, st_t, *stage, q=q)

    def free_stage(wr):
        _gla_free_stage(c, wr, q_ref, fx_ref, iv_ref, g_ref, gam_ref, *stage, q=q, seq_len=seq_len, layer=layer,
                        n_chunks=n_chunks)

    @pl.when(c == 0)
    def _():
        for h in range(HGRN_HEADS):
            st_t[h] = s0_ref[0, h].T
        if pipelined:
            for ref in stage[:-1]:
                ref[1] = jnp.zeros(ref.shape[1:], ref.dtype)
            el_s[1] = jnp.ones(el_s.shape[1:], F32)

    if pipelined:
        @pl.when(c % 2 == 0)
        def _():
            state_stage(1)
            free_stage(0)

        @pl.when(c % 2 == 1)
        def _():
            state_stage(0)
            free_stage(1)
    else:
        free_stage(0)
        state_stage(0)

    @pl.when(c == (n_chunks if pipelined else n_chunks - 1))
    def _():
        for h in range(HGRN_HEADS):
            s_o[0, h] = st_t[h].T


def _gla_state_stage(rd, nw_ref, o_ref, st_t, qd_s, kd_s, kdec_s, v_s, gs_s, el_s, *, q):
    tril = _tril_f32(q)
    ks = [slice(h * HGRN_KEY_DIM, (h + 1) * HGRN_KEY_DIM) for h in range(HGRN_HEADS)]
    vs = [slice(h * HGRN_VAL_DIM, (h + 1) * HGRN_VAL_DIM) for h in range(HGRN_HEADS)]
    qd_b = [qd_s[rd, :, s] for s in ks]
    v_b = [v_s[rd, :, s] for s in vs]
    e_last = el_s[rd, 0:1, :]
    att = [_dot_nt(qd_b[h], kd_s[rd, :, ks[h]]) for h in range(HGRN_HEADS)]
    o_state = [_dot_nt(qd_b[h], st_t[h].astype(BF16)) for h in range(HGRN_HEADS)]
    kv = [_dot_tn(v_b[h], kdec_s[rd, :, ks[h]]) for h in range(HGRN_HEADS)]
    for h in range(HGRN_HEADS):
        o = _dot(jnp.where(tril, att[h], 0.0).astype(BF16), v_b[h]) + o_state[h]
        st_t[h] = st_t[h] * e_last[:, ks[h]] + kv[h]
        on = o * lax.rsqrt(jnp.mean(o * o, axis=-1, keepdims=True) + RMS_EPS) * nw_ref[...]
        o_ref[0, :, vs[h]] = (on * gs_s[rd, :, vs[h]]).astype(BF16)


def _gla_free_stage(c, wr, q_ref, fx_ref, iv_ref, g_ref, gam_ref, qd_s, kd_s, kdec_s, v_s, gs_s, el_s, *, q,
                    seq_len, layer, n_chunks):
    rows = [gam_ref[l:l + 1, :] for l in range(DEPTH)]
    mx = functools.reduce(jnp.maximum, rows)
    es = [jnp.exp(r - mx) for r in rows]
    lb = sum(es[1:layer + 1]) / sum(es) if layer >= 1 else jnp.zeros_like(mx)

    f = lb + (1.0 - lb) * jax.nn.sigmoid(fx_ref[0])
    if seq_len % q != 0:
        row = jnp.minimum(c, n_chunks - 1) * q + lax.broadcasted_iota(jnp.int32, (q, 1), 0)
        f = jnp.where(row < seq_len, f, 1.0)
    kk = 1.0 - f
    b = _cumsum_rows(jnp.log(f), q, split=True)
    b_last = b[q - 1:q, :]
    qd_s[wr] = (q_ref[0] * jnp.exp(b)).astype(BF16)
    kd_s[wr] = (kk * jnp.exp(-b)).astype(BF16)
    kdec_s[wr] = (kk * jnp.exp(b_last - b)).astype(BF16)
    v_s[wr] = iv_ref[0].astype(BF16)
    gs_s[wr] = _silu(g_ref[0])
    el_s[wr] = jnp.broadcast_to(jnp.exp(b_last), (SUBLANES, HGRN_K))


def _gla(qa, fx, iv, g, gamma, norm_w, s0, seq_len, layer):
    b, lp, _ = qa.shape
    q = _chunk_len(seq_len, HGRN_CHUNK)
    assert lp % q == 0 and lp - seq_len < q
    gam8 = jnp.zeros((SUBLANES, HGRN_K), F32).at[:DEPTH].set(gamma)
    n_chunks = lp // q
    n_steps, seq, seq_out = _stage_grid(n_chunks, n_chunks > 1)
    per_b4 = lambda i, c: (i, 0, 0, 0)
    st_shape = (HGRN_HEADS, HGRN_KEY_DIM, HGRN_VAL_DIM)
    in_specs = [pl.BlockSpec((1, q, HGRN_K), seq), pl.BlockSpec((1, q, HGRN_K), seq),
                pl.BlockSpec((1, q, HGRN_V), seq), pl.BlockSpec((1, q, HGRN_V), seq),
                _const_spec((SUBLANES, HGRN_K)), _const_spec((1, HGRN_VAL_DIM)),
                pl.BlockSpec((1,) + st_shape, per_b4)]
    stage_k = pltpu.VMEM((2, q, HGRN_K), BF16)
    o, s_o = pl.pallas_call(
        functools.partial(_gla_kernel, q=q, seq_len=seq_len, layer=layer, n_chunks=n_chunks),
        grid=(b, n_steps), in_specs=in_specs,
        out_specs=[pl.BlockSpec((1, q, HGRN_V), seq_out), pl.BlockSpec((1,) + st_shape, per_b4)],
        out_shape=[jax.ShapeDtypeStruct((b, lp, HGRN_V), BF16), jax.ShapeDtypeStruct((b,) + st_shape, F32)],
        scratch_shapes=[pltpu.VMEM((HGRN_HEADS, HGRN_VAL_DIM, HGRN_KEY_DIM), F32),
                        stage_k, stage_k, stage_k, pltpu.VMEM((2, q, HGRN_V), BF16),
                        pltpu.VMEM((2, q, HGRN_V), F32), pltpu.VMEM((2, SUBLANES, HGRN_K), F32)],
        compiler_params=_cparams(("arbitrary", "arbitrary")), name="gla",
    )(qa, fx, iv, g, gam8, norm_w.reshape(1, -1), s0)
    return o, s_o


def _select_topk_rows(gate_t, n_valid, nblk, width):
    blk = lax.broadcasted_iota(jnp.int32, (nblk, width), 0)
    blk_f = blk.astype(F32)
    g = jnp.where(blk < n_valid, gate_t, -jnp.inf)
    sel = jnp.zeros((nblk, width), F32)
    for _ in range(MOBA_TOPK):
        mx = jnp.max(g, axis=0, keepdims=True)
        first = jnp.min(jnp.where(g == mx, blk_f, float(nblk)), axis=0, keepdims=True)
        pick = (blk_f == first) & (mx > -jnp.inf)
        sel = jnp.where(pick, 1.0, sel)
        g = jnp.where(pick, -jnp.inf, g)
    return jnp.where(sel > 0.0, 0.0, -jnp.inf)


def _key_aug_columns(tm):
    lane = lax.broadcasted_iota(jnp.int32, (tm, LANES), 1)
    r = lax.broadcasted_iota(jnp.int32, (tm, LANES), 0).astype(F32)
    return jnp.where((lane >= MOBA_HEAD_DIM) & (lane < MOBA_HEAD_DIM + ALIBI_PARTS), r, 0.0)


def _moba_prompt_kernel(slopes_ref, kn_ref, qn_ref, dmin_ref, knt_ref, q_ref, ka_ref, vt_ref, km_ref, o_ref,
                        brow_s, ubuf, *, nblk):
    hp = pl.program_id(0)
    i = pl.program_id(1)
    tq = MOBA_BLOCK
    d = MOBA_HEAD_DIM
    q = q_ref[...]
    lane_k = lax.broadcasted_iota(jnp.int32, (nblk, 2 * d), 1)
    causal =(lax.broadcasted_iota(jnp.int32, (MOBA_BLOCK, tq), 0)
              <= lax.broadcasted_iota(jnp.int32, (MOBA_BLOCK, tq), 1))
    blk_f = lax.broadcasted_iota(jnp.int32, (nblk, tq), 0).astype(F32)
    col_f = lax.broadcasted_iota(jnp.int32, (nblk, tq), 1).astype(F32)
    aug_row = lax.broadcasted_iota(jnp.int32, (d, tq), 0)
    i_f = i.astype(F32)

    q_t = q.T
    km = km_ref[...]
    km2 = jnp.concatenate([jnp.where(lane_k < d, km, 0.0), jnp.where(lane_k >= d, km, 0.0)], axis=0)
    gate_both = _dot_split(km2, q_t)

    rhs = []
    for j in range(2):
        a2 = slopes_ref[2 * hp + j] * LOG2E
        sel = _select_topk_rows(gate_both[j * nblk:(j + 1) * nblk, :], i, nblk, tq)
        brow_s[j] = sel + a2 * (MOBA_BLOCK * (blk_f - i_f) - col_f)
        rest = jnp.full((d, tq), a2, F32)
        aug = jnp.zeros((d, tq), F32)
        for part in range(ALIBI_PARTS):
            term = rest.astype(BF16).astype(F32)
            aug = jnp.where(aug_row == part, term, aug)
            rest = rest - term
        q_h = q_t[j * d:(j + 1) * d, :] * (d ** -0.5 * LOG2E)
        rhs.append(jnp.concatenate([q_h, aug], axis=0).astype(BF16))

    def block_update(st, u, b_row, v_t):
        m_run, o_t = st
        m_new = jnp.maximum(m_run, jnp.max(u, axis=0, keepdims=True) + b_row)
        alpha = jnp.exp2(m_run - m_new)
        p = jnp.exp2(u - (m_new - b_row))
        return m_new, alpha * o_t + _dot(v_t, p.astype(BF16))

    def score(n, slot):
        for j in range(2):
            ubuf[slot, j] = _dot(ka_ref[j, n], rhs[j])

    def consume(n, slot, sts, own):
        new = []
        for j in range(2):
            u = ubuf[slot, j]
            if own:
                u = jnp.where(causal, u, -jnp.inf)
                b_row = -(slopes_ref[2 * hp + j] * LOG2E) * col_f[0:1, :]
            else:
                b_row = brow_s[j, pl.ds(n, 1), :]
            new.append(block_update(sts[j], u, b_row, vt_ref[n, j * MOBA_VT_ROWS:(j + 1) * MOBA_VT_ROWS, :]))
        return tuple(new)

    def step(n, slot, slot_ahead, sts):
        score(jnp.minimum(n + MOBA_AHEAD, i), slot_ahead)
        return consume(n, slot, sts, False)

    def group(base, unroll, sts):
        for k in range(unroll):
            sts = step(base + k, k % MOBA_SLOTS, (k + MOBA_AHEAD) % MOBA_SLOTS, sts)
        return sts

    def first_needed(j):
        hd = 2 * hp + j
        a2 = slopes_ref[hd] * LOG2E
        c = d ** -0.5 * LOG2E
        qn = qn_ref[i * MOBA_HEADS + hd]
        floor = c * (dmin_ref[i * MOBA_HEADS + hd] - BOUND_SLACK * qn * kn_ref[i * MOBA_HEADS + hd])

        blk_row = lax.broadcasted_iota(jnp.int32, (1, nblk), 1).astype(F32)
        ceil = (c * (1.0 + BOUND_SLACK) * qn * knt_ref[pl.ds(hd, 1), :]
                + a2 * (MOBA_BLOCK * (blk_row - i_f) + (MOBA_BLOCK - 1)))
        needed = (blk_row < i_f) & (ceil >= floor - UNDERFLOW_LOG2)
        return jnp.min(jnp.where(needed, blk_row, i_f)).astype(jnp.int32)

    start = (jnp.minimum(first_needed(0), first_needed(1)) // MOBA_SLOTS) * MOBA_SLOTS

    init = (jnp.full((1, tq), NEG_BIG, F32), jnp.zeros((MOBA_VT_ROWS, tq), F32))
    for k in range(MOBA_AHEAD):
        score(jnp.minimum(start + k, i), k)
    n_big = (i - start) // MOBA_UNROLL
    sts = lax.fori_loop(0, n_big, lambda g, s: group(start + g * MOBA_UNROLL, MOBA_UNROLL, s), (init, init))
    base = start + n_big * MOBA_UNROLL
    n_small = (i - base) // MOBA_SLOTS
    sts = lax.fori_loop(0, n_small, lambda g, s: group(base + g * MOBA_SLOTS, MOBA_SLOTS, s), sts)
    base = base + n_small * MOBA_SLOTS
    rem = i - base
    for k in range(MOBA_SLOTS - 1):
        sts = lax.cond(rem > k, lambda s, k=k: step(base + k, k, (k + MOBA_AHEAD) % MOBA_SLOTS, s),
                       lambda s: s, sts)
    outs = []
    for _, o_t in consume(i, rem, sts, True):
        outs.append(o_t[0:d, :] / o_t[d:d + 1, :])
    o_ref[...] = jnp.concatenate(outs, axis=0).T.astype(BF16)


def _alibi_slopes():
    return jnp.asarray(np.exp2(-8.0 * np.arange(1, MOBA_HEADS + 1) / MOBA_HEADS), dtype=F32)


def _moba_prompt(q, ka, vt, km, bounds):
    s = q.shape[0]
    nblk = s // MOBA_BLOCK
    assert nblk * MOBA_BLOCK == s and 2 * MOBA_HEAD_DIM == LANES
    pairs = MOBA_HEADS // 2
    tables = [bounds[:, r, :MOBA_HEADS].reshape(-1) for r in range(3)]
    kn_t = bounds[:, 0, :MOBA_HEADS].T
    smem = pl.BlockSpec(memory_space=pltpu.SMEM)
    return pl.pallas_call(
        functools.partial(_moba_prompt_kernel, nblk=nblk),
        grid=(pairs, nblk),
        in_specs=[smem, smem, smem, smem, _const_spec((MOBA_HEADS, nblk)),
                  pl.BlockSpec((MOBA_BLOCK, LANES), lambda hp, i: (i, hp)),
                  pl.BlockSpec((2, nblk, MOBA_BLOCK, LANES), lambda hp, i: (hp, 0, 0, 0)),
                  pl.BlockSpec((nblk, 2 * MOBA_VT_ROWS, MOBA_BLOCK), lambda hp, i: (0, hp, 0)),
                  pl.BlockSpec((nblk, LANES), lambda hp, i: (0, hp))],
        out_specs=pl.BlockSpec((MOBA_BLOCK, LANES), lambda hp, i: (i, hp)),
        out_shape=jax.ShapeDtypeStruct((s, MOBA_INNER), BF16),
        scratch_shapes=[pltpu.VMEM((2, nblk, MOBA_BLOCK), F32),
                        pltpu.VMEM((MOBA_SLOTS, 2, MOBA_BLOCK, MOBA_BLOCK), F32)],
        compiler_params=_cparams(("arbitrary", "arbitrary")), name="moba_prompt",
    )(_alibi_slopes(), *tables, kn_t, q, ka, vt, km)


PAGES_PER_STEP = 64
SAMPLE_AHEAD = 3
SAMPLE_BUFS = SAMPLE_AHEAD + 1


def _kmean_kernel(pt_ref, *refs):
    del pt_ref
    o_ref = refs[-1]
    s = pl.program_id(1)
    ppb = MOBA_BLOCK // PAGE_SIZE
    bps = PAGES_PER_STEP // ppb

    @pl.when(s == 0)
    def _():
        o_ref[...] = jnp.zeros(o_ref.shape, F32)

    acc = o_ref[0]
    lane = lax.broadcasted_iota(jnp.int32, acc.shape, 2)
    for m in range(bps):
        x = refs[m * ppb][0]
        for p in range(1, ppb):
            x = x + refs[m * ppb + p][0]
        mean = jnp.sum(x, axis=-1, keepdims=True) * (1.0 / MOBA_BLOCK)
        acc = jnp.where(lane == s * bps + m, mean, acc)
    o_ref[0] = acc


def _sample_kmean(k_pool_t, page_table):
    db, n_pages = page_table.shape
    ppb = MOBA_BLOCK // PAGE_SIZE
    n_full = n_pages // ppb
    assert n_full * ppb == n_pages and n_pages % PAGES_PER_STEP == 0
    steps = n_pages // PAGES_PER_STEP
    blk = (1, MOBA_HEADS, MOBA_HEAD_DIM, PAGE_SIZE)

    def page_spec(p):
        return pl.BlockSpec(blk, lambda b, s, pt: (pt[b * n_pages + s * PAGES_PER_STEP + p], 0, 0, 0))

    out_blk = (1, MOBA_HEADS, MOBA_HEAD_DIM, n_full)
    grid_spec = pltpu.PrefetchScalarGridSpec(
        num_scalar_prefetch=1, grid=(db, steps),
        in_specs=[page_spec(p) for p in range(PAGES_PER_STEP)],
        out_specs=pl.BlockSpec(out_blk, lambda b, s, pt: (b, 0, 0, 0)))
    return pl.pallas_call(
        _kmean_kernel, grid_spec=grid_spec,
        out_shape=jax.ShapeDtypeStruct((db,) + out_blk[1:], F32),
        compiler_params=_cparams(("arbitrary", "arbitrary")), name="sample_kmean",
    )(page_table.reshape(-1), *([k_pool_t] * PAGES_PER_STEP))


def _sample_select_kernel(q_ref, km_ref, idx_ref, *, n_full):
    t8 = q_ref.shape[2]
    blk = lax.broadcasted_iota(jnp.int32, (t8, n_full), 1).astype(F32)
    lane = lax.broadcasted_iota(jnp.int32, (t8, LANES), 1)
    for h in range(MOBA_HEADS):
        g = jnp.dot(q_ref[0, h], km_ref[0, h], precision=HIGHEST, preferred_element_type=F32)
        out = jnp.zeros((t8, LANES), jnp.int32)
        for k in range(MOBA_TOPK):
            mx = jnp.max(g, axis=-1, keepdims=True)
            first = jnp.min(jnp.where(g == mx, blk, float(n_full)), axis=-1, keepdims=True)
            out = jnp.where(lane == k, first.astype(jnp.int32), out)
            g = jnp.where(blk == first, -jnp.inf, g)
        idx_ref[0, h] = out


def _sample_select(qh, km):
    db, h, t8, d = qh.shape
    n_full = km.shape[3]
    assert n_full >= MOBA_TOPK
    b4 = lambda b: (b, 0, 0, 0)
    return pl.pallas_call(
        functools.partial(_sample_select_kernel, n_full=n_full), grid=(db,),
        in_specs=[pl.BlockSpec((1, h, t8, d), b4), pl.BlockSpec((1, h, d, n_full), b4)],
        out_specs=pl.BlockSpec((1, h, t8, LANES), b4),
        out_shape=jax.ShapeDtypeStruct((db, h, t8, LANES), jnp.int32),
        compiler_params=_cparams(("arbitrary",)), name="sample_select",
    )(qh, km)


def _sample_attn_kernel(idx_ref, phys_ref, slopes_ref, q_ref, kn_ref, vn_ref, kpool, vpool, o_ref,
                        kbuf, vbuf, sem, *, ds, past):
    ppb = MOBA_BLOCK // PAGE_SIZE
    n_sel = MOBA_TOPK * ppb
    n_pg = ds * n_sel
    b = pl.program_id(0)
    h = pl.program_id(1)
    n_heads = pl.num_programs(1)
    step = b * n_heads + h
    n_steps = pl.num_programs(0) * n_heads

    def page_copies(st, half):
        hh = st % n_heads
        out = []
        for k in range(n_pg):
            page = phys_ref[st * n_pg + k]
            out.append(pltpu.make_async_copy(kpool.at[page, hh], kbuf.at[half, k], sem.at[0, half]))
            out.append(pltpu.make_async_copy(vpool.at[page, hh], vbuf.at[half, k], sem.at[1, half]))
        return out

    @pl.when(step == 0)
    def _():
        for st in range(SAMPLE_AHEAD):
            for cp in page_copies(st, st):
                cp.start()

    @pl.when(step + SAMPLE_AHEAD < n_steps)
    def _():
        for cp in page_copies(step + SAMPLE_AHEAD, (step + SAMPLE_AHEAD) % SAMPLE_BUFS):
            cp.start()

    half = step % SAMPLE_BUFS
    for cp in page_copies(step, half):
        cp.wait()
    k_refs = [kbuf.at[half, k] for k in range(n_pg)]
    v_refs = [vbuf.at[half, k] for k in range(n_pg)]
    slope = slopes_ref[h]
    t8 = q_ref.shape[3]
    c_page = lax.broadcasted_iota(jnp.int32, (1, PAGE_SIZE), 1).astype(F32)
    c_new = lax.broadcasted_iota(jnp.int32, (1, t8), 1)
    qf = q_ref[0, 0] * (MOBA_HEAD_DIM ** -0.5)
    k_new = kn_ref[0, 0]
    v_new = vn_ref[0, 0]
    o_ref[0, 0] = jnp.zeros((MOBA_HEAD_DIM, t8), F32)
    for t in range(ds):
        q_t = qf[:, t:t + 1]
        t_pos = float(past + t)
        scores = []
        for sp in range(n_sel):
            slot, p = divmod(sp, ppb)
            blk_idx = idx_ref[((b * MOBA_HEADS + h) * ds + t) * MOBA_TOPK + slot]
            pos0 = (blk_idx * MOBA_BLOCK + p * PAGE_SIZE).astype(F32)
            s = jnp.sum(k_refs[t * n_sel + sp][...] * q_t, axis=0, keepdims=True)
            scores.append(s - slope * (t_pos - (pos0 + c_page)))
        s_new = jnp.sum(k_new * q_t, axis=0, keepdims=True)
        s_new = s_new - slope * (t_pos - (float(past) + c_new.astype(F32)))
        s_new = jnp.where(c_new <= t, s_new, -jnp.inf)
        mx = jnp.max(s_new, axis=-1, keepdims=True)
        for s in scores:
            mx = jnp.maximum(mx, jnp.max(s, axis=-1, keepdims=True))
        p_new = jnp.exp(s_new - mx)
        l = jnp.sum(p_new, axis=-1, keepdims=True)
        o = jnp.sum(p_new * v_new, axis=-1, keepdims=True)
        acc = jnp.zeros((MOBA_HEAD_DIM, PAGE_SIZE), F32)
        for sp in range(n_sel):
            pr = jnp.exp(scores[sp] - mx)
            l = l + jnp.sum(pr, axis=-1, keepdims=True)
            acc = acc + pr * v_refs[t * n_sel + sp][...]
        o = o + jnp.sum(acc, axis=-1, keepdims=True)
        o_ref[0, 0, :, t:t + 1] = o / l


def _sample_attn(qh, kh, vh, idx, k_pool, v_pool, page_table, ds):
    db, h, d, t8 = qh.shape
    n_pages = page_table.shape[1]
    ppb = MOBA_BLOCK // PAGE_SIZE
    assert n_pages % ppb == 0
    past = n_pages * PAGE_SIZE
    n_sel = MOBA_TOPK * ppb
    logical = idx[..., None] * ppb + jnp.arange(ppb, dtype=jnp.int32)
    phys = page_table[jnp.arange(db)[:, None, None, None], logical.reshape(db, h, ds, n_sel)]
    tok = pl.BlockSpec((1, 1, d, t8), lambda b, hh, *_: (b, hh, 0, 0))
    hbm = pl.BlockSpec(memory_space=pl.ANY)
    assert db * h >= SAMPLE_AHEAD
    page_buf = pltpu.VMEM((SAMPLE_BUFS, ds * n_sel, d, PAGE_SIZE), F32)
    grid_spec = pltpu.PrefetchScalarGridSpec(
        num_scalar_prefetch=2, grid=(db, h),
        in_specs=[pl.BlockSpec(memory_space=pltpu.SMEM), tok, tok, tok, hbm, hbm],
        out_specs=tok,
        scratch_shapes=[page_buf, page_buf, pltpu.SemaphoreType.DMA((2, SAMPLE_BUFS))])
    return pl.pallas_call(
        functools.partial(_sample_attn_kernel, ds=ds, past=past), grid_spec=grid_spec,
        out_shape=jax.ShapeDtypeStruct((db, h, d, t8), F32),
        compiler_params=_cparams(("arbitrary", "arbitrary")), name="sample_attn",
    )(idx.reshape(-1), phys.reshape(-1), _alibi_slopes(), qh, kh, vh, k_pool, v_pool)


def _split_w_in_a(w, f32_query):
    wb = w.astype(BF16)
    wdt = jnp.zeros((D_MODEL, LANES), BF16).at[:, :SSD_HEADS].set(wb[:, OFF_DT:OFF_Q])
    wq = w[:, OFF_Q:OFF_K] if f32_query else wb[:, OFF_Q:OFF_K]
    return [wb[:, :OFF_XBC], wb[:, OFF_XBC:OFF_DT], wdt, wq, wb[:, OFF_K:OFF_V], wb[:, OFF_V:]]


def _pad_seq(a, b, l, lp):
    a = a.reshape(b, l, a.shape[-1])
    return a if lp == l else jnp.pad(a, ((0, 0), (0, lp - l), (0, 0)))


def _round_up(n, m):
    return -(-n // m) * m


def _heads(a, b, l):
    return a.reshape(b, l, MOBA_HEADS, MOBA_HEAD_DIM).transpose(0, 2, 1, 3)


def _pad_tokens(a):
    l = a.shape[-2]
    return jnp.pad(a, ((0, 0),) * (a.ndim - 2) + ((0, _round_up(l, SUBLANES) - l), (0, 0)))


def _trunk(x, bsz, length, prompt, caches, p):
    ssm_conv0, ssm0, hgrn0, ffn_conv0 = caches[:4]
    m = bsz * length
    k_rows = v_rows = ssm_c = ssm_s = hgrn_s = None
    ffn_c = []
    for layer in range(DEPTH):
        nw = p['norm_mix'][layer].reshape(1, -1)
        if layer % 2 == 0:
            ia = layer // 2
            assert ia == 0
            ws = _split_w_in_a(p['w_in_a'][ia], f32_query=not prompt)
            outs = _inproj_a(x, nw, ws, prompt)
            z, xbc, dtr, q = outs[:4]
            lp = _round_up(length, _chunk_len(length, SSD_CHUNK))
            y_ssd, hist, s_new = _ssd(
                _pad_seq(z, bsz, length, lp), _pad_seq(xbc, bsz, length, lp), _pad_seq(dtr, bsz, length, lp),
                ssm_conv0[ia], ssm0[ia], p['ssd_conv_w'][ia], p['ssd_conv_b'][ia], p['ssd_dt_bias'][ia],
                p['ssd_a_log'][ia], p['ssd_d'][ia], p['ssd_norm_w'][ia], length)
            y_ssd = y_ssd[:, :length].reshape(m, SSD_INNER)
            if prompt:
                assert bsz == 1
                kr, vr, ka, vt, km, bounds = outs[4:]
                o_att = _moba_prompt(q, ka, vt, km.reshape(-1, MOBA_INNER), bounds)
                k_rows, v_rows = jnp.swapaxes(kr, -1, -2)[None], jnp.swapaxes(vr, -1, -2)[None]
            else:
                k, v = outs[4:]
                k_pool, v_pool, page_table = caches[4:]
                k_pool_t = jnp.swapaxes(k_pool[ia], -1, -2)
                v_pool_t = jnp.swapaxes(v_pool[ia], -1, -2)
                qh, kh, vh = (_pad_tokens(_heads(t, bsz, length)) for t in (q, k, v))
                qt, kt, vt = (jnp.swapaxes(t, -1, -2) for t in (qh, kh, vh))
                km = _sample_kmean(k_pool_t, page_table)
                idx = _sample_select(qh, km)[:, :, :length, :MOBA_TOPK]
                o = _sample_attn(qt, kt, vt, idx, k_pool_t, v_pool_t, page_table, length)
                o_att = o[..., :length].transpose(0, 3, 1, 2).reshape(m, MOBA_INNER).astype(BF16)
                k_rows, v_rows = kh[:, :, :length], vh[:, :, :length]
            wo = p['w_out_a'][ia].astype(BF16)
            mix_acts, mix_ws = [y_ssd, o_att], [wo[:SSD_INNER], wo[SSD_INNER:]]
            ssm_c, ssm_s = hist, s_new
        else:
            ic = layer // 2
            assert ic == 0
            wc = p['w_in_c'][ic].astype(BF16)
            ws = [wc[:, :HGRN_K], wc[:, HGRN_K:2 * HGRN_K], wc[:, 2 * HGRN_K:2 * HGRN_K + HGRN_V],
                  wc[:, 2 * HGRN_K + HGRN_V:]]
            qa, fx, iv, g = _norm_matmul(x, nw, ws, [F32, F32, BF16, F32])
            lp = _round_up(length, _chunk_len(length, HGRN_CHUNK))
            o, s_new = _gla(*(_pad_seq(t, bsz, length, lp) for t in (qa, fx, iv, g)),
                            p['hgrn_lb_gamma'], p['hgrn_norm_w'][ic], hgrn0[ic], length, layer)
            mix_acts, mix_ws = [o[:, :length].reshape(m, HGRN_V)], [p['w_out_c'][ic].astype(BF16)]
            hgrn_s = s_new
        nf = p['norm_final'].reshape(1, -1) if layer == DEPTH - 1 else None
        hist = ffn_conv0[layer, 0] if prompt else ffn_conv0[layer]
        x, fh = _ffn(x, mix_acts, mix_ws, p['norm_ffn'][layer].reshape(1, -1),
                     p['ffn_w_up'].astype(BF16), p['ffn_conv_w'][layer],
                     p['ffn_conv_b'][layer].reshape(1, -1), p['ffn_w_down'].astype(BF16), layer, hist,
                     None if prompt else length, nf)
        ffn_c.append(fh[None] if prompt else fh)
    return (x.reshape(bsz, length, D_MODEL), k_rows[None], v_rows[None], ssm_s[None], ssm_c[None],
            hgrn_s[None], jnp.stack(ffn_c))


def kernel(x_prompt, x_sample, cache_k_pool, cache_v_pool, page_table, state_ssm, state_ssm_conv, state_hgrn,
           state_ffn_conv, norm_mix, norm_ffn, norm_final, w_in_a, w_out_a, ssd_conv_w, ssd_conv_b,
           ssd_dt_bias, ssd_a_log, ssd_d, ssd_norm_w, w_in_c, w_out_c, hgrn_lb_gamma, hgrn_norm_w, ffn_w_up,
           ffn_conv_w, ffn_conv_b, ffn_w_down):
    p = dict(norm_mix=norm_mix, norm_ffn=norm_ffn, norm_final=norm_final, w_in_a=w_in_a, w_out_a=w_out_a,
             ssd_conv_w=ssd_conv_w, ssd_conv_b=ssd_conv_b, ssd_dt_bias=ssd_dt_bias, ssd_a_log=ssd_a_log,
             ssd_d=ssd_d, ssd_norm_w=ssd_norm_w, w_in_c=w_in_c, w_out_c=w_out_c, hgrn_lb_gamma=hgrn_lb_gamma,
             hgrn_norm_w=hgrn_norm_w, ffn_w_up=ffn_w_up, ffn_conv_w=ffn_conv_w, ffn_conv_b=ffn_conv_b,
             ffn_w_down=ffn_w_down)
    bp, sp, _ = x_prompt.shape
    db, ds, _ = x_sample.shape
    na, nc = (DEPTH + 1) // 2, DEPTH // 2
    zeros = lambda *s: jnp.zeros(s, F32)
    prompt_caches = (zeros(na, bp, SSD_CONV - 1, SSD_CONV_DIM), zeros(na, bp, SSD_HEADS, SSD_HEAD_DIM, SSD_STATE),
                     zeros(nc, bp, HGRN_HEADS, HGRN_KEY_DIM, HGRN_VAL_DIM), zeros(DEPTH, bp, FFN_CONV - 1, FFN_DIM))
    outs_p = _trunk(x_prompt.reshape(bp * sp, D_MODEL), bp, sp, True, prompt_caches, p)
    sample_caches = (state_ssm_conv, state_ssm, state_hgrn, state_ffn_conv, cache_k_pool, cache_v_pool, page_table)
    outs_s = _trunk(x_sample.reshape(db * ds, D_MODEL), db, ds, False, sample_caches, p)
    yp, krp, vrp, ssp, scp, hgp, fcp = outs_p
    ys, krs, vrs, sss, scs, hgs, fcs = outs_s
    return (yp, ys, krp, vrp, ssp, scp, hgp, fcp, krs, vrs, sss, scs, hgs, fcs)
```

```python
import functools
import math

import numpy as np
import jax
import jax.numpy as jnp
from jax import lax
from jax.experimental import pallas as pl
from jax.experimental.pallas import tpu as pltpu

F32 = jnp.float32
BF16 = jnp.bfloat16
HIGHEST = lax.Precision.HIGHEST

D_MODEL = 1024
DEPTH = 2
PAGE_SIZE = 128
SSD_HEADS = 8
SSD_HEAD_DIM = 64
SSD_INNER = SSD_HEADS * SSD_HEAD_DIM
SSD_STATE = 64
SSD_GROUPS = 2
SSD_CONV = 4
SSD_CHUNK = 128
SSD_GN = SSD_GROUPS * SSD_STATE
SSD_CONV_DIM = SSD_INNER + 2 * SSD_GN
MOBA_HEADS = 8
MOBA_HEAD_DIM = 64
MOBA_INNER = MOBA_HEADS * MOBA_HEAD_DIM
MOBA_BLOCK = 256
MOBA_TOPK = 3
OFF_XBC = SSD_INNER
OFF_DT = OFF_XBC + SSD_CONV_DIM
OFF_Q = OFF_DT + SSD_HEADS
OFF_K = OFF_Q + MOBA_INNER
OFF_V = OFF_K + MOBA_INNER
HGRN_HEADS = 8
HGRN_KEY_DIM = 128
HGRN_VAL_DIM = D_MODEL // HGRN_HEADS
HGRN_K = HGRN_HEADS * HGRN_KEY_DIM
HGRN_V = HGRN_HEADS * HGRN_VAL_DIM
HGRN_CHUNK = 64
FFN_DIM = 2816
FFN_CONV = 3
RMS_EPS = 1e-6

LANES = 128
SUBLANES = 8
VMEM_LIMIT_BYTES = 56 * 1024 * 1024

ROW_TILE = 256
FFN_ROW_TILE = 512
FFN_COL_CHUNK = 256
NEG_BIG = -1e30
LOG2E = math.log2(math.e)
ALIBI_PARTS = 3
MOBA_VT_ROWS = MOBA_HEAD_DIM + 16
BOUND_SLACK = 0.02
UNDERFLOW_LOG2 = 160.0
MOBA_AHEAD = 2
MOBA_SLOTS = MOBA_AHEAD + 1
MOBA_UNROLL = 4 * MOBA_SLOTS


def _cparams(sem, flags=None):
    return pltpu.CompilerParams(dimension_semantics=sem, vmem_limit_bytes=VMEM_LIMIT_BYTES, flags=flags)


def _const_spec(shape):
    nd = len(shape)
    return pl.BlockSpec(shape, lambda *_: (0,) * nd)


def _rms(x, w):
    y = x * lax.rsqrt(jnp.mean(x * x, axis=-1, keepdims=True) + RMS_EPS)
    return y * w


def _rms_bf16(x, w):
    return _rms(x, w).astype(BF16)


def _dot(a, b):
    return jnp.dot(a, b, preferred_element_type=F32)


def _dot_split(a, b):
    a_hi = a.astype(BF16)
    a_lo = (a - a_hi.astype(F32)).astype(BF16)
    b_hi = b.astype(BF16)
    b_lo = (b - b_hi.astype(F32)).astype(BF16)
    return _dot(jnp.concatenate([a_hi, a_lo, a_hi], axis=1), jnp.concatenate([b_hi, b_hi, b_lo], axis=0))


def _dot_nt(a, b):
    return lax.dot_general(a, b, (((1,), (1,)), ((), ())), preferred_element_type=F32)


def _dot_tn(a, b):
    return lax.dot_general(a, b, (((0,), (0,)), ((), ())), preferred_element_type=F32)


def _tril_f32(n):
    r = lax.broadcasted_iota(jnp.int32, (n, n), 0)
    c = lax.broadcasted_iota(jnp.int32, (n, n), 1)
    return r >= c


def _cumsum_rows(x, q, split):
    if not split:
        return jnp.dot(_tril_f32(q).astype(F32), x, precision=HIGHEST, preferred_element_type=F32)
    tri = _tril_f32(q).astype(BF16)
    acc = None
    rest = x
    for _ in range(3):
        term = rest.astype(BF16)
        part = _dot(tri, term)
        acc = part if acc is None else acc + part
        rest = rest - term.astype(F32)
    return acc


def _silu(x):
    return x * jax.nn.sigmoid(x)


def _inproj_a_kernel(x_ref, nw_ref, wz, wxbc, wdt, wq, wk, wv, *rest, prompt):
    outs = rest[1:] if prompt else rest
    h = _rms(x_ref[...], nw_ref[...])
    hb = h.astype(BF16)
    z_o, xbc_o, dt_o, q_o = outs[:4]
    z_o[...] = _dot(hb, wz[...])
    xbc_o[...] = _dot(hb, wxbc[...])
    dt_o[...] = _dot(hb, wdt[...])
    if wq.dtype == F32:
        q = jnp.dot(h, wq[...], precision=HIGHEST, preferred_element_type=F32)
    else:
        q = _dot(hb, wq[...])
    q_o[...] = q
    k = _dot(hb, wk[...])
    v = _dot(hb, wv[...])
    if not prompt:
        k_o, v_o = outs[4:]
        k_o[...] = k
        v_o[...] = v
        return
    kr_o, vr_o, ka_o, vt_o, km_o, bnd_o = outs[4:]
    tm = k.shape[0]
    head_sel = rest[0][...]
    k_norm = jnp.sqrt(jnp.max(_dot((k * k).astype(BF16), head_sel), axis=0, keepdims=True))
    q_norm = jnp.sqrt(jnp.max(_dot((q * q).astype(BF16), head_sel), axis=0, keepdims=True))
    qk_min = jnp.min(_dot((q * k).astype(BF16), head_sel), axis=0, keepdims=True)
    row8 = lax.broadcasted_iota(jnp.int32, (SUBLANES, LANES), 0)
    bnd_o[0] = jnp.where(row8 == 0, k_norm, jnp.where(row8 == 1, q_norm, jnp.where(row8 == 2, qk_min, 0.0)))
    lane = lax.broadcasted_iota(jnp.int32, (tm, LANES), 1)
    key_aug = _key_aug_columns(tm)
    for hp in range(MOBA_HEADS // 2):
        pair = k[:, hp * LANES:(hp + 1) * LANES]
        for j, kh in enumerate((pair, pltpu.roll(pair, MOBA_HEAD_DIM, 1))):
            ka_o[2 * hp + j, 0] = jnp.where(lane < MOBA_HEAD_DIM, kh, key_aug).astype(BF16)
    k_t = k.T
    v_t = v.T
    for p in range(tm // PAGE_SIZE):
        toks = slice(p * PAGE_SIZE, (p + 1) * PAGE_SIZE)
        for h in range(MOBA_HEADS):
            dims = slice(h * MOBA_HEAD_DIM, (h + 1) * MOBA_HEAD_DIM)
            kr_o[p, h] = k_t[dims, toks]
            vr_o[p, h] = v_t[dims, toks]
    ones_rows = (lax.broadcasted_iota(jnp.int32, (MOBA_VT_ROWS - MOBA_HEAD_DIM, tm), 0) == 0).astype(F32)
    pieces = []
    for h in range(MOBA_HEADS):
        pieces += [v_t[h * MOBA_HEAD_DIM:(h + 1) * MOBA_HEAD_DIM, :], ones_rows]
    vt_o[0] = jnp.concatenate(pieces, axis=0).astype(BF16)
    km_o[0] = jnp.mean(k, axis=0, keepdims=True)


def _inproj_a(x, nw, ws, prompt):
    m = x.shape[0]
    tm = ROW_TILE if prompt else m
    assert m % tm == 0
    if prompt:
        assert tm == MOBA_BLOCK
    nt = m // tm
    widths = [w.shape[1] for w in ws]
    row = lambda i: (i, 0)
    in_specs = [pl.BlockSpec((tm, D_MODEL), row), _const_spec((1, D_MODEL))]
    in_specs += [_const_spec(w.shape) for w in ws]
    out_shape = [jax.ShapeDtypeStruct((m, n), F32) for n in widths[:4]]
    out_specs = [pl.BlockSpec((tm, n), row) for n in widths[:4]]
    if prompt:
        npg = m // PAGE_SIZE
        ppt = tm // PAGE_SIZE
        rows_shape = (npg, MOBA_HEADS, MOBA_HEAD_DIM, PAGE_SIZE)
        rows_spec = pl.BlockSpec((ppt, MOBA_HEADS, MOBA_HEAD_DIM, PAGE_SIZE), lambda i: (i, 0, 0, 0))
        out_shape += [jax.ShapeDtypeStruct(rows_shape, F32)] * 2
        out_specs += [rows_spec, rows_spec]
        out_shape += [jax.ShapeDtypeStruct((MOBA_HEADS, nt, tm, LANES), BF16),
                      jax.ShapeDtypeStruct((nt, MOBA_HEADS * MOBA_VT_ROWS, tm), BF16),
                      jax.ShapeDtypeStruct((nt, 1, MOBA_INNER), F32),
                      jax.ShapeDtypeStruct((nt, SUBLANES, LANES), F32)]
        out_specs += [pl.BlockSpec((MOBA_HEADS, 1, tm, LANES), lambda i: (0, i, 0, 0)),
                      pl.BlockSpec((1, MOBA_HEADS * MOBA_VT_ROWS, tm), lambda i: (i, 0, 0)),
                      pl.BlockSpec((1, 1, MOBA_INNER), lambda i: (i, 0, 0)),
                      pl.BlockSpec((1, SUBLANES, LANES), lambda i: (i, 0, 0))]
        head_sel = (jnp.arange(MOBA_INNER)[:, None] // MOBA_HEAD_DIM == jnp.arange(LANES)[None, :]).astype(BF16)
        extra, extra_specs = [head_sel], [_const_spec(head_sel.shape)]
    else:
        out_shape += [jax.ShapeDtypeStruct((m, MOBA_INNER), F32)] * 2
        out_specs += [pl.BlockSpec((tm, MOBA_INNER), row)] * 2
        extra, extra_specs = [], []
    return pl.pallas_call(
        functools.partial(_inproj_a_kernel, prompt=prompt),
        grid=(nt,), in_specs=in_specs + extra_specs, out_specs=out_specs, out_shape=out_shape,
        compiler_params=_cparams(("arbitrary",)), name="inproj_a",
    )(x, nw, *ws, *extra)


def _norm_matmul_kernel(x_ref, nw_ref, *refs):
    n = len(refs) // 2
    hb = _rms_bf16(x_ref[...], nw_ref[...])
    for w, o in zip(refs[:n], refs[n:]):
        o[...] = _dot(hb, w[...]).astype(o.dtype)


def _norm_matmul(x, nw, ws, out_dtypes):
    m = x.shape[0]
    tm = min(FFN_ROW_TILE, m)
    assert m % tm == 0
    row = lambda i: (i, 0)
    in_specs = [pl.BlockSpec((tm, D_MODEL), row), _const_spec((1, D_MODEL))]
    in_specs += [pl.BlockSpec(w.shape, lambda i: (0, 0), pipeline_mode=pl.Buffered(1)) for w in ws]
    return pl.pallas_call(
        _norm_matmul_kernel, grid=(m // tm,), in_specs=in_specs,
        out_specs=[pl.BlockSpec((tm, w.shape[1]), row) for w in ws],
        out_shape=[jax.ShapeDtypeStruct((m, w.shape[1]), dt) for w, dt in zip(ws, out_dtypes)],
        compiler_params=_cparams(("arbitrary",)), name="norm_matmul",
    )(x, nw, *ws)


def _ffn_kernel(*refs, tm, seq_len, final_norm, n_mix):
    it = iter(refs)
    x_ref = next(it)
    mix_refs = [next(it) for _ in range(2 * n_mix)]
    nw_ref, wup_ref, cw_ref, cb_ref, wdn_ref = (next(it) for _ in range(5))
    if seq_len is None:
        hist_ref = next(it)
    else:
        h1_ref, h2_ref = next(it), next(it)
    nf_ref = next(it) if final_norm else None
    o_ref = next(it)
    tail_ref = next(it)
    up_s, act_s = next(it), next(it)

    i = pl.program_id(0)
    f = FFN_DIM
    if seq_len is None:
        @pl.when(i == 0)
        def _():
            up_s[0:SUBLANES, 0:f] = hist_ref[...]
    else:
        up_s[0:SUBLANES, 0:f] = jnp.zeros((SUBLANES, f), F32)

    x = x_ref[...]
    for a_ref, w_ref in zip(mix_refs[:n_mix], mix_refs[n_mix:]):
        x = x + _dot(a_ref[...], w_ref[...])
    hb = _rms_bf16(x, nw_ref[...])
    up_s[SUBLANES:SUBLANES + tm, :] = _dot(hb, wup_ref[0])

    if seq_len is not None:
        t = lax.broadcasted_iota(jnp.int32, (tm, 1), 0) % seq_len
    for c in range(0, f, FFN_COL_CHUNK):
        cols = slice(c, c + FFN_COL_CHUNK)
        g0 = up_s[SUBLANES:SUBLANES + tm, cols]
        g1 = up_s[SUBLANES - 1:SUBLANES - 1 + tm, cols]
        g2 = up_s[SUBLANES - 2:SUBLANES - 2 + tm, cols]
        if seq_len is not None:
            g1 = jnp.where(t >= 1, g1, h1_ref[:, cols])
            g2 = jnp.where(t >= 2, g2, h2_ref[:, cols])
        val = up_s[SUBLANES:SUBLANES + tm, f + c:f + c + FFN_COL_CHUNK]
        conv = g2 * cw_ref[0:1, cols]
        conv = conv + g1 * cw_ref[1:2, cols]
        conv = conv + g0 * cw_ref[2:3, cols]
        conv = conv + cb_ref[:, cols]
        act_s[:, cols] = (_silu(conv) * val).astype(BF16)

    out = x + _dot(act_s[...], wdn_ref[0])
    o_ref[...] = _rms(out, nf_ref[...]) if final_norm else out

    if seq_len is None:
        last = up_s[tm:tm + SUBLANES, 0:f]
        up_s[0:SUBLANES, 0:f] = last

        @pl.when(i == pl.num_programs(0) - 1)
        def _():
            tail_ref[...] = last
    else:
        tail_ref[...] = up_s[SUBLANES:SUBLANES + tm, 0:f]


def _ffn(x, mix_acts, mix_ws, nw, wup, cw, cb, wdn, layer, hist, seq_len, nf):
    assert FFN_CONV == 3
    m = x.shape[0]
    f = FFN_DIM
    final_norm = nf is not None
    row = lambda i: (i, 0)
    if seq_len is None:
        tm = FFN_ROW_TILE
        hist8 = jnp.zeros((SUBLANES, f), F32).at[SUBLANES - 2:].set(hist)
        extra = [hist8]
        extra_specs = [_const_spec((SUBLANES, f))]
        tail_shape = (SUBLANES, f)
    else:
        tm = m
        nb = m // seq_len
        assert seq_len >= 2 and nb * seq_len == m
        h1 = jnp.zeros((nb, seq_len, f), F32).at[:, 0].set(hist[:, 1]).reshape(m, f)
        h2 = jnp.zeros((nb, seq_len, f), F32).at[:, 0:2].set(hist).reshape(m, f)
        extra = [h1, h2]
        extra_specs = [_const_spec((m, f))] * 2
        tail_shape = (m, f)
    assert m % tm == 0
    ins = [x] + list(mix_acts) + list(mix_ws) + [nw, wup, cw, cb, wdn] + extra
    in_specs = [pl.BlockSpec((tm, D_MODEL), row)]
    in_specs += [pl.BlockSpec((tm, a.shape[1]), row) for a in mix_acts]
    once = pl.Buffered(1)
    in_specs += [pl.BlockSpec(w.shape, lambda i: (0, 0), pipeline_mode=once) for w in mix_ws]
    layer_slab = lambda w: pl.BlockSpec((1,) + w.shape[1:], lambda i: (layer, 0, 0), pipeline_mode=once)
    in_specs += [_const_spec((1, D_MODEL)), layer_slab(wup), _const_spec(cw.shape),
                 _const_spec(cb.shape), layer_slab(wdn)] + extra_specs
    if final_norm:
        ins.append(nf)
        in_specs.append(_const_spec((1, D_MODEL)))
    xo, tail = pl.pallas_call(
        functools.partial(_ffn_kernel, tm=tm, seq_len=seq_len, final_norm=final_norm, n_mix=len(mix_acts)),
        grid=(m // tm,), in_specs=in_specs,
        out_specs=[pl.BlockSpec((tm, D_MODEL), row), _const_spec(tail_shape)],
        out_shape=[jax.ShapeDtypeStruct((m, D_MODEL), F32), jax.ShapeDtypeStruct(tail_shape, F32)],
        scratch_shapes=[pltpu.VMEM((SUBLANES + tm, 2 * f), F32), pltpu.VMEM((tm, f), BF16)],
        compiler_params=_cparams(("arbitrary",)), name="conv_ffn",
    )(*ins)
    if seq_len is None:
        new_hist = tail[SUBLANES - 2:]
    else:
        new_hist = tail.reshape(m // seq_len, seq_len, f)[:, seq_len - 2:]
    return xo, new_hist


def _ssd_kernel(z_ref, xbc_ref, dt_ref, hist_ref, s0_ref, cw_ref, cb_ref, dtb_ref, alog_ref, dsk_ref,
                nw_ref, y_ref, hist_o, s_o, cbuf, st, ybuf, m1_s, xdt_s, ce_s, xw_s, bm_s, xsd_s, zs_s, el_s,
                *, q, seq_len, n_chunks, pipelined):
    c = pl.program_id(1)
    stage = (m1_s, xdt_s, ce_s, xw_s, bm_s, xsd_s, zs_s, el_s)

    def state_stage(rd):
        _ssd_state_stage(rd, nw_ref, y_ref, st, ybuf, *stage)

    def free_stage(wr):
        _ssd_free_stage(c, wr, z_ref, xbc_ref, dt_ref, cw_ref, cb_ref, dtb_ref, alog_ref, dsk_ref, hist_o, cbuf,
                        *stage, q=q, seq_len=seq_len, n_chunks=n_chunks)

    @pl.when(c == 0)
    def _():
        cbuf[0:SUBLANES, :] = hist_ref[0]
        st[...] = s0_ref[0]
        if pipelined:
            for ref in stage[:-1]:
                ref[1] = jnp.zeros(ref.shape[1:], ref.dtype)
            el_s[1] = jnp.ones(el_s.shape[1:], F32)

    if pipelined:
        @pl.when(c % 2 == 0)
        def _():
            state_stage(1)
            free_stage(0)

        @pl.when(c % 2 == 1)
        def _():
            state_stage(0)
            free_stage(1)
    else:
        free_stage(0)
        state_stage(0)

    @pl.when(c == (n_chunks if pipelined else n_chunks - 1))
    def _():
        s_o[0] = st[...]


def _ssd_state_stage(rd, nw_ref, y_ref, st, ybuf, m1_s, xdt_s, ce_s, xw_s, bm_s, xsd_s, zs_s, el_s):
    rep = SSD_HEADS // SSD_GROUPS
    hd = SSD_HEAD_DIM
    e_last = el_s[rd, 0:1, :]
    y_state = [_dot_nt(ce_s[rd, h], st[h].astype(BF16)) for h in range(SSD_HEADS)]
    s_inc = [_dot_tn(xw_s[rd, h], bm_s[rd, h // rep]) for h in range(SSD_HEADS)]
    y_intra = [_dot(m1_s[rd, h], xdt_s[rd, h]) for h in range(SSD_HEADS)]
    for h in range(SSD_HEADS):
        st[h] = st[h] * e_last[:, h:h + 1] + s_inc[h]
        ybuf[:, h * hd:(h + 1) * hd] = y_intra[h] + y_state[h]
    yz = (ybuf[...] + xsd_s[rd]) * zs_s[rd]
    gw = SSD_INNER // SSD_GROUPS
    for g in range(SSD_GROUPS):
        yg = yz[:, g * gw:(g + 1) * gw]
        yn = yg * lax.rsqrt(jnp.mean(yg * yg, axis=-1, keepdims=True) + RMS_EPS)
        y_ref[0, :, g * gw:(g + 1) * gw] = (yn * nw_ref[:, g * gw:(g + 1) * gw]).astype(BF16)


def _ssd_free_stage(c, wr, z_ref, xbc_ref, dt_ref, cw_ref, cb_ref, dtb_ref, alog_ref, dsk_ref, hist_o, cbuf,
                    m1_s, xdt_s, ce_s, xw_s, bm_s, xsd_s, zs_s, el_s, *, q, seq_len, n_chunks):
    rep = SSD_HEADS // SSD_GROUPS
    hd = SSD_HEAD_DIM
    hrows = SSD_CONV - 1
    h0 = SUBLANES - hrows
    cbuf[SUBLANES:SUBLANES + q, :] = xbc_ref[0]
    conv = cbuf[h0:h0 + q, :] * cw_ref[0:1, :]
    for j in range(1, SSD_CONV):
        conv = conv + cbuf[h0 + j:h0 + j + q, :] * cw_ref[j:j + 1, :]
    xc = _silu(conv + cb_ref[...])

    xdt_raw = dt_ref[0] + dtb_ref[...]
    dt = jnp.maximum(xdt_raw, 0.0) + jnp.log1p(jnp.exp(-jnp.abs(xdt_raw)))
    if seq_len % q != 0:
        row = jnp.minimum(c, n_chunks - 1) * q + lax.broadcasted_iota(jnp.int32, (q, 1), 0)
        dt = jnp.where(row < seq_len, dt, 0.0)
    a = -jnp.exp(alog_ref[...])
    acum = _cumsum_rows(dt * a, q, split=False)
    acum_t = acum.T
    a_last = acum[q - 1:q, :]
    e_acum = jnp.exp(acum)
    w_in = jnp.exp(a_last - acum) * dt
    el_s[wr] = jnp.broadcast_to(jnp.exp(a_last), (SUBLANES, LANES))
    tril = _tril_f32(q)

    cbs, cms = [], []
    for g in range(SSD_GROUPS):
        bm = xc[:, SSD_INNER + g * SSD_STATE:SSD_INNER + (g + 1) * SSD_STATE].astype(BF16)
        cm = xc[:, SSD_INNER + SSD_GN + g * SSD_STATE:SSD_INNER + SSD_GN + (g + 1) * SSD_STATE]
        bm_s[wr, g] = bm
        cms.append(cm)
        cbs.append(_dot_nt(cm.astype(BF16), bm))

    for h in range(SSD_HEADS):
        g = h // rep
        xs = xc[:, h * hd:(h + 1) * hd]
        seg = acum[:, h:h + 1] - acum_t[h:h + 1, :]
        decay = jnp.exp(jnp.where(tril, seg, -jnp.inf))
        m1_s[wr, h] = (cbs[g] * decay).astype(BF16)
        xdt_s[wr, h] = (xs * dt[:, h:h + 1]).astype(BF16)
        ce_s[wr, h] = (cms[g] * e_acum[:, h:h + 1]).astype(BF16)
        xw_s[wr, h] = (xs * w_in[:, h:h + 1]).astype(BF16)
        xsd_s[wr, :, h * hd:(h + 1) * hd] = xs * dsk_ref[:, h:h + 1]
    zs_s[wr] = _silu(z_ref[0])

    @pl.when(c == n_chunks - 1)
    def _():
        l_last = seq_len - (n_chunks - 1) * q
        hist_o[0] = cbuf[l_last:l_last + SUBLANES, :]

    cbuf[0:SUBLANES, :] = cbuf[q:q + SUBLANES, :]


BF16_SUBLANES = 16


def _chunk_len(seq_len, max_chunk):
    return min(max_chunk, _round_up(seq_len, BF16_SUBLANES))


def _stage_grid(n_chunks, pipelined):
    if not pipelined:
        same = lambda i, c: (i, c, 0)
        return n_chunks, same, same
    return (n_chunks + 1, lambda i, c: (i, jnp.minimum(c, n_chunks - 1), 0),
            lambda i, c: (i, jnp.maximum(c - 1, 0), 0))


def _pad_lanes(v, fill=0.0):
    return jnp.full((1, LANES), fill, F32).at[0, :v.shape[0]].set(v.astype(F32))


def _ssd(z, xbc, dtr, hist, s0, conv_w, conv_b, dt_bias, a_log, d_skip, norm_w, seq_len):
    b, lp, _ = z.shape
    q = _chunk_len(seq_len, SSD_CHUNK)
    assert lp % q == 0 and lp - seq_len < q
    hrows = SSD_CONV - 1
    hist8 = jnp.zeros((b, SUBLANES, SSD_CONV_DIM), F32).at[:, SUBLANES - hrows:].set(hist)
    cw8 = jnp.zeros((SUBLANES, SSD_CONV_DIM), F32).at[:SSD_CONV].set(conv_w)
    n_chunks = lp // q
    pipelined = False
    n_steps, seq, seq_out = _stage_grid(n_chunks, pipelined)
    per_b3 = lambda i, c: (i, 0, 0)
    per_b4 = lambda i, c: (i, 0, 0, 0)
    in_specs = [pl.BlockSpec((1, q, SSD_INNER), seq), pl.BlockSpec((1, q, SSD_CONV_DIM), seq),
                pl.BlockSpec((1, q, LANES), seq), pl.BlockSpec((1, SUBLANES, SSD_CONV_DIM), per_b3),
                pl.BlockSpec((1, SSD_HEADS, SSD_HEAD_DIM, SSD_STATE), per_b4),
                _const_spec((SUBLANES, SSD_CONV_DIM)), _const_spec((1, SSD_CONV_DIM)),
                _const_spec((1, LANES)), _const_spec((1, LANES)), _const_spec((1, LANES)),
                _const_spec((1, SSD_INNER))]
    out_shape = [jax.ShapeDtypeStruct((b, lp, SSD_INNER), BF16),
                 jax.ShapeDtypeStruct((b, SUBLANES, SSD_CONV_DIM), F32),
                 jax.ShapeDtypeStruct((b, SSD_HEADS, SSD_HEAD_DIM, SSD_STATE), F32)]
    out_specs = [pl.BlockSpec((1, q, SSD_INNER), seq_out), pl.BlockSpec((1, SUBLANES, SSD_CONV_DIM), per_b3),
                 pl.BlockSpec((1, SSD_HEADS, SSD_HEAD_DIM, SSD_STATE), per_b4)]
    per_head = pltpu.VMEM((2, SSD_HEADS, q, SSD_HEAD_DIM), BF16)
    y, hist_o, s_o = pl.pallas_call(
        functools.partial(_ssd_kernel, q=q, seq_len=seq_len, n_chunks=n_chunks, pipelined=pipelined),
        grid=(b, n_steps), in_specs=in_specs, out_specs=out_specs, out_shape=out_shape,
        scratch_shapes=[pltpu.VMEM((SUBLANES + q, SSD_CONV_DIM), F32),
                        pltpu.VMEM((SSD_HEADS, SSD_HEAD_DIM, SSD_STATE), F32),
                        pltpu.VMEM((q, SSD_INNER), F32),
                        pltpu.VMEM((2, SSD_HEADS, q, q), BF16), per_head, per_head, per_head,
                        pltpu.VMEM((2, SSD_GROUPS, q, SSD_STATE), BF16),
                        pltpu.VMEM((2, q, SSD_INNER), F32), pltpu.VMEM((2, q, SSD_INNER), F32),
                        pltpu.VMEM((2, SUBLANES, LANES), F32)],
        compiler_params=_cparams(("arbitrary", "arbitrary")), name="ssd",
    )(z, xbc, dtr, hist8, s0, cw8, conv_b.reshape(1, -1), _pad_lanes(dt_bias), _pad_lanes(a_log),
      _pad_lanes(d_skip), norm_w.reshape(1, -1))
    return y, hist_o[:, SUBLANES - hrows:], s_o


def _gla_kernel(q_ref, fx_ref, iv_ref, g_ref, gam_ref, nw_ref, s0_ref, o_ref, s_o, st_t, qd_s, kd_s, kdec_s,
                v_s, gs_s, el_s, *, q, seq_len, layer, n_chunks):
    c = pl.program_id(1)
    pipelined = n_chunks > 1
    stage = (qd_s, kd_s, kdec_s, v_s, gs_s, el_s)

    def state_stage(rd):
        _gla_state_stage(rd, nw_ref, o_ref, st_t, *stage, q=q)

    def free_stage(wr):
        _gla_free_stage(c, wr, q_ref, fx_ref, iv_ref, g_ref, gam_ref, *stage, q=q, seq_len=seq_len, layer=layer,
                        n_chunks=n_chunks)

    @pl.when(c == 0)
    def _():
        for h in range(HGRN_HEADS):
            st_t[h] = s0_ref[0, h].T
        if pipelined:
            for ref in stage[:-1]:
                ref[1] = jnp.zeros(ref.shape[1:], ref.dtype)
            el_s[1] = jnp.ones(el_s.shape[1:], F32)

    if pipelined:
        @pl.when(c % 2 == 0)
        def _():
            state_stage(1)
            free_stage(0)

        @pl.when(c % 2 == 1)
        def _():
            state_stage(0)
            free_stage(1)
    else:
        free_stage(0)
        state_stage(0)

    @pl.when(c == (n_chunks if pipelined else n_chunks - 1))
    def _():
        for h in range(HGRN_HEADS):
            s_o[0, h] = st_t[h].T


def _gla_state_stage(rd, nw_ref, o_ref, st_t, qd_s, kd_s, kdec_s, v_s, gs_s, el_s, *, q):
    tril = _tril_f32(q)
    ks = [slice(h * HGRN_KEY_DIM, (h + 1) * HGRN_KEY_DIM) for h in range(HGRN_HEADS)]
    vs = [slice(h * HGRN_VAL_DIM, (h + 1) * HGRN_VAL_DIM) for h in range(HGRN_HEADS)]
    qd_b = [qd_s[rd, :, s] for s in ks]
    v_b = [v_s[rd, :, s] for s in vs]
    e_last = el_s[rd, 0:1, :]
    att = [_dot_nt(qd_b[h], kd_s[rd, :, ks[h]]) for h in range(HGRN_HEADS)]
    o_state = [_dot_nt(qd_b[h], st_t[h].astype(BF16)) for h in range(HGRN_HEADS)]
    kv = [_dot_tn(v_b[h], kdec_s[rd, :, ks[h]]) for h in range(HGRN_HEADS)]
    for h in range(HGRN_HEADS):
        o = _dot(jnp.where(tril, att[h], 0.0).astype(BF16), v_b[h]) + o_state[h]
        st_t[h] = st_t[h] * e_last[:, ks[h]] + kv[h]
        on = o * lax.rsqrt(jnp.mean(o * o, axis=-1, keepdims=True) + RMS_EPS) * nw_ref[...]
        o_ref[0, :, vs[h]] = (on * gs_s[rd, :, vs[h]]).astype(BF16)


def _gla_free_stage(c, wr, q_ref, fx_ref, iv_ref, g_ref, gam_ref, qd_s, kd_s, kdec_s, v_s, gs_s, el_s, *, q,
                    seq_len, layer, n_chunks):
    rows = [gam_ref[l:l + 1, :] for l in range(DEPTH)]
    mx = functools.reduce(jnp.maximum, rows)
    es = [jnp.exp(r - mx) for r in rows]
    lb = sum(es[1:layer + 1]) / sum(es) if layer >= 1 else jnp.zeros_like(mx)

    f = lb + (1.0 - lb) * jax.nn.sigmoid(fx_ref[0])
    if seq_len % q != 0:
        row = jnp.minimum(c, n_chunks - 1) * q + lax.broadcasted_iota(jnp.int32, (q, 1), 0)
        f = jnp.where(row < seq_len, f, 1.0)
    kk = 1.0 - f
    b = _cumsum_rows(jnp.log(f), q, split=True)
    b_last = b[q - 1:q, :]
    qd_s[wr] = (q_ref[0] * jnp.exp(b)).astype(BF16)
    kd_s[wr] = (kk * jnp.exp(-b)).astype(BF16)
    kdec_s[wr] = (kk * jnp.exp(b_last - b)).astype(BF16)
    v_s[wr] = iv_ref[0].astype(BF16)
    gs_s[wr] = _silu(g_ref[0])
    el_s[wr] = jnp.broadcast_to(jnp.exp(b_last), (SUBLANES, HGRN_K))


def _gla(qa, fx, iv, g, gamma, norm_w, s0, seq_len, layer):
    b, lp, _ = qa.shape
    q = _chunk_len(seq_len, HGRN_CHUNK)
    assert lp % q == 0 and lp - seq_len < q
    gam8 = jnp.zeros((SUBLANES, HGRN_K), F32).at[:DEPTH].set(gamma)
    n_chunks = lp // q
    n_steps, seq, seq_out = _stage_grid(n_chunks, n_chunks > 1)
    per_b4 = lambda i, c: (i, 0, 0, 0)
    st_shape = (HGRN_HEADS, HGRN_KEY_DIM, HGRN_VAL_DIM)
    in_specs = [pl.BlockSpec((1, q, HGRN_K), seq), pl.BlockSpec((1, q, HGRN_K), seq),
                pl.BlockSpec((1, q, HGRN_V), seq), pl.BlockSpec((1, q, HGRN_V), seq),
                _const_spec((SUBLANES, HGRN_K)), _const_spec((1, HGRN_VAL_DIM)),
                pl.BlockSpec((1,) + st_shape, per_b4)]
    stage_k = pltpu.VMEM((2, q, HGRN_K), BF16)
    o, s_o = pl.pallas_call(
        functools.partial(_gla_kernel, q=q, seq_len=seq_len, layer=layer, n_chunks=n_chunks),
        grid=(b, n_steps), in_specs=in_specs,
        out_specs=[pl.BlockSpec((1, q, HGRN_V), seq_out), pl.BlockSpec((1,) + st_shape, per_b4)],
        out_shape=[jax.ShapeDtypeStruct((b, lp, HGRN_V), BF16), jax.ShapeDtypeStruct((b,) + st_shape, F32)],
        scratch_shapes=[pltpu.VMEM((HGRN_HEADS, HGRN_VAL_DIM, HGRN_KEY_DIM), F32),
                        stage_k, stage_k, stage_k, pltpu.VMEM((2, q, HGRN_V), BF16),
                        pltpu.VMEM((2, q, HGRN_V), F32), pltpu.VMEM((2, SUBLANES, HGRN_K), F32)],
        compiler_params=_cparams(("arbitrary", "arbitrary")), name="gla",
    )(qa, fx, iv, g, gam8, norm_w.reshape(1, -1), s0)
    return o, s_o


def _select_topk_rows(gate_t, n_valid, nblk, width):
    blk = lax.broadcasted_iota(jnp.int32, (nblk, width), 0)
    blk_f = blk.astype(F32)
    g = jnp.where(blk < n_valid, gate_t, -jnp.inf)
    sel = jnp.zeros((nblk, width), F32)
    for _ in range(MOBA_TOPK):
        mx = jnp.max(g, axis=0, keepdims=True)
        first = jnp.min(jnp.where(g == mx, blk_f, float(nblk)), axis=0, keepdims=True)
        pick = (blk_f == first) & (mx > -jnp.inf)
        sel = jnp.where(pick, 1.0, sel)
        g = jnp.where(pick, -jnp.inf, g)
    return jnp.where(sel > 0.0, 0.0, -jnp.inf)


def _key_aug_columns(tm):
    lane = lax.broadcasted_iota(jnp.int32, (tm, LANES), 1)
    r = lax.broadcasted_iota(jnp.int32, (tm, LANES), 0).astype(F32)
    return jnp.where((lane >= MOBA_HEAD_DIM) & (lane < MOBA_HEAD_DIM + ALIBI_PARTS), r, 0.0)


def _moba_prompt_kernel(slopes_ref, kn_ref, qn_ref, dmin_ref, knt_ref, q_ref, ka_ref, vt_ref, km_ref, o_ref,
                        brow_s, ubuf, *, nblk):
    hp = pl.program_id(0)
    i = pl.program_id(1)
    tq = MOBA_BLOCK
    d = MOBA_HEAD_DIM
    q = q_ref[...]
    lane_k = lax.broadcasted_iota(jnp.int32, (nblk, 2 * d), 1)
    causal =(lax.broadcasted_iota(jnp.int32, (MOBA_BLOCK, tq), 0)
              <= lax.broadcasted_iota(jnp.int32, (MOBA_BLOCK, tq), 1))
    blk_f = lax.broadcasted_iota(jnp.int32, (nblk, tq), 0).astype(F32)
    col_f = lax.broadcasted_iota(jnp.int32, (nblk, tq), 1).astype(F32)
    aug_row = lax.broadcasted_iota(jnp.int32, (d, tq), 0)
    i_f = i.astype(F32)

    q_t = q.T
    km = km_ref[...]
    km2 = jnp.concatenate([jnp.where(lane_k < d, km, 0.0), jnp.where(lane_k >= d, km, 0.0)], axis=0)
    gate_both = _dot_split(km2, q_t)

    rhs = []
    for j in range(2):
        a2 = slopes_ref[2 * hp + j] * LOG2E
        sel = _select_topk_rows(gate_both[j * nblk:(j + 1) * nblk, :], i, nblk, tq)
        brow_s[j] = sel + a2 * (MOBA_BLOCK * (blk_f - i_f) - col_f)
        rest = jnp.full((d, tq), a2, F32)
        aug = jnp.zeros((d, tq), F32)
        for part in range(ALIBI_PARTS):
            term = rest.astype(BF16).astype(F32)
            aug = jnp.where(aug_row == part, term, aug)
            rest = rest - term
        q_h = q_t[j * d:(j + 1) * d, :] * (d ** -0.5 * LOG2E)
        rhs.append(jnp.concatenate([q_h, aug], axis=0).astype(BF16))

    def block_update(st, u, b_row, v_t):
        m_run, o_t = st
        m_new = jnp.maximum(m_run, jnp.max(u, axis=0, keepdims=True) + b_row)
        alpha = jnp.exp2(m_run - m_new)
        p = jnp.exp2(u - (m_new - b_row))
        return m_new, alpha * o_t + _dot(v_t, p.astype(BF16))

    def score(n, slot):
        for j in range(2):
            ubuf[slot, j] = _dot(ka_ref[j, n], rhs[j])

    def consume(n, slot, sts, own):
        new = []
        for j in range(2):
            u = ubuf[slot, j]
            if own:
                u = jnp.where(causal, u, -jnp.inf)
                b_row = -(slopes_ref[2 * hp + j] * LOG2E) * col_f[0:1, :]
            else:
                b_row = brow_s[j, pl.ds(n, 1), :]
            new.append(block_update(sts[j], u, b_row, vt_ref[n, j * MOBA_VT_ROWS:(j + 1) * MOBA_VT_ROWS, :]))
        return tuple(new)

    def step(n, slot, slot_ahead, sts):
        score(jnp.minimum(n + MOBA_AHEAD, i), slot_ahead)
        return consume(n, slot, sts, False)

    def group(base, unroll, sts):
        for k in range(unroll):
            sts = step(base + k, k % MOBA_SLOTS, (k + MOBA_AHEAD) % MOBA_SLOTS, sts)
        return sts

    def first_needed(j):
        hd = 2 * hp + j
        a2 = slopes_ref[hd] * LOG2E
        c = d ** -0.5 * LOG2E
        qn = qn_ref[i * MOBA_HEADS + hd]
        floor = c * (dmin_ref[i * MOBA_HEADS + hd] - BOUND_SLACK * qn * kn_ref[i * MOBA_HEADS + hd])

        blk_row = lax.broadcasted_iota(jnp.int32, (1, nblk), 1).astype(F32)
        ceil = (c * (1.0 + BOUND_SLACK) * qn * knt_ref[pl.ds(hd, 1), :]
                + a2 * (MOBA_BLOCK * (blk_row - i_f) + (MOBA_BLOCK - 1)))
        needed = (blk_row < i_f) & (ceil >= floor - UNDERFLOW_LOG2)
        return jnp.min(jnp.where(needed, blk_row, i_f)).astype(jnp.int32)

    start = (jnp.minimum(first_needed(0), first_needed(1)) // MOBA_SLOTS) * MOBA_SLOTS

    init = (jnp.full((1, tq), NEG_BIG, F32), jnp.zeros((MOBA_VT_ROWS, tq), F32))
    for k in range(MOBA_AHEAD):
        score(jnp.minimum(start + k, i), k)
    n_big = (i - start) // MOBA_UNROLL
    sts = lax.fori_loop(0, n_big, lambda g, s: group(start + g * MOBA_UNROLL, MOBA_UNROLL, s), (init, init))
    base = start + n_big * MOBA_UNROLL
    n_small = (i - base) // MOBA_SLOTS
    sts = lax.fori_loop(0, n_small, lambda g, s: group(base + g * MOBA_SLOTS, MOBA_SLOTS, s), sts)
    base = base + n_small * MOBA_SLOTS
    rem = i - base
    for k in range(MOBA_SLOTS - 1):
        sts = lax.cond(rem > k, lambda s, k=k: step(base + k, k, (k + MOBA_AHEAD) % MOBA_SLOTS, s),
                       lambda s: s, sts)
    outs = []
    for _, o_t in consume(i, rem, sts, True):
        outs.append(o_t[0:d, :] / o_t[d:d + 1, :])
    o_ref[...] = jnp.concatenate(outs, axis=0).T.astype(BF16)


def _alibi_slopes():
    return jnp.asarray(np.exp2(-8.0 * np.arange(1, MOBA_HEADS + 1) / MOBA_HEADS), dtype=F32)


def _moba_prompt(q, ka, vt, km, bounds):
    s = q.shape[0]
    nblk = s // MOBA_BLOCK
    assert nblk * MOBA_BLOCK == s and 2 * MOBA_HEAD_DIM == LANES
    pairs = MOBA_HEADS // 2
    tables = [bounds[:, r, :MOBA_HEADS].reshape(-1) for r in range(3)]
    kn_t = bounds[:, 0, :MOBA_HEADS].T
    smem = pl.BlockSpec(memory_space=pltpu.SMEM)
    return pl.pallas_call(
        functools.partial(_moba_prompt_kernel, nblk=nblk),
        grid=(pairs, nblk),
        in_specs=[smem, smem, smem, smem, _const_spec((MOBA_HEADS, nblk)),
                  pl.BlockSpec((MOBA_BLOCK, LANES), lambda hp, i: (i, hp)),
                  pl.BlockSpec((2, nblk, MOBA_BLOCK, LANES), lambda hp, i: (hp, 0, 0, 0)),
                  pl.BlockSpec((nblk, 2 * MOBA_VT_ROWS, MOBA_BLOCK), lambda hp, i: (0, hp, 0)),
                  pl.BlockSpec((nblk, LANES), lambda hp, i: (0, hp))],
        out_specs=pl.BlockSpec((MOBA_BLOCK, LANES), lambda hp, i: (i, hp)),
        out_shape=jax.ShapeDtypeStruct((s, MOBA_INNER), BF16),
        scratch_shapes=[pltpu.VMEM((2, nblk, MOBA_BLOCK), F32),
                        pltpu.VMEM((MOBA_SLOTS, 2, MOBA_BLOCK, MOBA_BLOCK), F32)],
        compiler_params=_cparams(("arbitrary", "arbitrary")), name="moba_prompt",
    )(_alibi_slopes(), *tables, kn_t, q, ka, vt, km)


PAGES_PER_STEP = 64
SAMPLE_AHEAD = 3
SAMPLE_BUFS = SAMPLE_AHEAD + 1


def _kmean_kernel(pt_ref, *refs):
    del pt_ref
    o_ref = refs[-1]
    s = pl.program_id(1)
    ppb = MOBA_BLOCK // PAGE_SIZE
    bps = PAGES_PER_STEP // ppb

    @pl.when(s == 0)
    def _():
        o_ref[...] = jnp.zeros(o_ref.shape, F32)

    acc = o_ref[0]
    lane = lax.broadcasted_iota(jnp.int32, acc.shape, 2)
    for m in range(bps):
        x = refs[m * ppb][0]
        for p in range(1, ppb):
            x = x + refs[m * ppb + p][0]
        mean = jnp.sum(x, axis=-1, keepdims=True) * (1.0 / MOBA_BLOCK)
        acc = jnp.where(lane == s * bps + m, mean, acc)
    o_ref[0] = acc


def _sample_kmean(k_pool_t, page_table):
    db, n_pages = page_table.shape
    ppb = MOBA_BLOCK // PAGE_SIZE
    n_full = n_pages // ppb
    assert n_full * ppb == n_pages and n_pages % PAGES_PER_STEP == 0
    steps = n_pages // PAGES_PER_STEP
    blk = (1, MOBA_HEADS, MOBA_HEAD_DIM, PAGE_SIZE)

    def page_spec(p):
        return pl.BlockSpec(blk, lambda b, s, pt: (pt[b * n_pages + s * PAGES_PER_STEP + p], 0, 0, 0))

    out_blk = (1, MOBA_HEADS, MOBA_HEAD_DIM, n_full)
    grid_spec = pltpu.PrefetchScalarGridSpec(
        num_scalar_prefetch=1, grid=(db, steps),
        in_specs=[page_spec(p) for p in range(PAGES_PER_STEP)],
        out_specs=pl.BlockSpec(out_blk, lambda b, s, pt: (b, 0, 0, 0)))
    return pl.pallas_call(
        _kmean_kernel, grid_spec=grid_spec,
        out_shape=jax.ShapeDtypeStruct((db,) + out_blk[1:], F32),
        compiler_params=_cparams(("arbitrary", "arbitrary")), name="sample_kmean",
    )(page_table.reshape(-1), *([k_pool_t] * PAGES_PER_STEP))


def _sample_select_kernel(q_ref, km_ref, idx_ref, *, n_full):
    t8 = q_ref.shape[2]
    blk = lax.broadcasted_iota(jnp.int32, (t8, n_full), 1).astype(F32)
    lane = lax.broadcasted_iota(jnp.int32, (t8, LANES), 1)
    for h in range(MOBA_HEADS):
        g = jnp.dot(q_ref[0, h], km_ref[0, h], precision=HIGHEST, preferred_element_type=F32)
        out = jnp.zeros((t8, LANES), jnp.int32)
        for k in range(MOBA_TOPK):
            mx = jnp.max(g, axis=-1, keepdims=True)
            first = jnp.min(jnp.where(g == mx, blk, float(n_full)), axis=-1, keepdims=True)
            out = jnp.where(lane == k, first.astype(jnp.int32), out)
            g = jnp.where(blk == first, -jnp.inf, g)
        idx_ref[0, h] = out


def _sample_select(qh, km):
    db, h, t8, d = qh.shape
    n_full = km.shape[3]
    assert n_full >= MOBA_TOPK
    b4 = lambda b: (b, 0, 0, 0)
    return pl.pallas_call(
        functools.partial(_sample_select_kernel, n_full=n_full), grid=(db,),
        in_specs=[pl.BlockSpec((1, h, t8, d), b4), pl.BlockSpec((1, h, d, n_full), b4)],
        out_specs=pl.BlockSpec((1, h, t8, LANES), b4),
        out_shape=jax.ShapeDtypeStruct((db, h, t8, LANES), jnp.int32),
        compiler_params=_cparams(("arbitrary",)), name="sample_select",
    )(qh, km)


def _sample_attn_kernel(idx_ref, phys_ref, slopes_ref, q_ref, kn_ref, vn_ref, kpool, vpool, o_ref,
                        kbuf, vbuf, sem, *, ds, past):
    ppb = MOBA_BLOCK // PAGE_SIZE
    n_sel = MOBA_TOPK * ppb
    n_pg = ds * n_sel
    b = pl.program_id(0)
    h = pl.program_id(1)
    n_heads = pl.num_programs(1)
    step = b * n_heads + h
    n_steps = pl.num_programs(0) * n_heads

    def page_copies(st, half):
        hh = st % n_heads
        out = []
        for k in range(n_pg):
            page = phys_ref[st * n_pg + k]
            out.append(pltpu.make_async_copy(kpool.at[page, hh], kbuf.at[half, k], sem.at[0, half]))
            out.append(pltpu.make_async_copy(vpool.at[page, hh], vbuf.at[half, k], sem.at[1, half]))
        return out

    @pl.when(step == 0)
    def _():
        for st in range(SAMPLE_AHEAD):
            for n, cp in enumerate(page_copies(st, st)):
                cp.start(priority=n % 2)

    @pl.when(step + SAMPLE_AHEAD < n_steps)
    def _():
        for n, cp in enumerate(page_copies(step + SAMPLE_AHEAD, (step + SAMPLE_AHEAD) % SAMPLE_BUFS)):
            cp.start(priority=n % 2)

    half = step % SAMPLE_BUFS
    for cp in page_copies(step, half):
        cp.wait()
    k_refs = [kbuf.at[half, k] for k in range(n_pg)]
    v_refs = [vbuf.at[half, k] for k in range(n_pg)]
    slope = slopes_ref[h]
    t8 = q_ref.shape[3]
    c_page = lax.broadcasted_iota(jnp.int32, (1, PAGE_SIZE), 1).astype(F32)
    c_new = lax.broadcasted_iota(jnp.int32, (1, t8), 1)
    qf = q_ref[0, 0] * (MOBA_HEAD_DIM ** -0.5)
    k_new = kn_ref[0, 0]
    v_new = vn_ref[0, 0]
    o_ref[0, 0] = jnp.zeros((MOBA_HEAD_DIM, t8), F32)
    for t in range(ds):
        q_t = qf[:, t:t + 1]
        t_pos = float(past + t)
        scores = []
        for sp in range(n_sel):
            slot, p = divmod(sp, ppb)
            blk_idx = idx_ref[((b * MOBA_HEADS + h) * ds + t) * MOBA_TOPK + slot]
            pos0 = (blk_idx * MOBA_BLOCK + p * PAGE_SIZE).astype(F32)
            s = jnp.sum(k_refs[t * n_sel + sp][...] * q_t, axis=0, keepdims=True)
            scores.append(s - slope * (t_pos - (pos0 + c_page)))
        s_new = jnp.sum(k_new * q_t, axis=0, keepdims=True)
        s_new = s_new - slope * (t_pos - (float(past) + c_new.astype(F32)))
        s_new = jnp.where(c_new <= t, s_new, -jnp.inf)
        mx = jnp.max(s_new, axis=-1, keepdims=True)
        for s in scores:
            mx = jnp.maximum(mx, jnp.max(s, axis=-1, keepdims=True))
        p_new = jnp.exp(s_new - mx)
        l = jnp.sum(p_new, axis=-1, keepdims=True)
        o = jnp.sum(p_new * v_new, axis=-1, keepdims=True)
        acc = jnp.zeros((MOBA_HEAD_DIM, PAGE_SIZE), F32)
        for sp in range(n_sel):
            pr = jnp.exp(scores[sp] - mx)
            l = l + jnp.sum(pr, axis=-1, keepdims=True)
            acc = acc + pr * v_refs[t * n_sel + sp][...]
        o = o + jnp.sum(acc, axis=-1, keepdims=True)
        o_ref[0, 0, :, t:t + 1] = o / l


def _sample_attn(qh, kh, vh, idx, k_pool, v_pool, page_table, ds):
    db, h, d, t8 = qh.shape
    n_pages = page_table.shape[1]
    ppb = MOBA_BLOCK // PAGE_SIZE
    assert n_pages % ppb == 0
    past = n_pages * PAGE_SIZE
    n_sel = MOBA_TOPK * ppb
    logical = idx[..., None] * ppb + jnp.arange(ppb, dtype=jnp.int32)
    phys = page_table[jnp.arange(db)[:, None, None, None], logical.reshape(db, h, ds, n_sel)]
    tok = pl.BlockSpec((1, 1, d, t8), lambda b, hh, *_: (b, hh, 0, 0))
    hbm = pl.BlockSpec(memory_space=pl.ANY)
    assert db * h >= SAMPLE_AHEAD
    page_buf = pltpu.VMEM((SAMPLE_BUFS, ds * n_sel, d, PAGE_SIZE), F32)
    grid_spec = pltpu.PrefetchScalarGridSpec(
        num_scalar_prefetch=2, grid=(db, h),
        in_specs=[pl.BlockSpec(memory_space=pltpu.SMEM), tok, tok, tok, hbm, hbm],
        out_specs=tok,
        scratch_shapes=[page_buf, page_buf, pltpu.SemaphoreType.DMA((2, SAMPLE_BUFS))])
    return pl.pallas_call(
        functools.partial(_sample_attn_kernel, ds=ds, past=past), grid_spec=grid_spec,
        out_shape=jax.ShapeDtypeStruct((db, h, d, t8), F32),
        compiler_params=_cparams(("arbitrary", "arbitrary")), name="sample_attn",
    )(idx.reshape(-1), phys.reshape(-1), _alibi_slopes(), qh, kh, vh, k_pool, v_pool)


def _split_w_in_a(w, f32_query):
    wb = w.astype(BF16)
    wdt = jnp.zeros((D_MODEL, LANES), BF16).at[:, :SSD_HEADS].set(wb[:, OFF_DT:OFF_Q])
    wq = w[:, OFF_Q:OFF_K] if f32_query else wb[:, OFF_Q:OFF_K]
    return [wb[:, :OFF_XBC], wb[:, OFF_XBC:OFF_DT], wdt, wq, wb[:, OFF_K:OFF_V], wb[:, OFF_V:]]


def _pad_seq(a, b, l, lp):
    a = a.reshape(b, l, a.shape[-1])
    return a if lp == l else jnp.pad(a, ((0, 0), (0, lp - l), (0, 0)))


def _round_up(n, m):
    return -(-n // m) * m


def _heads(a, b, l):
    return a.reshape(b, l, MOBA_HEADS, MOBA_HEAD_DIM).transpose(0, 2, 1, 3)


def _pad_tokens(a):
    l = a.shape[-2]
    return jnp.pad(a, ((0, 0),) * (a.ndim - 2) + ((0, _round_up(l, SUBLANES) - l), (0, 0)))


def _trunk(x, bsz, length, prompt, caches, p):
    ssm_conv0, ssm0, hgrn0, ffn_conv0 = caches[:4]
    m = bsz * length
    k_rows = v_rows = ssm_c = ssm_s = hgrn_s = None
    ffn_c = []
    for layer in range(DEPTH):
        nw = p['norm_mix'][layer].reshape(1, -1)
        if layer % 2 == 0:
            ia = layer // 2
            assert ia == 0
            ws = _split_w_in_a(p['w_in_a'][ia], f32_query=not prompt)
            outs = _inproj_a(x, nw, ws, prompt)
            z, xbc, dtr, q = outs[:4]
            lp = _round_up(length, _chunk_len(length, SSD_CHUNK))
            y_ssd, hist, s_new = _ssd(
                _pad_seq(z, bsz, length, lp), _pad_seq(xbc, bsz, length, lp), _pad_seq(dtr, bsz, length, lp),
                ssm_conv0[ia], ssm0[ia], p['ssd_conv_w'][ia], p['ssd_conv_b'][ia], p['ssd_dt_bias'][ia],
                p['ssd_a_log'][ia], p['ssd_d'][ia], p['ssd_norm_w'][ia], length)
            y_ssd = y_ssd[:, :length].reshape(m, SSD_INNER)
            if prompt:
                assert bsz == 1
                kr, vr, ka, vt, km, bounds = outs[4:]
                o_att = _moba_prompt(q, ka, vt, km.reshape(-1, MOBA_INNER), bounds)
                k_rows, v_rows = jnp.swapaxes(kr, -1, -2)[None], jnp.swapaxes(vr, -1, -2)[None]
            else:
                k, v = outs[4:]
                k_pool, v_pool, page_table = caches[4:]
                k_pool_t = jnp.swapaxes(k_pool[ia], -1, -2)
                v_pool_t = jnp.swapaxes(v_pool[ia], -1, -2)
                qh, kh, vh = (_pad_tokens(_heads(t, bsz, length)) for t in (q, k, v))
                qt, kt, vt = (jnp.swapaxes(t, -1, -2) for t in (qh, kh, vh))
                km = _sample_kmean(k_pool_t, page_table)
                idx = _sample_select(qh, km)[:, :, :length, :MOBA_TOPK]
                o = _sample_attn(qt, kt, vt, idx, k_pool_t, v_pool_t, page_table, length)
                o_att = o[..., :length].transpose(0, 3, 1, 2).reshape(m, MOBA_INNER).astype(BF16)
                k_rows, v_rows = kh[:, :, :length], vh[:, :, :length]
            wo = p['w_out_a'][ia].astype(BF16)
            mix_acts, mix_ws = [y_ssd, o_att], [wo[:SSD_INNER], wo[SSD_INNER:]]
            ssm_c, ssm_s = hist, s_new
        else:
            ic = layer // 2
            assert ic == 0
            wc = p['w_in_c'][ic].astype(BF16)
            ws = [wc[:, :HGRN_K], wc[:, HGRN_K:2 * HGRN_K], wc[:, 2 * HGRN_K:2 * HGRN_K + HGRN_V],
                  wc[:, 2 * HGRN_K + HGRN_V:]]
            qa, fx, iv, g = _norm_matmul(x, nw, ws, [F32, F32, BF16, F32])
            lp = _round_up(length, _chunk_len(length, HGRN_CHUNK))
            o, s_new = _gla(*(_pad_seq(t, bsz, length, lp) for t in (qa, fx, iv, g)),
                            p['hgrn_lb_gamma'], p['hgrn_norm_w'][ic], hgrn0[ic], length, layer)
            mix_acts, mix_ws = [o[:, :length].reshape(m, HGRN_V)], [p['w_out_c'][ic].astype(BF16)]
            hgrn_s = s_new
        nf = p['norm_final'].reshape(1, -1) if layer == DEPTH - 1 else None
        hist = ffn_conv0[layer, 0] if prompt else ffn_conv0[layer]
        x, fh = _ffn(x, mix_acts, mix_ws, p['norm_ffn'][layer].reshape(1, -1),
                     p['ffn_w_up'].astype(BF16), p['ffn_conv_w'][layer],
                     p['ffn_conv_b'][layer].reshape(1, -1), p['ffn_w_down'].astype(BF16), layer, hist,
                     None if prompt else length, nf)
        ffn_c.append(fh[None] if prompt else fh)
    return (x.reshape(bsz, length, D_MODEL), k_rows[None], v_rows[None], ssm_s[None], ssm_c[None],
            hgrn_s[None], jnp.stack(ffn_c))


def kernel(x_prompt, x_sample, cache_k_pool, cache_v_pool, page_table, state_ssm, state_ssm_conv, state_hgrn,
           state_ffn_conv, norm_mix, norm_ffn, norm_final, w_in_a, w_out_a, ssd_conv_w, ssd_conv_b,
           ssd_dt_bias, ssd_a_log, ssd_d, ssd_norm_w, w_in_c, w_out_c, hgrn_lb_gamma, hgrn_norm_w, ffn_w_up,
           ffn_conv_w, ffn_conv_b, ffn_w_down):
    p = dict(norm_mix=norm_mix, norm_ffn=norm_ffn, norm_final=norm_final, w_in_a=w_in_a, w_out_a=w_out_a,
             ssd_conv_w=ssd_conv_w, ssd_conv_b=ssd_conv_b, ssd_dt_bias=ssd_dt_bias, ssd_a_log=ssd_a_log,
             ssd_d=ssd_d, ssd_norm_w=ssd_norm_w, w_in_c=w_in_c, w_out_c=w_out_c, hgrn_lb_gamma=hgrn_lb_gamma,
             hgrn_norm_w=hgrn_norm_w, ffn_w_up=ffn_w_up, ffn_conv_w=ffn_conv_w, ffn_conv_b=ffn_conv_b,
             ffn_w_down=ffn_w_down)
    bp, sp, _ = x_prompt.shape
    db, ds, _ = x_sample.shape
    na, nc = (DEPTH + 1) // 2, DEPTH // 2
    zeros = lambda *s: jnp.zeros(s, F32)
    prompt_caches = (zeros(na, bp, SSD_CONV - 1, SSD_CONV_DIM), zeros(na, bp, SSD_HEADS, SSD_HEAD_DIM, SSD_STATE),
                     zeros(nc, bp, HGRN_HEADS, HGRN_KEY_DIM, HGRN_VAL_DIM), zeros(DEPTH, bp, FFN_CONV - 1, FFN_DIM))
    outs_p = _trunk(x_prompt.reshape(bp * sp, D_MODEL), bp, sp, True, prompt_caches, p)
    sample_caches = (state_ssm_conv, state_ssm, state_hgrn, state_ffn_conv, cache_k_pool, cache_v_pool, page_table)
    outs_s = _trunk(x_sample.reshape(db * ds, D_MODEL), db, ds, False, sample_caches, p)
    yp, krp, vrp, ssp, scp, hgp, fcp = outs_p
    ys, krs, vrs, sss, scs, hgs, fcs = outs_s
    return (yp, ys, krp, vrp, ssp, scp, hgp, fcp, krs, vrs, sss, scs, hgs, fcs)
```
